```python
import jax, jax.numpy as jnp
from jax import lax
import numpy as np

D_MODEL = 1024
BATCH = 2
SEQ = 8192
DEPTH = 1

HEAD_DIM = 64
GRID_W = 64
NA_HEADS = 8
NA_KH_MAX = 8
NA_KW = 16
NA_QCB = NA_KW
NA_SPAN = 2 * NA_KW
SW_HEADS = 8
SW_KV_HEADS = 2
SW_WINDOW = 128
SW_BLOCK = 128
T5_BUCKETS = 32
T5_MAX_DIST = 128
FFN_HIDDEN = -(-8 * D_MODEL // (3 * 256)) * 256

A_W = NA_HEADS * HEAD_DIM
B_QW = SW_HEADS * HEAD_DIM
B_KVW = SW_KV_HEADS * HEAD_DIM
IN_WIDTHS = [A_W, A_W, A_W, B_QW, B_KVW, B_KVW, D_MODEL, D_MODEL]
IN_WIDTH = sum(IN_WIDTHS)
IN_SPLITS = list(np.cumsum(IN_WIDTHS)[:-1])
RMS_EPS = 1e-6
NEG_INF = -1e30

kernel_name = "hybrid_natten_swa_gated_encoder_block"


def rms_norm(x, g):
    xf = x.astype(jnp.float32)
    y = xf * lax.rsqrt(jnp.mean(xf * xf, axis=-1, keepdims=True) + RMS_EPS)
    return y.astype(x.dtype) * g


def t5_bucket(rel):
    half = T5_BUCKETS // 2
    max_exact = half // 2
    ret = (rel > 0).astype(np.int32) * half
    n = np.abs(rel)
    large = max_exact + (np.log(np.maximum(n, 1) / max_exact)
                         / np.log(T5_MAX_DIST / max_exact) * (half - max_exact)).astype(np.int32)
    large = np.minimum(large, half - 1)
    return ret + np.where(n < max_exact, n, large)


def neighbourhood_attention(q, k, v, rpb):
    B, T, H, D = q.shape
    rows = T // GRID_W
    kh = min(NA_KH_MAX, rows)
    ncb = GRID_W // NA_QCB
    r = np.arange(rows)
    row_start = np.clip(r - kh // 2, 0, rows - kh)
    key_rows = row_start[:, None] + np.arange(kh)[None, :]
    j = np.arange(ncb)
    span_start = np.clip(j * NA_QCB - NA_KW // 2, 0, GRID_W - NA_SPAN)
    key_cols = span_start[:, None] + np.arange(NA_SPAN)[None, :]
    q_cols = j[:, None] * NA_QCB + np.arange(NA_QCB)[None, :]
    win_start = np.clip(q_cols - NA_KW // 2, 0, GRID_W - NA_KW)
    kc = key_cols[:, None, :]
    in_win = (kc >= win_start[..., None]) & (kc < win_start[..., None] + NA_KW)

    idx = (key_rows[:, None, :, None] * GRID_W + key_cols[None, :, None, :]).reshape(rows, ncb, kh * NA_SPAN)
    kg = jnp.take(k, idx, axis=1)
    vg = jnp.take(v, idx, axis=1)
    qb = q.reshape(B, rows, ncb, NA_QCB, H, D)
    s = jnp.einsum('brjqhd,brjkhd->brjhqk', qb, kg,
                   preferred_element_type=jnp.float32) * (HEAD_DIM ** -0.5)

    dr = key_rows - r[:, None] + (NA_KH_MAX - 1)
    dc = np.clip(kc - q_cols[..., None], -(NA_KW - 1), NA_KW - 1) + (NA_KW - 1)
    bias = rpb[dr[:, None, None, :, None], dc[None, :, :, None, :]]
    bias = jnp.transpose(bias, (0, 1, 5, 2, 3, 4)).astype(jnp.float32)
    bias = jnp.where(in_win[None, :, None, :, None, :], bias, NEG_INF)
    bias = bias.reshape(rows, ncb, H, NA_QCB, kh * NA_SPAN)
    p = jax.nn.softmax(s + bias[None], axis=-1)
    o = jnp.einsum('brjhqk,brjkhd->brjqhd', p.astype(v.dtype), vg)
    return o.reshape(B, T, H, D)


def sliding_window_attention(q, k, v, sink, t5_table):
    B, T, H, D = q.shape
    kvh = k.shape[2]
    g = H // kvh
    nb = T // SW_BLOCK
    pad = ((0, 0), (SW_BLOCK, SW_BLOCK), (0, 0), (0, 0))
    kp = jnp.pad(k, pad).reshape(B, nb + 2, SW_BLOCK, kvh, D)
    vp = jnp.pad(v, pad).reshape(B, nb + 2, SW_BLOCK, kvh, D)
    kw = jnp.concatenate([kp[:, :-2], kp[:, 1:-1], kp[:, 2:]], axis=2)
    vw = jnp.concatenate([vp[:, :-2], vp[:, 1:-1], vp[:, 2:]], axis=2)
    qb = q.reshape(B, nb, SW_BLOCK, kvh, g, D)
    s = jnp.einsum('bnqkgd,bnskd->bnkgqs', qb, kw,
                   preferred_element_type=jnp.float32) * (HEAD_DIM ** -0.5)

    rel = np.arange(3 * SW_BLOCK)[None, :] - SW_BLOCK - np.arange(SW_BLOCK)[:, None]
    bias = t5_table[t5_bucket(rel)]
    bias = jnp.transpose(bias, (2, 0, 1)).reshape(kvh, g, SW_BLOCK, 3 * SW_BLOCK).astype(jnp.float32)
    key_pos = np.arange(nb)[:, None] * SW_BLOCK + np.arange(3 * SW_BLOCK)[None, :] - SW_BLOCK
    valid = (np.abs(rel) <= SW_WINDOW)[None] & ((key_pos >= 0) & (key_pos < T))[:, None, :]
    s = jnp.where(valid[None, :, None, None], s + bias[None, None], NEG_INF)

    sk = sink.astype(jnp.float32).reshape(1, 1, kvh, g, 1, 1)
    m = jnp.maximum(jnp.max(s, axis=-1, keepdims=True), sk)
    e = jnp.exp(s - m)
    p = e / (jnp.sum(e, axis=-1, keepdims=True) + jnp.exp(sk - m))
    o = jnp.einsum('bnkgqs,bnskd->bnqkgd', p.astype(v.dtype), vw)
    return o.reshape(B, T, H, D)


def setup_inputs(seed: int = 0) -> dict:
    key = jax.random.key(seed)
    ks = jax.random.split(key, 18)
    f32 = jnp.float32

    def w(k, shape, fan_in):
        return jax.random.normal(k, shape, f32) * fan_in ** -0.5

    def gain(k, shape):
        return 1.0 + 0.05 * jax.random.normal(k, shape, f32)

    return {
        "x": jax.random.normal(ks[0], (BATCH, SEQ, D_MODEL), f32),
        "norm_mix": gain(ks[1], (DEPTH, D_MODEL)),
        "w_in": w(ks[2], (DEPTH, D_MODEL, IN_WIDTH), D_MODEL),
        "q_norm_a": gain(ks[3], (DEPTH, HEAD_DIM)),
        "k_norm_a": gain(ks[4], (DEPTH, HEAD_DIM)),
        "rpb_a": 0.1 * jax.random.normal(ks[5], (DEPTH, 2 * NA_KH_MAX - 1, 2 * NA_KW - 1, NA_HEADS), f32),
        "q_norm_b": gain(ks[6], (DEPTH, HEAD_DIM)),
        "k_norm_b": gain(ks[7], (DEPTH, HEAD_DIM)),
        "sink_b": 1.0 + 0.5 * jax.random.normal(ks[8], (DEPTH, SW_HEADS), f32),
        "t5_table": 0.1 * jax.random.normal(ks[9], (T5_BUCKETS, SW_HEADS), f32),
        "w_branch_a": w(ks[10], (DEPTH, A_W, D_MODEL), A_W),
        "w_branch_b": w(ks[11], (DEPTH, B_QW, D_MODEL), B_QW),
        "w_out": w(ks[12], (DEPTH, D_MODEL, D_MODEL), D_MODEL),
        "norm_ffn": gain(ks[13], (DEPTH, D_MODEL)),
        "w_gate": w(ks[14], (DEPTH, D_MODEL, FFN_HIDDEN), D_MODEL),
        "w_up": w(ks[15], (DEPTH, D_MODEL, FFN_HIDDEN), D_MODEL),
        "w_down": w(ks[16], (DEPTH, FFN_HIDDEN, D_MODEL), FFN_HIDDEN),
    }


def reference(x, norm_mix, w_in, q_norm_a, k_norm_a, rpb_a, q_norm_b, k_norm_b, sink_b,
              t5_table, w_branch_a, w_branch_b, w_out, norm_ffn, w_gate, w_up, w_down):
    B, T, _ = x.shape
    for l in range(DEPTH):
        h = rms_norm(x, norm_mix[l])
        proj = jnp.einsum('btd,de->bte', h, w_in[l])
        q_a, k_a, v_a, q_b, k_b, v_b, g_a, g_b = jnp.split(proj, IN_SPLITS, axis=-1)

        q_a = rms_norm(q_a.reshape(B, T, NA_HEADS, HEAD_DIM), q_norm_a[l])
        k_a = rms_norm(k_a.reshape(B, T, NA_HEADS, HEAD_DIM), k_norm_a[l])
        v_a = v_a.reshape(B, T, NA_HEADS, HEAD_DIM)
        o_a = neighbourhood_attention(q_a, k_a, v_a, rpb_a[l]).reshape(B, T, A_W)

        q_b = rms_norm(q_b.reshape(B, T, SW_HEADS, HEAD_DIM), q_norm_b[l])
        k_b = rms_norm(k_b.reshape(B, T, SW_KV_HEADS, HEAD_DIM), k_norm_b[l])
        v_b = v_b.reshape(B, T, SW_KV_HEADS, HEAD_DIM)
        o_b = sliding_window_attention(q_b, k_b, v_b, sink_b[l], t5_table).reshape(B, T, B_QW)

        y = (jax.nn.sigmoid(g_a) * jnp.einsum('bte,ed->btd', o_a, w_branch_a[l])
             + jax.nn.sigmoid(g_b) * jnp.einsum('bte,ed->btd', o_b, w_branch_b[l]))
        x = x + jnp.einsum('btd,de->bte', y, w_out[l])

        h = rms_norm(x, norm_ffn[l])
        u = jax.nn.silu(jnp.einsum('btd,df->btf', h, w_gate[l])) * jnp.einsum('btd,df->btf', h, w_up[l])
        x = x + jnp.einsum('btf,fd->btd', u, w_down[l])
    return x
```

```python
import functools

import jax
import jax.numpy as jnp
import numpy as np
from jax import lax
from jax.experimental import pallas as pl
from jax.experimental.pallas import tpu as pltpu

F32 = jnp.float32
BF16 = jnp.bfloat16

D_MODEL = 1024
HEAD_DIM = 64
GRID_W = 64
NA_HEADS = 8
NA_KH = 8
NA_KW = 16
SW_HEADS = 8
SW_KV_HEADS = 2
SW_GROUP = SW_HEADS // SW_KV_HEADS
SW_WINDOW = 128
SW_BLOCK = 128
T5_BUCKETS = 32
T5_MAX_DIST = 128
A_W = NA_HEADS * HEAD_DIM
B_QW = SW_HEADS * HEAD_DIM
B_KVW = SW_KV_HEADS * HEAD_DIM
QKV_W = 3 * A_W + B_QW + 2 * B_KVW
RMS_EPS = 1e-6
NEG_INF = -1e30
QK_SCALE = HEAD_DIM ** -0.5

LANES = 128
MXU_N = 256
VMEM_LIMIT = 56 * 1024 * 1024

TOK_TILE = 512
NA_ROWS_PER_STEP = 8


def _resident(shape):
    return pl.BlockSpec(shape, lambda *_: (0,) * len(shape), pipeline_mode=pl.Buffered(1))


def _params(n_axes):
    return pltpu.CompilerParams(dimension_semantics=("arbitrary",) * n_axes, vmem_limit_bytes=VMEM_LIMIT)


def _rms_scale(xf):
    return lax.rsqrt(jnp.mean(xf * xf, axis=-1, keepdims=True) + RMS_EPS)


def _group_rms_scale(p, ones_bd):
    y = p * p
    y_hi = y.astype(BF16)
    y_lo = (y - y_hi.astype(F32)).astype(BF16)
    ss = (jnp.dot(y_hi, ones_bd, preferred_element_type=F32)
          + jnp.dot(y_lo, ones_bd, preferred_element_type=F32))
    return lax.rsqrt(ss * (1.0 / HEAD_DIM) + RMS_EPS)


def _in_proj_kernel(x_ref, g_ref, w_ref, bd_ref, qk_gain_ref,
                    qa_ref, ka_ref, va_ref, qb_ref, kbd_ref, vbd_ref, h_scr):
    xf = x_ref[...]
    h_scr[...] = ((xf * _rms_scale(xf)) * g_ref[...]).astype(BF16)
    bd = bd_ref[...]

    def proj(c0, n):
        return jnp.dot(h_scr[...], w_ref[:, c0:c0 + n], preferred_element_type=F32)

    for out_ref, base, gain_row in ((qa_ref, 0, 0), (ka_ref, A_W, 1), (qb_ref, 3 * A_W, 2)):
        for c in range(0, A_W, MXU_N):
            p = proj(base + c, MXU_N)
            scale = _group_rms_scale(p, bd)
            out_ref[:, c:c + MXU_N] = (p * scale * qk_gain_ref[gain_row:gain_row + 1, :]).astype(BF16)
    for c in range(0, A_W, MXU_N):
        va_ref[:, c:c + MXU_N] = proj(2 * A_W + c, MXU_N).astype(BF16)

    p = proj(3 * A_W + B_QW, 2 * B_KVW)
    kb = p[:, :B_KVW]
    kb = kb * _group_rms_scale(kb, bd[:LANES, :LANES]) * qk_gain_ref[3:4, :LANES]
    vb = p[:, B_KVW:]
    lo = lax.broadcasted_iota(jnp.int32, kb.shape, 1) < HEAD_DIM
    for src, out_ref in ((kb, kbd_ref), (vb, vbd_ref)):
        swapped = pltpu.roll(src, HEAD_DIM, axis=1)
        out_ref[:, :LANES] = jnp.where(lo, src, swapped).astype(BF16)
        out_ref[:, LANES:] = jnp.where(lo, swapped, src).astype(BF16)


def _in_proj(x2, g_mix, w_qkv, ones_bd, qk_gain):
    n = x2.shape[0]
    tm = TOK_TILE
    tile = lambda w: pl.BlockSpec((tm, w), lambda i: (i, 0))
    out_w = (A_W, A_W, A_W, B_QW, 2 * LANES, 2 * LANES)
    return pl.pallas_call(
        _in_proj_kernel,
        grid=(n // tm,),
        in_specs=[tile(D_MODEL), _resident((1, D_MODEL)), _resident((D_MODEL, QKV_W)),
                  _resident((MXU_N, MXU_N)), _resident((4, MXU_N))],
        out_specs=[tile(w) for w in out_w],
        out_shape=[jax.ShapeDtypeStruct((n, w), BF16) for w in out_w],
        scratch_shapes=[pltpu.VMEM((tm, D_MODEL), BF16)],
        compiler_params=_params(1),
        name="in_proj",
    )(x2, g_mix, w_qkv, ones_bd, qk_gain)


def _na_kernel(q_ref, k_ref, v_ref, bias_ref, o_ref, *, rows):
    t = pl.program_id(1)
    lo = lax.broadcasted_iota(jnp.int32, (GRID_W, LANES), 1) < HEAD_DIM
    nt = (((1,), (1,)), ((), ()))

    def row_body(i, carry):
        r = t * NA_ROWS_PER_STEP + i
        rs = jnp.clip(r - NA_KH // 2, 0, rows - NA_KH)
        d0 = (NA_KH - 1) - (r - rs)
        kstart = pl.multiple_of(rs * GRID_W, GRID_W)
        qstart = pl.multiple_of(i * GRID_W, GRID_W)
        for p in range(NA_HEADS // 2):
            cols = slice(p * LANES, (p + 1) * LANES)
            q = q_ref[0, pl.ds(qstart, GRID_W), cols]
            zero = jnp.zeros_like(q)
            q2 = jnp.concatenate([jnp.where(lo, q, zero), jnp.where(lo, zero, q)], axis=0)
            kk = k_ref[0, pl.ds(kstart, NA_KH * GRID_W), cols]
            s = lax.dot_general(q2, kk, nt, preferred_element_type=F32)
            bias = jnp.concatenate([bias_ref[p, d0 + 2 * v] for v in range(NA_KH // 2)], axis=1)
            s = s + bias
            m = jnp.max(s, axis=-1, keepdims=True)
            e = jnp.exp(s - m)
            denom = jnp.sum(e, axis=-1, keepdims=True)
            vv = v_ref[0, pl.ds(kstart, NA_KH * GRID_W), cols]
            o2 = jnp.dot(e.astype(BF16), vv, preferred_element_type=F32) / denom
            o = jnp.where(lo, o2[:GRID_W], o2[GRID_W:])
            o_ref[0, pl.ds(qstart, GRID_W), cols] = o.astype(BF16)
        return carry

    lax.fori_loop(0, NA_ROWS_PER_STEP, row_body, 0)


def _na_attn(qa, ka, va, bias_slabs):
    b, t, _ = qa.shape
    rows = t // GRID_W
    tq = NA_ROWS_PER_STEP * GRID_W
    full = pl.BlockSpec((1, t, A_W), lambda bi, ti: (bi, 0, 0))
    tile = pl.BlockSpec((1, tq, A_W), lambda bi, ti: (bi, ti, 0))
    return pl.pallas_call(
        functools.partial(_na_kernel, rows=rows),
        grid=(b, rows // NA_ROWS_PER_STEP),
        in_specs=[tile, full, full, _resident(bias_slabs.shape)],
        out_specs=tile,
        out_shape=jax.ShapeDtypeStruct((b, t, A_W), BF16),
        compiler_params=_params(2),
        name="na_attn",
    )(qa, ka, va, bias_slabs)


def _na_bias_slabs(rpb):
    c = np.arange(GRID_W)
    kc = c[None, :]
    win_start = np.clip(c - NA_KW // 2, 0, GRID_W - NA_KW)[:, None]
    in_win = (kc >= win_start) & (kc < win_start + NA_KW)
    dc = np.clip(kc - c[:, None], -(NA_KW - 1), NA_KW - 1) + (NA_KW - 1)
    tab = jnp.where(in_win[None, :, :, None], rpb[:, dc, :].astype(F32), NEG_INF)
    tab = jnp.transpose(tab, (3, 0, 1, 2))
    nd = 2 * NA_KH - 2
    pair = jnp.concatenate([tab[:, :nd], tab[:, 1:nd + 1]], axis=-1)
    pair = pair.reshape(NA_HEADS // 2, 2, nd, GRID_W, LANES)
    return jnp.transpose(pair, (0, 2, 1, 3, 4)).reshape(NA_HEADS // 2, nd, LANES, LANES)


def _swa_kernel(sink_ref, q_ref, kp_ref, kc_ref, kn_ref, vp_ref, vc_ref, vn_ref, bias_ref, o_ref, *, nb):
    n = pl.program_id(1)
    lo = lax.broadcasted_iota(jnp.int32, (SW_BLOCK, LANES), 1) < HEAD_DIM
    col = lax.broadcasted_iota(jnp.int32, (1, 3 * SW_BLOCK), 1)
    valid = ((col >= SW_BLOCK) | (n > 0)) & ((col < 2 * SW_BLOCK) | (n < nb - 1))
    nt = (((1,), (1,)), ((), ()))
    for kvh in range(SW_KV_HEADS):
        kcols = slice(kvh * LANES, (kvh + 1) * LANES)
        kcat = jnp.concatenate([kp_ref[0, :, kcols], kc_ref[0, :, kcols], kn_ref[0, :, kcols]], axis=0)
        vcat = jnp.concatenate([vp_ref[0, :, kcols], vc_ref[0, :, kcols], vn_ref[0, :, kcols]], axis=0)
        parts = []
        for j in range(2):
            qb = q_ref[0, :, (2 * kvh + j) * LANES:(2 * kvh + j + 1) * LANES]
            zero = jnp.zeros_like(qb)
            parts += [jnp.where(lo, qb, zero), jnp.where(lo, zero, qb)]
        q4 = jnp.concatenate(parts, axis=0)
        s = lax.dot_general(q4, kcat, nt, preferred_element_type=F32)
        s = jnp.where(valid, s + bias_ref[kvh], NEG_INF)
        es, denoms = [], []
        for g in range(SW_GROUP):
            sg = s[g * SW_BLOCK:(g + 1) * SW_BLOCK]
            sk = sink_ref[kvh * SW_GROUP + g]
            m = jnp.maximum(jnp.max(sg, axis=-1, keepdims=True), sk)
            e = jnp.exp(sg - m)
            denoms.append(jnp.sum(e, axis=-1, keepdims=True) + jnp.exp(sk - m))
            es.append(e.astype(BF16))
        o4 = jnp.dot(jnp.concatenate(es, axis=0), vcat, preferred_element_type=F32)
        og = [o4[g * SW_BLOCK:(g + 1) * SW_BLOCK] / denoms[g] for g in range(SW_GROUP)]
        for j in range(2):
            o_ref[0, :, (2 * kvh + j) * LANES:(2 * kvh + j + 1) * LANES] = (
                jnp.where(lo, og[2 * j], og[2 * j + 1]).astype(BF16))


def _swa_attn(sink, qb, kbd, vbd, bias):
    b, t, _ = qb.shape
    nb = t // SW_BLOCK
    kv_w = kbd.shape[-1]
    qtile = pl.BlockSpec((1, SW_BLOCK, B_QW), lambda bi, ni: (bi, ni, 0))
    prev = pl.BlockSpec((1, SW_BLOCK, kv_w), lambda bi, ni: (bi, jnp.maximum(ni - 1, 0), 0))
    cur = pl.BlockSpec((1, SW_BLOCK, kv_w), lambda bi, ni: (bi, ni, 0))
    nxt = pl.BlockSpec((1, SW_BLOCK, kv_w), lambda bi, ni: (bi, jnp.minimum(ni + 1, nb - 1), 0))
    return pl.pallas_call(
        functools.partial(_swa_kernel, nb=nb),
        grid=(b, nb),
        in_specs=[pl.BlockSpec(memory_space=pltpu.SMEM), qtile, prev, cur, nxt, prev, cur, nxt,
                  _resident(bias.shape)],
        out_specs=qtile,
        out_shape=jax.ShapeDtypeStruct((b, t, B_QW), BF16),
        compiler_params=_params(2),
        name="swa_attn",
    )(sink, qb, kbd, kbd, kbd, vbd, vbd, vbd, bias)


def _t5_bucket(rel):
    half = T5_BUCKETS // 2
    max_exact = half // 2
    ret = (rel > 0).astype(np.int32) * half
    n = np.abs(rel)
    large = max_exact + (np.log(np.maximum(n, 1) / max_exact)
                         / np.log(T5_MAX_DIST / max_exact) * (half - max_exact)).astype(np.int32)
    large = np.minimum(large, half - 1)
    return ret + np.where(n < max_exact, n, large)


def _swa_bias(t5_table):
    rel = np.arange(3 * SW_BLOCK)[None, :] - SW_BLOCK - np.arange(SW_BLOCK)[:, None]
    bias = t5_table[_t5_bucket(rel)].astype(F32)
    bias = jnp.where((np.abs(rel) <= SW_WINDOW)[:, :, None], bias, NEG_INF)
    bias = jnp.transpose(bias, (2, 0, 1))
    return bias.reshape(SW_KV_HEADS, SW_GROUP * SW_BLOCK, 3 * SW_BLOCK)


def _mix_kernel(x_ref, oa_ref, ob_ref, g_ref, wg_ref, wa_ref, wb_ref, wo_ref, x1_ref, h_scr, y_scr):
    xf = x_ref[...]
    h_scr[...] = ((xf * _rms_scale(xf)) * g_ref[...]).astype(BF16)
    for c in range(0, D_MODEL, MXU_N):
        cs = slice(c, c + MXU_N)
        ga = jnp.dot(h_scr[...], wg_ref[:, cs], preferred_element_type=F32)
        gb = jnp.dot(h_scr[...], wg_ref[:, D_MODEL + c:D_MODEL + c + MXU_N], preferred_element_type=F32)
        ya = jnp.dot(oa_ref[...], wa_ref[:, cs], preferred_element_type=F32)
        yb = jnp.dot(ob_ref[...], wb_ref[:, cs], preferred_element_type=F32)
        y_scr[:, cs] = (jax.nn.sigmoid(ga) * ya + jax.nn.sigmoid(gb) * yb).astype(BF16)
    for c in range(0, D_MODEL, MXU_N):
        cs = slice(c, c + MXU_N)
        x1_ref[:, cs] = x_ref[:, cs] + jnp.dot(y_scr[...], wo_ref[:, cs], preferred_element_type=F32)


def _mix_out(x2, oa, ob, g_mix, w_gates, w_a, w_b, w_o):
    n = x2.shape[0]
    tm = TOK_TILE
    tile = lambda w: pl.BlockSpec((tm, w), lambda i: (i, 0))
    return pl.pallas_call(
        _mix_kernel,
        grid=(n // tm,),
        in_specs=[tile(D_MODEL), tile(A_W), tile(B_QW), _resident((1, D_MODEL)),
                  _resident(w_gates.shape), _resident(w_a.shape), _resident(w_b.shape), _resident(w_o.shape)],
        out_specs=tile(D_MODEL),
        out_shape=jax.ShapeDtypeStruct((n, D_MODEL), F32),
        scratch_shapes=[pltpu.VMEM((tm, D_MODEL), BF16), pltpu.VMEM((tm, D_MODEL), BF16)],
        compiler_params=_params(1),
        name="mix_out",
    )(x2, oa, ob, g_mix, w_gates, w_a, w_b, w_o)


def _ffn_kernel(x_ref, g_ref, wg_ref, wu_ref, wd_ref, o_ref, h_scr, a_scr):
    xf = x_ref[...]
    h_scr[...] = ((xf * _rms_scale(xf)) * g_ref[...]).astype(BF16)
    hidden = wg_ref.shape[1]
    for c in range(0, hidden, MXU_N):
        cs = slice(c, c + MXU_N)
        gate = jnp.dot(h_scr[...], wg_ref[:, cs], preferred_element_type=F32)
        up = jnp.dot(h_scr[...], wu_ref[:, cs], preferred_element_type=F32)
        a_scr[:, cs] = (jax.nn.silu(gate) * up).astype(BF16)
    for c in range(0, D_MODEL, MXU_N):
        cs = slice(c, c + MXU_N)
        o_ref[:, cs] = x_ref[:, cs] + jnp.dot(a_scr[...], wd_ref[:, cs], preferred_element_type=F32)


def _ffn(x1, g_ffn, w_gate, w_up, w_down):
    n = x1.shape[0]
    tm = TOK_TILE
    hidden = w_gate.shape[1]
    tile = pl.BlockSpec((tm, D_MODEL), lambda i: (i, 0))
    return pl.pallas_call(
        _ffn_kernel,
        grid=(n // tm,),
        in_specs=[tile, _resident((1, D_MODEL)), _resident(w_gate.shape), _resident(w_up.shape),
                  _resident(w_down.shape)],
        out_specs=tile,
        out_shape=jax.ShapeDtypeStruct((n, D_MODEL), F32),
        scratch_shapes=[pltpu.VMEM((tm, D_MODEL), BF16), pltpu.VMEM((tm, hidden), BF16)],
        compiler_params=_params(1),
        name="ffn",
    )(x1, g_ffn, w_gate, w_up, w_down)


def _layer(x2, b, t, norm_mix, w_in, q_norm_a, k_norm_a, rpb_a, q_norm_b, k_norm_b, sink_b, t5_table,
           w_branch_a, w_branch_b, w_out, norm_ffn, w_gate, w_up, w_down):
    g_mix = norm_mix.reshape(1, D_MODEL)
    w_qkv = w_in[:, :QKV_W].astype(BF16)
    w_gates = w_in[:, QKV_W:].astype(BF16)
    ones_bd = jnp.asarray(np.kron(np.eye(MXU_N // HEAD_DIM), np.ones((HEAD_DIM, HEAD_DIM))), BF16)
    reps = MXU_N // HEAD_DIM
    qk_gain = jnp.stack([jnp.tile(q_norm_a * QK_SCALE, reps), jnp.tile(k_norm_a, reps),
                         jnp.tile(q_norm_b * QK_SCALE, reps), jnp.tile(k_norm_b, reps)]).astype(F32)

    qa, ka, va, qb, kbd, vbd = _in_proj(x2, g_mix, w_qkv, ones_bd, qk_gain)
    n = x2.shape[0]
    r3 = lambda a: a.reshape(b, t, a.shape[-1])
    oa = _na_attn(r3(qa), r3(ka), r3(va), _na_bias_slabs(rpb_a))
    ob = _swa_attn(sink_b.astype(F32), r3(qb), r3(kbd), r3(vbd), _swa_bias(t5_table))
    x1 = _mix_out(x2, oa.reshape(n, A_W), ob.reshape(n, B_QW), g_mix, w_gates,
                  w_branch_a.astype(BF16), w_branch_b.astype(BF16), w_out.astype(BF16))
    return _ffn(x1, norm_ffn.reshape(1, D_MODEL), w_gate.astype(BF16), w_up.astype(BF16),
                w_down.astype(BF16))


def kernel(x, norm_mix, w_in, q_norm_a, k_norm_a, rpb_a, q_norm_b, k_norm_b, sink_b, t5_table,
           w_branch_a, w_branch_b, w_out, norm_ffn, w_gate, w_up, w_down):
    b, t, d = x.shape
    x2 = x.reshape(b * t, d)
    depth = norm_mix.shape[0]
    for l in range(depth):
        x2 = _layer(x2, b, t, norm_mix[l], w_in[l], q_norm_a[l], k_norm_a[l], rpb_a[l], q_norm_b[l],
                    k_norm_b[l], sink_b[l], t5_table, w_branch_a[l], w_branch_b[l], w_out[l],
                    norm_ffn[l], w_gate[l], w_up[l], w_down[l])
    return x2.reshape(b, t, d)
```

```python
import functools

import jax
import jax.numpy as jnp
import numpy as np
from jax import lax
from jax.experimental import pallas as pl
from jax.experimental.pallas import tpu as pltpu

F32 = jnp.float32
BF16 = jnp.bfloat16

D_MODEL = 1024
HEAD_DIM = 64
GRID_W = 64
NA_HEADS = 8
NA_KH = 8
NA_KW = 16
SW_HEADS = 8
SW_KV_HEADS = 2
SW_GROUP = SW_HEADS // SW_KV_HEADS
SW_WINDOW = 128
SW_BLOCK = 128
T5_BUCKETS = 32
T5_MAX_DIST = 128
A_W = NA_HEADS * HEAD_DIM
B_QW = SW_HEADS * HEAD_DIM
B_KVW = SW_KV_HEADS * HEAD_DIM
QKV_W = 3 * A_W + B_QW + 2 * B_KVW
RMS_EPS = 1e-6
NEG_INF = -1e30
QK_SCALE = HEAD_DIM ** -0.5

LANES = 128
MXU_N = 256
VMEM_LIMIT = 56 * 1024 * 1024

TOK_TILE = 512
NA_ROWS_PER_STEP = 8
PROJ_SKEW = 2
NA_SKEW = 4
SW_SKEW = 2
SW_BLOCKS_PER_STEP = 4


def _resident(shape):
    return pl.BlockSpec(shape, lambda *_: (0,) * len(shape), pipeline_mode=pl.Buffered(1))


def _params(n_axes):
    return pltpu.CompilerParams(dimension_semantics=("arbitrary",) * n_axes, vmem_limit_bytes=VMEM_LIMIT)


def _rms_scale(xf):
    return lax.rsqrt(jnp.mean(xf * xf, axis=-1, keepdims=True) + RMS_EPS)


def _group_rms_scale(p, ones_bd):
    ss = jnp.dot((p * p).astype(BF16), ones_bd, preferred_element_type=F32)
    return lax.rsqrt(ss * (1.0 / HEAD_DIM) + RMS_EPS)


def _in_proj_kernel(x_ref, g_ref, w_ref, bd_ref, qk_gain_ref,
                    qa_ref, ka_ref, va_ref, qb_ref, kbd_ref, vbd_ref, h_scr, p_scr):
    xf = x_ref[...]
    h_scr[...] = ((xf * _rms_scale(xf)) * g_ref[...]).astype(BF16)
    bd = bd_ref[...]

    def normed(out_ref, c, gain_row):
        def epilogue(slot):
            p = p_scr[slot]
            scale = _group_rms_scale(p, bd)
            out_ref[:, c:c + MXU_N] = (p * scale * qk_gain_ref[gain_row:gain_row + 1, :]).astype(BF16)
        return epilogue

    def plain(out_ref, c):
        def epilogue(slot):
            out_ref[:, c:c + MXU_N] = p_scr[slot].astype(BF16)
        return epilogue

    def kv_b(slot):
        kb = p_scr[slot, :, :B_KVW]
        kb = kb * _group_rms_scale(kb, bd[:LANES, :LANES]) * qk_gain_ref[3:4, :LANES]
        vb = p_scr[slot, :, B_KVW:]
        lo = lax.broadcasted_iota(jnp.int32, kb.shape, 1) < HEAD_DIM
        for src, out_ref in ((kb, kbd_ref), (vb, vbd_ref)):
            swapped = pltpu.roll(src, HEAD_DIM, axis=1)
            out_ref[:, :LANES] = jnp.where(lo, src, swapped).astype(BF16)
            out_ref[:, LANES:] = jnp.where(lo, swapped, src).astype(BF16)

    units = []
    for c in range(0, A_W, MXU_N):
        units += [(c, normed(qa_ref, c, 0)), (2 * A_W + c, plain(va_ref, c)),
                  (A_W + c, normed(ka_ref, c, 1)), (3 * A_W + c, normed(qb_ref, c, 2))]
    units.append((3 * A_W + B_QW, kv_b))
    n_slots = PROJ_SKEW + 1
    for idx in range(len(units) + PROJ_SKEW):
        if idx < len(units):
            c0 = units[idx][0]
            p_scr[idx % n_slots] = jnp.dot(h_scr[...], w_ref[:, c0:c0 + MXU_N], preferred_element_type=F32)
        if idx >= PROJ_SKEW:
            units[idx - PROJ_SKEW][1]((idx - PROJ_SKEW) % n_slots)


def _in_proj(x2, g_mix, w_qkv, ones_bd, qk_gain):
    n = x2.shape[0]
    tm = TOK_TILE
    tile = lambda w: pl.BlockSpec((tm, w), lambda i: (i, 0))
    out_w = (A_W, A_W, A_W, B_QW, 2 * LANES, 2 * LANES)
    return pl.pallas_call(
        _in_proj_kernel,
        grid=(n // tm,),
        in_specs=[tile(D_MODEL), _resident((1, D_MODEL)), _resident((D_MODEL, QKV_W)),
                  _resident((MXU_N, MXU_N)), _resident((4, MXU_N))],
        out_specs=[tile(w) for w in out_w],
        out_shape=[jax.ShapeDtypeStruct((n, w), BF16) for w in out_w],
        scratch_shapes=[pltpu.VMEM((tm, D_MODEL), BF16), pltpu.VMEM((PROJ_SKEW + 1, tm, MXU_N), F32)],
        compiler_params=_params(1),
        name="in_proj",
    )(x2, g_mix, w_qkv, ones_bd, qk_gain)


def _na_kernel(rpb_ref, dc_ref, q_ref, k_ref, v_ref, o_ref, bias_ref, s_ref, *, rows):
    t = pl.program_id(1)
    lo = lax.broadcasted_iota(jnp.int32, (GRID_W, LANES), 1) < HEAD_DIM
    nt = (((1,), (1,)), ((), ()))
    n_off = 2 * NA_KW - 1
    n_slab = 2 * NA_KH - 2

    @pl.when((pl.program_id(0) == 0) & (t == 0))
    def _build_bias():
        dc = dc_ref[...]
        lo_row = lax.broadcasted_iota(jnp.int32, (1, LANES), 1) < HEAD_DIM

        def build(idx, carry):
            h = idx // n_slab
            d = idx - h * n_slab
            acc = jnp.full((GRID_W, LANES), NEG_INF, F32)
            for o in range(n_off):
                left = rpb_ref[(d * n_off + o) * NA_HEADS + h]
                right = rpb_ref[((d + 1) * n_off + o) * NA_HEADS + h]
                acc = jnp.where(dc == o, jnp.where(lo_row, left, right), acc)
            row0 = pl.multiple_of((h % 2) * GRID_W, GRID_W)
            bias_ref[h // 2, d, pl.ds(row0, GRID_W), :] = acc
            return carry

        lax.fori_loop(0, NA_HEADS * n_slab, build, 0)

    row_info = []
    for i in range(NA_ROWS_PER_STEP):
        r = t * NA_ROWS_PER_STEP + i
        rs = jnp.clip(r - NA_KH // 2, 0, rows - NA_KH)
        row_info.append(((NA_KH - 1) - (r - rs), pl.multiple_of(rs * GRID_W, GRID_W)))

    def scores(i, p, slot):
        _, kstart = row_info[i]
        cols = slice(p * LANES, (p + 1) * LANES)
        q = q_ref[0, i * GRID_W:(i + 1) * GRID_W, cols]
        zero = jnp.zeros_like(q)
        q2 = jnp.concatenate([jnp.where(lo, q, zero), jnp.where(lo, zero, q)], axis=0)
        kk = k_ref[0, pl.ds(kstart, NA_KH * GRID_W), cols]
        s_ref[slot] = lax.dot_general(q2, kk, nt, preferred_element_type=F32)

    def finish(i, p, slot):
        d0, kstart = row_info[i]
        cols = slice(p * LANES, (p + 1) * LANES)
        bias = jnp.concatenate([bias_ref[p, d0 + 2 * v] for v in range(NA_KH // 2)], axis=1)
        s = s_ref[slot] + bias
        m = jnp.max(s, axis=-1, keepdims=True)
        e = jnp.exp(s - m)
        denom = jnp.sum(e, axis=-1, keepdims=True)
        vv = v_ref[0, pl.ds(kstart, NA_KH * GRID_W), cols]
        o2 = jnp.dot(e.astype(BF16), vv, preferred_element_type=F32) / denom
        o = jnp.where(lo, o2[:GRID_W], o2[GRID_W:])
        o_ref[0, i * GRID_W:(i + 1) * GRID_W, cols] = o.astype(BF16)

    units = [(i, p) for i in range(NA_ROWS_PER_STEP) for p in range(NA_HEADS // 2)]
    n_slots = NA_SKEW + 1
    for idx in range(len(units) + NA_SKEW):
        if idx < len(units):
            scores(*units[idx], idx % n_slots)
        if idx >= NA_SKEW:
            finish(*units[idx - NA_SKEW], (idx - NA_SKEW) % n_slots)


def _na_col_offsets():
    c = np.arange(GRID_W)
    kc = c[None, :]
    win_start = np.clip(c - NA_KW // 2, 0, GRID_W - NA_KW)[:, None]
    in_win = (kc >= win_start) & (kc < win_start + NA_KW)
    dc = np.where(in_win, kc - c[:, None] + (NA_KW - 1), -1).astype(np.int32)
    return np.concatenate([dc, dc], axis=1)


def _na_attn(rpb, qa, ka, va):
    b, t, _ = qa.shape
    rows = t // GRID_W
    tq = NA_ROWS_PER_STEP * GRID_W
    full = pl.BlockSpec((1, t, A_W), lambda bi, ti: (bi, 0, 0))
    tile = pl.BlockSpec((1, tq, A_W), lambda bi, ti: (bi, ti, 0))
    return pl.pallas_call(
        functools.partial(_na_kernel, rows=rows),
        grid=(b, rows // NA_ROWS_PER_STEP),
        in_specs=[pl.BlockSpec(memory_space=pltpu.SMEM), _resident((GRID_W, LANES)), tile, full, full],
        out_specs=tile,
        out_shape=jax.ShapeDtypeStruct((b, t, A_W), BF16),
        scratch_shapes=[pltpu.VMEM((NA_HEADS // 2, 2 * NA_KH - 2, LANES, LANES), F32),
                        pltpu.VMEM((NA_SKEW + 1, 2 * GRID_W, NA_KH * GRID_W), F32)],
        compiler_params=_params(2),
        name="na_attn",
    )(rpb.astype(F32).reshape(-1), jnp.asarray(_na_col_offsets()), qa, ka, va)


def _swa_kernel(sink_ref, t5_ref, bucket_ref, q_ref, kp_ref, kc_ref, kn_ref, vp_ref, vc_ref, vn_ref, o_ref,
                bias_ref, s_ref, *, n_tiles):
    n = pl.program_id(1)
    lo = lax.broadcasted_iota(jnp.int32, (SW_BLOCK, LANES), 1) < HEAD_DIM
    nt = (((1,), (1,)), ((), ()))

    @pl.when((pl.program_id(0) == 0) & (n == 0))
    def _build_bias():
        bucket = bucket_ref[...]

        def build(h, carry):
            acc = jnp.full(bucket.shape, NEG_INF, F32)
            for bkt in range(T5_BUCKETS):
                acc = jnp.where(bucket == bkt, t5_ref[bkt * SW_HEADS + h], acc)
            row0 = pl.multiple_of((h % SW_GROUP) * SW_BLOCK, SW_BLOCK)
            bias_ref[h // SW_GROUP, pl.ds(row0, SW_BLOCK), :] = acc
            return carry

        lax.fori_loop(0, SW_HEADS, build, 0)

    def window(prev_ref, own_ref, next_ref, kvh, j):
        kcols = slice(kvh * LANES, (kvh + 1) * LANES)
        blk = lambda b: own_ref[0, b * SW_BLOCK:(b + 1) * SW_BLOCK, kcols]
        first = prev_ref[0, :, kcols] if j == 0 else blk(j - 1)
        last = next_ref[0, :, kcols] if j == SW_BLOCKS_PER_STEP - 1 else blk(j + 1)
        return jnp.concatenate([first, blk(j), last], axis=0)

    def scores(kvh, j, slot):
        rows = slice(j * SW_BLOCK, (j + 1) * SW_BLOCK)
        parts = []
        for c in range(2):
            qb = q_ref[0, rows, (2 * kvh + c) * LANES:(2 * kvh + c + 1) * LANES]
            zero = jnp.zeros_like(qb)
            parts += [jnp.where(lo, qb, zero), jnp.where(lo, zero, qb)]
        q4 = jnp.concatenate(parts, axis=0)
        s_ref[slot] = lax.dot_general(q4, window(kp_ref, kc_ref, kn_ref, kvh, j), nt,
                                      preferred_element_type=F32)

    def finish(kvh, j, slot):
        rows = slice(j * SW_BLOCK, (j + 1) * SW_BLOCK)
        es, denoms = [], []
        for g in range(SW_GROUP):
            grows = slice(g * SW_BLOCK, (g + 1) * SW_BLOCK)
            sg = s_ref[slot, grows, :] + bias_ref[kvh, grows, :]
            if j == 0:
                sg = jnp.concatenate([jnp.where(n > 0, sg[:, :SW_BLOCK], NEG_INF), sg[:, SW_BLOCK:]], axis=1)
            if j == SW_BLOCKS_PER_STEP - 1:
                sg = jnp.concatenate([sg[:, :2 * SW_BLOCK],
                                      jnp.where(n < n_tiles - 1, sg[:, 2 * SW_BLOCK:], NEG_INF)], axis=1)
            sk = sink_ref[kvh * SW_GROUP + g]
            m = jnp.maximum(jnp.max(sg, axis=-1, keepdims=True), sk)
            e = jnp.exp(sg - m)
            denoms.append(jnp.sum(e, axis=-1, keepdims=True) + jnp.exp(sk - m))
            es.append(e.astype(BF16))
        o4 = jnp.dot(jnp.concatenate(es, axis=0), window(vp_ref, vc_ref, vn_ref, kvh, j),
                     preferred_element_type=F32)
        og = [o4[g * SW_BLOCK:(g + 1) * SW_BLOCK] / denoms[g] for g in range(SW_GROUP)]
        for c in range(2):
            o_ref[0, rows, (2 * kvh + c) * LANES:(2 * kvh + c + 1) * LANES] = (
                jnp.where(lo, og[2 * c], og[2 * c + 1]).astype(BF16))

    units = [(kvh, j) for j in range(SW_BLOCKS_PER_STEP) for kvh in range(SW_KV_HEADS)]
    n_slots = SW_SKEW + 1
    for idx in range(len(units) + SW_SKEW):
        if idx < len(units):
            scores(*units[idx], idx % n_slots)
        if idx >= SW_SKEW:
            finish(*units[idx - SW_SKEW], (idx - SW_SKEW) % n_slots)


def _t5_bucket(rel):
    half = T5_BUCKETS // 2
    max_exact = half // 2
    ret = (rel > 0).astype(np.int32) * half
    n = np.abs(rel)
    large = max_exact + (np.log(np.maximum(n, 1) / max_exact)
                         / np.log(T5_MAX_DIST / max_exact) * (half - max_exact)).astype(np.int32)
    large = np.minimum(large, half - 1)
    return ret + np.where(n < max_exact, n, large)


def _swa_bucket_index():
    rel = np.arange(3 * SW_BLOCK)[None, :] - SW_BLOCK - np.arange(SW_BLOCK)[:, None]
    return np.where(np.abs(rel) <= SW_WINDOW, _t5_bucket(rel), -1).astype(np.int32)


def _swa_attn(sink, t5_table, qb, kbd, vbd):
    b, t, _ = qb.shape
    tq = SW_BLOCKS_PER_STEP * SW_BLOCK
    n_tiles = t // tq
    last_blk = t // SW_BLOCK - 1
    kv_w = kbd.shape[-1]
    qtile = pl.BlockSpec((1, tq, B_QW), lambda bi, ni: (bi, ni, 0))
    own = pl.BlockSpec((1, tq, kv_w), lambda bi, ni: (bi, ni, 0))
    prev = pl.BlockSpec((1, SW_BLOCK, kv_w),
                        lambda bi, ni: (bi, jnp.maximum(ni * SW_BLOCKS_PER_STEP - 1, 0), 0))
    nxt = pl.BlockSpec((1, SW_BLOCK, kv_w),
                       lambda bi, ni: (bi, jnp.minimum((ni + 1) * SW_BLOCKS_PER_STEP, last_blk), 0))
    smem = pl.BlockSpec(memory_space=pltpu.SMEM)
    return pl.pallas_call(
        functools.partial(_swa_kernel, n_tiles=n_tiles),
        grid=(b, n_tiles),
        in_specs=[smem, smem, _resident((SW_BLOCK, 3 * SW_BLOCK)), qtile, prev, own, nxt, prev, own, nxt],
        out_specs=qtile,
        out_shape=jax.ShapeDtypeStruct((b, t, B_QW), BF16),
        scratch_shapes=[pltpu.VMEM((SW_KV_HEADS, SW_GROUP * SW_BLOCK, 3 * SW_BLOCK), F32),
                        pltpu.VMEM((SW_SKEW + 1, SW_GROUP * SW_BLOCK, 3 * SW_BLOCK), F32)],
        compiler_params=_params(2),
        name="swa_attn",
    )(sink.astype(F32), t5_table.astype(F32).reshape(-1), jnp.asarray(_swa_bucket_index()),
      qb, kbd, kbd, kbd, vbd, vbd, vbd)


def _mix_kernel(x_ref, oa_ref, ob_ref, g_ref, wg_ref, wa_ref, wb_ref, wo_ref, x1_ref, h_scr, y_scr):
    xf = x_ref[...]
    h_scr[...] = ((xf * _rms_scale(xf)) * g_ref[...]).astype(BF16)
    for c in range(0, D_MODEL, MXU_N):
        cs = slice(c, c + MXU_N)
        ga = jnp.dot(h_scr[...], wg_ref[:, cs], preferred_element_type=F32)
        gb = jnp.dot(h_scr[...], wg_ref[:, D_MODEL + c:D_MODEL + c + MXU_N], preferred_element_type=F32)
        ya = jnp.dot(oa_ref[...], wa_ref[:, cs], preferred_element_type=F32)
        yb = jnp.dot(ob_ref[...], wb_ref[:, cs], preferred_element_type=F32)
        y_scr[:, cs] = (jax.nn.sigmoid(ga) * ya + jax.nn.sigmoid(gb) * yb).astype(BF16)
    for c in range(0, D_MODEL, MXU_N):
        cs = slice(c, c + MXU_N)
        x1_ref[:, cs] = x_ref[:, cs] + jnp.dot(y_scr[...], wo_ref[:, cs], preferred_element_type=F32)


def _mix_out(x2, oa, ob, g_mix, w_gates, w_a, w_b, w_o):
    n = x2.shape[0]
    tm = TOK_TILE
    tile = lambda w: pl.BlockSpec((tm, w), lambda i: (i, 0))
    return pl.pallas_call(
        _mix_kernel,
        grid=(n // tm,),
        in_specs=[tile(D_MODEL), tile(A_W), tile(B_QW), _resident((1, D_MODEL)),
                  _resident(w_gates.shape), _resident(w_a.shape), _resident(w_b.shape), _resident(w_o.shape)],
        out_specs=tile(D_MODEL),
        out_shape=jax.ShapeDtypeStruct((n, D_MODEL), F32),
        scratch_shapes=[pltpu.VMEM((tm, D_MODEL), BF16), pltpu.VMEM((tm, D_MODEL), BF16)],
        compiler_params=_params(1),
        name="mix_out",
    )(x2, oa, ob, g_mix, w_gates, w_a, w_b, w_o)


def _ffn_kernel(x_ref, g_ref, wg_ref, wu_ref, wd_ref, o_ref, h_scr, a_scr):
    xf = x_ref[...]
    h_scr[...] = ((xf * _rms_scale(xf)) * g_ref[...]).astype(BF16)
    hidden = wg_ref.shape[1]
    for c in range(0, hidden, MXU_N):
        cs = slice(c, c + MXU_N)
        gate = jnp.dot(h_scr[...], wg_ref[:, cs], preferred_element_type=F32)
        up = jnp.dot(h_scr[...], wu_ref[:, cs], preferred_element_type=F32)
        a_scr[:, cs] = (jax.nn.silu(gate) * up).astype(BF16)
    for c in range(0, D_MODEL, MXU_N):
        cs = slice(c, c + MXU_N)
        o_ref[:, cs] = x_ref[:, cs] + jnp.dot(a_scr[...], wd_ref[:, cs], preferred_element_type=F32)


def _ffn(x1, g_ffn, w_gate, w_up, w_down):
    n = x1.shape[0]
    tm = TOK_TILE
    hidden = w_gate.shape[1]
    tile = pl.BlockSpec((tm, D_MODEL), lambda i: (i, 0))
    return pl.pallas_call(
        _ffn_kernel,
        grid=(n // tm,),
        in_specs=[tile, _resident((1, D_MODEL)), _resident(w_gate.shape), _resident(w_up.shape),
                  _resident(w_down.shape)],
        out_specs=tile,
        out_shape=jax.ShapeDtypeStruct((n, D_MODEL), F32),
        scratch_shapes=[pltpu.VMEM((tm, D_MODEL), BF16), pltpu.VMEM((tm, hidden), BF16)],
        compiler_params=_params(1),
        name="ffn",
    )(x1, g_ffn, w_gate, w_up, w_down)


def _layer(x2, b, t, norm_mix, w_in, q_norm_a, k_norm_a, rpb_a, q_norm_b, k_norm_b, sink_b, t5_table,
           w_branch_a, w_branch_b, w_out, norm_ffn, w_gate, w_up, w_down):
    g_mix = norm_mix.reshape(1, D_MODEL)
    w_qkv = w_in[:, :QKV_W].astype(BF16)
    w_gates = w_in[:, QKV_W:].astype(BF16)
    ones_bd = jnp.asarray(np.kron(np.eye(MXU_N // HEAD_DIM), np.ones((HEAD_DIM, HEAD_DIM))), BF16)
    reps = MXU_N // HEAD_DIM
    qk_gain = jnp.stack([jnp.tile(q_norm_a * QK_SCALE, reps), jnp.tile(k_norm_a, reps),
                         jnp.tile(q_norm_b * QK_SCALE, reps), jnp.tile(k_norm_b, reps)]).astype(F32)

    qa, ka, va, qb, kbd, vbd = _in_proj(x2, g_mix, w_qkv, ones_bd, qk_gain)
    n = x2.shape[0]
    r3 = lambda a: a.reshape(b, t, a.shape[-1])
    oa = _na_attn(rpb_a, r3(qa), r3(ka), r3(va))
    ob = _swa_attn(sink_b, t5_table, r3(qb), r3(kbd), r3(vbd))
    x1 = _mix_out(x2, oa.reshape(n, A_W), ob.reshape(n, B_QW), g_mix, w_gates,
                  w_branch_a.astype(BF16), w_branch_b.astype(BF16), w_out.astype(BF16))
    return _ffn(x1, norm_ffn.reshape(1, D_MODEL), w_gate.astype(BF16), w_up.astype(BF16),
                w_down.astype(BF16))


def kernel(x, norm_mix, w_in, q_norm_a, k_norm_a, rpb_a, q_norm_b, k_norm_b, sink_b, t5_table,
           w_branch_a, w_branch_b, w_out, norm_ffn, w_gate, w_up, w_down):
    b, t, d = x.shape
    x2 = x.reshape(b * t, d)
    depth = norm_mix.shape[0]
    for l in range(depth):
        x2 = _layer(x2, b, t, norm_mix[l], w_in[l], q_norm_a[l], k_norm_a[l], rpb_a[l], q_norm_b[l],
                    k_norm_b[l], sink_b[l], t5_table, w_branch_a[l], w_branch_b[l], w_out[l],
                    norm_ffn[l], w_gate[l], w_up[l], w_down[l])
    return x2.reshape(b, t, d)
```

```python
import functools

import jax
import jax.numpy as jnp
import numpy as np
from jax import lax
from jax.experimental import pallas as pl
from jax.experimental.pallas import tpu as pltpu

F32 = jnp.float32
BF16 = jnp.bfloat16

D_MODEL = 1024
HEAD_DIM = 64
GRID_W = 64
NA_HEADS = 8
NA_KH = 8
NA_KW = 16
SW_HEADS = 8
SW_KV_HEADS = 2
SW_GROUP = SW_HEADS // SW_KV_HEADS
SW_WINDOW = 128
SW_BLOCK = 128
T5_BUCKETS = 32
T5_MAX_DIST = 128
A_W = NA_HEADS * HEAD_DIM
B_QW = SW_HEADS * HEAD_DIM
B_KVW = SW_KV_HEADS * HEAD_DIM
QKV_W = 3 * A_W + B_QW + 2 * B_KVW
RMS_EPS = 1e-6
NEG_INF = -1e30
QK_SCALE = HEAD_DIM ** -0.5
LOG2E = 1.4426950408889634

LANES = 128
MXU_N = 256
VMEM_LIMIT = 56 * 1024 * 1024

TOK_TILE = 512
NA_ROWS_PER_STEP = 8
PROJ_SKEW = 2
NA_SKEW = 4
SW_SKEW = 4
SW_BLOCKS_PER_STEP = 8


def _resident(shape):
    return pl.BlockSpec(shape, lambda *_: (0,) * len(shape), pipeline_mode=pl.Buffered(1))


def _params(n_axes):
    return pltpu.CompilerParams(dimension_semantics=("arbitrary",) * n_axes, vmem_limit_bytes=VMEM_LIMIT)


def _rms_scale(xf):
    return lax.rsqrt(jnp.mean(xf * xf, axis=-1, keepdims=True) + RMS_EPS)


def _group_rms_scale(p):
    y = p * p
    lo = lax.broadcasted_iota(jnp.int32, (p.shape[0], LANES), 1) < HEAD_DIM
    scales = []
    for c in range(0, p.shape[1], LANES):
        yc = y[:, c:c + LANES]
        s_lo = jnp.sum(jnp.where(lo, yc, 0.0), axis=-1, keepdims=True)
        s_hi = jnp.sum(jnp.where(lo, 0.0, yc), axis=-1, keepdims=True)
        r_lo = lax.rsqrt(s_lo + HEAD_DIM * RMS_EPS)
        r_hi = lax.rsqrt(s_hi + HEAD_DIM * RMS_EPS)
        scales.append(jnp.where(lo, r_lo, r_hi))
    return jnp.concatenate(scales, axis=1)


def _in_proj_kernel(x_ref, g_ref, w_ref, qk_gain_ref,
                    qa_ref, ka_ref, va_ref, qb_ref, kbd_ref, vbd_ref, h_scr, p_scr):
    xf = x_ref[...]
    h_scr[...] = ((xf * _rms_scale(xf)) * g_ref[...]).astype(BF16)

    def normed(out_ref, c, gain_row):
        def epilogue(slot):
            p = p_scr[slot]
            scale = _group_rms_scale(p)
            out_ref[:, c:c + MXU_N] = (p * scale * qk_gain_ref[gain_row:gain_row + 1, :]).astype(BF16)
        return epilogue

    def plain(out_ref, c):
        def epilogue(slot):
            out_ref[:, c:c + MXU_N] = p_scr[slot].astype(BF16)
        return epilogue

    def kv_b(slot):
        kb = p_scr[slot, :, :B_KVW]
        kb = kb * _group_rms_scale(kb) * qk_gain_ref[3:4, :LANES]
        vb = p_scr[slot, :, B_KVW:]
        lo = lax.broadcasted_iota(jnp.int32, kb.shape, 1) < HEAD_DIM
        for src, out_ref in ((kb, kbd_ref), (vb, vbd_ref)):
            swapped = pltpu.roll(src, HEAD_DIM, axis=1)
            out_ref[:, :LANES] = jnp.where(lo, src, swapped).astype(BF16)
            out_ref[:, LANES:] = jnp.where(lo, swapped, src).astype(BF16)

    units = []
    for c in range(0, A_W, MXU_N):
        units += [(c, normed(qa_ref, c, 0)), (A_W + c, normed(ka_ref, c, 1)), (3 * A_W + c, normed(qb_ref, c, 2))]
    units.append((3 * A_W + B_QW, kv_b))
    units += [(2 * A_W + c, plain(va_ref, c)) for c in range(0, A_W, MXU_N)]
    n_slots = PROJ_SKEW + 1
    for idx in range(len(units) + PROJ_SKEW):
        if idx < len(units):
            c0 = units[idx][0]
            p_scr[idx % n_slots] = jnp.dot(h_scr[...], w_ref[:, c0:c0 + MXU_N], preferred_element_type=F32)
        if idx >= PROJ_SKEW:
            units[idx - PROJ_SKEW][1]((idx - PROJ_SKEW) % n_slots)


def _in_proj(x2, g_mix, w_qkv, qk_gain):
    n = x2.shape[0]
    tm = TOK_TILE
    tile = lambda w: pl.BlockSpec((tm, w), lambda i: (i, 0))
    out_w = (A_W, A_W, A_W, B_QW, 2 * LANES, 2 * LANES)
    return pl.pallas_call(
        _in_proj_kernel,
        grid=(n // tm,),
        in_specs=[tile(D_MODEL), _resident((1, D_MODEL)), _resident((D_MODEL, QKV_W)),
                  _resident((4, MXU_N))],
        out_specs=[tile(w) for w in out_w],
        out_shape=[jax.ShapeDtypeStruct((n, w), BF16) for w in out_w],
        scratch_shapes=[pltpu.VMEM((tm, D_MODEL), BF16), pltpu.VMEM((PROJ_SKEW + 1, tm, MXU_N), F32)],
        compiler_params=_params(1),
        name="in_proj",
    )(x2, g_mix, w_qkv, qk_gain)


def _na_kernel(rpb_ref, dc_ref, q_ref, k_ref, v_ref, o_ref, bias_ref, s_ref, *, rows):
    t = pl.program_id(1)
    lo = lax.broadcasted_iota(jnp.int32, (GRID_W, LANES), 1) < HEAD_DIM
    nt = (((1,), (1,)), ((), ()))
    n_off = 2 * NA_KW - 1
    n_slab = 2 * NA_KH - 2

    @pl.when((pl.program_id(0) == 0) & (t == 0))
    def _build_bias():
        dc = dc_ref[...]
        lo_row = lax.broadcasted_iota(jnp.int32, (1, LANES), 1) < HEAD_DIM

        def build(idx, carry):
            h = idx // n_slab
            d = idx - h * n_slab
            acc = jnp.full((GRID_W, LANES), NEG_INF, F32)
            for o in range(n_off):
                left = rpb_ref[(d * n_off + o) * NA_HEADS + h] * LOG2E
                right = rpb_ref[((d + 1) * n_off + o) * NA_HEADS + h] * LOG2E
                acc = jnp.where(dc == o, jnp.where(lo_row, left, right), acc)
            row0 = pl.multiple_of((h % 2) * GRID_W, GRID_W)
            bias_ref[h // 2, d, pl.ds(row0, GRID_W), :] = acc
            return carry

        lax.fori_loop(0, NA_HEADS * n_slab, build, 0)

    row_info = []
    for i in range(NA_ROWS_PER_STEP):
        r = t * NA_ROWS_PER_STEP + i
        rs = jnp.clip(r - NA_KH // 2, 0, rows - NA_KH)
        row_info.append(((NA_KH - 1) - (r - rs), pl.multiple_of(rs * GRID_W, GRID_W)))

    def scores(i, p, slot):
        d0, kstart = row_info[i]
        cols = slice(p * LANES, (p + 1) * LANES)
        q = q_ref[0, i * GRID_W:(i + 1) * GRID_W, cols]
        zero = jnp.zeros_like(q)
        q2 = jnp.concatenate([jnp.where(lo, q, zero), jnp.where(lo, zero, q)], axis=0)
        kk = k_ref[0, pl.ds(kstart, NA_KH * GRID_W), cols]
        s = lax.dot_general(q2, kk, nt, preferred_element_type=F32)
        for v in range(NA_KH // 2):
            vcols = slice(v * LANES, (v + 1) * LANES)
            s_ref[slot, :, vcols] = s[:, vcols] + bias_ref[p, d0 + 2 * v]

    ones_cols = jnp.ones((NA_KH * GRID_W, LANES), BF16)

    def finish(i, p, slot):
        _, kstart = row_info[i]
        cols = slice(p * LANES, (p + 1) * LANES)
        m = jnp.max(s_ref[slot], axis=-1, keepdims=True)
        e = jnp.exp2(s_ref[slot] - m).astype(BF16)
        vv = jnp.concatenate([v_ref[0, pl.ds(kstart, NA_KH * GRID_W), cols], ones_cols], axis=1)
        o2 = jnp.dot(e, vv, preferred_element_type=F32)
        o2 = o2[:, :LANES] / o2[:, LANES:]
        o = jnp.where(lo, o2[:GRID_W], o2[GRID_W:])
        o_ref[0, i * GRID_W:(i + 1) * GRID_W, cols] = o.astype(BF16)

    units = [(i, p) for i in range(NA_ROWS_PER_STEP) for p in range(NA_HEADS // 2)]
    n_slots = NA_SKEW + 1
    for idx in range(len(units) + NA_SKEW):
        if idx < len(units):
            scores(*units[idx], idx % n_slots)
        if idx >= NA_SKEW:
            finish(*units[idx - NA_SKEW], (idx - NA_SKEW) % n_slots)


def _na_col_offsets():
    c = np.arange(GRID_W)
    kc = c[None, :]
    win_start = np.clip(c - NA_KW // 2, 0, GRID_W - NA_KW)[:, None]
    in_win = (kc >= win_start) & (kc < win_start + NA_KW)
    dc = np.where(in_win, kc - c[:, None] + (NA_KW - 1), -1).astype(np.int32)
    return np.concatenate([dc, dc], axis=1)


def _na_attn(rpb, qa, ka, va):
    b, t, _ = qa.shape
    rows = t // GRID_W
    tq = NA_ROWS_PER_STEP * GRID_W
    full = pl.BlockSpec((1, t, A_W), lambda bi, ti: (bi, 0, 0))
    tile = pl.BlockSpec((1, tq, A_W), lambda bi, ti: (bi, ti, 0))
    return pl.pallas_call(
        functools.partial(_na_kernel, rows=rows),
        grid=(b, rows // NA_ROWS_PER_STEP),
        in_specs=[pl.BlockSpec(memory_space=pltpu.SMEM), _resident((GRID_W, LANES)), tile, full, full],
        out_specs=tile,
        out_shape=jax.ShapeDtypeStruct((b, t, A_W), BF16),
        scratch_shapes=[pltpu.VMEM((NA_HEADS // 2, 2 * NA_KH - 2, LANES, LANES), F32),
                        pltpu.VMEM((NA_SKEW + 1, 2 * GRID_W, NA_KH * GRID_W), F32)],
        compiler_params=_params(2),
        name="na_attn",
    )(rpb.astype(F32).reshape(-1), jnp.asarray(_na_col_offsets()), qa, ka, va)


def _swa_kernel(sink_ref, t5_ref, bucket_ref, q_ref, kp_ref, kc_ref, kn_ref, vp_ref, vc_ref, vn_ref, o_ref,
                bias_ref, s_ref, *, n_tiles):
    n = pl.program_id(1)
    lo = lax.broadcasted_iota(jnp.int32, (SW_BLOCK, LANES), 1) < HEAD_DIM
    nt = (((1,), (1,)), ((), ()))

    @pl.when((pl.program_id(0) == 0) & (n == 0))
    def _build_bias():
        bucket = bucket_ref[...]

        def build(h, carry):
            acc = jnp.full(bucket.shape, NEG_INF, F32)
            for bkt in range(T5_BUCKETS):
                acc = jnp.where(bucket == bkt, t5_ref[bkt * SW_HEADS + h] * LOG2E, acc)
            row0 = pl.multiple_of((h % SW_GROUP) * SW_BLOCK, SW_BLOCK)
            bias_ref[h // SW_GROUP, pl.ds(row0, SW_BLOCK), :] = acc
            return carry

        lax.fori_loop(0, SW_HEADS, build, 0)

    def window(prev_ref, own_ref, next_ref, kvh, j):
        kcols = slice(kvh * LANES, (kvh + 1) * LANES)
        blk = lambda b: own_ref[0, b * SW_BLOCK:(b + 1) * SW_BLOCK, kcols]
        first = prev_ref[0, :, kcols] if j == 0 else blk(j - 1)
        last = next_ref[0, :, kcols] if j == SW_BLOCKS_PER_STEP - 1 else blk(j + 1)
        return jnp.concatenate([first, blk(j), last], axis=0)

    def scores(kvh, j, c, slot):
        rows = slice(j * SW_BLOCK, (j + 1) * SW_BLOCK)
        qb = q_ref[0, rows, (2 * kvh + c) * LANES:(2 * kvh + c + 1) * LANES]
        zero = jnp.zeros_like(qb)
        q2 = jnp.concatenate([jnp.where(lo, qb, zero), jnp.where(lo, zero, qb)], axis=0)
        s_ref[slot] = lax.dot_general(q2, window(kp_ref, kc_ref, kn_ref, kvh, j), nt,
                                      preferred_element_type=F32)

    ones_cols = jnp.ones((3 * SW_BLOCK, LANES), BF16)

    def finish(kvh, j, c, slot):
        rows = slice(j * SW_BLOCK, (j + 1) * SW_BLOCK)
        es, sink_terms = [], []
        for hh in range(2):
            hrows = slice(hh * SW_BLOCK, (hh + 1) * SW_BLOCK)
            brows = slice((2 * c + hh) * SW_BLOCK, (2 * c + hh + 1) * SW_BLOCK)
            sg = s_ref[slot, hrows, :] + bias_ref[kvh, brows, :]
            if j == 0:
                sg = jnp.concatenate([jnp.where(n > 0, sg[:, :SW_BLOCK], NEG_INF), sg[:, SW_BLOCK:]], axis=1)
            if j == SW_BLOCKS_PER_STEP - 1:
                sg = jnp.concatenate([sg[:, :2 * SW_BLOCK],
                                      jnp.where(n < n_tiles - 1, sg[:, 2 * SW_BLOCK:], NEG_INF)], axis=1)
            sk = sink_ref[kvh * SW_GROUP + 2 * c + hh] * LOG2E
            m = jnp.maximum(jnp.max(sg, axis=-1, keepdims=True), sk)
            es.append(jnp.exp2(sg - m).astype(BF16))
            sink_terms.append(jnp.exp2(sk - m))
        vv = jnp.concatenate([window(vp_ref, vc_ref, vn_ref, kvh, j), ones_cols], axis=1)
        o2 = jnp.dot(jnp.concatenate(es, axis=0), vv, preferred_element_type=F32)
        og = []
        for hh in range(2):
            hrows = slice(hh * SW_BLOCK, (hh + 1) * SW_BLOCK)
            og.append(o2[hrows, :LANES] / (o2[hrows, LANES:] + sink_terms[hh]))
        o_ref[0, rows, (2 * kvh + c) * LANES:(2 * kvh + c + 1) * LANES] = (
            jnp.where(lo, og[0], og[1]).astype(BF16))

    units = [(kvh, j, c) for j in range(SW_BLOCKS_PER_STEP) for kvh in range(SW_KV_HEADS) for c in range(2)]
    n_slots = SW_SKEW + 1
    for idx in range(len(units) + SW_SKEW):
        if idx < len(units):
            scores(*units[idx], idx % n_slots)
        if idx >= SW_SKEW:
            finish(*units[idx - SW_SKEW], (idx - SW_SKEW) % n_slots)


def _t5_bucket(rel):
    half = T5_BUCKETS // 2
    max_exact = half // 2
    ret = (rel > 0).astype(np.int32) * half
    n = np.abs(rel)
    large = max_exact + (np.log(np.maximum(n, 1) / max_exact)
                         / np.log(T5_MAX_DIST / max_exact) * (half - max_exact)).astype(np.int32)
    large = np.minimum(large, half - 1)
    return ret + np.where(n < max_exact, n, large)


def _swa_bucket_index():
    rel = np.arange(3 * SW_BLOCK)[None, :] - SW_BLOCK - np.arange(SW_BLOCK)[:, None]
    return np.where(np.abs(rel) <= SW_WINDOW, _t5_bucket(rel), -1).astype(np.int32)


def _swa_attn(sink, t5_table, qb, kbd, vbd):
    b, t, _ = qb.shape
    tq = SW_BLOCKS_PER_STEP * SW_BLOCK
    n_tiles = t // tq
    last_blk = t // SW_BLOCK - 1
    kv_w = kbd.shape[-1]
    qtile = pl.BlockSpec((1, tq, B_QW), lambda bi, ni: (bi, ni, 0))
    own = pl.BlockSpec((1, tq, kv_w), lambda bi, ni: (bi, ni, 0))
    prev = pl.BlockSpec((1, SW_BLOCK, kv_w),
                        lambda bi, ni: (bi, jnp.maximum(ni * SW_BLOCKS_PER_STEP - 1, 0), 0))
    nxt = pl.BlockSpec((1, SW_BLOCK, kv_w),
                       lambda bi, ni: (bi, jnp.minimum((ni + 1) * SW_BLOCKS_PER_STEP, last_blk), 0))
    smem = pl.BlockSpec(memory_space=pltpu.SMEM)
    return pl.pallas_call(
        functools.partial(_swa_kernel, n_tiles=n_tiles),
        grid=(b, n_tiles),
        in_specs=[smem, smem, _resident((SW_BLOCK, 3 * SW_BLOCK)), qtile, prev, own, nxt, prev, own, nxt],
        out_specs=qtile,
        out_shape=jax.ShapeDtypeStruct((b, t, B_QW), BF16),
        scratch_shapes=[pltpu.VMEM((SW_KV_HEADS, SW_GROUP * SW_BLOCK, 3 * SW_BLOCK), F32),
                        pltpu.VMEM((SW_SKEW + 1, 2 * SW_BLOCK, 3 * SW_BLOCK), F32)],
        compiler_params=_params(2),
        name="swa_attn",
    )(sink.astype(F32), t5_table.astype(F32).reshape(-1), jnp.asarray(_swa_bucket_index()),
      qb, kbd, kbd, kbd, vbd, vbd, vbd)


def _mix_kernel(x_ref, oa_ref, ob_ref, g_ref, wg_ref, wa_ref, wb_ref, wo_ref, x1_ref, h_scr, y_scr):
    xf = x_ref[...]
    h_scr[...] = ((xf * _rms_scale(xf)) * g_ref[...]).astype(BF16)
    for c in range(0, D_MODEL, MXU_N):
        cs = slice(c, c + MXU_N)
        ga = jnp.dot(h_scr[...], wg_ref[:, cs], preferred_element_type=F32)
        gb = jnp.dot(h_scr[...], wg_ref[:, D_MODEL + c:D_MODEL + c + MXU_N], preferred_element_type=F32)
        ya = jnp.dot(oa_ref[...], wa_ref[:, cs], preferred_element_type=F32)
        yb = jnp.dot(ob_ref[...], wb_ref[:, cs], preferred_element_type=F32)
        y_scr[:, cs] = (jax.nn.sigmoid(ga) * ya + jax.nn.sigmoid(gb) * yb).astype(BF16)
    for c in range(0, D_MODEL, MXU_N):
        cs = slice(c, c + MXU_N)
        x1_ref[:, cs] = x_ref[:, cs] + jnp.dot(y_scr[...], wo_ref[:, cs], preferred_element_type=F32)


def _mix_out(x2, oa, ob, g_mix, w_gates, w_a, w_b, w_o):
    n = x2.shape[0]
    tm = TOK_TILE
    tile = lambda w: pl.BlockSpec((tm, w), lambda i: (i, 0))
    return pl.pallas_call(
        _mix_kernel,
        grid=(n // tm,),
        in_specs=[tile(D_MODEL), tile(A_W), tile(B_QW), _resident((1, D_MODEL)),
                  _resident(w_gates.shape), _resident(w_a.shape), _resident(w_b.shape), _resident(w_o.shape)],
        out_specs=tile(D_MODEL),
        out_shape=jax.ShapeDtypeStruct((n, D_MODEL), F32),
        scratch_shapes=[pltpu.VMEM((tm, D_MODEL), BF16), pltpu.VMEM((tm, D_MODEL), BF16)],
        compiler_params=_params(1),
        name="mix_out",
    )(x2, oa, ob, g_mix, w_gates, w_a, w_b, w_o)


def _ffn_kernel(x_ref, g_ref, wg_ref, wu_ref, wd_ref, o_ref, h_scr, a_scr):
    xf = x_ref[...]
    h_scr[...] = ((xf * _rms_scale(xf)) * g_ref[...]).astype(BF16)
    hidden = wg_ref.shape[1]
    for c in range(0, hidden, MXU_N):
        cs = slice(c, c + MXU_N)
        gate = jnp.dot(h_scr[...], wg_ref[:, cs], preferred_element_type=F32)
        up = jnp.dot(h_scr[...], wu_ref[:, cs], preferred_element_type=F32)
        a_scr[:, cs] = (jax.nn.silu(gate) * up).astype(BF16)
    for c in range(0, D_MODEL, MXU_N):
        cs = slice(c, c + MXU_N)
        o_ref[:, cs] = x_ref[:, cs] + jnp.dot(a_scr[...], wd_ref[:, cs], preferred_element_type=F32)


def _ffn(x1, g_ffn, w_gate, w_up, w_down):
    n = x1.shape[0]
    tm = TOK_TILE
    hidden = w_gate.shape[1]
    tile = pl.BlockSpec((tm, D_MODEL), lambda i: (i, 0))
    return pl.pallas_call(
        _ffn_kernel,
        grid=(n // tm,),
        in_specs=[tile, _resident((1, D_MODEL)), _resident(w_gate.shape), _resident(w_up.shape),
                  _resident(w_down.shape)],
        out_specs=tile,
        out_shape=jax.ShapeDtypeStruct((n, D_MODEL), F32),
        scratch_shapes=[pltpu.VMEM((tm, D_MODEL), BF16), pltpu.VMEM((tm, hidden), BF16)],
        compiler_params=_params(1),
        name="ffn",
    )(x1, g_ffn, w_gate, w_up, w_down)


def _layer(x2, b, t, norm_mix, w_in, q_norm_a, k_norm_a, rpb_a, q_norm_b, k_norm_b, sink_b, t5_table,
           w_branch_a, w_branch_b, w_out, norm_ffn, w_gate, w_up, w_down):
    g_mix = norm_mix.reshape(1, D_MODEL)
    w_qkv = w_in[:, :QKV_W].astype(BF16)
    w_gates = w_in[:, QKV_W:].astype(BF16)
    reps = MXU_N // HEAD_DIM
    root_d = HEAD_DIM ** 0.5
    q_fold = QK_SCALE * root_d * LOG2E
    qk_gain = jnp.stack([jnp.tile(q_norm_a * q_fold, reps), jnp.tile(k_norm_a * root_d, reps),
                         jnp.tile(q_norm_b * q_fold, reps), jnp.tile(k_norm_b * root_d, reps)]).astype(F32)

    qa, ka, va, qb, kbd, vbd = _in_proj(x2, g_mix, w_qkv, qk_gain)
    n = x2.shape[0]
    r3 = lambda a: a.reshape(b, t, a.shape[-1])
    oa = _na_attn(rpb_a, r3(qa), r3(ka), r3(va))
    ob = _swa_attn(sink_b, t5_table, r3(qb), r3(kbd), r3(vbd))
    x1 = _mix_out(x2, oa.reshape(n, A_W), ob.reshape(n, B_QW), g_mix, w_gates,
                  w_branch_a.astype(BF16), w_branch_b.astype(BF16), w_out.astype(BF16))
    return _ffn(x1, norm_ffn.reshape(1, D_MODEL), w_gate.astype(BF16), w_up.astype(BF16),
                w_down.astype(BF16))


def kernel(x, norm_mix, w_in, q_norm_a, k_norm_a, rpb_a, q_norm_b, k_norm_b, sink_b, t5_table,
           w_branch_a, w_branch_b, w_out, norm_ffn, w_gate, w_up, w_down):
    b, t, d = x.shape
    x2 = x.reshape(b * t, d)
    depth = norm_mix.shape[0]
    for l in range(depth):
        x2 = _layer(x2, b, t, norm_mix[l], w_in[l], q_norm_a[l], k_norm_a[l], rpb_a[l], q_norm_b[l],
                    k_norm_b[l], sink_b[l], t5_table, w_branch_a[l], w_branch_b[l], w_out[l],
                    norm_ffn[l], w_gate[l], w_up[l], w_down[l])
    return x2.reshape(b, t, d)
```

```python
import functools

import jax
import jax.numpy as jnp
import numpy as np
from jax import lax
from jax.experimental import pallas as pl
from jax.experimental.pallas import tpu as pltpu

F32 = jnp.float32
BF16 = jnp.bfloat16

D_MODEL = 1024
HEAD_DIM = 64
GRID_W = 64
NA_HEADS = 8
NA_KH = 8
NA_KW = 16
SW_HEADS = 8
SW_KV_HEADS = 2
SW_GROUP = SW_HEADS // SW_KV_HEADS
SW_WINDOW = 128
SW_BLOCK = 128
T5_BUCKETS = 32
T5_MAX_DIST = 128
A_W = NA_HEADS * HEAD_DIM
B_QW = SW_HEADS * HEAD_DIM
B_KVW = SW_KV_HEADS * HEAD_DIM
QKV_W = 3 * A_W + B_QW + 2 * B_KVW
RMS_EPS = 1e-6
NEG_INF = -1e30
QK_SCALE = HEAD_DIM ** -0.5
LOG2E = 1.4426950408889634

LANES = 128
MXU_N = 256
VMEM_LIMIT = 56 * 1024 * 1024

TOK_TILE = 1024
ROW_BLOCK = 512
NA_ROWS_PER_STEP = 8
PROJ_SKEW = 2
NA_SKEW = 4
SW_SKEW = 4
SW_BLOCKS_PER_STEP = 8


def _resident(shape):
    return pl.BlockSpec(shape, lambda *_: (0,) * len(shape), pipeline_mode=pl.Buffered(1))


def _params(n_axes):
    return pltpu.CompilerParams(dimension_semantics=("arbitrary",) * n_axes, vmem_limit_bytes=VMEM_LIMIT)


def _rms_scale(xf):
    return lax.rsqrt(jnp.mean(xf * xf, axis=-1, keepdims=True) + RMS_EPS)


def _group_rms_scale(p):
    y = p * p
    lo = lax.broadcasted_iota(jnp.int32, (p.shape[0], LANES), 1) < HEAD_DIM
    scales = []
    for c in range(0, p.shape[1], LANES):
        yc = y[:, c:c + LANES]
        s_lo = jnp.sum(jnp.where(lo, yc, 0.0), axis=-1, keepdims=True)
        s_hi = jnp.sum(jnp.where(lo, 0.0, yc), axis=-1, keepdims=True)
        r_lo = lax.rsqrt(s_lo + HEAD_DIM * RMS_EPS)
        r_hi = lax.rsqrt(s_hi + HEAD_DIM * RMS_EPS)
        scales.append(jnp.where(lo, r_lo, r_hi))
    return jnp.concatenate(scales, axis=1)


def _cast_rows(pairs):
    for src_ref, dst_ref in pairs:
        dst_ref[...] = src_ref[:, src_ref.shape[1] - dst_ref.shape[1]:].astype(BF16)


def _in_proj_kernel(x_ref, g_ref, w_ref, qk_gain_ref, wg_f32, wa_f32, wb_f32, wo_f32,
                    qa_ref, ka_ref, va_ref, qb_ref, kbd_ref, vbd_ref, wg_bf, wa_bf, wb_bf, wo_bf,
                    w_scr, h_scr, p_scr):
    @pl.when(pl.program_id(0) == 0)
    def _cast_own_weights():
        for c in range(0, QKV_W, MXU_N):
            w_scr[:, c:c + MXU_N] = w_ref[:, c:c + MXU_N].astype(BF16)

    _cast_rows(((wg_f32, wg_bf), (wa_f32, wa_bf), (wb_f32, wb_bf), (wo_f32, wo_bf)))

    def normed(out_ref, rows, c, gain_row):
        def epilogue(slot):
            p = p_scr[slot]
            scale = _group_rms_scale(p)
            out_ref[rows, c:c + MXU_N] = (p * scale * qk_gain_ref[gain_row:gain_row + 1, :]).astype(BF16)
        return epilogue

    def plain(out_ref, rows, c):
        def epilogue(slot):
            out_ref[rows, c:c + MXU_N] = p_scr[slot].astype(BF16)
        return epilogue

    def kv_b(rows):
        def epilogue(slot):
            kb = p_scr[slot, :, :B_KVW]
            kb = kb * _group_rms_scale(kb) * qk_gain_ref[3:4, :LANES]
            vb = p_scr[slot, :, B_KVW:]
            lo = lax.broadcasted_iota(jnp.int32, kb.shape, 1) < HEAD_DIM
            for src, out_ref in ((kb, kbd_ref), (vb, vbd_ref)):
                swapped = pltpu.roll(src, HEAD_DIM, axis=1)
                out_ref[rows, :LANES] = jnp.where(lo, src, swapped).astype(BF16)
                out_ref[rows, LANES:] = jnp.where(lo, swapped, src).astype(BF16)
        return epilogue

    units = []
    for rb in range(0, x_ref.shape[0], ROW_BLOCK):
        rows = slice(rb, rb + ROW_BLOCK)
        xf = x_ref[rows, :]
        h_scr[rows, :] = ((xf * _rms_scale(xf)) * g_ref[...]).astype(BF16)
        for c in range(0, A_W, MXU_N):
            units += [(rows, c, normed(qa_ref, rows, c, 0)), (rows, A_W + c, normed(ka_ref, rows, c, 1)),
                      (rows, 3 * A_W + c, normed(qb_ref, rows, c, 2))]
        units.append((rows, 3 * A_W + B_QW, kv_b(rows)))
        units += [(rows, 2 * A_W + c, plain(va_ref, rows, c)) for c in range(0, A_W, MXU_N)]
    n_slots = PROJ_SKEW + 1
    for idx in range(len(units) + PROJ_SKEW):
        if idx < len(units):
            rows, c0, _ = units[idx]
            p_scr[idx % n_slots] = jnp.dot(h_scr[rows, :], w_scr[:, c0:c0 + MXU_N], preferred_element_type=F32)
        if idx >= PROJ_SKEW:
            units[idx - PROJ_SKEW][2]((idx - PROJ_SKEW) % n_slots)


def _row_stream(n_rows, n_cols, steps, layer):
    return pl.BlockSpec((None, n_rows // steps, n_cols), lambda i: (layer, i, 0))


def _in_proj(x2, g_mix, qk_gain, w_in, w_a, w_b, w_o, layer):
    n = x2.shape[0]
    tm = TOK_TILE
    steps = n // tm
    tile = lambda w: pl.BlockSpec((tm, w), lambda i: (i, 0))
    gate_w = w_in.shape[2] - QKV_W
    out_w = (A_W, A_W, A_W, B_QW, 2 * LANES, 2 * LANES)
    streamed = ((w_in, gate_w), (w_a, D_MODEL), (w_b, D_MODEL), (w_o, D_MODEL))
    outs = pl.pallas_call(
        _in_proj_kernel,
        grid=(steps,),
        in_specs=[tile(D_MODEL), _resident((1, D_MODEL)),
                  pl.BlockSpec((None, D_MODEL, QKV_W), lambda i: (layer, 0, 0), pipeline_mode=pl.Buffered(1)),
                  _resident((4, MXU_N))]
                 + [_row_stream(w.shape[1], w.shape[2], steps, layer) for w, _ in streamed],
        out_specs=[tile(w) for w in out_w]
                  + [pl.BlockSpec((w.shape[1] // steps, cols), lambda i: (i, 0)) for w, cols in streamed],
        out_shape=[jax.ShapeDtypeStruct((n, w), BF16) for w in out_w]
                  + [jax.ShapeDtypeStruct((w.shape[1], cols), BF16) for w, cols in streamed],
        scratch_shapes=[pltpu.VMEM((D_MODEL, QKV_W), BF16), pltpu.VMEM((tm, D_MODEL), BF16),
                        pltpu.VMEM((PROJ_SKEW + 1, ROW_BLOCK, MXU_N), F32)],
        compiler_params=_params(1),
        name="in_proj",
    )(x2, g_mix, w_in, qk_gain, w_in, w_a, w_b, w_o)
    return outs[:6], outs[6:]


def _na_kernel(rpb_ref, dc_ref, q_ref, k_ref, v_ref, o_ref, bias_ref, s_ref, *, rows):
    t = pl.program_id(1)
    lo = lax.broadcasted_iota(jnp.int32, (GRID_W, LANES), 1) < HEAD_DIM
    nt = (((1,), (1,)), ((), ()))
    n_off = 2 * NA_KW - 1
    n_slab = 2 * NA_KH - 2

    @pl.when((pl.program_id(0) == 0) & (t == 0))
    def _build_bias():
        dc = dc_ref[...]
        lo_row = lax.broadcasted_iota(jnp.int32, (1, LANES), 1) < HEAD_DIM

        def build(idx, carry):
            h = idx // n_slab
            d = idx - h * n_slab
            acc = jnp.full((GRID_W, LANES), NEG_INF, F32)
            for o in range(n_off):
                left = rpb_ref[(d * n_off + o) * NA_HEADS + h] * LOG2E
                right = rpb_ref[((d + 1) * n_off + o) * NA_HEADS + h] * LOG2E
                acc = jnp.where(dc == o, jnp.where(lo_row, left, right), acc)
            row0 = pl.multiple_of((h % 2) * GRID_W, GRID_W)
            bias_ref[h // 2, d, pl.ds(row0, GRID_W), :] = acc
            return carry

        lax.fori_loop(0, NA_HEADS * n_slab, build, 0)

    row_info = []
    for i in range(NA_ROWS_PER_STEP):
        r = t * NA_ROWS_PER_STEP + i
        rs = jnp.clip(r - NA_KH // 2, 0, rows - NA_KH)
        row_info.append(((NA_KH - 1) - (r - rs), pl.multiple_of(rs * GRID_W, GRID_W)))

    def scores(i, p, slot):
        d0, kstart = row_info[i]
        cols = slice(p * LANES, (p + 1) * LANES)
        q = q_ref[0, i * GRID_W:(i + 1) * GRID_W, cols]
        zero = jnp.zeros_like(q)
        q2 = jnp.concatenate([jnp.where(lo, q, zero), jnp.where(lo, zero, q)], axis=0)
        kk = k_ref[0, pl.ds(kstart, NA_KH * GRID_W), cols]
        s = lax.dot_general(q2, kk, nt, preferred_element_type=F32)
        for v in range(NA_KH // 2):
            vcols = slice(v * LANES, (v + 1) * LANES)
            s_ref[slot, :, vcols] = s[:, vcols] + bias_ref[p, d0 + 2 * v]

    ones_cols = jnp.ones((NA_KH * GRID_W, LANES), BF16)

    def finish(i, p, slot):
        _, kstart = row_info[i]
        cols = slice(p * LANES, (p + 1) * LANES)
        m = jnp.max(s_ref[slot], axis=-1, keepdims=True)
        e = jnp.exp2(s_ref[slot] - m).astype(BF16)
        vv = jnp.concatenate([v_ref[0, pl.ds(kstart, NA_KH * GRID_W), cols], ones_cols], axis=1)
        o2 = jnp.dot(e, vv, preferred_element_type=F32)
        o2 = o2[:, :LANES] / o2[:, LANES:]
        o = jnp.where(lo, o2[:GRID_W], o2[GRID_W:])
        o_ref[0, i * GRID_W:(i + 1) * GRID_W, cols] = o.astype(BF16)

    units = [(i, p) for i in range(NA_ROWS_PER_STEP) for p in range(NA_HEADS // 2)]
    n_slots = NA_SKEW + 1
    for idx in range(len(units) + NA_SKEW):
        if idx < len(units):
            scores(*units[idx], idx % n_slots)
        if idx >= NA_SKEW:
            finish(*units[idx - NA_SKEW], (idx - NA_SKEW) % n_slots)


def _na_col_offsets():
    c = np.arange(GRID_W)
    kc = c[None, :]
    win_start = np.clip(c - NA_KW // 2, 0, GRID_W - NA_KW)[:, None]
    in_win = (kc >= win_start) & (kc < win_start + NA_KW)
    dc = np.where(in_win, kc - c[:, None] + (NA_KW - 1), -1).astype(np.int32)
    return np.concatenate([dc, dc], axis=1)


def _na_attn(rpb, qa, ka, va):
    b, t, _ = qa.shape
    rows = t // GRID_W
    tq = NA_ROWS_PER_STEP * GRID_W
    full = pl.BlockSpec((1, t, A_W), lambda bi, ti: (bi, 0, 0))
    tile = pl.BlockSpec((1, tq, A_W), lambda bi, ti: (bi, ti, 0))
    return pl.pallas_call(
        functools.partial(_na_kernel, rows=rows),
        grid=(b, rows // NA_ROWS_PER_STEP),
        in_specs=[pl.BlockSpec(memory_space=pltpu.SMEM), _resident((GRID_W, LANES)), tile, full, full],
        out_specs=tile,
        out_shape=jax.ShapeDtypeStruct((b, t, A_W), BF16),
        scratch_shapes=[pltpu.VMEM((NA_HEADS // 2, 2 * NA_KH - 2, LANES, LANES), F32),
                        pltpu.VMEM((NA_SKEW + 1, 2 * GRID_W, NA_KH * GRID_W), F32)],
        compiler_params=_params(2),
        name="na_attn",
    )(rpb.astype(F32).reshape(-1), jnp.asarray(_na_col_offsets()), qa, ka, va)


def _swa_kernel(sink_ref, t5_ref, bucket_ref, q_ref, kp_ref, kc_ref, kn_ref, vp_ref, vc_ref, vn_ref, o_ref,
                bias_ref, s_ref, *, n_tiles):
    n = pl.program_id(1)
    lo = lax.broadcasted_iota(jnp.int32, (SW_BLOCK, LANES), 1) < HEAD_DIM
    nt = (((1,), (1,)), ((), ()))

    @pl.when((pl.program_id(0) == 0) & (n == 0))
    def _build_bias():
        bucket = bucket_ref[...]

        def build(h, carry):
            acc = jnp.full(bucket.shape, NEG_INF, F32)
            for bkt in range(T5_BUCKETS):
                acc = jnp.where(bucket == bkt, t5_ref[bkt * SW_HEADS + h] * LOG2E, acc)
            row0 = pl.multiple_of((h % SW_GROUP) * SW_BLOCK, SW_BLOCK)
            bias_ref[h // SW_GROUP, pl.ds(row0, SW_BLOCK), :] = acc
            return carry

        lax.fori_loop(0, SW_HEADS, build, 0)

    def window(prev_ref, own_ref, next_ref, kvh, j):
        kcols = slice(kvh * LANES, (kvh + 1) * LANES)
        blk = lambda b: own_ref[0, b * SW_BLOCK:(b + 1) * SW_BLOCK, kcols]
        first = prev_ref[0, :, kcols] if j == 0 else blk(j - 1)
        last = next_ref[0, :, kcols] if j == SW_BLOCKS_PER_STEP - 1 else blk(j + 1)
        return jnp.concatenate([first, blk(j), last], axis=0)

    def scores(kvh, j, c, slot):
        rows = slice(j * SW_BLOCK, (j + 1) * SW_BLOCK)
        qb = q_ref[0, rows, (2 * kvh + c) * LANES:(2 * kvh + c + 1) * LANES]
        zero = jnp.zeros_like(qb)
        q2 = jnp.concatenate([jnp.where(lo, qb, zero), jnp.where(lo, zero, qb)], axis=0)
        s_ref[slot] = lax.dot_general(q2, window(kp_ref, kc_ref, kn_ref, kvh, j), nt,
                                      preferred_element_type=F32)

    ones_cols = jnp.ones((3 * SW_BLOCK, LANES), BF16)

    def finish(kvh, j, c, slot):
        rows = slice(j * SW_BLOCK, (j + 1) * SW_BLOCK)
        es, sink_terms = [], []
        for hh in range(2):
            hrows = slice(hh * SW_BLOCK, (hh + 1) * SW_BLOCK)
            brows = slice((2 * c + hh) * SW_BLOCK, (2 * c + hh + 1) * SW_BLOCK)
            sg = s_ref[slot, hrows, :] + bias_ref[kvh, brows, :]
            if j == 0:
                sg = jnp.concatenate([jnp.where(n > 0, sg[:, :SW_BLOCK], NEG_INF), sg[:, SW_BLOCK:]], axis=1)
            if j == SW_BLOCKS_PER_STEP - 1:
                sg = jnp.concatenate([sg[:, :2 * SW_BLOCK],
                                      jnp.where(n < n_tiles - 1, sg[:, 2 * SW_BLOCK:], NEG_INF)], axis=1)
            sk = sink_ref[kvh * SW_GROUP + 2 * c + hh] * LOG2E
            m = jnp.maximum(jnp.max(sg, axis=-1, keepdims=True), sk)
            es.append(jnp.exp2(sg - m).astype(BF16))
            sink_terms.append(jnp.exp2(sk - m))
        vv = jnp.concatenate([window(vp_ref, vc_ref, vn_ref, kvh, j), ones_cols], axis=1)
        o2 = jnp.dot(jnp.concatenate(es, axis=0), vv, preferred_element_type=F32)
        og = []
        for hh in range(2):
            hrows = slice(hh * SW_BLOCK, (hh + 1) * SW_BLOCK)
            og.append(o2[hrows, :LANES] / (o2[hrows, LANES:] + sink_terms[hh]))
        o_ref[0, rows, (2 * kvh + c) * LANES:(2 * kvh + c + 1) * LANES] = (
            jnp.where(lo, og[0], og[1]).astype(BF16))

    units = [(kvh, j, c) for j in range(SW_BLOCKS_PER_STEP) for kvh in range(SW_KV_HEADS) for c in range(2)]
    n_slots = SW_SKEW + 1
    for idx in range(len(units) + SW_SKEW):
        if idx < len(units):
            scores(*units[idx], idx % n_slots)
        if idx >= SW_SKEW:
            finish(*units[idx - SW_SKEW], (idx - SW_SKEW) % n_slots)


def _t5_bucket(rel):
    half = T5_BUCKETS // 2
    max_exact = half // 2
    ret = (rel > 0).astype(np.int32) * half
    n = np.abs(rel)
    large = max_exact + (np.log(np.maximum(n, 1) / max_exact)
                         / np.log(T5_MAX_DIST / max_exact) * (half - max_exact)).astype(np.int32)
    large = np.minimum(large, half - 1)
    return ret + np.where(n < max_exact, n, large)


def _swa_bucket_index():
    rel = np.arange(3 * SW_BLOCK)[None, :] - SW_BLOCK - np.arange(SW_BLOCK)[:, None]
    return np.where(np.abs(rel) <= SW_WINDOW, _t5_bucket(rel), -1).astype(np.int32)


def _swa_attn(sink, t5_table, qb, kbd, vbd):
    b, t, _ = qb.shape
    tq = SW_BLOCKS_PER_STEP * SW_BLOCK
    n_tiles = t // tq
    last_blk = t // SW_BLOCK - 1
    kv_w = kbd.shape[-1]
    qtile = pl.BlockSpec((1, tq, B_QW), lambda bi, ni: (bi, ni, 0))
    own = pl.BlockSpec((1, tq, kv_w), lambda bi, ni: (bi, ni, 0))
    prev = pl.BlockSpec((1, SW_BLOCK, kv_w),
                        lambda bi, ni: (bi, jnp.maximum(ni * SW_BLOCKS_PER_STEP - 1, 0), 0))
    nxt = pl.BlockSpec((1, SW_BLOCK, kv_w),
                       lambda bi, ni: (bi, jnp.minimum((ni + 1) * SW_BLOCKS_PER_STEP, last_blk), 0))
    smem = pl.BlockSpec(memory_space=pltpu.SMEM)
    return pl.pallas_call(
        functools.partial(_swa_kernel, n_tiles=n_tiles),
        grid=(b, n_tiles),
        in_specs=[smem, smem, _resident((SW_BLOCK, 3 * SW_BLOCK)), qtile, prev, own, nxt, prev, own, nxt],
        out_specs=qtile,
        out_shape=jax.ShapeDtypeStruct((b, t, B_QW), BF16),
        scratch_shapes=[pltpu.VMEM((SW_KV_HEADS, SW_GROUP * SW_BLOCK, 3 * SW_BLOCK), F32),
                        pltpu.VMEM((SW_SKEW + 1, 2 * SW_BLOCK, 3 * SW_BLOCK), F32)],
        compiler_params=_params(2),
        name="swa_attn",
    )(sink.astype(F32), t5_table.astype(F32).reshape(-1), jnp.asarray(_swa_bucket_index()),
      qb, kbd, kbd, kbd, vbd, vbd, vbd)


def _mix_kernel(x_ref, oa_ref, ob_ref, g_ref, wg_ref, wa_ref, wb_ref, wo_ref, wgate_f32, wup_f32, wdown_f32,
                x1_ref, wgate_bf, wup_bf, wdown_bf, h_scr, y_scr):
    _cast_rows(((wgate_f32, wgate_bf), (wup_f32, wup_bf), (wdown_f32, wdown_bf)))
    for rb in range(0, x_ref.shape[0], ROW_BLOCK):
        rows = slice(rb, rb + ROW_BLOCK)
        xf = x_ref[rows, :]
        h_scr[rows, :] = ((xf * _rms_scale(xf)) * g_ref[...]).astype(BF16)
        for c in range(0, D_MODEL, MXU_N):
            cs = slice(c, c + MXU_N)
            gs = slice(D_MODEL + c, D_MODEL + c + MXU_N)
            ga = jnp.dot(h_scr[rows, :], wg_ref[:, cs], preferred_element_type=F32)
            gb = jnp.dot(h_scr[rows, :], wg_ref[:, gs], preferred_element_type=F32)
            ya = jnp.dot(oa_ref[rows, :], wa_ref[:, cs], preferred_element_type=F32)
            yb = jnp.dot(ob_ref[rows, :], wb_ref[:, cs], preferred_element_type=F32)
            y_scr[rows, cs] = (jax.nn.sigmoid(ga) * ya + jax.nn.sigmoid(gb) * yb).astype(BF16)
        for c in range(0, D_MODEL, MXU_N):
            cs = slice(c, c + MXU_N)
            x1_ref[rows, cs] = x_ref[rows, cs] + jnp.dot(y_scr[rows, :], wo_ref[:, cs], preferred_element_type=F32)


def _mix_out(x2, oa, ob, g_mix, mix_weights, w_gate, w_up, w_down, layer):
    n = x2.shape[0]
    tm = TOK_TILE
    steps = n // tm
    tile = lambda w: pl.BlockSpec((tm, w), lambda i: (i, 0))
    streamed = (w_gate, w_up, w_down)
    outs = pl.pallas_call(
        _mix_kernel,
        grid=(steps,),
        in_specs=[tile(D_MODEL), tile(A_W), tile(B_QW), _resident((1, D_MODEL))]
                 + [_resident(w.shape) for w in mix_weights]
                 + [_row_stream(w.shape[1], w.shape[2], steps, layer) for w in streamed],
        out_specs=[tile(D_MODEL)]
                  + [pl.BlockSpec((w.shape[1] // steps, w.shape[2]), lambda i: (i, 0)) for w in streamed],
        out_shape=[jax.ShapeDtypeStruct((n, D_MODEL), F32)]
                  + [jax.ShapeDtypeStruct(w.shape[1:], BF16) for w in streamed],
        scratch_shapes=[pltpu.VMEM((tm, D_MODEL), BF16), pltpu.VMEM((tm, D_MODEL), BF16)],
        compiler_params=_params(1),
        name="mix_out",
    )(x2, oa, ob, g_mix, *mix_weights, w_gate, w_up, w_down)
    return outs[0], outs[1:]


def _ffn_kernel(x_ref, g_ref, wg_ref, wu_ref, wd_ref, o_ref, h_scr, a_scr):
    hidden = wg_ref.shape[1]
    for rb in range(0, x_ref.shape[0], ROW_BLOCK):
        rows = slice(rb, rb + ROW_BLOCK)
        xf = x_ref[rows, :]
        h_scr[rows, :] = ((xf * _rms_scale(xf)) * g_ref[...]).astype(BF16)
        for c in range(0, hidden, MXU_N):
            cs = slice(c, c + MXU_N)
            gate = jnp.dot(h_scr[rows, :], wg_ref[:, cs], preferred_element_type=F32)
            up = jnp.dot(h_scr[rows, :], wu_ref[:, cs], preferred_element_type=F32)
            a_scr[rows, cs] = (jax.nn.silu(gate) * up).astype(BF16)
        for c in range(0, D_MODEL, MXU_N):
            cs = slice(c, c + MXU_N)
            o_ref[rows, cs] = x_ref[rows, cs] + jnp.dot(a_scr[rows, :], wd_ref[:, cs], preferred_element_type=F32)


def _ffn(x1, g_ffn, w_gate, w_up, w_down):
    n = x1.shape[0]
    tm = TOK_TILE
    hidden = w_gate.shape[1]
    tile = pl.BlockSpec((tm, D_MODEL), lambda i: (i, 0))
    return pl.pallas_call(
        _ffn_kernel,
        grid=(n // tm,),
        in_specs=[tile, _resident((1, D_MODEL)), _resident(w_gate.shape), _resident(w_up.shape),
                  _resident(w_down.shape)],
        out_specs=tile,
        out_shape=jax.ShapeDtypeStruct((n, D_MODEL), F32),
        scratch_shapes=[pltpu.VMEM((tm, D_MODEL), BF16), pltpu.VMEM((tm, hidden), BF16)],
        compiler_params=_params(1),
        name="ffn",
    )(x1, g_ffn, w_gate, w_up, w_down)


def _layer(x2, b, t, layer, norm_mix, w_in, q_norm_a, k_norm_a, rpb_a, q_norm_b, k_norm_b, sink_b, t5_table,
           w_branch_a, w_branch_b, w_out, norm_ffn, w_gate, w_up, w_down):
    g_mix = norm_mix[layer].reshape(1, D_MODEL)
    reps = MXU_N // HEAD_DIM
    root_d = HEAD_DIM ** 0.5
    q_fold = QK_SCALE * root_d * LOG2E
    qk_gain = jnp.stack([jnp.tile(q_norm_a[layer] * q_fold, reps), jnp.tile(k_norm_a[layer] * root_d, reps),
                         jnp.tile(q_norm_b[layer] * q_fold, reps), jnp.tile(k_norm_b[layer] * root_d, reps)]
                        ).astype(F32)

    (qa, ka, va, qb, kbd, vbd), mix_weights = _in_proj(x2, g_mix, qk_gain, w_in, w_branch_a, w_branch_b,
                                                       w_out, layer)
    n = x2.shape[0]
    r3 = lambda a: a.reshape(b, t, a.shape[-1])
    oa = _na_attn(rpb_a[layer], r3(qa), r3(ka), r3(va))
    ob = _swa_attn(sink_b[layer], t5_table, r3(qb), r3(kbd), r3(vbd))
    x1, ffn_weights = _mix_out(x2, oa.reshape(n, A_W), ob.reshape(n, B_QW), g_mix, mix_weights,
                               w_gate, w_up, w_down, layer)
    return _ffn(x1, norm_ffn[layer].reshape(1, D_MODEL), *ffn_weights)


def kernel(x, norm_mix, w_in, q_norm_a, k_norm_a, rpb_a, q_norm_b, k_norm_b, sink_b, t5_table,
           w_branch_a, w_branch_b, w_out, norm_ffn, w_gate, w_up, w_down):
    b, t, d = x.shape
    x2 = x.reshape(b * t, d)
    for layer in range(norm_mix.shape[0]):
        x2 = _layer(x2, b, t, layer, norm_mix, w_in, q_norm_a, k_norm_a, rpb_a, q_norm_b, k_norm_b, sink_b,
                    t5_table, w_branch_a, w_branch_b, w_out, norm_ffn, w_gate, w_up, w_down)
    return x2.reshape(b, t, d)
```

```python
import functools

import jax
import jax.numpy as jnp
import numpy as np
from jax import lax
from jax.experimental import pallas as pl
from jax.experimental.pallas import tpu as pltpu

F32 = jnp.float32
BF16 = jnp.bfloat16

D_MODEL = 1024
HEAD_DIM = 64
GRID_W = 64
NA_HEADS = 8
NA_KH = 8
NA_KW = 16
SW_HEADS = 8
SW_KV_HEADS = 2
SW_GROUP = SW_HEADS // SW_KV_HEADS
SW_WINDOW = 128
SW_BLOCK = 128
T5_BUCKETS = 32
T5_MAX_DIST = 128
A_W = NA_HEADS * HEAD_DIM
B_QW = SW_HEADS * HEAD_DIM
B_KVW = SW_KV_HEADS * HEAD_DIM
QKV_W = 3 * A_W + B_QW + 2 * B_KVW
RMS_EPS = 1e-6
NEG_INF = -1e30
QK_SCALE = HEAD_DIM ** -0.5
LOG2E = 1.4426950408889634

LANES = 128
MXU_N = 256
VMEM_LIMIT = 56 * 1024 * 1024

TOK_TILE = 1024
ROW_BLOCK = 512
NA_ROWS_PER_STEP = 16
PROJ_SKEW = 2
NA_SKEW = 4
SW_SKEW = 4
SW_BLOCKS_PER_STEP = 8


def _resident(shape):
    return pl.BlockSpec(shape, lambda *_: (0,) * len(shape), pipeline_mode=pl.Buffered(1))


def _params(n_axes):
    return pltpu.CompilerParams(dimension_semantics=("arbitrary",) * n_axes, vmem_limit_bytes=VMEM_LIMIT)


def _rms_scale(xf):
    return lax.rsqrt(jnp.mean(xf * xf, axis=-1, keepdims=True) + RMS_EPS)


def _group_rms_scale(p):
    y = p * p
    lo = lax.broadcasted_iota(jnp.int32, (p.shape[0], LANES), 1) < HEAD_DIM
    scales = []
    for c in range(0, p.shape[1], LANES):
        yc = y[:, c:c + LANES]
        s_lo = jnp.sum(jnp.where(lo, yc, 0.0), axis=-1, keepdims=True)
        s_hi = jnp.sum(jnp.where(lo, 0.0, yc), axis=-1, keepdims=True)
        r_lo = lax.rsqrt(s_lo + HEAD_DIM * RMS_EPS)
        r_hi = lax.rsqrt(s_hi + HEAD_DIM * RMS_EPS)
        scales.append(jnp.where(lo, r_lo, r_hi))
    return jnp.concatenate(scales, axis=1)


def _cast_rows(pairs):
    for src_ref, dst_ref in pairs:
        dst_ref[...] = src_ref[:, src_ref.shape[1] - dst_ref.shape[1]:].astype(BF16)


def _in_proj_kernel(x_ref, g_ref, w_ref, qk_gain_ref, wg_f32, wa_f32, wb_f32, wo_f32,
                    qa_ref, ka_ref, va_ref, qb_ref, kbd_ref, vbd_ref, wg_bf, wa_bf, wb_bf, wo_bf,
                    w_scr, h_scr, p_scr):
    @pl.when(pl.program_id(0) == 0)
    def _cast_own_weights():
        for c in range(0, QKV_W, MXU_N):
            w_scr[:, c:c + MXU_N] = w_ref[:, c:c + MXU_N].astype(BF16)

    _cast_rows(((wg_f32, wg_bf), (wa_f32, wa_bf), (wb_f32, wb_bf), (wo_f32, wo_bf)))

    def normed(out_ref, rows, c, gain_row):
        def epilogue(slot):
            p = p_scr[slot]
            scale = _group_rms_scale(p)
            out_ref[rows, c:c + MXU_N] = (p * scale * qk_gain_ref[gain_row:gain_row + 1, :]).astype(BF16)
        return epilogue

    def plain(out_ref, rows, c):
        def epilogue(slot):
            out_ref[rows, c:c + MXU_N] = p_scr[slot].astype(BF16)
        return epilogue

    def kv_b(rows):
        def epilogue(slot):
            kb = p_scr[slot, :, :B_KVW]
            kb = kb * _group_rms_scale(kb) * qk_gain_ref[3:4, :LANES]
            vb = p_scr[slot, :, B_KVW:]
            lo = lax.broadcasted_iota(jnp.int32, kb.shape, 1) < HEAD_DIM
            for src, out_ref in ((kb, kbd_ref), (vb, vbd_ref)):
                swapped = pltpu.roll(src, HEAD_DIM, axis=1)
                out_ref[rows, :LANES] = jnp.where(lo, src, swapped).astype(BF16)
                out_ref[rows, LANES:] = jnp.where(lo, swapped, src).astype(BF16)
        return epilogue

    units = []
    for rb in range(0, x_ref.shape[0], ROW_BLOCK):
        rows = slice(rb, rb + ROW_BLOCK)
        xf = x_ref[rows, :]
        h_scr[rows, :] = ((xf * _rms_scale(xf)) * g_ref[...]).astype(BF16)
        for c in range(0, A_W, MXU_N):
            units += [(rows, c, normed(qa_ref, rows, c, 0)), (rows, A_W + c, normed(ka_ref, rows, c, 1)),
                      (rows, 3 * A_W + c, normed(qb_ref, rows, c, 2))]
        units.append((rows, 3 * A_W + B_QW, kv_b(rows)))
        units += [(rows, 2 * A_W + c, plain(va_ref, rows, c)) for c in range(0, A_W, MXU_N)]
    n_slots = PROJ_SKEW + 1
    for idx in range(len(units) + PROJ_SKEW):
        if idx < len(units):
            rows, c0, _ = units[idx]
            p_scr[idx % n_slots] = jnp.dot(h_scr[rows, :], w_scr[:, c0:c0 + MXU_N], preferred_element_type=F32)
        if idx >= PROJ_SKEW:
            units[idx - PROJ_SKEW][2]((idx - PROJ_SKEW) % n_slots)


def _row_stream(n_rows, n_cols, steps, layer):
    return pl.BlockSpec((None, n_rows // steps, n_cols), lambda i: (layer, i, 0))


def _in_proj(x2, g_mix, qk_gain, w_in, w_a, w_b, w_o, layer):
    n = x2.shape[0]
    tm = TOK_TILE
    steps = n // tm
    tile = lambda w: pl.BlockSpec((tm, w), lambda i: (i, 0))
    gate_w = w_in.shape[2] - QKV_W
    out_w = (A_W, A_W, A_W, B_QW, 2 * LANES, 2 * LANES)
    streamed = ((w_in, gate_w), (w_a, D_MODEL), (w_b, D_MODEL), (w_o, D_MODEL))
    outs = pl.pallas_call(
        _in_proj_kernel,
        grid=(steps,),
        in_specs=[tile(D_MODEL), _resident((1, D_MODEL)),
                  pl.BlockSpec((None, D_MODEL, QKV_W), lambda i: (layer, 0, 0), pipeline_mode=pl.Buffered(1)),
                  _resident((4, MXU_N))]
                 + [_row_stream(w.shape[1], w.shape[2], steps, layer) for w, _ in streamed],
        out_specs=[tile(w) for w in out_w]
                  + [pl.BlockSpec((w.shape[1] // steps, cols), lambda i: (i, 0)) for w, cols in streamed],
        out_shape=[jax.ShapeDtypeStruct((n, w), BF16) for w in out_w]
                  + [jax.ShapeDtypeStruct((w.shape[1], cols), BF16) for w, cols in streamed],
        scratch_shapes=[pltpu.VMEM((D_MODEL, QKV_W), BF16), pltpu.VMEM((tm, D_MODEL), BF16),
                        pltpu.VMEM((PROJ_SKEW + 1, ROW_BLOCK, MXU_N), F32)],
        compiler_params=_params(1),
        name="in_proj",
    )(x2, g_mix, w_in, qk_gain, w_in, w_a, w_b, w_o)
    return outs[:6], outs[6:]


def _na_kernel(rpb_ref, dc_ref, q_ref, k_ref, v_ref, o_ref, bias_ref, s_ref, *, rows):
    t = pl.program_id(1)
    lo = lax.broadcasted_iota(jnp.int32, (GRID_W, LANES), 1) < HEAD_DIM
    nt = (((1,), (1,)), ((), ()))
    n_slab = 2 * NA_KH - 2

    @pl.when((pl.program_id(0) == 0) & (t == 0))
    def _build_bias():
        in_win = dc_ref[...] >= 0

        def build(idx, carry):
            h = idx // n_slab
            d = idx - h * n_slab
            rows_d = jnp.broadcast_to(rpb_ref[pl.ds(d * NA_HEADS + h, 1), :] * LOG2E, (GRID_W, LANES))
            rows_d1 = jnp.broadcast_to(rpb_ref[pl.ds((d + 1) * NA_HEADS + h, 1), :] * LOG2E, (GRID_W, LANES))
            left = pltpu.roll(rows_d, LANES - (NA_KW - 1), axis=1, stride=1, stride_axis=0)
            right = pltpu.roll(rows_d1, HEAD_DIM - (NA_KW - 1), axis=1, stride=1, stride_axis=0)
            row0 = pl.multiple_of((h % 2) * GRID_W, GRID_W)
            bias_ref[h // 2, d, pl.ds(row0, GRID_W), :] = jnp.where(in_win, jnp.where(lo, left, right), NEG_INF)
            return carry

        lax.fori_loop(0, NA_HEADS * n_slab, build, 0, unroll=NA_HEADS)

    win0 =_na_window_start(t, rows)
    row_info = []
    for i in range(NA_ROWS_PER_STEP):
        r = t * NA_ROWS_PER_STEP + i
        rs = jnp.clip(r - NA_KH // 2, 0, rows - NA_KH)
        row_info.append(((NA_KH - 1) - (r - rs), pl.multiple_of((rs - win0) * GRID_W, GRID_W)))

    def scores(i, p, slot):
        d0, kstart = row_info[i]
        cols = slice(p * LANES, (p + 1) * LANES)
        q = q_ref[0, i * GRID_W:(i + 1) * GRID_W, cols]
        zero = jnp.zeros_like(q)
        q2 = jnp.concatenate([jnp.where(lo, q, zero), jnp.where(lo, zero, q)], axis=0)
        kk = k_ref[0, pl.ds(kstart, NA_KH * GRID_W), cols]
        s = lax.dot_general(q2, kk, nt, preferred_element_type=F32)
        for v in range(NA_KH // 2):
            vcols = slice(v * LANES, (v + 1) * LANES)
            s_ref[slot, :, vcols] = s[:, vcols] + bias_ref[p, d0 + 2 * v]

    ones_cols = jnp.ones((NA_KH * GRID_W, LANES), BF16)

    def finish(i, p, slot):
        _, kstart = row_info[i]
        cols = slice(p * LANES, (p + 1) * LANES)
        m = jnp.max(s_ref[slot], axis=-1, keepdims=True)
        e = jnp.exp2(s_ref[slot] - m).astype(BF16)
        vv = jnp.concatenate([v_ref[0, pl.ds(kstart, NA_KH * GRID_W), cols], ones_cols], axis=1)
        o2 = jnp.dot(e, vv, preferred_element_type=F32)
        o2 = o2[:, :LANES] / o2[:, LANES:]
        o = jnp.where(lo, o2[:GRID_W], o2[GRID_W:])
        o_ref[0, i * GRID_W:(i + 1) * GRID_W, cols] = o.astype(BF16)

    units = [(i, p) for i in range(NA_ROWS_PER_STEP) for p in range(NA_HEADS // 2)]
    n_slots = NA_SKEW + 1
    for idx in range(len(units) + NA_SKEW):
        if idx < len(units):
            scores(*units[idx], idx % n_slots)
        if idx >= NA_SKEW:
            finish(*units[idx - NA_SKEW], (idx - NA_SKEW) % n_slots)


def _na_window_start(step, rows):
    return jnp.clip(step * NA_ROWS_PER_STEP - NA_KH // 2, 0, rows - (NA_ROWS_PER_STEP + NA_KH))


def _na_col_offsets():
    c = np.arange(GRID_W)
    kc = c[None, :]
    win_start = np.clip(c - NA_KW // 2, 0, GRID_W - NA_KW)[:, None]
    in_win = (kc >= win_start) & (kc < win_start + NA_KW)
    dc = np.where(in_win, kc - c[:, None] + (NA_KW - 1), -1).astype(np.int32)
    return np.concatenate([dc, dc], axis=1)


def _na_attn(rpb, qa, ka, va):
    b, t, _ = qa.shape
    rows = t // GRID_W
    tq = NA_ROWS_PER_STEP * GRID_W
    tile = pl.BlockSpec((1, tq, A_W), lambda bi, ti: (bi, ti, 0))
    window = pl.BlockSpec((pl.Element(1), pl.Element((NA_ROWS_PER_STEP + NA_KH) * GRID_W), pl.Element(A_W)),
                          lambda bi, ti: (bi, _na_window_start(ti, rows) * GRID_W, 0))
    table = jnp.transpose(rpb.astype(F32), (0, 2, 1)).reshape(-1, rpb.shape[1])
    table = jnp.pad(table, ((0, 0), (0, LANES - rpb.shape[1])))
    return pl.pallas_call(
        functools.partial(_na_kernel, rows=rows),
        grid=(b, rows // NA_ROWS_PER_STEP),
        in_specs=[_resident(table.shape), _resident((GRID_W, LANES)), tile, window, window],
        out_specs=tile,
        out_shape=jax.ShapeDtypeStruct((b, t, A_W), BF16),
        scratch_shapes=[pltpu.VMEM((NA_HEADS // 2, 2 * NA_KH - 2, LANES, LANES), F32),
                        pltpu.VMEM((NA_SKEW + 1, 2 * GRID_W, NA_KH * GRID_W), F32)],
        compiler_params=_params(2),
        name="na_attn",
    )(table, jnp.asarray(_na_col_offsets()), qa, ka, va)


def _swa_kernel(sink_ref, t5_ref, bucket_ref, q_ref, kp_ref, kc_ref, kn_ref, vp_ref, vc_ref, vn_ref, o_ref,
                bias_ref, s_ref, *, n_tiles):
    n = pl.program_id(1)
    lo = lax.broadcasted_iota(jnp.int32, (SW_BLOCK, LANES), 1) < HEAD_DIM
    nt = (((1,), (1,)), ((), ()))

    @pl.when((pl.program_id(0) == 0) & (n == 0))
    def _build_bias():
        bucket = bucket_ref[...]

        def build(h, carry):
            acc = jnp.full(bucket.shape, NEG_INF, F32)
            for bkt in range(T5_BUCKETS):
                acc = jnp.where(bucket == bkt, t5_ref[bkt * SW_HEADS + h] * LOG2E, acc)
            row0 = pl.multiple_of((h % SW_GROUP) * SW_BLOCK, SW_BLOCK)
            bias_ref[h // SW_GROUP, pl.ds(row0, SW_BLOCK), :] = acc
            return carry

        lax.fori_loop(0, SW_HEADS, build, 0)

    def window(prev_ref, own_ref, next_ref, kvh, j):
        kcols = slice(kvh * LANES, (kvh + 1) * LANES)
        blk = lambda b: own_ref[0, b * SW_BLOCK:(b + 1) * SW_BLOCK, kcols]
        first = prev_ref[0, :, kcols] if j == 0 else blk(j - 1)
        last = next_ref[0, :, kcols] if j == SW_BLOCKS_PER_STEP - 1 else blk(j + 1)
        return jnp.concatenate([first, blk(j), last], axis=0)

    def scores(kvh, j, c, slot):
        rows = slice(j * SW_BLOCK, (j + 1) * SW_BLOCK)
        qb = q_ref[0, rows, (2 * kvh + c) * LANES:(2 * kvh + c + 1) * LANES]
        zero = jnp.zeros_like(qb)
        q2 = jnp.concatenate([jnp.where(lo, qb, zero), jnp.where(lo, zero, qb)], axis=0)
        s_ref[slot] = lax.dot_general(q2, window(kp_ref, kc_ref, kn_ref, kvh, j), nt,
                                      preferred_element_type=F32)

    ones_cols = jnp.ones((3 * SW_BLOCK, LANES), BF16)

    def finish(kvh, j, c, slot):
        rows = slice(j * SW_BLOCK, (j + 1) * SW_BLOCK)
        es, sink_terms = [], []
        for hh in range(2):
            hrows = slice(hh * SW_BLOCK, (hh + 1) * SW_BLOCK)
            brows = slice((2 * c + hh) * SW_BLOCK, (2 * c + hh + 1) * SW_BLOCK)
            sg = s_ref[slot, hrows, :] + bias_ref[kvh, brows, :]
            if j == 0:
                sg = jnp.concatenate([jnp.where(n > 0, sg[:, :SW_BLOCK], NEG_INF), sg[:, SW_BLOCK:]], axis=1)
            if j == SW_BLOCKS_PER_STEP - 1:
                sg = jnp.concatenate([sg[:, :2 * SW_BLOCK],
                                      jnp.where(n < n_tiles - 1, sg[:, 2 * SW_BLOCK:], NEG_INF)], axis=1)
            sk = sink_ref[kvh * SW_GROUP + 2 * c + hh] * LOG2E
            m = jnp.maximum(jnp.max(sg, axis=-1, keepdims=True), sk)
            es.append(jnp.exp2(sg - m).astype(BF16))
            sink_terms.append(jnp.exp2(sk - m))
        vv = jnp.concatenate([window(vp_ref, vc_ref, vn_ref, kvh, j), ones_cols], axis=1)
        o2 = jnp.dot(jnp.concatenate(es, axis=0), vv, preferred_element_type=F32)
        og = []
        for hh in range(2):
            hrows = slice(hh * SW_BLOCK, (hh + 1) * SW_BLOCK)
            og.append(o2[hrows, :LANES] / (o2[hrows, LANES:] + sink_terms[hh]))
        o_ref[0, rows, (2 * kvh + c) * LANES:(2 * kvh + c + 1) * LANES] = (
            jnp.where(lo, og[0], og[1]).astype(BF16))

    units = [(kvh, j, c) for j in range(SW_BLOCKS_PER_STEP) for kvh in range(SW_KV_HEADS) for c in range(2)]
    n_slots = SW_SKEW + 1
    for idx in range(len(units) + SW_SKEW):
        if idx < len(units):
            scores(*units[idx], idx % n_slots)
        if idx >= SW_SKEW:
            finish(*units[idx - SW_SKEW], (idx - SW_SKEW) % n_slots)


def _t5_bucket(rel):
    half = T5_BUCKETS // 2
    max_exact = half // 2
    ret = (rel > 0).astype(np.int32) * half
    n = np.abs(rel)
    large = max_exact + (np.log(np.maximum(n, 1) / max_exact)
                         / np.log(T5_MAX_DIST / max_exact) * (half - max_exact)).astype(np.int32)
    large = np.minimum(large, half - 1)
    return ret + np.where(n < max_exact, n, large)


def _swa_bucket_index():
    rel = np.arange(3 * SW_BLOCK)[None, :] - SW_BLOCK - np.arange(SW_BLOCK)[:, None]
    return np.where(np.abs(rel) <= SW_WINDOW, _t5_bucket(rel), -1).astype(np.int32)


def _swa_attn(sink, t5_table, qb, kbd, vbd):
    b, t, _ = qb.shape
    tq = SW_BLOCKS_PER_STEP * SW_BLOCK
    n_tiles = t // tq
    last_blk = t // SW_BLOCK - 1
    kv_w = kbd.shape[-1]
    qtile = pl.BlockSpec((1, tq, B_QW), lambda bi, ni: (bi, ni, 0))
    own = pl.BlockSpec((1, tq, kv_w), lambda bi, ni: (bi, ni, 0))
    prev = pl.BlockSpec((1, SW_BLOCK, kv_w),
                        lambda bi, ni: (bi, jnp.maximum(ni * SW_BLOCKS_PER_STEP - 1, 0), 0))
    nxt = pl.BlockSpec((1, SW_BLOCK, kv_w),
                       lambda bi, ni: (bi, jnp.minimum((ni + 1) * SW_BLOCKS_PER_STEP, last_blk), 0))
    smem = pl.BlockSpec(memory_space=pltpu.SMEM)
    return pl.pallas_call(
        functools.partial(_swa_kernel, n_tiles=n_tiles),
        grid=(b, n_tiles),
        in_specs=[smem, smem, _resident((SW_BLOCK, 3 * SW_BLOCK)), qtile, prev, own, nxt, prev, own, nxt],
        out_specs=qtile,
        out_shape=jax.ShapeDtypeStruct((b, t, B_QW), BF16),
        scratch_shapes=[pltpu.VMEM((SW_KV_HEADS, SW_GROUP * SW_BLOCK, 3 * SW_BLOCK), F32),
                        pltpu.VMEM((SW_SKEW + 1, 2 * SW_BLOCK, 3 * SW_BLOCK), F32)],
        compiler_params=_params(2),
        name="swa_attn",
    )(sink.astype(F32), t5_table.astype(F32).reshape(-1), jnp.asarray(_swa_bucket_index()),
      qb, kbd, kbd, kbd, vbd, vbd, vbd)


def _mix_kernel(x_ref, oa_ref, ob_ref, g_ref, wg_ref, wa_ref, wb_ref, wo_ref, wgate_f32, wup_f32, wdown_f32,
                x1_ref, wgate_bf, wup_bf, wdown_bf, h_scr, y_scr):
    _cast_rows(((wgate_f32, wgate_bf), (wup_f32, wup_bf), (wdown_f32, wdown_bf)))
    for rb in range(0, x_ref.shape[0], ROW_BLOCK):
        rows = slice(rb, rb + ROW_BLOCK)
        xf = x_ref[rows, :]
        h_scr[rows, :] = ((xf * _rms_scale(xf)) * g_ref[...]).astype(BF16)
        for c in range(0, D_MODEL, MXU_N):
            cs = slice(c, c + MXU_N)
            gs = slice(D_MODEL + c, D_MODEL + c + MXU_N)
            ga = jnp.dot(h_scr[rows, :], wg_ref[:, cs], preferred_element_type=F32)
            gb = jnp.dot(h_scr[rows, :], wg_ref[:, gs], preferred_element_type=F32)
            ya = jnp.dot(oa_ref[rows, :], wa_ref[:, cs], preferred_element_type=F32)
            yb = jnp.dot(ob_ref[rows, :], wb_ref[:, cs], preferred_element_type=F32)
            y_scr[rows, cs] = (jax.nn.sigmoid(ga) * ya + jax.nn.sigmoid(gb) * yb).astype(BF16)
        for c in range(0, D_MODEL, MXU_N):
            cs = slice(c, c + MXU_N)
            x1_ref[rows, cs] = x_ref[rows, cs] + jnp.dot(y_scr[rows, :], wo_ref[:, cs], preferred_element_type=F32)


def _mix_out(x2, oa, ob, g_mix, mix_weights, w_gate, w_up, w_down, layer):
    n = x2.shape[0]
    tm = TOK_TILE
    steps = n // tm
    tile = lambda w: pl.BlockSpec((tm, w), lambda i: (i, 0))
    streamed = (w_gate, w_up, w_down)
    outs = pl.pallas_call(
        _mix_kernel,
        grid=(steps,),
        in_specs=[tile(D_MODEL), tile(A_W), tile(B_QW), _resident((1, D_MODEL))]
                 + [_resident(w.shape) for w in mix_weights]
                 + [_row_stream(w.shape[1], w.shape[2], steps, layer) for w in streamed],
        out_specs=[tile(D_MODEL)]
                  + [pl.BlockSpec((w.shape[1] // steps, w.shape[2]), lambda i: (i, 0)) for w in streamed],
        out_shape=[jax.ShapeDtypeStruct((n, D_MODEL), F32)]
                  + [jax.ShapeDtypeStruct(w.shape[1:], BF16) for w in streamed],
        scratch_shapes=[pltpu.VMEM((tm, D_MODEL), BF16), pltpu.VMEM((tm, D_MODEL), BF16)],
        compiler_params=_params(1),
        name="mix_out",
    )(x2, oa, ob, g_mix, *mix_weights, w_gate, w_up, w_down)
    return outs[0], outs[1:]


def _ffn_kernel(x_ref, g_ref, wg_ref, wu_ref, wd_ref, o_ref, h_scr, a_scr):
    hidden = wg_ref.shape[1]
    for rb in range(0, x_ref.shape[0], ROW_BLOCK):
        rows = slice(rb, rb + ROW_BLOCK)
        xf = x_ref[rows, :]
        h_scr[rows, :] = ((xf * _rms_scale(xf)) * g_ref[...]).astype(BF16)
        for c in range(0, hidden, MXU_N):
            cs = slice(c, c + MXU_N)
            gate = jnp.dot(h_scr[rows, :], wg_ref[:, cs], preferred_element_type=F32)
            up = jnp.dot(h_scr[rows, :], wu_ref[:, cs], preferred_element_type=F32)
            a_scr[rows, cs] = (jax.nn.silu(gate) * up).astype(BF16)
        for c in range(0, D_MODEL, MXU_N):
            cs = slice(c, c + MXU_N)
            o_ref[rows, cs] = x_ref[rows, cs] + jnp.dot(a_scr[rows, :], wd_ref[:, cs], preferred_element_type=F32)


def _ffn(x1, g_ffn, w_gate, w_up, w_down):
    n = x1.shape[0]
    tm = TOK_TILE
    hidden = w_gate.shape[1]
    tile = pl.BlockSpec((tm, D_MODEL), lambda i: (i, 0))
    return pl.pallas_call(
        _ffn_kernel,
        grid=(n // tm,),
        in_specs=[tile, _resident((1, D_MODEL)), _resident(w_gate.shape), _resident(w_up.shape),
                  _resident(w_down.shape)],
        out_specs=tile,
        out_shape=jax.ShapeDtypeStruct((n, D_MODEL), F32),
        scratch_shapes=[pltpu.VMEM((tm, D_MODEL), BF16), pltpu.VMEM((tm, hidden), BF16)],
        compiler_params=_params(1),
        name="ffn",
    )(x1, g_ffn, w_gate, w_up, w_down)


def _layer(x2, b, t, layer, norm_mix, w_in, q_norm_a, k_norm_a, rpb_a, q_norm_b, k_norm_b, sink_b, t5_table,
           w_branch_a, w_branch_b, w_out, norm_ffn, w_gate, w_up, w_down):
    g_mix = norm_mix[layer].reshape(1, D_MODEL)
    reps = MXU_N // HEAD_DIM
    root_d = HEAD_DIM ** 0.5
    q_fold = QK_SCALE * root_d * LOG2E
    qk_gain = jnp.stack([jnp.tile(q_norm_a[layer] * q_fold, reps), jnp.tile(k_norm_a[layer] * root_d, reps),
                         jnp.tile(q_norm_b[layer] * q_fold, reps), jnp.tile(k_norm_b[layer] * root_d, reps)]
                        ).astype(F32)

    (qa, ka, va, qb, kbd, vbd), mix_weights = _in_proj(x2, g_mix, qk_gain, w_in, w_branch_a, w_branch_b,
                                                       w_out, layer)
    n = x2.shape[0]
    r3 = lambda a: a.reshape(b, t, a.shape[-1])
    oa = _na_attn(rpb_a[layer], r3(qa), r3(ka), r3(va))
    ob = _swa_attn(sink_b[layer], t5_table, r3(qb), r3(kbd), r3(vbd))
    x1, ffn_weights = _mix_out(x2, oa.reshape(n, A_W), ob.reshape(n, B_QW), g_mix, mix_weights,
                               w_gate, w_up, w_down, layer)
    return _ffn(x1, norm_ffn[layer].reshape(1, D_MODEL), *ffn_weights)


def kernel(x, norm_mix, w_in, q_norm_a, k_norm_a, rpb_a, q_norm_b, k_norm_b, sink_b, t5_table,
           w_branch_a, w_branch_b, w_out, norm_ffn, w_gate, w_up, w_down):
    b, t, d = x.shape
    x2 = x.reshape(b * t, d)
    for layer in range(norm_mix.shape[0]):
        x2 = _layer(x2, b, t, layer, norm_mix, w_in, q_norm_a, k_norm_a, rpb_a, q_norm_b, k_norm_b, sink_b,
                    t5_table, w_branch_a, w_branch_b, w_out, norm_ffn, w_gate, w_up, w_down)
    return x2.reshape(b, t, d)
```

```python
import functools

import jax
import jax.numpy as jnp
import numpy as np
from jax import lax
from jax.experimental import pallas as pl
from jax.experimental.pallas import tpu as pltpu

F32 = jnp.float32
BF16 = jnp.bfloat16

D_MODEL = 1024
HEAD_DIM = 64
GRID_W = 64
NA_HEADS = 8
NA_KH = 8
NA_KW = 16
SW_HEADS = 8
SW_KV_HEADS = 2
SW_GROUP = SW_HEADS // SW_KV_HEADS
SW_WINDOW = 128
SW_BLOCK = 128
T5_BUCKETS = 32
T5_MAX_DIST = 128
A_W = NA_HEADS * HEAD_DIM
B_QW = SW_HEADS * HEAD_DIM
B_KVW = SW_KV_HEADS * HEAD_DIM
QKV_W = 3 * A_W + B_QW + 2 * B_KVW
RMS_EPS = 1e-6
NEG_INF = -1e30
QK_SCALE = HEAD_DIM ** -0.5
LOG2E = 1.4426950408889634

LANES = 128
MXU_N = 256
VMEM_LIMIT = 56 * 1024 * 1024

TOK_TILE = 1024
ROW_BLOCK = 512
NA_ROWS_PER_STEP = 16
PROJ_SKEW = 2
NA_SKEW = 4
SW_SKEW = 4
SW_BLOCKS_PER_STEP = 8
NA_SLOTS = 8
SW_SLOTS = 8


def _resident(shape):
    return pl.BlockSpec(shape, lambda *_: (0,) * len(shape), pipeline_mode=pl.Buffered(1))


def _params(n_axes):
    return pltpu.CompilerParams(dimension_semantics=("arbitrary",) * n_axes, vmem_limit_bytes=VMEM_LIMIT)


def _rms_scale(xf):
    return lax.rsqrt(jnp.mean(xf * xf, axis=-1, keepdims=True) + RMS_EPS)


def _group_rms_scale(p):
    y = p * p
    lo = lax.broadcasted_iota(jnp.int32, (p.shape[0], LANES), 1) < HEAD_DIM
    scales = []
    for c in range(0, p.shape[1], LANES):
        yc = y[:, c:c + LANES]
        s_lo = jnp.sum(jnp.where(lo, yc, 0.0), axis=-1, keepdims=True)
        s_hi = jnp.sum(jnp.where(lo, 0.0, yc), axis=-1, keepdims=True)
        r_lo = lax.rsqrt(s_lo + HEAD_DIM * RMS_EPS)
        r_hi = lax.rsqrt(s_hi + HEAD_DIM * RMS_EPS)
        scales.append(jnp.where(lo, r_lo, r_hi))
    return jnp.concatenate(scales, axis=1)


def _cast_rows(pairs):
    for src_ref, dst_ref in pairs:
        dst_ref[...] = src_ref[:, src_ref.shape[1] - dst_ref.shape[1]:].astype(BF16)


def _in_proj_kernel(x_ref, g_ref, w_ref, qk_gain_ref, wg_f32, wa_f32, wb_f32, wo_f32,
                    qa_ref, ka_ref, va_ref, qb_ref, kbd_ref, vbd_ref, wg_bf, wa_bf, wb_bf, wo_bf,
                    w_scr, h_scr, p_scr):
    @pl.when(pl.program_id(0) == 0)
    def _cast_own_weights():
        for c in range(0, QKV_W, MXU_N):
            w_scr[:, c:c + MXU_N] = w_ref[:, c:c + MXU_N].astype(BF16)

    _cast_rows(((wg_f32, wg_bf), (wa_f32, wa_bf), (wb_f32, wb_bf), (wo_f32, wo_bf)))

    def normed(out_ref, rows, c, gain_row):
        def epilogue(slot):
            p = p_scr[slot]
            scale = _group_rms_scale(p)
            out_ref[rows, c:c + MXU_N] = (p * scale * qk_gain_ref[gain_row:gain_row + 1, :]).astype(BF16)
        return epilogue

    def plain(out_ref, rows, c):
        def epilogue(slot):
            out_ref[rows, c:c + MXU_N] = p_scr[slot].astype(BF16)
        return epilogue

    def kv_b(rows):
        def epilogue(slot):
            kb = p_scr[slot, :, :B_KVW]
            kb = kb * _group_rms_scale(kb) * qk_gain_ref[3:4, :LANES]
            vb = p_scr[slot, :, B_KVW:]
            lo = lax.broadcasted_iota(jnp.int32, kb.shape, 1) < HEAD_DIM
            for src, out_ref in ((kb, kbd_ref), (vb, vbd_ref)):
                swapped = pltpu.roll(src, HEAD_DIM, axis=1)
                out_ref[rows, :LANES] = jnp.where(lo, src, swapped).astype(BF16)
                out_ref[rows, LANES:] = jnp.where(lo, swapped, src).astype(BF16)
        return epilogue

    units = []
    for rb in range(0, x_ref.shape[0], ROW_BLOCK):
        rows = slice(rb, rb + ROW_BLOCK)
        xf = x_ref[rows, :]
        h_scr[rows, :] = ((xf * _rms_scale(xf)) * g_ref[...]).astype(BF16)
        for c in range(0, A_W, MXU_N):
            units += [(rows, c, normed(qa_ref, rows, c, 0)), (rows, A_W + c, normed(ka_ref, rows, c, 1)),
                      (rows, 3 * A_W + c, normed(qb_ref, rows, c, 2))]
        units.append((rows, 3 * A_W + B_QW, kv_b(rows)))
        units += [(rows, 2 * A_W + c, plain(va_ref, rows, c)) for c in range(0, A_W, MXU_N)]
    n_slots = PROJ_SKEW + 1
    for idx in range(len(units) + PROJ_SKEW):
        if idx < len(units):
            rows, c0, _ = units[idx]
            p_scr[idx % n_slots] = jnp.dot(h_scr[rows, :], w_scr[:, c0:c0 + MXU_N], preferred_element_type=F32)
        if idx >= PROJ_SKEW:
            units[idx - PROJ_SKEW][2]((idx - PROJ_SKEW) % n_slots)


def _row_stream(n_rows, n_cols, steps, layer):
    return pl.BlockSpec((None, n_rows // steps, n_cols), lambda i: (layer, i, 0))


def _in_proj(x2, g_mix, qk_gain, w_in, w_a, w_b, w_o, layer):
    n = x2.shape[0]
    tm = TOK_TILE
    steps = n // tm
    tile = lambda w: pl.BlockSpec((tm, w), lambda i: (i, 0))
    gate_w = w_in.shape[2] - QKV_W
    out_w = (A_W, A_W, A_W, B_QW, 2 * LANES, 2 * LANES)
    streamed = ((w_in, gate_w), (w_a, D_MODEL), (w_b, D_MODEL), (w_o, D_MODEL))
    outs = pl.pallas_call(
        _in_proj_kernel,
        grid=(steps,),
        in_specs=[tile(D_MODEL), _resident((1, D_MODEL)),
                  pl.BlockSpec((None, D_MODEL, QKV_W), lambda i: (layer, 0, 0), pipeline_mode=pl.Buffered(1)),
                  _resident((4, MXU_N))]
                 + [_row_stream(w.shape[1], w.shape[2], steps, layer) for w, _ in streamed],
        out_specs=[tile(w) for w in out_w]
                  + [pl.BlockSpec((w.shape[1] // steps, cols), lambda i: (i, 0)) for w, cols in streamed],
        out_shape=[jax.ShapeDtypeStruct((n, w), BF16) for w in out_w]
                  + [jax.ShapeDtypeStruct((w.shape[1], cols), BF16) for w, cols in streamed],
        scratch_shapes=[pltpu.VMEM((D_MODEL, QKV_W), BF16), pltpu.VMEM((tm, D_MODEL), BF16),
                        pltpu.VMEM((PROJ_SKEW + 1, ROW_BLOCK, MXU_N), F32)],
        compiler_params=_params(1),
        name="in_proj",
    )(x2, g_mix, w_in, qk_gain, w_in, w_a, w_b, w_o)
    return outs[:6], outs[6:]


def _na_kernel(rpb_ref, dc_ref, q_ref, k_ref, v_ref, o_ref, bias_ref, s_ref, *, rows):
    t = pl.program_id(1)
    lo = lax.broadcasted_iota(jnp.int32, (GRID_W, LANES), 1) < HEAD_DIM
    nt = (((1,), (1,)), ((), ()))
    n_slab = 2 * NA_KH - 2

    @pl.when((pl.program_id(0) == 0) & (t == 0))
    def _build_bias():
        in_win = dc_ref[...] >= 0

        def build(idx, carry):
            h = idx // n_slab
            d = idx - h * n_slab
            rows_d = jnp.broadcast_to(rpb_ref[pl.ds(d * NA_HEADS + h, 1), :] * LOG2E, (GRID_W, LANES))
            rows_d1 = jnp.broadcast_to(rpb_ref[pl.ds((d + 1) * NA_HEADS + h, 1), :] * LOG2E, (GRID_W, LANES))
            left = pltpu.roll(rows_d, LANES - (NA_KW - 1), axis=1, stride=1, stride_axis=0)
            right = pltpu.roll(rows_d1, HEAD_DIM - (NA_KW - 1), axis=1, stride=1, stride_axis=0)
            row0 = pl.multiple_of((h % 2) * GRID_W, GRID_W)
            bias_ref[h // 2, d, pl.ds(row0, GRID_W), :] = jnp.where(in_win, jnp.where(lo, left, right), NEG_INF)
            return carry

        lax.fori_loop(0, NA_HEADS * n_slab, build, 0, unroll=NA_HEADS)

    row_info = []
    for i in range(NA_ROWS_PER_STEP):
        r = t * NA_ROWS_PER_STEP + i
        rs = jnp.clip(r - NA_KH // 2, 0, rows - NA_KH)
        row_info.append(((NA_KH - 1) - (r - rs), pl.multiple_of(rs * GRID_W, GRID_W)))

    def scores(i, p, slot):
        d0, kstart = row_info[i]
        cols = slice(p * LANES, (p + 1) * LANES)
        q = q_ref[0, i * GRID_W:(i + 1) * GRID_W, cols]
        zero = jnp.zeros_like(q)
        q2 = jnp.concatenate([jnp.where(lo, q, zero), jnp.where(lo, zero, q)], axis=0)
        kk = k_ref[0, pl.ds(kstart, NA_KH * GRID_W), cols]
        s = lax.dot_general(q2, kk, nt, preferred_element_type=F32)
        for v in range(NA_KH // 2):
            vcols = slice(v * LANES, (v + 1) * LANES)
            s_ref[slot, :, vcols] = s[:, vcols] + bias_ref[p, d0 + 2 * v]

    ones_cols = jnp.ones((NA_KH * GRID_W, LANES), BF16)

    def finish(i, p, slot):
        _, kstart = row_info[i]
        cols = slice(p * LANES, (p + 1) * LANES)
        m = jnp.max(s_ref[slot], axis=-1, keepdims=True)
        e = jnp.exp2(s_ref[slot] - m).astype(BF16)
        vv = jnp.concatenate([v_ref[0, pl.ds(kstart, NA_KH * GRID_W), cols], ones_cols], axis=1)
        o2 = jnp.dot(e, vv, preferred_element_type=F32)
        o2 = o2[:, :LANES] / o2[:, LANES:]
        o = jnp.where(lo, o2[:GRID_W], o2[GRID_W:])
        o_ref[0, i * GRID_W:(i + 1) * GRID_W, cols] = o.astype(BF16)

    units = [(i, p) for i in range(NA_ROWS_PER_STEP) for p in range(NA_HEADS // 2)]
    n_slots = s_ref.shape[0]
    for idx in range(len(units) + NA_SKEW):
        if idx < len(units):
            scores(*units[idx], idx % n_slots)
        if idx >= NA_SKEW:
            finish(*units[idx - NA_SKEW], (idx - NA_SKEW) % n_slots)


def _na_col_offsets():
    c = np.arange(GRID_W)
    kc = c[None, :]
    win_start = np.clip(c - NA_KW // 2, 0, GRID_W - NA_KW)[:, None]
    in_win = (kc >= win_start) & (kc < win_start + NA_KW)
    dc = np.where(in_win, kc - c[:, None] + (NA_KW - 1), -1).astype(np.int32)
    return np.concatenate([dc, dc], axis=1)


def _na_attn(rpb, qa, ka, va):
    b, t, _ = qa.shape
    rows = t // GRID_W
    tq = NA_ROWS_PER_STEP * GRID_W
    tile = pl.BlockSpec((1, tq, A_W), lambda bi, ti: (bi, ti, 0))
    full = pl.BlockSpec((1, t, A_W), lambda bi, ti: (bi, 0, 0))
    table = jnp.transpose(rpb.astype(F32), (0, 2, 1)).reshape(-1, rpb.shape[1])
    table = jnp.pad(table, ((0, 0), (0, LANES - rpb.shape[1])))
    return pl.pallas_call(
        functools.partial(_na_kernel, rows=rows),
        grid=(b, rows // NA_ROWS_PER_STEP),
        in_specs=[_resident(table.shape), _resident((GRID_W, LANES)), tile, full, full],
        out_specs=tile,
        out_shape=jax.ShapeDtypeStruct((b, t, A_W), BF16),
        scratch_shapes=[pltpu.VMEM((NA_HEADS // 2, 2 * NA_KH - 2, LANES, LANES), F32),
                        pltpu.VMEM((NA_SLOTS, 2 * GRID_W, NA_KH * GRID_W), F32)],
        compiler_params=_params(2),
        name="na_attn",
    )(table, jnp.asarray(_na_col_offsets()), qa, ka, va)


def _swa_kernel(sink_ref, t5_ref, bucket_ref, q_ref, kp_ref, kc_ref, kn_ref, vp_ref, vc_ref, vn_ref, o_ref,
                bias_ref, s_ref, *, n_tiles):
    n = pl.program_id(1)
    lo = lax.broadcasted_iota(jnp.int32, (SW_BLOCK, LANES), 1) < HEAD_DIM
    nt = (((1,), (1,)), ((), ()))

    @pl.when((pl.program_id(0) == 0) & (n == 0))
    def _build_bias():
        bucket = bucket_ref[...]

        def build(h, carry):
            acc = jnp.full(bucket.shape, NEG_INF, F32)
            for bkt in range(T5_BUCKETS):
                acc = jnp.where(bucket == bkt, t5_ref[bkt * SW_HEADS + h] * LOG2E, acc)
            row0 = pl.multiple_of((h % SW_GROUP) * SW_BLOCK, SW_BLOCK)
            bias_ref[h // SW_GROUP, pl.ds(row0, SW_BLOCK), :] = acc
            return carry

        lax.fori_loop(0, SW_HEADS, build, 0)

    def window(prev_ref, own_ref, next_ref, kvh, j):
        kcols = slice(kvh * LANES, (kvh + 1) * LANES)
        blk = lambda b: own_ref[0, b * SW_BLOCK:(b + 1) * SW_BLOCK, kcols]
        first = prev_ref[0, :, kcols] if j == 0 else blk(j - 1)
        last = next_ref[0, :, kcols] if j == SW_BLOCKS_PER_STEP - 1 else blk(j + 1)
        return jnp.concatenate([first, blk(j), last], axis=0)

    def scores(kvh, j, c, slot):
        rows = slice(j * SW_BLOCK, (j + 1) * SW_BLOCK)
        qb = q_ref[0, rows, (2 * kvh + c) * LANES:(2 * kvh + c + 1) * LANES]
        zero = jnp.zeros_like(qb)
        q2 = jnp.concatenate([jnp.where(lo, qb, zero), jnp.where(lo, zero, qb)], axis=0)
        s_ref[slot] = lax.dot_general(q2, window(kp_ref, kc_ref, kn_ref, kvh, j), nt,
                                      preferred_element_type=F32)

    ones_cols = jnp.ones((3 * SW_BLOCK, LANES), BF16)

    def finish(kvh, j, c, slot):
        rows = slice(j * SW_BLOCK, (j + 1) * SW_BLOCK)
        es, sink_terms = [], []
        for hh in range(2):
            hrows = slice(hh * SW_BLOCK, (hh + 1) * SW_BLOCK)
            brows = slice((2 * c + hh) * SW_BLOCK, (2 * c + hh + 1) * SW_BLOCK)
            sg = s_ref[slot, hrows, :] + bias_ref[kvh, brows, :]
            if j == 0:
                sg = jnp.concatenate([jnp.where(n > 0, sg[:, :SW_BLOCK], NEG_INF), sg[:, SW_BLOCK:]], axis=1)
            if j == SW_BLOCKS_PER_STEP - 1:
                sg = jnp.concatenate([sg[:, :2 * SW_BLOCK],
                                      jnp.where(n < n_tiles - 1, sg[:, 2 * SW_BLOCK:], NEG_INF)], axis=1)
            sk = sink_ref[kvh * SW_GROUP + 2 * c + hh] * LOG2E
            m = jnp.maximum(jnp.max(sg, axis=-1, keepdims=True), sk)
            es.append(jnp.exp2(sg - m).astype(BF16))
            sink_terms.append(jnp.exp2(sk - m))
        vv = jnp.concatenate([window(vp_ref, vc_ref, vn_ref, kvh, j), ones_cols], axis=1)
        o2 = jnp.dot(jnp.concatenate(es, axis=0), vv, preferred_element_type=F32)
        og = []
        for hh in range(2):
            hrows = slice(hh * SW_BLOCK, (hh + 1) * SW_BLOCK)
            og.append(o2[hrows, :LANES] / (o2[hrows, LANES:] + sink_terms[hh]))
        o_ref[0, rows, (2 * kvh + c) * LANES:(2 * kvh + c + 1) * LANES] = (
            jnp.where(lo, og[0], og[1]).astype(BF16))

    units = [(kvh, j, c) for j in range(SW_BLOCKS_PER_STEP) for kvh in range(SW_KV_HEADS) for c in range(2)]
    n_slots = s_ref.shape[0]
    for idx in range(len(units) + SW_SKEW):
        if idx < len(units):
            scores(*units[idx], idx % n_slots)
        if idx >= SW_SKEW:
            finish(*units[idx - SW_SKEW], (idx - SW_SKEW) % n_slots)


def _t5_bucket(rel):
    half = T5_BUCKETS // 2
    max_exact = half // 2
    ret = (rel > 0).astype(np.int32) * half
    n = np.abs(rel)
    large = max_exact + (np.log(np.maximum(n, 1) / max_exact)
                         / np.log(T5_MAX_DIST / max_exact) * (half - max_exact)).astype(np.int32)
    large = np.minimum(large, half - 1)
    return ret + np.where(n < max_exact, n, large)


def _swa_bucket_index():
    rel = np.arange(3 * SW_BLOCK)[None, :] - SW_BLOCK - np.arange(SW_BLOCK)[:, None]
    return np.where(np.abs(rel) <= SW_WINDOW, _t5_bucket(rel), -1).astype(np.int32)


def _swa_attn(sink, t5_table, qb, kbd, vbd):
    b, t, _ = qb.shape
    tq = SW_BLOCKS_PER_STEP * SW_BLOCK
    n_tiles = t // tq
    last_blk = t // SW_BLOCK - 1
    kv_w = kbd.shape[-1]
    qtile = pl.BlockSpec((1, tq, B_QW), lambda bi, ni: (bi, ni, 0))
    own = pl.BlockSpec((1, tq, kv_w), lambda bi, ni: (bi, ni, 0))
    prev = pl.BlockSpec((1, SW_BLOCK, kv_w),
                        lambda bi, ni: (bi, jnp.maximum(ni * SW_BLOCKS_PER_STEP - 1, 0), 0))
    nxt = pl.BlockSpec((1, SW_BLOCK, kv_w),
                       lambda bi, ni: (bi, jnp.minimum((ni + 1) * SW_BLOCKS_PER_STEP, last_blk), 0))
    smem = pl.BlockSpec(memory_space=pltpu.SMEM)
    return pl.pallas_call(
        functools.partial(_swa_kernel, n_tiles=n_tiles),
        grid=(b, n_tiles),
        in_specs=[smem, smem, _resident((SW_BLOCK, 3 * SW_BLOCK)), qtile, prev, own, nxt, prev, own, nxt],
        out_specs=qtile,
        out_shape=jax.ShapeDtypeStruct((b, t, B_QW), BF16),
        scratch_shapes=[pltpu.VMEM((SW_KV_HEADS, SW_GROUP * SW_BLOCK, 3 * SW_BLOCK), F32),
                        pltpu.VMEM((SW_SLOTS, 2 * SW_BLOCK, 3 * SW_BLOCK), F32)],
        compiler_params=_params(2),
        name="swa_attn",
    )(sink.astype(F32), t5_table.astype(F32).reshape(-1), jnp.asarray(_swa_bucket_index()),
      qb, kbd, kbd, kbd, vbd, vbd, vbd)


def _mix_kernel(x_ref, oa_ref, ob_ref, g_ref, wg_ref, wa_ref, wb_ref, wo_ref, wgate_f32, wup_f32, wdown_f32,
                x1_ref, wgate_bf, wup_bf, wdown_bf, h_scr, y_scr):
    _cast_rows(((wgate_f32, wgate_bf), (wup_f32, wup_bf), (wdown_f32, wdown_bf)))
    for rb in range(0, x_ref.shape[0], ROW_BLOCK):
        rows = slice(rb, rb + ROW_BLOCK)
        xf = x_ref[rows, :]
        h_scr[rows, :] = ((xf * _rms_scale(xf)) * g_ref[...]).astype(BF16)
        for c in range(0, D_MODEL, MXU_N):
            cs = slice(c, c + MXU_N)
            gs = slice(D_MODEL + c, D_MODEL + c + MXU_N)
            ga = jnp.dot(h_scr[rows, :], wg_ref[:, cs], preferred_element_type=F32)
            gb = jnp.dot(h_scr[rows, :], wg_ref[:, gs], preferred_element_type=F32)
            ya = jnp.dot(oa_ref[rows, :], wa_ref[:, cs], preferred_element_type=F32)
            yb = jnp.dot(ob_ref[rows, :], wb_ref[:, cs], preferred_element_type=F32)
            y_scr[rows, cs] = (jax.nn.sigmoid(ga) * ya + jax.nn.sigmoid(gb) * yb).astype(BF16)
        for c in range(0, D_MODEL, MXU_N):
            cs = slice(c, c + MXU_N)
            x1_ref[rows, cs] = x_ref[rows, cs] + jnp.dot(y_scr[rows, :], wo_ref[:, cs], preferred_element_type=F32)


def _mix_out(x2, oa, ob, g_mix, mix_weights, w_gate, w_up, w_down, layer):
    n = x2.shape[0]
    tm = TOK_TILE
    steps = n // tm
    tile = lambda w: pl.BlockSpec((tm, w), lambda i: (i, 0))
    streamed = (w_gate, w_up, w_down)
    outs = pl.pallas_call(
        _mix_kernel,
        grid=(steps,),
        in_specs=[tile(D_MODEL), tile(A_W), tile(B_QW), _resident((1, D_MODEL))]
                 + [_resident(w.shape) for w in mix_weights]
                 + [_row_stream(w.shape[1], w.shape[2], steps, layer) for w in streamed],
        out_specs=[tile(D_MODEL)]
                  + [pl.BlockSpec((w.shape[1] // steps, w.shape[2]), lambda i: (i, 0)) for w in streamed],
        out_shape=[jax.ShapeDtypeStruct((n, D_MODEL), F32)]
                  + [jax.ShapeDtypeStruct(w.shape[1:], BF16) for w in streamed],
        scratch_shapes=[pltpu.VMEM((tm, D_MODEL), BF16), pltpu.VMEM((tm, D_MODEL), BF16)],
        compiler_params=_params(1),
        name="mix_out",
    )(x2, oa, ob, g_mix, *mix_weights, w_gate, w_up, w_down)
    return outs[0], outs[1:]


def _ffn_kernel(x_ref, g_ref, wg_ref, wu_ref, wd_ref, o_ref, h_scr, a_scr):
    hidden = wg_ref.shape[1]
    for rb in range(0, x_ref.shape[0], ROW_BLOCK):
        rows = slice(rb, rb + ROW_BLOCK)
        xf = x_ref[rows, :]
        h_scr[rows, :] = ((xf * _rms_scale(xf)) * g_ref[...]).astype(BF16)
        for c in range(0, hidden, MXU_N):
            cs = slice(c, c + MXU_N)
            gate = jnp.dot(h_scr[rows, :], wg_ref[:, cs], preferred_element_type=F32)
            up = jnp.dot(h_scr[rows, :], wu_ref[:, cs], preferred_element_type=F32)
            a_scr[rows, cs] = (jax.nn.silu(gate) * up).astype(BF16)
        for c in range(0, D_MODEL, MXU_N):
            cs = slice(c, c + MXU_N)
            o_ref[rows, cs] = x_ref[rows, cs] + jnp.dot(a_scr[rows, :], wd_ref[:, cs], preferred_element_type=F32)


def _ffn(x1, g_ffn, w_gate, w_up, w_down):
    n = x1.shape[0]
    tm = TOK_TILE
    hidden = w_gate.shape[1]
    tile = pl.BlockSpec((tm, D_MODEL), lambda i: (i, 0))
    return pl.pallas_call(
        _ffn_kernel,
        grid=(n // tm,),
        in_specs=[tile, _resident((1, D_MODEL)), _resident(w_gate.shape), _resident(w_up.shape),
                  _resident(w_down.shape)],
        out_specs=tile,
        out_shape=jax.ShapeDtypeStruct((n, D_MODEL), F32),
        scratch_shapes=[pltpu.VMEM((tm, D_MODEL), BF16), pltpu.VMEM((tm, hidden), BF16)],
        compiler_params=_params(1),
        name="ffn",
    )(x1, g_ffn, w_gate, w_up, w_down)


def _layer(x2, b, t, layer, norm_mix, w_in, q_norm_a, k_norm_a, rpb_a, q_norm_b, k_norm_b, sink_b, t5_table,
           w_branch_a, w_branch_b, w_out, norm_ffn, w_gate, w_up, w_down):
    g_mix = norm_mix[layer].reshape(1, D_MODEL)
    reps = MXU_N // HEAD_DIM
    root_d = HEAD_DIM ** 0.5
    q_fold = QK_SCALE * root_d * LOG2E
    qk_gain = jnp.stack([jnp.tile(q_norm_a[layer] * q_fold, reps), jnp.tile(k_norm_a[layer] * root_d, reps),
                         jnp.tile(q_norm_b[layer] * q_fold, reps), jnp.tile(k_norm_b[layer] * root_d, reps)]
                        ).astype(F32)

    (qa, ka, va, qb, kbd, vbd), mix_weights = _in_proj(x2, g_mix, qk_gain, w_in, w_branch_a, w_branch_b,
                                                       w_out, layer)
    n = x2.shape[0]
    r3 = lambda a: a.reshape(b, t, a.shape[-1])
    oa = _na_attn(rpb_a[layer], r3(qa), r3(ka), r3(va))
    ob = _swa_attn(sink_b[layer], t5_table, r3(qb), r3(kbd), r3(vbd))
    x1, ffn_weights = _mix_out(x2, oa.reshape(n, A_W), ob.reshape(n, B_QW), g_mix, mix_weights,
                               w_gate, w_up, w_down, layer)
    return _ffn(x1, norm_ffn[layer].reshape(1, D_MODEL), *ffn_weights)


def kernel(x, norm_mix, w_in, q_norm_a, k_norm_a, rpb_a, q_norm_b, k_norm_b, sink_b, t5_table,
           w_branch_a, w_branch_b, w_out, norm_ffn, w_gate, w_up, w_down):
    b, t, d = x.shape
    x2 = x.reshape(b * t, d)
    for layer in range(norm_mix.shape[0]):
        x2 = _layer(x2, b, t, layer, norm_mix, w_in, q_norm_a, k_norm_a, rpb_a, q_norm_b, k_norm_b, sink_b,
                    t5_table, w_branch_a, w_branch_b, w_out, norm_ffn, w_gate, w_up, w_down)
    return x2.reshape(b, t, d)
```

```python
import functools

import jax
import jax.numpy as jnp
import numpy as np
from jax import lax
from jax.experimental import pallas as pl
from jax.experimental.pallas import tpu as pltpu

F32 = jnp.float32
BF16 = jnp.bfloat16

D_MODEL = 1024
HEAD_DIM = 64
GRID_W = 64
NA_HEADS = 8
NA_KH = 8
NA_KW = 16
SW_HEADS = 8
SW_KV_HEADS = 2
SW_GROUP = SW_HEADS // SW_KV_HEADS
SW_WINDOW = 128
SW_BLOCK = 128
T5_BUCKETS = 32
T5_MAX_DIST = 128
A_W = NA_HEADS * HEAD_DIM
B_QW = SW_HEADS * HEAD_DIM
B_KVW = SW_KV_HEADS * HEAD_DIM
QKV_W = 3 * A_W + B_QW + 2 * B_KVW
RMS_EPS = 1e-6
NEG_INF = -1e30
QK_SCALE = HEAD_DIM ** -0.5
LOG2E = 1.4426950408889634

LANES = 128
MXU_N = 256
VMEM_LIMIT = 56 * 1024 * 1024

TOK_TILE = 1024
ROW_BLOCK = 512
NA_ROWS_PER_STEP = 16
PROJ_SKEW = 4
NA_SKEW = 4
SW_SKEW = 4
SW_BLOCKS_PER_STEP = 8
NA_SLOTS = NA_SKEW + 1
NA_TABLE_HALF = 32
SW_SLOTS = SW_SKEW + 1


def _resident(shape):
    return pl.BlockSpec(shape, lambda *_: (0,) * len(shape), pipeline_mode=pl.Buffered(1))


def _params(n_axes):
    return pltpu.CompilerParams(dimension_semantics=("arbitrary",) * n_axes, vmem_limit_bytes=VMEM_LIMIT)


def _rms_scale(xf):
    return lax.rsqrt(jnp.mean(xf * xf, axis=-1, keepdims=True) + RMS_EPS)


def _group_rms_scale(p):
    y = p * p
    lo = lax.broadcasted_iota(jnp.int32, (p.shape[0], LANES), 1) < HEAD_DIM
    scales = []
    for c in range(0, p.shape[1], LANES):
        yc = y[:, c:c + LANES]
        s_lo = jnp.sum(jnp.where(lo, yc, 0.0), axis=-1, keepdims=True)
        s_hi = jnp.sum(jnp.where(lo, 0.0, yc), axis=-1, keepdims=True)
        r_lo = lax.rsqrt(s_lo + HEAD_DIM * RMS_EPS)
        r_hi = lax.rsqrt(s_hi + HEAD_DIM * RMS_EPS)
        scales.append(jnp.where(lo, r_lo, r_hi))
    return jnp.concatenate(scales, axis=1)


def _cast_rows(pairs):
    for src_ref, dst_ref in pairs:
        dst_ref[...] = src_ref[:, src_ref.shape[1] - dst_ref.shape[1]:].astype(BF16)


def _in_proj_kernel(x_ref, g_ref, w_ref, qk_gain_ref, wg_f32, wa_f32, wb_f32, wo_f32,
                    qa_ref, ka_ref, va_ref, qb_ref, kbd_ref, vbd_ref, wg_bf, wa_bf, wb_bf, wo_bf,
                    w_scr, h_scr, p_scr):
    @pl.when(pl.program_id(0) == 0)
    def _cast_own_weights():
        for c in range(0, QKV_W, MXU_N):
            w_scr[:, c:c + MXU_N] = w_ref[:, c:c + MXU_N].astype(BF16)

    _cast_rows(((wg_f32, wg_bf), (wa_f32, wa_bf), (wb_f32, wb_bf), (wo_f32, wo_bf)))

    def normed(out_ref, rows, c, gain_row):
        def epilogue(slot):
            p = p_scr[slot]
            scale = _group_rms_scale(p)
            out_ref[rows, c:c + MXU_N] = (p * scale * qk_gain_ref[gain_row:gain_row + 1, :]).astype(BF16)
        return epilogue

    def plain(out_ref, rows, c):
        def epilogue(slot):
            out_ref[rows, c:c + MXU_N] = p_scr[slot].astype(BF16)
        return epilogue

    def kv_b(rows):
        def epilogue(slot):
            kb = p_scr[slot, :, :B_KVW]
            kb = kb * _group_rms_scale(kb) * qk_gain_ref[3:4, :LANES]
            vb = p_scr[slot, :, B_KVW:]
            lo = lax.broadcasted_iota(jnp.int32, kb.shape, 1) < HEAD_DIM
            for src, out_ref in ((kb, kbd_ref), (vb, vbd_ref)):
                swapped = pltpu.roll(src, HEAD_DIM, axis=1)
                out_ref[rows, :LANES] = jnp.where(lo, src, swapped).astype(BF16)
                out_ref[rows, LANES:] = jnp.where(lo, swapped, src).astype(BF16)
        return epilogue

    units = []
    for rb in range(0, x_ref.shape[0], ROW_BLOCK):
        rows = slice(rb, rb + ROW_BLOCK)
        xf = x_ref[rows, :]
        h_scr[rows, :] = ((xf * _rms_scale(xf)) * g_ref[...]).astype(BF16)
        for c in range(0, A_W, MXU_N):
            units += [(rows, c, normed(qa_ref, rows, c, 0)), (rows, A_W + c, normed(ka_ref, rows, c, 1)),
                      (rows, 3 * A_W + c, normed(qb_ref, rows, c, 2))]
        units.append((rows, 3 * A_W + B_QW, kv_b(rows)))
        units += [(rows, 2 * A_W + c, plain(va_ref, rows, c)) for c in range(0, A_W, MXU_N)]
    n_slots = PROJ_SKEW + 1
    for idx in range(len(units) + PROJ_SKEW):
        if idx < len(units):
            rows, c0, _ = units[idx]
            p_scr[idx % n_slots] = jnp.dot(h_scr[rows, :], w_scr[:, c0:c0 + MXU_N], preferred_element_type=F32)
        if idx >= PROJ_SKEW:
            units[idx - PROJ_SKEW][2]((idx - PROJ_SKEW) % n_slots)


def _row_stream(n_rows, n_cols, steps, layer):
    return pl.BlockSpec((None, n_rows // steps, n_cols), lambda i: (layer, i, 0))


def _in_proj(x2, g_mix, qk_gain, w_in, w_a, w_b, w_o, layer):
    n = x2.shape[0]
    tm = TOK_TILE
    steps = n // tm
    tile = lambda w: pl.BlockSpec((tm, w), lambda i: (i, 0))
    gate_w = w_in.shape[2] - QKV_W
    out_w = (A_W, A_W, A_W, B_QW, 2 * LANES, 2 * LANES)
    streamed = ((w_in, gate_w), (w_a, D_MODEL), (w_b, D_MODEL), (w_o, D_MODEL))
    outs = pl.pallas_call(
        _in_proj_kernel,
        grid=(steps,),
        in_specs=[tile(D_MODEL), _resident((1, D_MODEL)),
                  pl.BlockSpec((None, D_MODEL, QKV_W), lambda i: (layer, 0, 0), pipeline_mode=pl.Buffered(1)),
                  _resident((4, MXU_N))]
                 + [_row_stream(w.shape[1], w.shape[2], steps, layer) for w, _ in streamed],
        out_specs=[tile(w) for w in out_w]
                  + [pl.BlockSpec((w.shape[1] // steps, cols), lambda i: (i, 0)) for w, cols in streamed],
        out_shape=[jax.ShapeDtypeStruct((n, w), BF16) for w in out_w]
                  + [jax.ShapeDtypeStruct((w.shape[1], cols), BF16) for w, cols in streamed],
        scratch_shapes=[pltpu.VMEM((D_MODEL, QKV_W), BF16), pltpu.VMEM((tm, D_MODEL), BF16),
                        pltpu.VMEM((PROJ_SKEW + 1, ROW_BLOCK, MXU_N), F32)],
        compiler_params=_params(1),
        name="in_proj",
    )(x2, g_mix, w_in, qk_gain, w_in, w_a, w_b, w_o)
    return outs[:6], outs[6:]


def _na_kernel(rpb_ref, dc_ref, q_ref, k_ref, v_ref, o_ref, bias_ref, s_ref, *, rows):
    t = pl.program_id(1)
    lo = lax.broadcasted_iota(jnp.int32, (GRID_W, LANES), 1) < HEAD_DIM
    nt = (((1,), (1,)), ((), ()))
    n_slab = 2 * NA_KH - 2

    @pl.when((pl.program_id(0) == 0) & (t == 0))
    def _build_bias():
        offs = dc_ref[...]
        in_win = offs >= 0
        take = jnp.maximum(offs, 0)

        def build(idx, carry):
            h = idx // n_slab
            d = idx - h * n_slab
            row = jnp.broadcast_to(rpb_ref[pl.ds(d * NA_HEADS + h, 1), :] * LOG2E, (GRID_W, LANES))
            slab = jnp.where(in_win, jnp.take_along_axis(row, take, axis=1), NEG_INF)
            row0 = pl.multiple_of((h % 2) * GRID_W, GRID_W)
            bias_ref[h // 2, d, pl.ds(row0, GRID_W), :] = slab
            return carry

        lax.fori_loop(0, NA_HEADS * n_slab, build, 0, unroll=NA_HEADS)

    win0 = _na_window_start(t, rows)
    row_info = []
    for i in range(NA_ROWS_PER_STEP):
        r = t * NA_ROWS_PER_STEP + i
        rs = jnp.clip(r - NA_KH // 2, 0, rows - NA_KH)
        row_info.append(((NA_KH - 1) - (r - rs), pl.multiple_of((rs - win0) * GRID_W, GRID_W)))

    def scores(i, p, slot):
        d0, kstart = row_info[i]
        cols = slice(p * LANES, (p + 1) * LANES)
        q = q_ref[0, i * GRID_W:(i + 1) * GRID_W, cols]
        zero = jnp.zeros_like(q)
        q2 = jnp.concatenate([jnp.where(lo, q, zero), jnp.where(lo, zero, q)], axis=0)
        kk = k_ref[0, pl.ds(kstart, NA_KH * GRID_W), cols]
        s = lax.dot_general(q2, kk, nt, preferred_element_type=F32)
        for v in range(NA_KH // 2):
            vcols = slice(v * LANES, (v + 1) * LANES)
            s_ref[slot, :, vcols] = s[:, vcols] + bias_ref[p, d0 + 2 * v]

    ones_cols = jnp.ones((NA_KH * GRID_W, LANES), BF16)

    def finish(i, p, slot):
        _, kstart = row_info[i]
        cols = slice(p * LANES, (p + 1) * LANES)
        m = jnp.max(s_ref[slot], axis=-1, keepdims=True)
        e = jnp.exp2(s_ref[slot] - m).astype(BF16)
        vv = jnp.concatenate([v_ref[0, pl.ds(kstart, NA_KH * GRID_W), cols], ones_cols], axis=1)
        o2 = jnp.dot(e, vv, preferred_element_type=F32)
        o2 = o2[:, :LANES] / o2[:, LANES:]
        o = jnp.where(lo, o2[:GRID_W], o2[GRID_W:])
        o_ref[0, i * GRID_W:(i + 1) * GRID_W, cols] = o.astype(BF16)

    units = [(i, p) for i in range(NA_ROWS_PER_STEP) for p in range(NA_HEADS // 2)]
    n_slots = s_ref.shape[0]
    for idx in range(len(units) + NA_SKEW):
        if idx < len(units):
            scores(*units[idx], idx % n_slots)
        if idx >= NA_SKEW:
            finish(*units[idx - NA_SKEW], (idx - NA_SKEW) % n_slots)


def _na_window_start(step, rows):
    return jnp.clip(step * NA_ROWS_PER_STEP - NA_KH // 2, 0, rows - (NA_ROWS_PER_STEP + NA_KH))


def _na_col_offsets():
    c = np.arange(GRID_W)
    kc = c[None, :]
    win_start = np.clip(c - NA_KW // 2, 0, GRID_W - NA_KW)[:, None]
    in_win = (kc >= win_start) & (kc < win_start + NA_KW)
    dc = kc - c[:, None] + (NA_KW - 1)
    return np.concatenate([np.where(in_win, dc, -1), np.where(in_win, dc + NA_TABLE_HALF, -1)],
                          axis=1).astype(np.int32)


def _na_attn(rpb, qa, ka, va):
    b, t, _ = qa.shape
    rows = t // GRID_W
    tq = NA_ROWS_PER_STEP * GRID_W
    tile = pl.BlockSpec((1, tq, A_W), lambda bi, ti: (bi, ti, 0))
    window = pl.BlockSpec((pl.Element(1), pl.Element((NA_ROWS_PER_STEP + NA_KH) * GRID_W), pl.Element(A_W)),
                          lambda bi, ti: (bi, _na_window_start(ti, rows) * GRID_W, 0))
    per_head = jnp.transpose(rpb.astype(F32), (0, 2, 1))
    pad = lambda a, w: jnp.pad(a, ((0, 0), (0, 0), (0, w - a.shape[-1])))
    table = jnp.concatenate([pad(per_head[:-1], NA_TABLE_HALF), pad(per_head[1:], LANES - NA_TABLE_HALF)], axis=-1)
    table = table.reshape(-1, LANES)
    return pl.pallas_call(
        functools.partial(_na_kernel, rows=rows),
        grid=(b, rows // NA_ROWS_PER_STEP),
        in_specs=[_resident(table.shape), _resident((GRID_W, LANES)), tile, window, window],
        out_specs=tile,
        out_shape=jax.ShapeDtypeStruct((b, t, A_W), BF16),
        scratch_shapes=[pltpu.VMEM((NA_HEADS // 2, 2 * NA_KH - 2, LANES, LANES), F32),
                        pltpu.VMEM((NA_SLOTS, 2 * GRID_W, NA_KH * GRID_W), F32)],
        compiler_params=_params(2),
        name="na_attn",
    )(table, jnp.asarray(_na_col_offsets()), qa, ka, va)


def _swa_kernel(sink_ref, t5_ref, bucket_ref, q_ref, kp_ref, kc_ref, kn_ref, vp_ref, vc_ref, vn_ref, o_ref,
                bias_ref, s_ref, *, n_tiles):
    n = pl.program_id(1)
    lo = lax.broadcasted_iota(jnp.int32, (SW_BLOCK, LANES), 1) < HEAD_DIM
    nt = (((1,), (1,)), ((), ()))

    @pl.when((pl.program_id(0) == 0) & (n == 0))
    def _build_bias():
        bucket = bucket_ref[...]

        def build(h, carry):
            acc = jnp.full(bucket.shape, NEG_INF, F32)
            for bkt in range(T5_BUCKETS):
                acc = jnp.where(bucket == bkt, t5_ref[bkt * SW_HEADS + h] * LOG2E, acc)
            row0 = pl.multiple_of((h % SW_GROUP) * SW_BLOCK, SW_BLOCK)
            bias_ref[h // SW_GROUP, pl.ds(row0, SW_BLOCK), :] = acc
            return carry

        lax.fori_loop(0, SW_HEADS, build, 0)

    def window(prev_ref, own_ref, next_ref, kvh, j):
        kcols = slice(kvh * LANES, (kvh + 1) * LANES)
        blk = lambda b: own_ref[0, b * SW_BLOCK:(b + 1) * SW_BLOCK, kcols]
        first = prev_ref[0, :, kcols] if j == 0 else blk(j - 1)
        last = next_ref[0, :, kcols] if j == SW_BLOCKS_PER_STEP - 1 else blk(j + 1)
        return jnp.concatenate([first, blk(j), last], axis=0)

    def scores(kvh, j, c, slot):
        rows = slice(j * SW_BLOCK, (j + 1) * SW_BLOCK)
        qb = q_ref[0, rows, (2 * kvh + c) * LANES:(2 * kvh + c + 1) * LANES]
        zero = jnp.zeros_like(qb)
        q2 = jnp.concatenate([jnp.where(lo, qb, zero), jnp.where(lo, zero, qb)], axis=0)
        s_ref[slot] = lax.dot_general(q2, window(kp_ref, kc_ref, kn_ref, kvh, j), nt,
                                      preferred_element_type=F32)

    ones_cols = jnp.ones((3 * SW_BLOCK, LANES), BF16)

    def finish(kvh, j, c, slot):
        rows = slice(j * SW_BLOCK, (j + 1) * SW_BLOCK)
        es, sink_terms = [], []
        for hh in range(2):
            hrows = slice(hh * SW_BLOCK, (hh + 1) * SW_BLOCK)
            brows = slice((2 * c + hh) * SW_BLOCK, (2 * c + hh + 1) * SW_BLOCK)
            sg = s_ref[slot, hrows, :] + bias_ref[kvh, brows, :]
            if j == 0:
                sg = jnp.concatenate([jnp.where(n > 0, sg[:, :SW_BLOCK], NEG_INF), sg[:, SW_BLOCK:]], axis=1)
            if j == SW_BLOCKS_PER_STEP - 1:
                sg = jnp.concatenate([sg[:, :2 * SW_BLOCK],
                                      jnp.where(n < n_tiles - 1, sg[:, 2 * SW_BLOCK:], NEG_INF)], axis=1)
            sk = sink_ref[kvh * SW_GROUP + 2 * c + hh] * LOG2E
            m = jnp.maximum(jnp.max(sg, axis=-1, keepdims=True), sk)
            es.append(jnp.exp2(sg - m).astype(BF16))
            sink_terms.append(jnp.exp2(sk - m))
        vv = jnp.concatenate([window(vp_ref, vc_ref, vn_ref, kvh, j), ones_cols], axis=1)
        o2 = jnp.dot(jnp.concatenate(es, axis=0), vv, preferred_element_type=F32)
        og = []
        for hh in range(2):
            hrows = slice(hh * SW_BLOCK, (hh + 1) * SW_BLOCK)
            og.append(o2[hrows, :LANES] / (o2[hrows, LANES:] + sink_terms[hh]))
        o_ref[0, rows, (2 * kvh + c) * LANES:(2 * kvh + c + 1) * LANES] = (
            jnp.where(lo, og[0], og[1]).astype(BF16))

    units = [(kvh, j, c) for j in range(SW_BLOCKS_PER_STEP) for kvh in range(SW_KV_HEADS) for c in range(2)]
    n_slots = s_ref.shape[0]
    for idx in range(len(units) + SW_SKEW):
        if idx < len(units):
            scores(*units[idx], idx % n_slots)
        if idx >= SW_SKEW:
            finish(*units[idx - SW_SKEW], (idx - SW_SKEW) % n_slots)


def _t5_bucket(rel):
    half = T5_BUCKETS // 2
    max_exact = half // 2
    ret = (rel > 0).astype(np.int32) * half
    n = np.abs(rel)
    large = max_exact + (np.log(np.maximum(n, 1) / max_exact)
                         / np.log(T5_MAX_DIST / max_exact) * (half - max_exact)).astype(np.int32)
    large = np.minimum(large, half - 1)
    return ret + np.where(n < max_exact, n, large)


def _swa_bucket_index():
    rel = np.arange(3 * SW_BLOCK)[None, :] - SW_BLOCK - np.arange(SW_BLOCK)[:, None]
    return np.where(np.abs(rel) <= SW_WINDOW, _t5_bucket(rel), -1).astype(np.int32)


def _swa_attn(sink, t5_table, qb, kbd, vbd):
    b, t, _ = qb.shape
    tq = SW_BLOCKS_PER_STEP * SW_BLOCK
    n_tiles = t // tq
    last_blk = t // SW_BLOCK - 1
    kv_w = kbd.shape[-1]
    qtile = pl.BlockSpec((1, tq, B_QW), lambda bi, ni: (bi, ni, 0))
    own = pl.BlockSpec((1, tq, kv_w), lambda bi, ni: (bi, ni, 0))
    prev = pl.BlockSpec((1, SW_BLOCK, kv_w),
                        lambda bi, ni: (bi, jnp.maximum(ni * SW_BLOCKS_PER_STEP - 1, 0), 0))
    nxt = pl.BlockSpec((1, SW_BLOCK, kv_w),
                       lambda bi, ni: (bi, jnp.minimum((ni + 1) * SW_BLOCKS_PER_STEP, last_blk), 0))
    smem = pl.BlockSpec(memory_space=pltpu.SMEM)
    return pl.pallas_call(
        functools.partial(_swa_kernel, n_tiles=n_tiles),
        grid=(b, n_tiles),
        in_specs=[smem, smem, _resident((SW_BLOCK, 3 * SW_BLOCK)), qtile, prev, own, nxt, prev, own, nxt],
        out_specs=qtile,
        out_shape=jax.ShapeDtypeStruct((b, t, B_QW), BF16),
        scratch_shapes=[pltpu.VMEM((SW_KV_HEADS, SW_GROUP * SW_BLOCK, 3 * SW_BLOCK), F32),
                        pltpu.VMEM((SW_SLOTS, 2 * SW_BLOCK, 3 * SW_BLOCK), F32)],
        compiler_params=_params(2),
        name="swa_attn",
    )(sink.astype(F32), t5_table.astype(F32).reshape(-1), jnp.asarray(_swa_bucket_index()),
      qb, kbd, kbd, kbd, vbd, vbd, vbd)


def _mix_kernel(x_ref, oa_ref, ob_ref, g_ref, wg_ref, wa_ref, wb_ref, wo_ref, wgate_f32, wup_f32, wdown_f32,
                x1_ref, wgate_bf, wup_bf, wdown_bf, h_scr, y_scr):
    _cast_rows(((wgate_f32, wgate_bf), (wup_f32, wup_bf), (wdown_f32, wdown_bf)))
    for rb in range(0, x_ref.shape[0], ROW_BLOCK):
        rows = slice(rb, rb + ROW_BLOCK)
        xf = x_ref[rows, :]
        h_scr[rows, :] = ((xf * _rms_scale(xf)) * g_ref[...]).astype(BF16)
        for c in range(0, D_MODEL, MXU_N):
            cs = slice(c, c + MXU_N)
            gs = slice(D_MODEL + c, D_MODEL + c + MXU_N)
            ga = jnp.dot(h_scr[rows, :], wg_ref[:, cs], preferred_element_type=F32)
            gb = jnp.dot(h_scr[rows, :], wg_ref[:, gs], preferred_element_type=F32)
            ya = jnp.dot(oa_ref[rows, :], wa_ref[:, cs], preferred_element_type=F32)
            yb = jnp.dot(ob_ref[rows, :], wb_ref[:, cs], preferred_element_type=F32)
            y_scr[rows, cs] = (jax.nn.sigmoid(ga) * ya + jax.nn.sigmoid(gb) * yb).astype(BF16)
        for c in range(0, D_MODEL, MXU_N):
            cs = slice(c, c + MXU_N)
            x1_ref[rows, cs] = x_ref[rows, cs] + jnp.dot(y_scr[rows, :], wo_ref[:, cs], preferred_element_type=F32)


def _mix_out(x2, oa, ob, g_mix, mix_weights, w_gate, w_up, w_down, layer):
    n = x2.shape[0]
    tm = TOK_TILE
    steps = n // tm
    tile = lambda w: pl.BlockSpec((tm, w), lambda i: (i, 0))
    streamed = (w_gate, w_up, w_down)
    outs = pl.pallas_call(
        _mix_kernel,
        grid=(steps,),
        in_specs=[tile(D_MODEL), tile(A_W), tile(B_QW), _resident((1, D_MODEL))]
                 + [_resident(w.shape) for w in mix_weights]
                 + [_row_stream(w.shape[1], w.shape[2], steps, layer) for w in streamed],
        out_specs=[tile(D_MODEL)]
                  + [pl.BlockSpec((w.shape[1] // steps, w.shape[2]), lambda i: (i, 0)) for w in streamed],
        out_shape=[jax.ShapeDtypeStruct((n, D_MODEL), F32)]
                  + [jax.ShapeDtypeStruct(w.shape[1:], BF16) for w in streamed],
        scratch_shapes=[pltpu.VMEM((tm, D_MODEL), BF16), pltpu.VMEM((tm, D_MODEL), BF16)],
        compiler_params=_params(1),
        name="mix_out",
    )(x2, oa, ob, g_mix, *mix_weights, w_gate, w_up, w_down)
    return outs[0], outs[1:]


def _ffn_kernel(x_ref, g_ref, wg_ref, wu_ref, wd_ref, o_ref, h_scr, a_scr):
    hidden = wg_ref.shape[1]
    for rb in range(0, x_ref.shape[0], ROW_BLOCK):
        rows = slice(rb, rb + ROW_BLOCK)
        xf = x_ref[rows, :]
        h_scr[rows, :] = ((xf * _rms_scale(xf)) * g_ref[...]).astype(BF16)
        for c in range(0, hidden, MXU_N):
            cs = slice(c, c + MXU_N)
            gate = jnp.dot(h_scr[rows, :], wg_ref[:, cs], preferred_element_type=F32)
            up = jnp.dot(h_scr[rows, :], wu_ref[:, cs], preferred_element_type=F32)
            a_scr[rows, cs] = (jax.nn.silu(gate) * up).astype(BF16)
        for c in range(0, D_MODEL, MXU_N):
            cs = slice(c, c + MXU_N)
            o_ref[rows, cs] = x_ref[rows, cs] + jnp.dot(a_scr[rows, :], wd_ref[:, cs], preferred_element_type=F32)


def _ffn(x1, g_ffn, w_gate, w_up, w_down):
    n = x1.shape[0]
    tm = TOK_TILE
    hidden = w_gate.shape[1]
    tile = pl.BlockSpec((tm, D_MODEL), lambda i: (i, 0))
    return pl.pallas_call(
        _ffn_kernel,
        grid=(n // tm,),
        in_specs=[tile, _resident((1, D_MODEL)), _resident(w_gate.shape), _resident(w_up.shape),
                  _resident(w_down.shape)],
        out_specs=tile,
        out_shape=jax.ShapeDtypeStruct((n, D_MODEL), F32),
        scratch_shapes=[pltpu.VMEM((tm, D_MODEL), BF16), pltpu.VMEM((tm, hidden), BF16)],
        compiler_params=_params(1),
        name="ffn",
    )(x1, g_ffn, w_gate, w_up, w_down)


def _layer(x2, b, t, layer, norm_mix, w_in, q_norm_a, k_norm_a, rpb_a, q_norm_b, k_norm_b, sink_b, t5_table,
           w_branch_a, w_branch_b, w_out, norm_ffn, w_gate, w_up, w_down):
    g_mix = norm_mix[layer].reshape(1, D_MODEL)
    reps = MXU_N // HEAD_DIM
    root_d = HEAD_DIM ** 0.5
    q_fold = QK_SCALE * root_d * LOG2E
    qk_gain = jnp.stack([jnp.tile(q_norm_a[layer] * q_fold, reps), jnp.tile(k_norm_a[layer] * root_d, reps),
                         jnp.tile(q_norm_b[layer] * q_fold, reps), jnp.tile(k_norm_b[layer] * root_d, reps)]
                        ).astype(F32)

    (qa, ka, va, qb, kbd, vbd), mix_weights = _in_proj(x2, g_mix, qk_gain, w_in, w_branch_a, w_branch_b,
                                                       w_out, layer)
    n = x2.shape[0]
    r3 = lambda a: a.reshape(b, t, a.shape[-1])
    oa = _na_attn(rpb_a[layer], r3(qa), r3(ka), r3(va))
    ob = _swa_attn(sink_b[layer], t5_table, r3(qb), r3(kbd), r3(vbd))
    x1, ffn_weights = _mix_out(x2, oa.reshape(n, A_W), ob.reshape(n, B_QW), g_mix, mix_weights,
                               w_gate, w_up, w_down, layer)
    return _ffn(x1, norm_ffn[layer].reshape(1, D_MODEL), *ffn_weights)


def kernel(x, norm_mix, w_in, q_norm_a, k_norm_a, rpb_a, q_norm_b, k_norm_b, sink_b, t5_table,
           w_branch_a, w_branch_b, w_out, norm_ffn, w_gate, w_up, w_down):
    b, t, d = x.shape
    x2 = x.reshape(b * t, d)
    for layer in range(norm_mix.shape[0]):
        x2 = _layer(x2, b, t, layer, norm_mix, w_in, q_norm_a, k_norm_a, rpb_a, q_norm_b, k_norm_b, sink_b,
                    t5_table, w_branch_a, w_branch_b, w_out, norm_ffn, w_gate, w_up, w_down)
    return x2.reshape(b, t, d)
```

```python
import functools

import jax
import jax.numpy as jnp
import numpy as np
from jax import lax
from jax.experimental import pallas as pl
from jax.experimental.pallas import tpu as pltpu

F32 = jnp.float32
BF16 = jnp.bfloat16

D_MODEL = 1024
HEAD_DIM = 64
GRID_W = 64
NA_HEADS = 8
NA_KH = 8
NA_KW = 16
SW_HEADS = 8
SW_KV_HEADS = 2
SW_GROUP = SW_HEADS // SW_KV_HEADS
SW_WINDOW = 128
SW_BLOCK = 128
T5_BUCKETS = 32
T5_MAX_DIST = 128
A_W = NA_HEADS * HEAD_DIM
B_QW = SW_HEADS * HEAD_DIM
B_KVW = SW_KV_HEADS * HEAD_DIM
QKV_W = 3 * A_W + B_QW + 2 * B_KVW
RMS_EPS = 1e-6
NEG_INF = -1e30
QK_SCALE = HEAD_DIM ** -0.5
LOG2E = 1.4426950408889634

LANES = 128
MXU_N = 256
VMEM_LIMIT = 56 * 1024 * 1024

TOK_TILE = 1024
ROW_BLOCK = 512
NA_ROWS_PER_STEP = 16
PROJ_SKEW = 4
NA_SKEW = 4
SW_SKEW = 4
SW_BLOCKS_PER_STEP = 8
NA_SLOTS = NA_SKEW + 1
NA_TABLE_HALF = 32
SW_SLOTS = SW_SKEW + 1


def _resident(shape):
    return pl.BlockSpec(shape, lambda *_: (0,) * len(shape), pipeline_mode=pl.Buffered(1))


def _params(n_axes):
    return pltpu.CompilerParams(dimension_semantics=("arbitrary",) * n_axes, vmem_limit_bytes=VMEM_LIMIT)


def _rms_scale(xf):
    return lax.rsqrt(jnp.mean(xf * xf, axis=-1, keepdims=True) + RMS_EPS)


def _group_rms_scale(p):
    y = p * p
    lo = lax.broadcasted_iota(jnp.int32, (p.shape[0], LANES), 1) < HEAD_DIM
    scales = []
    for c in range(0, p.shape[1], LANES):
        yc = y[:, c:c + LANES]
        s_lo = jnp.sum(jnp.where(lo, yc, 0.0), axis=-1, keepdims=True)
        s_hi = jnp.sum(jnp.where(lo, 0.0, yc), axis=-1, keepdims=True)
        r_lo = lax.rsqrt(s_lo + HEAD_DIM * RMS_EPS)
        r_hi = lax.rsqrt(s_hi + HEAD_DIM * RMS_EPS)
        scales.append(jnp.where(lo, r_lo, r_hi))
    return jnp.concatenate(scales, axis=1)


def _cast_rows(pairs):
    for src_ref, dst_ref in pairs:
        dst_ref[...] = src_ref[:, src_ref.shape[1] - dst_ref.shape[1]:].astype(BF16)


def _in_proj_kernel(x_ref, g_ref, w_ref, qk_gain_ref, wg_f32, wa_f32, wb_f32, wo_f32,
                    qa_ref, ka_ref, va_ref, qb_ref, kbd_ref, vbd_ref, wg_bf, wa_bf, wb_bf, wo_bf,
                    w_scr, h_scr, p_scr):
    @pl.when(pl.program_id(0) == 0)
    def _cast_own_weights():
        for c in range(0, QKV_W, MXU_N):
            w_scr[:, c:c + MXU_N] = w_ref[:, c:c + MXU_N].astype(BF16)

    _cast_rows(((wg_f32, wg_bf), (wa_f32, wa_bf), (wb_f32, wb_bf), (wo_f32, wo_bf)))

    def normed(out_ref, rows, c, gain_row):
        def epilogue(slot):
            p = p_scr[slot]
            scale = _group_rms_scale(p)
            out_ref[rows, c:c + MXU_N] = (p * scale * qk_gain_ref[gain_row:gain_row + 1, :]).astype(BF16)
        return epilogue

    def plain(out_ref, rows, c):
        def epilogue(slot):
            out_ref[rows, c:c + MXU_N] = p_scr[slot].astype(BF16)
        return epilogue

    def kv_b(rows):
        def epilogue(slot):
            kb = p_scr[slot, :, :B_KVW]
            kb = kb * _group_rms_scale(kb) * qk_gain_ref[3:4, :LANES]
            vb = p_scr[slot, :, B_KVW:]
            lo = lax.broadcasted_iota(jnp.int32, kb.shape, 1) < HEAD_DIM
            for src, out_ref in ((kb, kbd_ref), (vb, vbd_ref)):
                swapped = pltpu.roll(src, HEAD_DIM, axis=1)
                out_ref[rows, :LANES] = jnp.where(lo, src, swapped).astype(BF16)
                out_ref[rows, LANES:] = jnp.where(lo, swapped, src).astype(BF16)
        return epilogue

    units = []
    for rb in range(0, x_ref.shape[0], ROW_BLOCK):
        rows = slice(rb, rb + ROW_BLOCK)
        xf = x_ref[rows, :]
        h_scr[rows, :] = ((xf * _rms_scale(xf)) * g_ref[...]).astype(BF16)
        for c in range(0, A_W, MXU_N):
            units += [(rows, c, normed(qa_ref, rows, c, 0)), (rows, A_W + c, normed(ka_ref, rows, c, 1)),
                      (rows, 3 * A_W + c, normed(qb_ref, rows, c, 2))]
        units.append((rows, 3 * A_W + B_QW, kv_b(rows)))
        units += [(rows, 2 * A_W + c, plain(va_ref, rows, c)) for c in range(0, A_W, MXU_N)]
    n_slots = PROJ_SKEW + 1
    for idx in range(len(units) + PROJ_SKEW):
        if idx < len(units):
            rows, c0, _ = units[idx]
            p_scr[idx % n_slots] = jnp.dot(h_scr[rows, :], w_scr[:, c0:c0 + MXU_N], preferred_element_type=F32)
        if idx >= PROJ_SKEW:
            units[idx - PROJ_SKEW][2]((idx - PROJ_SKEW) % n_slots)


def _row_stream(n_rows, n_cols, steps, layer):
    return pl.BlockSpec((None, n_rows // steps, n_cols), lambda i: (layer, i, 0))


def _in_proj(x2, g_mix, qk_gain, w_in, w_a, w_b, w_o, layer):
    n = x2.shape[0]
    tm = TOK_TILE
    steps = n // tm
    tile = lambda w: pl.BlockSpec((tm, w), lambda i: (i, 0))
    gate_w = w_in.shape[2] - QKV_W
    out_w = (A_W, A_W, A_W, B_QW, 2 * LANES, 2 * LANES)
    streamed = ((w_in, gate_w), (w_a, D_MODEL), (w_b, D_MODEL), (w_o, D_MODEL))
    outs = pl.pallas_call(
        _in_proj_kernel,
        grid=(steps,),
        in_specs=[tile(D_MODEL), _resident((1, D_MODEL)),
                  pl.BlockSpec((None, D_MODEL, QKV_W), lambda i: (layer, 0, 0), pipeline_mode=pl.Buffered(1)),
                  _resident((4, MXU_N))]
                 + [_row_stream(w.shape[1], w.shape[2], steps, layer) for w, _ in streamed],
        out_specs=[tile(w) for w in out_w]
                  + [pl.BlockSpec((w.shape[1] // steps, cols), lambda i: (i, 0)) for w, cols in streamed],
        out_shape=[jax.ShapeDtypeStruct((n, w), BF16) for w in out_w]
                  + [jax.ShapeDtypeStruct((w.shape[1], cols), BF16) for w, cols in streamed],
        scratch_shapes=[pltpu.VMEM((D_MODEL, QKV_W), BF16), pltpu.VMEM((tm, D_MODEL), BF16),
                        pltpu.VMEM((PROJ_SKEW + 1, ROW_BLOCK, MXU_N), F32)],
        compiler_params=_params(1),
        name="in_proj",
    )(x2, g_mix, w_in, qk_gain, w_in, w_a, w_b, w_o)
    return outs[:6], outs[6:]


def _na_kernel(rpb_ref, dc_ref, q_ref, k_ref, v_ref, o_ref, bias_ref, s_ref, *, rows):
    t = pl.program_id(1)
    lo = lax.broadcasted_iota(jnp.int32, (GRID_W, LANES), 1) < HEAD_DIM
    nt = (((1,), (1,)), ((), ()))
    n_slab = 2 * NA_KH - 2

    @pl.when((pl.program_id(0) == 0) & (t == 0))
    def _build_bias():
        offs = dc_ref[...]
        in_win = offs >= 0
        take = jnp.maximum(offs, 0)

        def build(idx, carry):
            h = idx // n_slab
            d = idx - h * n_slab
            row = jnp.broadcast_to(rpb_ref[pl.ds(d * NA_HEADS + h, 1), :] * LOG2E, (GRID_W, LANES))
            slab = jnp.where(in_win, jnp.take_along_axis(row, take, axis=1), NEG_INF)
            row0 = pl.multiple_of((h % 2) * GRID_W, GRID_W)
            bias_ref[h // 2, d, pl.ds(row0, GRID_W), :] = slab
            return carry

        lax.fori_loop(0, NA_HEADS * n_slab, build, 0, unroll=NA_HEADS)

    win0 = _na_window_start(t, rows)
    row_info = []
    for i in range(NA_ROWS_PER_STEP):
        r = t * NA_ROWS_PER_STEP + i
        rs = jnp.clip(r - NA_KH // 2, 0, rows - NA_KH)
        row_info.append(((NA_KH - 1) - (r - rs), pl.multiple_of((rs - win0) * GRID_W, GRID_W)))

    def scores(i, p, slot):
        d0, kstart = row_info[i]
        cols = slice(p * LANES, (p + 1) * LANES)
        q = q_ref[0, i * GRID_W:(i + 1) * GRID_W, cols]
        zero = jnp.zeros_like(q)
        q2 = jnp.concatenate([jnp.where(lo, q, zero), jnp.where(lo, zero, q)], axis=0)
        kk = k_ref[0, pl.ds(kstart, NA_KH * GRID_W), cols]
        s = lax.dot_general(q2, kk, nt, preferred_element_type=F32)
        for v in range(NA_KH // 2):
            vcols = slice(v * LANES, (v + 1) * LANES)
            s_ref[slot, :, vcols] = s[:, vcols] + bias_ref[p, d0 + 2 * v]

    ones_cols = jnp.ones((NA_KH * GRID_W, LANES), BF16)

    def finish(i, p, slot):
        _, kstart = row_info[i]
        cols = slice(p * LANES, (p + 1) * LANES)
        m = jnp.max(s_ref[slot], axis=-1, keepdims=True)
        e = jnp.exp2(s_ref[slot] - m).astype(BF16)
        vv = jnp.concatenate([v_ref[0, pl.ds(kstart, NA_KH * GRID_W), cols], ones_cols], axis=1)
        o2 = jnp.dot(e, vv, preferred_element_type=F32)
        o2 = o2[:, :LANES] / o2[:, LANES:]
        o = jnp.where(lo, o2[:GRID_W], o2[GRID_W:])
        o_ref[0, i * GRID_W:(i + 1) * GRID_W, cols] = o.astype(BF16)

    units = [(i, p) for i in range(NA_ROWS_PER_STEP) for p in range(NA_HEADS // 2)]
    n_slots = s_ref.shape[0]
    for idx in range(len(units) + NA_SKEW):
        if idx < len(units):
            scores(*units[idx], idx % n_slots)
        if idx >= NA_SKEW:
            finish(*units[idx - NA_SKEW], (idx - NA_SKEW) % n_slots)


def _na_window_start(step, rows):
    return jnp.clip(step * NA_ROWS_PER_STEP - NA_KH // 2, 0, rows - (NA_ROWS_PER_STEP + NA_KH))


def _na_col_offsets():
    c = np.arange(GRID_W)
    kc = c[None, :]
    win_start = np.clip(c - NA_KW // 2, 0, GRID_W - NA_KW)[:, None]
    in_win = (kc >= win_start) & (kc < win_start + NA_KW)
    dc = kc - c[:, None] + (NA_KW - 1)
    return np.concatenate([np.where(in_win, dc, -1), np.where(in_win, dc + NA_TABLE_HALF, -1)],
                          axis=1).astype(np.int32)


def _na_attn(rpb, qa, ka, va):
    b, t, _ = qa.shape
    rows = t // GRID_W
    tq = NA_ROWS_PER_STEP * GRID_W
    tile = pl.BlockSpec((1, tq, A_W), lambda bi, ti: (bi, ti, 0))
    window = pl.BlockSpec((pl.Element(1), pl.Element((NA_ROWS_PER_STEP + NA_KH) * GRID_W), pl.Element(A_W)),
                          lambda bi, ti: (bi, _na_window_start(ti, rows) * GRID_W, 0))
    per_head = jnp.transpose(rpb.astype(F32), (0, 2, 1))
    pad = lambda a, w: jnp.pad(a, ((0, 0), (0, 0), (0, w - a.shape[-1])))
    table = jnp.concatenate([pad(per_head[:-1], NA_TABLE_HALF), pad(per_head[1:], LANES - NA_TABLE_HALF)], axis=-1)
    table = table.reshape(-1, LANES)
    return pl.pallas_call(
        functools.partial(_na_kernel, rows=rows),
        grid=(b, rows // NA_ROWS_PER_STEP),
        in_specs=[_resident(table.shape), _resident((GRID_W, LANES)), tile, window, window],
        out_specs=tile,
        out_shape=jax.ShapeDtypeStruct((b, t, A_W), BF16),
        scratch_shapes=[pltpu.VMEM((NA_HEADS // 2, 2 * NA_KH - 2, LANES, LANES), F32),
                        pltpu.VMEM((NA_SLOTS, 2 * GRID_W, NA_KH * GRID_W), F32)],
        compiler_params=_params(2),
        name="na_attn",
    )(table, jnp.asarray(_na_col_offsets()), qa, ka, va)


def _swa_kernel(sink_ref, t5_ref, bucket_ref, q_ref, kp_ref, kc_ref, kn_ref, vp_ref, vc_ref, vn_ref, o_ref,
                bias_ref, s_ref, *, n_tiles):
    n = pl.program_id(1)
    lo = lax.broadcasted_iota(jnp.int32, (SW_BLOCK, LANES), 1) < HEAD_DIM
    nt = (((1,), (1,)), ((), ()))

    @pl.when((pl.program_id(0) == 0) & (n == 0))
    def _build_bias():
        bucket = bucket_ref[...]

        def build(h, carry):
            acc = jnp.full(bucket.shape, NEG_INF, F32)
            for bkt in range(T5_BUCKETS):
                acc = jnp.where(bucket == bkt, t5_ref[bkt * SW_HEADS + h] * LOG2E, acc)
            row0 = pl.multiple_of((h % SW_GROUP) * SW_BLOCK, SW_BLOCK)
            bias_ref[h // SW_GROUP, pl.ds(row0, SW_BLOCK), :] = acc
            return carry

        lax.fori_loop(0, SW_HEADS, build, 0)

    def window(prev_ref, own_ref, next_ref, kvh, j):
        kcols = slice(kvh * LANES, (kvh + 1) * LANES)
        blk = lambda b: own_ref[0, b * SW_BLOCK:(b + 1) * SW_BLOCK, kcols]
        first = prev_ref[0, :, kcols] if j == 0 else blk(j - 1)
        last = next_ref[0, :, kcols] if j == SW_BLOCKS_PER_STEP - 1 else blk(j + 1)
        return jnp.concatenate([first, blk(j), last], axis=0)

    def scores(kvh, j, c, hh, slot):
        rows = slice(j * SW_BLOCK, (j + 1) * SW_BLOCK)
        qb = q_ref[0, rows, (2 * kvh + c) * LANES:(2 * kvh + c + 1) * LANES]
        q1 = jnp.where(lo if hh == 0 else ~lo, qb, jnp.zeros_like(qb))
        s_ref[slot] = lax.dot_general(q1, window(kp_ref, kc_ref, kn_ref, kvh, j), nt,
                                      preferred_element_type=F32)

    ones_cols = jnp.ones((3 * SW_BLOCK, LANES), BF16)

    def finish(kvh, j, c, hh, slot):
        rows = slice(j * SW_BLOCK, (j + 1) * SW_BLOCK)
        head = kvh * SW_GROUP + 2 * c + hh
        brows = slice((2 * c + hh) * SW_BLOCK, (2 * c + hh + 1) * SW_BLOCK)
        sg = s_ref[slot] + bias_ref[kvh, brows, :]
        if j == 0:
            sg = jnp.concatenate([jnp.where(n > 0, sg[:, :SW_BLOCK], NEG_INF), sg[:, SW_BLOCK:]], axis=1)
        if j == SW_BLOCKS_PER_STEP - 1:
            sg = jnp.concatenate([sg[:, :2 * SW_BLOCK],
                                  jnp.where(n < n_tiles - 1, sg[:, 2 * SW_BLOCK:], NEG_INF)], axis=1)
        sk = sink_ref[head] * LOG2E
        m = jnp.maximum(jnp.max(sg, axis=-1, keepdims=True), sk)
        e = jnp.exp2(sg - m).astype(BF16)
        vv = jnp.concatenate([window(vp_ref, vc_ref, vn_ref, kvh, j), ones_cols], axis=1)
        o2 = jnp.dot(e, vv, preferred_element_type=F32)
        o = o2[:, :LANES] / (o2[:, LANES:] + jnp.exp2(sk - m))
        col0 = (2 * kvh + c) * LANES + hh * HEAD_DIM
        o_ref[0, rows, col0:col0 + HEAD_DIM] = o[:, hh * HEAD_DIM:(hh + 1) * HEAD_DIM].astype(BF16)

    units = [(kvh, j, c, hh) for j in range(SW_BLOCKS_PER_STEP) for kvh in range(SW_KV_HEADS)
             for c in range(2) for hh in range(2)]
    n_slots = s_ref.shape[0]
    for idx in range(len(units) + SW_SKEW):
        if idx < len(units):
            scores(*units[idx], idx % n_slots)
        if idx >= SW_SKEW:
            finish(*units[idx - SW_SKEW], (idx - SW_SKEW) % n_slots)


def _t5_bucket(rel):
    half = T5_BUCKETS // 2
    max_exact = half // 2
    ret = (rel > 0).astype(np.int32) * half
    n = np.abs(rel)
    large = max_exact + (np.log(np.maximum(n, 1) / max_exact)
                         / np.log(T5_MAX_DIST / max_exact) * (half - max_exact)).astype(np.int32)
    large = np.minimum(large, half - 1)
    return ret + np.where(n < max_exact, n, large)


def _swa_bucket_index():
    rel = np.arange(3 * SW_BLOCK)[None, :] - SW_BLOCK - np.arange(SW_BLOCK)[:, None]
    return np.where(np.abs(rel) <= SW_WINDOW, _t5_bucket(rel), -1).astype(np.int32)


def _swa_attn(sink, t5_table, qb, kbd, vbd):
    b, t, _ = qb.shape
    tq = SW_BLOCKS_PER_STEP * SW_BLOCK
    n_tiles = t // tq
    last_blk = t // SW_BLOCK - 1
    kv_w = kbd.shape[-1]
    qtile = pl.BlockSpec((1, tq, B_QW), lambda bi, ni: (bi, ni, 0))
    own = pl.BlockSpec((1, tq, kv_w), lambda bi, ni: (bi, ni, 0))
    prev = pl.BlockSpec((1, SW_BLOCK, kv_w),
                        lambda bi, ni: (bi, jnp.maximum(ni * SW_BLOCKS_PER_STEP - 1, 0), 0))
    nxt = pl.BlockSpec((1, SW_BLOCK, kv_w),
                       lambda bi, ni: (bi, jnp.minimum((ni + 1) * SW_BLOCKS_PER_STEP, last_blk), 0))
    smem = pl.BlockSpec(memory_space=pltpu.SMEM)
    return pl.pallas_call(
        functools.partial(_swa_kernel, n_tiles=n_tiles),
        grid=(b, n_tiles),
        in_specs=[smem, smem, _resident((SW_BLOCK, 3 * SW_BLOCK)), qtile, prev, own, nxt, prev, own, nxt],
        out_specs=qtile,
        out_shape=jax.ShapeDtypeStruct((b, t, B_QW), BF16),
        scratch_shapes=[pltpu.VMEM((SW_KV_HEADS, SW_GROUP * SW_BLOCK, 3 * SW_BLOCK), F32),
                        pltpu.VMEM((SW_SLOTS, SW_BLOCK, 3 * SW_BLOCK), F32)],
        compiler_params=_params(2),
        name="swa_attn",
    )(sink.astype(F32), t5_table.astype(F32).reshape(-1), jnp.asarray(_swa_bucket_index()),
      qb, kbd, kbd, kbd, vbd, vbd, vbd)


def _mix_kernel(x_ref, oa_ref, ob_ref, g_ref, wg_ref, wa_ref, wb_ref, wo_ref, wgate_f32, wup_f32, wdown_f32,
                x1_ref, wgate_bf, wup_bf, wdown_bf, h_scr, y_scr):
    _cast_rows(((wgate_f32, wgate_bf), (wup_f32, wup_bf), (wdown_f32, wdown_bf)))
    for rb in range(0, x_ref.shape[0], ROW_BLOCK):
        rows = slice(rb, rb + ROW_BLOCK)
        xf = x_ref[rows, :]
        h_scr[rows, :] = ((xf * _rms_scale(xf)) * g_ref[...]).astype(BF16)
        for c in range(0, D_MODEL, MXU_N):
            cs = slice(c, c + MXU_N)
            gs = slice(D_MODEL + c, D_MODEL + c + MXU_N)
            ga = jnp.dot(h_scr[rows, :], wg_ref[:, cs], preferred_element_type=F32)
            gb = jnp.dot(h_scr[rows, :], wg_ref[:, gs], preferred_element_type=F32)
            ya = jnp.dot(oa_ref[rows, :], wa_ref[:, cs], preferred_element_type=F32)
            yb = jnp.dot(ob_ref[rows, :], wb_ref[:, cs], preferred_element_type=F32)
            y_scr[rows, cs] = (jax.nn.sigmoid(ga) * ya + jax.nn.sigmoid(gb) * yb).astype(BF16)
        for c in range(0, D_MODEL, MXU_N):
            cs = slice(c, c + MXU_N)
            x1_ref[rows, cs] = x_ref[rows, cs] + jnp.dot(y_scr[rows, :], wo_ref[:, cs], preferred_element_type=F32)


def _mix_out(x2, oa, ob, g_mix, mix_weights, w_gate, w_up, w_down, layer):
    n = x2.shape[0]
    tm = TOK_TILE
    steps = n // tm
    tile = lambda w: pl.BlockSpec((tm, w), lambda i: (i, 0))
    streamed = (w_gate, w_up, w_down)
    outs = pl.pallas_call(
        _mix_kernel,
        grid=(steps,),
        in_specs=[tile(D_MODEL), tile(A_W), tile(B_QW), _resident((1, D_MODEL))]
                 + [_resident(w.shape) for w in mix_weights]
                 + [_row_stream(w.shape[1], w.shape[2], steps, layer) for w in streamed],
        out_specs=[tile(D_MODEL)]
                  + [pl.BlockSpec((w.shape[1] // steps, w.shape[2]), lambda i: (i, 0)) for w in streamed],
        out_shape=[jax.ShapeDtypeStruct((n, D_MODEL), F32)]
                  + [jax.ShapeDtypeStruct(w.shape[1:], BF16) for w in streamed],
        scratch_shapes=[pltpu.VMEM((tm, D_MODEL), BF16), pltpu.VMEM((tm, D_MODEL), BF16)],
        compiler_params=_params(1),
        name="mix_out",
    )(x2, oa, ob, g_mix, *mix_weights, w_gate, w_up, w_down)
    return outs[0], outs[1:]


def _ffn_kernel(x_ref, g_ref, wg_ref, wu_ref, wd_ref, o_ref, h_scr, a_scr):
    hidden = wg_ref.shape[1]
    for rb in range(0, x_ref.shape[0], ROW_BLOCK):
        rows = slice(rb, rb + ROW_BLOCK)
        xf = x_ref[rows, :]
        h_scr[rows, :] = ((xf * _rms_scale(xf)) * g_ref[...]).astype(BF16)
        for c in range(0, hidden, MXU_N):
            cs = slice(c, c + MXU_N)
            gate = jnp.dot(h_scr[rows, :], wg_ref[:, cs], preferred_element_type=F32)
            up = jnp.dot(h_scr[rows, :], wu_ref[:, cs], preferred_element_type=F32)
            a_scr[rows, cs] = (jax.nn.silu(gate) * up).astype(BF16)
        for c in range(0, D_MODEL, MXU_N):
            cs = slice(c, c + MXU_N)
            o_ref[rows, cs] = x_ref[rows, cs] + jnp.dot(a_scr[rows, :], wd_ref[:, cs], preferred_element_type=F32)


def _ffn(x1, g_ffn, w_gate, w_up, w_down):
    n = x1.shape[0]
    tm = TOK_TILE
    hidden = w_gate.shape[1]
    tile = pl.BlockSpec((tm, D_MODEL), lambda i: (i, 0))
    return pl.pallas_call(
        _ffn_kernel,
        grid=(n // tm,),
        in_specs=[tile, _resident((1, D_MODEL)), _resident(w_gate.shape), _resident(w_up.shape),
                  _resident(w_down.shape)],
        out_specs=tile,
        out_shape=jax.ShapeDtypeStruct((n, D_MODEL), F32),
        scratch_shapes=[pltpu.VMEM((tm, D_MODEL), BF16), pltpu.VMEM((tm, hidden), BF16)],
        compiler_params=_params(1),
        name="ffn",
    )(x1, g_ffn, w_gate, w_up, w_down)


def _layer(x2, b, t, layer, norm_mix, w_in, q_norm_a, k_norm_a, rpb_a, q_norm_b, k_norm_b, sink_b, t5_table,
           w_branch_a, w_branch_b, w_out, norm_ffn, w_gate, w_up, w_down):
    g_mix = norm_mix[layer].reshape(1, D_MODEL)
    reps = MXU_N // HEAD_DIM
    root_d = HEAD_DIM ** 0.5
    q_fold = QK_SCALE * root_d * LOG2E
    qk_gain = jnp.stack([jnp.tile(q_norm_a[layer] * q_fold, reps), jnp.tile(k_norm_a[layer] * root_d, reps),
                         jnp.tile(q_norm_b[layer] * q_fold, reps), jnp.tile(k_norm_b[layer] * root_d, reps)]
                        ).astype(F32)

    (qa, ka, va, qb, kbd, vbd), mix_weights = _in_proj(x2, g_mix, qk_gain, w_in, w_branch_a, w_branch_b,
                                                       w_out, layer)
    n = x2.shape[0]
    r3 = lambda a: a.reshape(b, t, a.shape[-1])
    oa = _na_attn(rpb_a[layer], r3(qa), r3(ka), r3(va))
    ob = _swa_attn(sink_b[layer], t5_table, r3(qb), r3(kbd), r3(vbd))
    x1, ffn_weights = _mix_out(x2, oa.reshape(n, A_W), ob.reshape(n, B_QW), g_mix, mix_weights,
                               w_gate, w_up, w_down, layer)
    return _ffn(x1, norm_ffn[layer].reshape(1, D_MODEL), *ffn_weights)


def kernel(x, norm_mix, w_in, q_norm_a, k_norm_a, rpb_a, q_norm_b, k_norm_b, sink_b, t5_table,
           w_branch_a, w_branch_b, w_out, norm_ffn, w_gate, w_up, w_down):
    b, t, d = x.shape
    x2 = x.reshape(b * t, d)
    for layer in range(norm_mix.shape[0]):
        x2 = _layer(x2, b, t, layer, norm_mix, w_in, q_norm_a, k_norm_a, rpb_a, q_norm_b, k_norm_b, sink_b,
                    t5_table, w_branch_a, w_branch_b, w_out, norm_ffn, w_gate, w_up, w_down)
    return x2.reshape(b, t, d)
```

```python
import functools

import jax
import jax.numpy as jnp
import numpy as np
from jax import lax
from jax.experimental import pallas as pl
from jax.experimental.pallas import tpu as pltpu

F32 = jnp.float32
BF16 = jnp.bfloat16

D_MODEL = 1024
HEAD_DIM = 64
GRID_W = 64
NA_HEADS = 8
NA_KH = 8
NA_KW = 16
SW_HEADS = 8
SW_KV_HEADS = 2
SW_GROUP = SW_HEADS // SW_KV_HEADS
SW_WINDOW = 128
SW_BLOCK = 128
T5_BUCKETS = 32
T5_MAX_DIST = 128
A_W = NA_HEADS * HEAD_DIM
B_QW = SW_HEADS * HEAD_DIM
B_KVW = SW_KV_HEADS * HEAD_DIM
QKV_W = 3 * A_W + B_QW + 2 * B_KVW
RMS_EPS = 1e-6
NEG_INF = -1e30
QK_SCALE = HEAD_DIM ** -0.5
LOG2E = 1.4426950408889634

LANES = 128
MXU_N = 256
VMEM_LIMIT = 56 * 1024 * 1024

TOK_TILE = 1024
ROW_BLOCK = 512
NA_ROWS_PER_STEP = 32
PROJ_SKEW = 4
NA_SKEW = 4
SW_SKEW = 4
SW_BLOCKS_PER_STEP = 16
NA_SLOTS = NA_SKEW + 1
NA_TABLE_HALF = 32
SW_SLOTS = SW_SKEW + 1


def _resident(shape):
    return pl.BlockSpec(shape, lambda *_: (0,) * len(shape), pipeline_mode=pl.Buffered(1))


def _params(n_axes):
    return pltpu.CompilerParams(dimension_semantics=("arbitrary",) * n_axes, vmem_limit_bytes=VMEM_LIMIT)


def _rms_scale(xf):
    return lax.rsqrt(jnp.mean(xf * xf, axis=-1, keepdims=True) + RMS_EPS)


def _group_rms_scale(p):
    y = p * p
    lo = lax.broadcasted_iota(jnp.int32, (p.shape[0], LANES), 1) < HEAD_DIM
    scales = []
    for c in range(0, p.shape[1], LANES):
        yc = y[:, c:c + LANES]
        s_lo = jnp.sum(jnp.where(lo, yc, 0.0), axis=-1, keepdims=True)
        s_hi = jnp.sum(jnp.where(lo, 0.0, yc), axis=-1, keepdims=True)
        r_lo = lax.rsqrt(s_lo + HEAD_DIM * RMS_EPS)
        r_hi = lax.rsqrt(s_hi + HEAD_DIM * RMS_EPS)
        scales.append(jnp.where(lo, r_lo, r_hi))
    return jnp.concatenate(scales, axis=1)


def _cast_rows(pairs):
    for src_ref, dst_ref in pairs:
        dst_ref[...] = src_ref[:, src_ref.shape[1] - dst_ref.shape[1]:].astype(BF16)


def _in_proj_kernel(x_ref, g_ref, w_ref, qk_gain_ref, wg_f32, wa_f32, wb_f32, wo_f32,
                    qa_ref, ka_ref, va_ref, qb_ref, kbd_ref, vbd_ref, wg_bf, wa_bf, wb_bf, wo_bf,
                    w_scr, h_scr, p_scr):
    @pl.when(pl.program_id(0) == 0)
    def _cast_own_weights():
        for c in range(0, QKV_W, MXU_N):
            w_scr[:, c:c + MXU_N] = w_ref[:, c:c + MXU_N].astype(BF16)

    _cast_rows(((wg_f32, wg_bf), (wa_f32, wa_bf), (wb_f32, wb_bf), (wo_f32, wo_bf)))

    def normed(out_ref, rows, c, gain_row):
        def epilogue(slot):
            p = p_scr[slot]
            scale = _group_rms_scale(p)
            out_ref[rows, c:c + MXU_N] = (p * scale * qk_gain_ref[gain_row:gain_row + 1, :]).astype(BF16)
        return epilogue

    def plain(out_ref, rows, c):
        def epilogue(slot):
            out_ref[rows, c:c + MXU_N] = p_scr[slot].astype(BF16)
        return epilogue

    def kv_b(rows):
        def epilogue(slot):
            kb = p_scr[slot, :, :B_KVW]
            kb = kb * _group_rms_scale(kb) * qk_gain_ref[3:4, :LANES]
            vb = p_scr[slot, :, B_KVW:]
            lo = lax.broadcasted_iota(jnp.int32, kb.shape, 1) < HEAD_DIM
            for src, out_ref in ((kb, kbd_ref), (vb, vbd_ref)):
                swapped = pltpu.roll(src, HEAD_DIM, axis=1)
                out_ref[rows, :LANES] = jnp.where(lo, src, swapped).astype(BF16)
                out_ref[rows, LANES:] = jnp.where(lo, swapped, src).astype(BF16)
        return epilogue

    units = []
    for rb in range(0, x_ref.shape[0], ROW_BLOCK):
        rows = slice(rb, rb + ROW_BLOCK)
        xf = x_ref[rows, :]
        h_scr[rows, :] = ((xf * _rms_scale(xf)) * g_ref[...]).astype(BF16)
        for c in range(0, A_W, MXU_N):
            units += [(rows, c, normed(qa_ref, rows, c, 0)), (rows, A_W + c, normed(ka_ref, rows, c, 1)),
                      (rows, 3 * A_W + c, normed(qb_ref, rows, c, 2))]
        units.append((rows, 3 * A_W + B_QW, kv_b(rows)))
        units += [(rows, 2 * A_W + c, plain(va_ref, rows, c)) for c in range(0, A_W, MXU_N)]
    n_slots = PROJ_SKEW + 1
    for idx in range(len(units) + PROJ_SKEW):
        if idx < len(units):
            rows, c0, _ = units[idx]
            p_scr[idx % n_slots] = jnp.dot(h_scr[rows, :], w_scr[:, c0:c0 + MXU_N], preferred_element_type=F32)
        if idx >= PROJ_SKEW:
            units[idx - PROJ_SKEW][2]((idx - PROJ_SKEW) % n_slots)


def _row_stream(n_rows, n_cols, steps, layer):
    return pl.BlockSpec((None, n_rows // steps, n_cols), lambda i: (layer, i, 0))


def _in_proj(x2, g_mix, qk_gain, w_in, w_a, w_b, w_o, layer):
    n = x2.shape[0]
    tm = TOK_TILE
    steps = n // tm
    tile = lambda w: pl.BlockSpec((tm, w), lambda i: (i, 0))
    gate_w = w_in.shape[2] - QKV_W
    out_w = (A_W, A_W, A_W, B_QW, 2 * LANES, 2 * LANES)
    streamed = ((w_in, gate_w), (w_a, D_MODEL), (w_b, D_MODEL), (w_o, D_MODEL))
    outs = pl.pallas_call(
        _in_proj_kernel,
        grid=(steps,),
        in_specs=[tile(D_MODEL), _resident((1, D_MODEL)),
                  pl.BlockSpec((None, D_MODEL, QKV_W), lambda i: (layer, 0, 0), pipeline_mode=pl.Buffered(1)),
                  _resident((4, MXU_N))]
                 + [_row_stream(w.shape[1], w.shape[2], steps, layer) for w, _ in streamed],
        out_specs=[tile(w) for w in out_w]
                  + [pl.BlockSpec((w.shape[1] // steps, cols), lambda i: (i, 0)) for w, cols in streamed],
        out_shape=[jax.ShapeDtypeStruct((n, w), BF16) for w in out_w]
                  + [jax.ShapeDtypeStruct((w.shape[1], cols), BF16) for w, cols in streamed],
        scratch_shapes=[pltpu.VMEM((D_MODEL, QKV_W), BF16), pltpu.VMEM((tm, D_MODEL), BF16),
                        pltpu.VMEM((PROJ_SKEW + 1, ROW_BLOCK, MXU_N), F32)],
        compiler_params=_params(1),
        name="in_proj",
    )(x2, g_mix, w_in, qk_gain, w_in, w_a, w_b, w_o)
    return outs[:6], outs[6:]


def _na_kernel(rpb_ref, dc_ref, q_ref, k_ref, v_ref, o_ref, bias_ref, s_ref, *, rows):
    t = pl.program_id(1)
    lo = lax.broadcasted_iota(jnp.int32, (GRID_W, LANES), 1) < HEAD_DIM
    nt = (((1,), (1,)), ((), ()))
    n_slab = 2 * NA_KH - 2

    @pl.when((pl.program_id(0) == 0) & (t == 0))
    def _build_bias():
        offs = dc_ref[...]
        in_win = offs >= 0
        take = jnp.maximum(offs, 0)

        def build(idx, carry):
            h = idx // n_slab
            d = idx - h * n_slab
            row = jnp.broadcast_to(rpb_ref[pl.ds(d * NA_HEADS + h, 1), :] * LOG2E, (GRID_W, LANES))
            slab = jnp.where(in_win, jnp.take_along_axis(row, take, axis=1), NEG_INF)
            row0 = pl.multiple_of((h % 2) * GRID_W, GRID_W)
            bias_ref[h // 2, d, pl.ds(row0, GRID_W), :] = slab
            return carry

        lax.fori_loop(0, NA_HEADS * n_slab, build, 0, unroll=NA_HEADS)

    win0 = _na_window_start(t, rows)
    row_info = []
    for i in range(NA_ROWS_PER_STEP):
        r = t * NA_ROWS_PER_STEP + i
        rs = jnp.clip(r - NA_KH // 2, 0, rows - NA_KH)
        row_info.append(((NA_KH - 1) - (r - rs), pl.multiple_of((rs - win0) * GRID_W, GRID_W)))

    def scores(i, p, slot):
        d0, kstart = row_info[i]
        cols = slice(p * LANES, (p + 1) * LANES)
        q = q_ref[0, i * GRID_W:(i + 1) * GRID_W, cols]
        zero = jnp.zeros_like(q)
        q2 = jnp.concatenate([jnp.where(lo, q, zero), jnp.where(lo, zero, q)], axis=0)
        kk = k_ref[0, pl.ds(kstart, NA_KH * GRID_W), cols]
        s = lax.dot_general(q2, kk, nt, preferred_element_type=F32)
        for v in range(NA_KH // 2):
            vcols = slice(v * LANES, (v + 1) * LANES)
            s_ref[slot, :, vcols] = s[:, vcols] + bias_ref[p, d0 + 2 * v]

    ones_cols = jnp.ones((NA_KH * GRID_W, LANES), BF16)

    def finish(i, p, slot):
        _, kstart = row_info[i]
        cols = slice(p * LANES, (p + 1) * LANES)
        m = jnp.max(s_ref[slot], axis=-1, keepdims=True)
        e = jnp.exp2(s_ref[slot] - m).astype(BF16)
        vv = jnp.concatenate([v_ref[0, pl.ds(kstart, NA_KH * GRID_W), cols], ones_cols], axis=1)
        o2 = jnp.dot(e, vv, preferred_element_type=F32)
        o2 = o2[:, :LANES] / o2[:, LANES:]
        o = jnp.where(lo, o2[:GRID_W], o2[GRID_W:])
        o_ref[0, i * GRID_W:(i + 1) * GRID_W, cols] = o.astype(BF16)

    units = [(i, p) for i in range(NA_ROWS_PER_STEP) for p in range(NA_HEADS // 2)]
    n_slots = s_ref.shape[0]
    for idx in range(len(units) + NA_SKEW):
        if idx < len(units):
            scores(*units[idx], idx % n_slots)
        if idx >= NA_SKEW:
            finish(*units[idx - NA_SKEW], (idx - NA_SKEW) % n_slots)


def _na_window_start(step, rows):
    return jnp.clip(step * NA_ROWS_PER_STEP - NA_KH // 2, 0, rows - (NA_ROWS_PER_STEP + NA_KH))


def _na_col_offsets():
    c = np.arange(GRID_W)
    kc = c[None, :]
    win_start = np.clip(c - NA_KW // 2, 0, GRID_W - NA_KW)[:, None]
    in_win = (kc >= win_start) & (kc < win_start + NA_KW)
    dc = kc - c[:, None] + (NA_KW - 1)
    return np.concatenate([np.where(in_win, dc, -1), np.where(in_win, dc + NA_TABLE_HALF, -1)],
                          axis=1).astype(np.int32)


def _na_attn(rpb, qa, ka, va):
    b, t, _ = qa.shape
    rows = t // GRID_W
    tq = NA_ROWS_PER_STEP * GRID_W
    tile = pl.BlockSpec((1, tq, A_W), lambda bi, ti: (bi, ti, 0))
    window = pl.BlockSpec((pl.Element(1), pl.Element((NA_ROWS_PER_STEP + NA_KH) * GRID_W), pl.Element(A_W)),
                          lambda bi, ti: (bi, _na_window_start(ti, rows) * GRID_W, 0))
    per_head = jnp.transpose(rpb.astype(F32), (0, 2, 1))
    pad = lambda a, w: jnp.pad(a, ((0, 0), (0, 0), (0, w - a.shape[-1])))
    table = jnp.concatenate([pad(per_head[:-1], NA_TABLE_HALF), pad(per_head[1:], LANES - NA_TABLE_HALF)], axis=-1)
    table = table.reshape(-1, LANES)
    return pl.pallas_call(
        functools.partial(_na_kernel, rows=rows),
        grid=(b, rows // NA_ROWS_PER_STEP),
        in_specs=[_resident(table.shape), _resident((GRID_W, LANES)), tile, window, window],
        out_specs=tile,
        out_shape=jax.ShapeDtypeStruct((b, t, A_W), BF16),
        scratch_shapes=[pltpu.VMEM((NA_HEADS // 2, 2 * NA_KH - 2, LANES, LANES), F32),
                        pltpu.VMEM((NA_SLOTS, 2 * GRID_W, NA_KH * GRID_W), F32)],
        compiler_params=_params(2),
        name="na_attn",
    )(table, jnp.asarray(_na_col_offsets()), qa, ka, va)


def _swa_kernel(sink_ref, t5_ref, bucket_ref, q_ref, kp_ref, kc_ref, kn_ref, vp_ref, vc_ref, vn_ref, o_ref,
                bias_ref, s_ref, *, n_tiles):
    n = pl.program_id(1)
    lo = lax.broadcasted_iota(jnp.int32, (SW_BLOCK, LANES), 1) < HEAD_DIM
    nt = (((1,), (1,)), ((), ()))

    @pl.when((pl.program_id(0) == 0) & (n == 0))
    def _build_bias():
        bucket = bucket_ref[...]

        in_band = bucket >= 0
        take = jnp.maximum(bucket, 0)

        def build(h, carry):
            row = jnp.broadcast_to(t5_ref[pl.ds(h, 1), :] * LOG2E, (SW_BLOCK, LANES))
            vals = [jnp.take_along_axis(row, take[:, c:c + LANES], axis=1) for c in range(0, 3 * SW_BLOCK, LANES)]
            row0 = pl.multiple_of((h % SW_GROUP) * SW_BLOCK, SW_BLOCK)
            bias_ref[h // SW_GROUP, pl.ds(row0, SW_BLOCK), :] = jnp.where(
                in_band, jnp.concatenate(vals, axis=1), NEG_INF)
            return carry

        lax.fori_loop(0, SW_HEADS, build, 0)

    def window(prev_ref, own_ref, next_ref, kvh, j):
        kcols = slice(kvh * LANES, (kvh + 1) * LANES)
        blk = lambda b: own_ref[0, b * SW_BLOCK:(b + 1) * SW_BLOCK, kcols]
        first = prev_ref[0, :, kcols] if j == 0 else blk(j - 1)
        last = next_ref[0, :, kcols] if j == SW_BLOCKS_PER_STEP - 1 else blk(j + 1)
        return jnp.concatenate([first, blk(j), last], axis=0)

    def scores(kvh, j, c, hh, slot):
        rows = slice(j * SW_BLOCK, (j + 1) * SW_BLOCK)
        qb = q_ref[0, rows, (2 * kvh + c) * LANES:(2 * kvh + c + 1) * LANES]
        q1 = jnp.where(lo if hh == 0 else ~lo, qb, jnp.zeros_like(qb))
        s_ref[slot] = lax.dot_general(q1, window(kp_ref, kc_ref, kn_ref, kvh, j), nt,
                                      preferred_element_type=F32)

    ones_cols = jnp.ones((3 * SW_BLOCK, LANES), BF16)

    def finish(kvh, j, c, hh, slot):
        rows = slice(j * SW_BLOCK, (j + 1) * SW_BLOCK)
        head = kvh * SW_GROUP + 2 * c + hh
        brows = slice((2 * c + hh) * SW_BLOCK, (2 * c + hh + 1) * SW_BLOCK)
        sg = s_ref[slot] + bias_ref[kvh, brows, :]
        if j == 0:
            sg = jnp.concatenate([jnp.where(n > 0, sg[:, :SW_BLOCK], NEG_INF), sg[:, SW_BLOCK:]], axis=1)
        if j == SW_BLOCKS_PER_STEP - 1:
            sg = jnp.concatenate([sg[:, :2 * SW_BLOCK],
                                  jnp.where(n < n_tiles - 1, sg[:, 2 * SW_BLOCK:], NEG_INF)], axis=1)
        sk = sink_ref[head] * LOG2E
        m = jnp.maximum(jnp.max(sg, axis=-1, keepdims=True), sk)
        e = jnp.exp2(sg - m).astype(BF16)
        vv = jnp.concatenate([window(vp_ref, vc_ref, vn_ref, kvh, j), ones_cols], axis=1)
        o2 = jnp.dot(e, vv, preferred_element_type=F32)
        o = o2[:, :LANES] / (o2[:, LANES:] + jnp.exp2(sk - m))
        col0 = (2 * kvh + c) * LANES + hh * HEAD_DIM
        o_ref[0, rows, col0:col0 + HEAD_DIM] = o[:, hh * HEAD_DIM:(hh + 1) * HEAD_DIM].astype(BF16)

    units = [(kvh, j, c, hh) for j in range(SW_BLOCKS_PER_STEP) for kvh in range(SW_KV_HEADS)
             for c in range(2) for hh in range(2)]
    n_slots = s_ref.shape[0]
    for idx in range(len(units) + SW_SKEW):
        if idx < len(units):
            scores(*units[idx], idx % n_slots)
        if idx >= SW_SKEW:
            finish(*units[idx - SW_SKEW], (idx - SW_SKEW) % n_slots)


def _t5_bucket(rel):
    half = T5_BUCKETS // 2
    max_exact = half // 2
    ret = (rel > 0).astype(np.int32) * half
    n = np.abs(rel)
    large = max_exact + (np.log(np.maximum(n, 1) / max_exact)
                         / np.log(T5_MAX_DIST / max_exact) * (half - max_exact)).astype(np.int32)
    large = np.minimum(large, half - 1)
    return ret + np.where(n < max_exact, n, large)


def _swa_bucket_index():
    rel = np.arange(3 * SW_BLOCK)[None, :] - SW_BLOCK - np.arange(SW_BLOCK)[:, None]
    return np.where(np.abs(rel) <= SW_WINDOW, _t5_bucket(rel), -1).astype(np.int32)


def _swa_attn(sink, t5_table, qb, kbd, vbd):
    b, t, _ = qb.shape
    tq = SW_BLOCKS_PER_STEP * SW_BLOCK
    n_tiles = t // tq
    last_blk = t // SW_BLOCK - 1
    kv_w = kbd.shape[-1]
    qtile = pl.BlockSpec((1, tq, B_QW), lambda bi, ni: (bi, ni, 0))
    own = pl.BlockSpec((1, tq, kv_w), lambda bi, ni: (bi, ni, 0))
    prev = pl.BlockSpec((1, SW_BLOCK, kv_w),
                        lambda bi, ni: (bi, jnp.maximum(ni * SW_BLOCKS_PER_STEP - 1, 0), 0))
    nxt = pl.BlockSpec((1, SW_BLOCK, kv_w),
                       lambda bi, ni: (bi, jnp.minimum((ni + 1) * SW_BLOCKS_PER_STEP, last_blk), 0))
    smem = pl.BlockSpec(memory_space=pltpu.SMEM)
    table = jnp.pad(t5_table.astype(F32).T, ((0, 0), (0, LANES - t5_table.shape[0])))
    return pl.pallas_call(
        functools.partial(_swa_kernel, n_tiles=n_tiles),
        grid=(b, n_tiles),
        in_specs=[smem, _resident((SW_HEADS, LANES)), _resident((SW_BLOCK, 3 * SW_BLOCK)),
                  qtile, prev, own, nxt, prev, own, nxt],
        out_specs=qtile,
        out_shape=jax.ShapeDtypeStruct((b, t, B_QW), BF16),
        scratch_shapes=[pltpu.VMEM((SW_KV_HEADS, SW_GROUP * SW_BLOCK, 3 * SW_BLOCK), F32),
                        pltpu.VMEM((SW_SLOTS, SW_BLOCK, 3 * SW_BLOCK), F32)],
        compiler_params=_params(2),
        name="swa_attn",
    )(sink.astype(F32), table, jnp.asarray(_swa_bucket_index()),
      qb, kbd, kbd, kbd, vbd, vbd, vbd)


def _mix_kernel(x_ref, oa_ref, ob_ref, g_ref, wg_ref, wa_ref, wb_ref, wo_ref, wgate_f32, wup_f32, wdown_f32,
                x1_ref, wgate_bf, wup_bf, wdown_bf, h_scr, y_scr):
    _cast_rows(((wgate_f32, wgate_bf), (wup_f32, wup_bf), (wdown_f32, wdown_bf)))
    for rb in range(0, x_ref.shape[0], ROW_BLOCK):
        rows = slice(rb, rb + ROW_BLOCK)
        xf = x_ref[rows, :]
        h_scr[rows, :] = ((xf * _rms_scale(xf)) * g_ref[...]).astype(BF16)
        for c in range(0, D_MODEL, MXU_N):
            cs = slice(c, c + MXU_N)
            gs = slice(D_MODEL + c, D_MODEL + c + MXU_N)
            ga = jnp.dot(h_scr[rows, :], wg_ref[:, cs], preferred_element_type=F32)
            gb = jnp.dot(h_scr[rows, :], wg_ref[:, gs], preferred_element_type=F32)
            ya = jnp.dot(oa_ref[rows, :], wa_ref[:, cs], preferred_element_type=F32)
            yb = jnp.dot(ob_ref[rows, :], wb_ref[:, cs], preferred_element_type=F32)
            y_scr[rows, cs] = (jax.nn.sigmoid(ga) * ya + jax.nn.sigmoid(gb) * yb).astype(BF16)
        for c in range(0, D_MODEL, MXU_N):
            cs = slice(c, c + MXU_N)
            x1_ref[rows, cs] = x_ref[rows, cs] + jnp.dot(y_scr[rows, :], wo_ref[:, cs], preferred_element_type=F32)


def _mix_out(x2, oa, ob, g_mix, mix_weights, w_gate, w_up, w_down, layer):
    n = x2.shape[0]
    tm = TOK_TILE
    steps = n // tm
    tile = lambda w: pl.BlockSpec((tm, w), lambda i: (i, 0))
    streamed = (w_gate, w_up, w_down)
    outs = pl.pallas_call(
        _mix_kernel,
        grid=(steps,),
        in_specs=[tile(D_MODEL), tile(A_W), tile(B_QW), _resident((1, D_MODEL))]
                 + [_resident(w.shape) for w in mix_weights]
                 + [_row_stream(w.shape[1], w.shape[2], steps, layer) for w in streamed],
        out_specs=[tile(D_MODEL)]
                  + [pl.BlockSpec((w.shape[1] // steps, w.shape[2]), lambda i: (i, 0)) for w in streamed],
        out_shape=[jax.ShapeDtypeStruct((n, D_MODEL), F32)]
                  + [jax.ShapeDtypeStruct(w.shape[1:], BF16) for w in streamed],
        scratch_shapes=[pltpu.VMEM((tm, D_MODEL), BF16), pltpu.VMEM((tm, D_MODEL), BF16)],
        compiler_params=_params(1),
        name="mix_out",
    )(x2, oa, ob, g_mix, *mix_weights, w_gate, w_up, w_down)
    return outs[0], outs[1:]


def _ffn_kernel(x_ref, g_ref, wg_ref, wu_ref, wd_ref, o_ref, h_scr, a_scr):
    hidden = wg_ref.shape[1]
    for rb in range(0, x_ref.shape[0], ROW_BLOCK):
        rows = slice(rb, rb + ROW_BLOCK)
        xf = x_ref[rows, :]
        h_scr[rows, :] = ((xf * _rms_scale(xf)) * g_ref[...]).astype(BF16)
        for c in range(0, hidden, MXU_N):
            cs = slice(c, c + MXU_N)
            gate = jnp.dot(h_scr[rows, :], wg_ref[:, cs], preferred_element_type=F32)
            up = jnp.dot(h_scr[rows, :], wu_ref[:, cs], preferred_element_type=F32)
            a_scr[rows, cs] = (jax.nn.silu(gate) * up).astype(BF16)
        for c in range(0, D_MODEL, MXU_N):
            cs = slice(c, c + MXU_N)
            o_ref[rows, cs] = x_ref[rows, cs] + jnp.dot(a_scr[rows, :], wd_ref[:, cs], preferred_element_type=F32)


def _ffn(x1, g_ffn, w_gate, w_up, w_down):
    n = x1.shape[0]
    tm = TOK_TILE
    hidden = w_gate.shape[1]
    tile = pl.BlockSpec((tm, D_MODEL), lambda i: (i, 0))
    return pl.pallas_call(
        _ffn_kernel,
        grid=(n // tm,),
        in_specs=[tile, _resident((1, D_MODEL)), _resident(w_gate.shape), _resident(w_up.shape),
                  _resident(w_down.shape)],
        out_specs=tile,
        out_shape=jax.ShapeDtypeStruct((n, D_MODEL), F32),
        scratch_shapes=[pltpu.VMEM((tm, D_MODEL), BF16), pltpu.VMEM((tm, hidden), BF16)],
        compiler_params=_params(1),
        name="ffn",
    )(x1, g_ffn, w_gate, w_up, w_down)


def _layer(x2, b, t, layer, norm_mix, w_in, q_norm_a, k_norm_a, rpb_a, q_norm_b, k_norm_b, sink_b, t5_table,
           w_branch_a, w_branch_b, w_out, norm_ffn, w_gate, w_up, w_down):
    g_mix = norm_mix[layer].reshape(1, D_MODEL)
    reps = MXU_N // HEAD_DIM
    root_d = HEAD_DIM ** 0.5
    q_fold = QK_SCALE * root_d * LOG2E
    qk_gain = jnp.stack([jnp.tile(q_norm_a[layer] * q_fold, reps), jnp.tile(k_norm_a[layer] * root_d, reps),
                         jnp.tile(q_norm_b[layer] * q_fold, reps), jnp.tile(k_norm_b[layer] * root_d, reps)]
                        ).astype(F32)

    (qa, ka, va, qb, kbd, vbd), mix_weights = _in_proj(x2, g_mix, qk_gain, w_in, w_branch_a, w_branch_b,
                                                       w_out, layer)
    n = x2.shape[0]
    r3 = lambda a: a.reshape(b, t, a.shape[-1])
    oa = _na_attn(rpb_a[layer], r3(qa), r3(ka), r3(va))
    ob = _swa_attn(sink_b[layer], t5_table, r3(qb), r3(kbd), r3(vbd))
    x1, ffn_weights = _mix_out(x2, oa.reshape(n, A_W), ob.reshape(n, B_QW), g_mix, mix_weights,
                               w_gate, w_up, w_down, layer)
    return _ffn(x1, norm_ffn[layer].reshape(1, D_MODEL), *ffn_weights)


def kernel(x, norm_mix, w_in, q_norm_a, k_norm_a, rpb_a, q_norm_b, k_norm_b, sink_b, t5_table,
           w_branch_a, w_branch_b, w_out, norm_ffn, w_gate, w_up, w_down):
    b, t, d = x.shape
    x2 = x.reshape(b * t, d)
    for layer in range(norm_mix.shape[0]):
        x2 = _layer(x2, b, t, layer, norm_mix, w_in, q_norm_a, k_norm_a, rpb_a, q_norm_b, k_norm_b, sink_b,
                    t5_table, w_branch_a, w_branch_b, w_out, norm_ffn, w_gate, w_up, w_down)
    return x2.reshape(b, t, d)
```

```python
import functools

import jax
import jax.numpy as jnp
import numpy as np
from jax import lax
from jax.experimental import pallas as pl
from jax.experimental.pallas import tpu as pltpu

F32 = jnp.float32
BF16 = jnp.bfloat16

D_MODEL = 1024
HEAD_DIM = 64
GRID_W = 64
NA_HEADS = 8
NA_KH = 8
NA_KW = 16
SW_HEADS = 8
SW_KV_HEADS = 2
SW_GROUP = SW_HEADS // SW_KV_HEADS
SW_WINDOW = 128
SW_BLOCK = 128
T5_BUCKETS = 32
T5_MAX_DIST = 128
A_W = NA_HEADS * HEAD_DIM
B_QW = SW_HEADS * HEAD_DIM
B_KVW = SW_KV_HEADS * HEAD_DIM
QKV_W = 3 * A_W + B_QW + 2 * B_KVW
RMS_EPS = 1e-6
NEG_INF = -1e30
QK_SCALE = HEAD_DIM ** -0.5
LOG2E = 1.4426950408889634

LANES = 128
MXU_N = 256
VMEM_LIMIT = 56 * 1024 * 1024

TOK_TILE = 1024
ROW_BLOCK = 256
NA_ROWS_PER_STEP = 32
PROJ_SKEW = 4
NA_SKEW = 4
SW_SKEW = 4
SW_BLOCKS_PER_STEP = 16
NA_SLOTS = NA_SKEW + 1
NA_TABLE_HALF = 32
SW_SLOTS = SW_SKEW + 1


def _resident(shape):
    return pl.BlockSpec(shape, lambda *_: (0,) * len(shape), pipeline_mode=pl.Buffered(1))


def _params(n_axes):
    return pltpu.CompilerParams(dimension_semantics=("arbitrary",) * n_axes, vmem_limit_bytes=VMEM_LIMIT)


def _rms_scale(xf):
    return lax.rsqrt(jnp.mean(xf * xf, axis=-1, keepdims=True) + RMS_EPS)


def _group_rms_scale(p):
    y = p * p
    lo = lax.broadcasted_iota(jnp.int32, (p.shape[0], LANES), 1) < HEAD_DIM
    scales = []
    for c in range(0, p.shape[1], LANES):
        yc = y[:, c:c + LANES]
        s_lo = jnp.sum(jnp.where(lo, yc, 0.0), axis=-1, keepdims=True)
        s_hi = jnp.sum(jnp.where(lo, 0.0, yc), axis=-1, keepdims=True)
        r_lo = lax.rsqrt(s_lo + HEAD_DIM * RMS_EPS)
        r_hi = lax.rsqrt(s_hi + HEAD_DIM * RMS_EPS)
        scales.append(jnp.where(lo, r_lo, r_hi))
    return jnp.concatenate(scales, axis=1)


def _cast_rows(pairs):
    for src_ref, dst_ref in pairs:
        dst_ref[...] = src_ref[:, src_ref.shape[1] - dst_ref.shape[1]:].astype(BF16)


def _in_proj_kernel(x_ref, g_ref, w_ref, qk_gain_ref, wg_f32, wa_f32, wb_f32, wo_f32,
                    qa_ref, ka_ref, va_ref, qb_ref, kbd_ref, vbd_ref, wg_bf, wa_bf, wb_bf, wo_bf,
                    w_scr, h_scr, p_scr):
    @pl.when(pl.program_id(0) == 0)
    def _cast_own_weights():
        for c in range(0, QKV_W, MXU_N):
            w_scr[:, c:c + MXU_N] = w_ref[:, c:c + MXU_N].astype(BF16)

    _cast_rows(((wg_f32, wg_bf), (wa_f32, wa_bf), (wb_f32, wb_bf), (wo_f32, wo_bf)))

    def normed(out_ref, rows, c, gain_row):
        def epilogue(slot):
            p = p_scr[slot]
            scale = _group_rms_scale(p)
            out_ref[rows, c:c + MXU_N] = (p * scale * qk_gain_ref[gain_row:gain_row + 1, :]).astype(BF16)
        return epilogue

    def plain(out_ref, rows, c):
        def epilogue(slot):
            out_ref[rows, c:c + MXU_N] = p_scr[slot].astype(BF16)
        return epilogue

    def kv_b(rows):
        def epilogue(slot):
            kb = p_scr[slot, :, :B_KVW]
            kb = kb * _group_rms_scale(kb) * qk_gain_ref[3:4, :LANES]
            vb = p_scr[slot, :, B_KVW:]
            lo = lax.broadcasted_iota(jnp.int32, kb.shape, 1) < HEAD_DIM
            for src, out_ref in ((kb, kbd_ref), (vb, vbd_ref)):
                swapped = pltpu.roll(src, HEAD_DIM, axis=1)
                out_ref[rows, :LANES] = jnp.where(lo, src, swapped).astype(BF16)
                out_ref[rows, LANES:] = jnp.where(lo, swapped, src).astype(BF16)
        return epilogue

    units = []
    for rb in range(0, x_ref.shape[0], ROW_BLOCK):
        rows = slice(rb, rb + ROW_BLOCK)
        xf = x_ref[rows, :]
        h_scr[rows, :] = ((xf * _rms_scale(xf)) * g_ref[...]).astype(BF16)
        for c in range(0, A_W, MXU_N):
            units += [(rows, c, normed(qa_ref, rows, c, 0)), (rows, A_W + c, normed(ka_ref, rows, c, 1)),
                      (rows, 3 * A_W + c, normed(qb_ref, rows, c, 2))]
        units.append((rows, 3 * A_W + B_QW, kv_b(rows)))
        units += [(rows, 2 * A_W + c, plain(va_ref, rows, c)) for c in range(0, A_W, MXU_N)]
    n_slots = PROJ_SKEW + 1
    for idx in range(len(units) + PROJ_SKEW):
        if idx < len(units):
            rows, c0, _ = units[idx]
            p_scr[idx % n_slots] = jnp.dot(h_scr[rows, :], w_scr[:, c0:c0 + MXU_N], preferred_element_type=F32)
        if idx >= PROJ_SKEW:
            units[idx - PROJ_SKEW][2]((idx - PROJ_SKEW) % n_slots)


def _row_stream(n_rows, n_cols, steps, layer):
    return pl.BlockSpec((None, n_rows // steps, n_cols), lambda i: (layer, i, 0))


def _in_proj(x2, g_mix, qk_gain, w_in, w_a, w_b, w_o, layer):
    n = x2.shape[0]
    tm = TOK_TILE
    steps = n // tm
    tile = lambda w: pl.BlockSpec((tm, w), lambda i: (i, 0))
    gate_w = w_in.shape[2] - QKV_W
    out_w = (A_W, A_W, A_W, B_QW, 2 * LANES, 2 * LANES)
    streamed = ((w_in, gate_w), (w_a, D_MODEL), (w_b, D_MODEL), (w_o, D_MODEL))
    outs = pl.pallas_call(
        _in_proj_kernel,
        grid=(steps,),
        in_specs=[tile(D_MODEL), _resident((1, D_MODEL)),
                  pl.BlockSpec((None, D_MODEL, QKV_W), lambda i: (layer, 0, 0), pipeline_mode=pl.Buffered(1)),
                  _resident((4, MXU_N))]
                 + [_row_stream(w.shape[1], w.shape[2], steps, layer) for w, _ in streamed],
        out_specs=[tile(w) for w in out_w]
                  + [pl.BlockSpec((w.shape[1] // steps, cols), lambda i: (i, 0)) for w, cols in streamed],
        out_shape=[jax.ShapeDtypeStruct((n, w), BF16) for w in out_w]
                  + [jax.ShapeDtypeStruct((w.shape[1], cols), BF16) for w, cols in streamed],
        scratch_shapes=[pltpu.VMEM((D_MODEL, QKV_W), BF16), pltpu.VMEM((tm, D_MODEL), BF16),
                        pltpu.VMEM((PROJ_SKEW + 1, ROW_BLOCK, MXU_N), F32)],
        compiler_params=_params(1),
        name="in_proj",
    )(x2, g_mix, w_in, qk_gain, w_in, w_a, w_b, w_o)
    return outs[:6], outs[6:]


def _na_kernel(rpb_ref, dc_ref, q_ref, k_ref, v_ref, o_ref, bias_ref, s_ref, *, rows):
    t = pl.program_id(1)
    lo = lax.broadcasted_iota(jnp.int32, (GRID_W, LANES), 1) < HEAD_DIM
    nt = (((1,), (1,)), ((), ()))
    n_slab = 2 * NA_KH - 2

    @pl.when((pl.program_id(0) == 0) & (t == 0))
    def _build_bias():
        offs = dc_ref[...]
        in_win = offs >= 0
        take = jnp.maximum(offs, 0)

        def build(idx, carry):
            h = idx // n_slab
            d = idx - h * n_slab
            row = jnp.broadcast_to(rpb_ref[pl.ds(d * NA_HEADS + h, 1), :] * LOG2E, (GRID_W, LANES))
            slab = jnp.where(in_win, jnp.take_along_axis(row, take, axis=1), NEG_INF)
            row0 = pl.multiple_of((h % 2) * GRID_W, GRID_W)
            bias_ref[h // 2, d, pl.ds(row0, GRID_W), :] = slab
            return carry

        lax.fori_loop(0, NA_HEADS * n_slab, build, 0, unroll=NA_HEADS)

    win0 = _na_window_start(t, rows)
    row_info = []
    for i in range(NA_ROWS_PER_STEP):
        r = t * NA_ROWS_PER_STEP + i
        rs = jnp.clip(r - NA_KH // 2, 0, rows - NA_KH)
        row_info.append(((NA_KH - 1) - (r - rs), pl.multiple_of((rs - win0) * GRID_W, GRID_W)))

    def scores(i, p, slot):
        d0, kstart = row_info[i]
        cols = slice(p * LANES, (p + 1) * LANES)
        q = q_ref[0, i * GRID_W:(i + 1) * GRID_W, cols]
        zero = jnp.zeros_like(q)
        q2 = jnp.concatenate([jnp.where(lo, q, zero), jnp.where(lo, zero, q)], axis=0)
        kk = k_ref[0, pl.ds(kstart, NA_KH * GRID_W), cols]
        s = lax.dot_general(q2, kk, nt, preferred_element_type=F32)
        for v in range(NA_KH // 2):
            vcols = slice(v * LANES, (v + 1) * LANES)
            s_ref[slot, :, vcols] = s[:, vcols] + bias_ref[p, d0 + 2 * v]

    ones_cols = jnp.ones((NA_KH * GRID_W, LANES), BF16)

    def finish(i, p, slot):
        _, kstart = row_info[i]
        cols = slice(p * LANES, (p + 1) * LANES)
        m = jnp.max(s_ref[slot], axis=-1, keepdims=True)
        e = jnp.exp2(s_ref[slot] - m).astype(BF16)
        vv = jnp.concatenate([v_ref[0, pl.ds(kstart, NA_KH * GRID_W), cols], ones_cols], axis=1)
        o2 = jnp.dot(e, vv, preferred_element_type=F32)
        o2 = o2[:, :LANES] / o2[:, LANES:]
        o = jnp.where(lo, o2[:GRID_W], o2[GRID_W:])
        o_ref[0, i * GRID_W:(i + 1) * GRID_W, cols] = o.astype(BF16)

    units = [(i, p) for i in range(NA_ROWS_PER_STEP) for p in range(NA_HEADS // 2)]
    n_slots = s_ref.shape[0]
    for idx in range(len(units) + NA_SKEW):
        if idx < len(units):
            scores(*units[idx], idx % n_slots)
        if idx >= NA_SKEW:
            finish(*units[idx - NA_SKEW], (idx - NA_SKEW) % n_slots)


def _na_window_start(step, rows):
    return jnp.clip(step * NA_ROWS_PER_STEP - NA_KH // 2, 0, rows - (NA_ROWS_PER_STEP + NA_KH))


def _na_col_offsets():
    c = np.arange(GRID_W)
    kc = c[None, :]
    win_start = np.clip(c - NA_KW // 2, 0, GRID_W - NA_KW)[:, None]
    in_win = (kc >= win_start) & (kc < win_start + NA_KW)
    dc = kc - c[:, None] + (NA_KW - 1)
    return np.concatenate([np.where(in_win, dc, -1), np.where(in_win, dc + NA_TABLE_HALF, -1)],
                          axis=1).astype(np.int32)


def _na_attn(rpb, qa, ka, va):
    b, t, _ = qa.shape
    rows = t // GRID_W
    tq = NA_ROWS_PER_STEP * GRID_W
    tile = pl.BlockSpec((1, tq, A_W), lambda bi, ti: (bi, ti, 0))
    window = pl.BlockSpec((pl.Element(1), pl.Element((NA_ROWS_PER_STEP + NA_KH) * GRID_W), pl.Element(A_W)),
                          lambda bi, ti: (bi, _na_window_start(ti, rows) * GRID_W, 0))
    per_head = jnp.transpose(rpb.astype(F32), (0, 2, 1))
    pad = lambda a, w: jnp.pad(a, ((0, 0), (0, 0), (0, w - a.shape[-1])))
    table = jnp.concatenate([pad(per_head[:-1], NA_TABLE_HALF), pad(per_head[1:], LANES - NA_TABLE_HALF)], axis=-1)
    table = table.reshape(-1, LANES)
    return pl.pallas_call(
        functools.partial(_na_kernel, rows=rows),
        grid=(b, rows // NA_ROWS_PER_STEP),
        in_specs=[_resident(table.shape), _resident((GRID_W, LANES)), tile, window, window],
        out_specs=tile,
        out_shape=jax.ShapeDtypeStruct((b, t, A_W), BF16),
        scratch_shapes=[pltpu.VMEM((NA_HEADS // 2, 2 * NA_KH - 2, LANES, LANES), F32),
                        pltpu.VMEM((NA_SLOTS, 2 * GRID_W, NA_KH * GRID_W), F32)],
        compiler_params=_params(2),
        name="na_attn",
    )(table, jnp.asarray(_na_col_offsets()), qa, ka, va)


def _swa_kernel(sink_ref, t5_ref, bucket_ref, q_ref, kp_ref, kc_ref, kn_ref, vp_ref, vc_ref, vn_ref, o_ref,
                bias_ref, s_ref, *, n_tiles):
    n = pl.program_id(1)
    lo = lax.broadcasted_iota(jnp.int32, (SW_BLOCK, LANES), 1) < HEAD_DIM
    nt = (((1,), (1,)), ((), ()))

    @pl.when((pl.program_id(0) == 0) & (n == 0))
    def _build_bias():
        bucket = bucket_ref[...]

        in_band = bucket >= 0
        take = jnp.maximum(bucket, 0)

        def build(h, carry):
            row = jnp.broadcast_to(t5_ref[pl.ds(h, 1), :] * LOG2E, (SW_BLOCK, LANES))
            vals = [jnp.take_along_axis(row, take[:, c:c + LANES], axis=1) for c in range(0, 3 * SW_BLOCK, LANES)]
            row0 = pl.multiple_of((h % SW_GROUP) * SW_BLOCK, SW_BLOCK)
            bias_ref[h // SW_GROUP, pl.ds(row0, SW_BLOCK), :] = jnp.where(
                in_band, jnp.concatenate(vals, axis=1), NEG_INF)
            return carry

        lax.fori_loop(0, SW_HEADS, build, 0)

    def window(prev_ref, own_ref, next_ref, kvh, j):
        kcols = slice(kvh * LANES, (kvh + 1) * LANES)
        blk = lambda b: own_ref[0, b * SW_BLOCK:(b + 1) * SW_BLOCK, kcols]
        first = prev_ref[0, :, kcols] if j == 0 else blk(j - 1)
        last = next_ref[0, :, kcols] if j == SW_BLOCKS_PER_STEP - 1 else blk(j + 1)
        return jnp.concatenate([first, blk(j), last], axis=0)

    def scores(kvh, j, c, hh, slot):
        rows = slice(j * SW_BLOCK, (j + 1) * SW_BLOCK)
        qb = q_ref[0, rows, (2 * kvh + c) * LANES:(2 * kvh + c + 1) * LANES]
        q1 = jnp.where(lo if hh == 0 else ~lo, qb, jnp.zeros_like(qb))
        s_ref[slot] = lax.dot_general(q1, window(kp_ref, kc_ref, kn_ref, kvh, j), nt,
                                      preferred_element_type=F32)

    ones_cols = jnp.ones((3 * SW_BLOCK, LANES), BF16)

    def finish(kvh, j, c, hh, slot):
        rows = slice(j * SW_BLOCK, (j + 1) * SW_BLOCK)
        head = kvh * SW_GROUP + 2 * c + hh
        brows = slice((2 * c + hh) * SW_BLOCK, (2 * c + hh + 1) * SW_BLOCK)
        sg = s_ref[slot] + bias_ref[kvh, brows, :]
        if j == 0:
            sg = jnp.concatenate([jnp.where(n > 0, sg[:, :SW_BLOCK], NEG_INF), sg[:, SW_BLOCK:]], axis=1)
        if j == SW_BLOCKS_PER_STEP - 1:
            sg = jnp.concatenate([sg[:, :2 * SW_BLOCK],
                                  jnp.where(n < n_tiles - 1, sg[:, 2 * SW_BLOCK:], NEG_INF)], axis=1)
        sk = sink_ref[head] * LOG2E
        m = jnp.maximum(jnp.max(sg, axis=-1, keepdims=True), sk)
        e = jnp.exp2(sg - m).astype(BF16)
        vv = jnp.concatenate([window(vp_ref, vc_ref, vn_ref, kvh, j), ones_cols], axis=1)
        o2 = jnp.dot(e, vv, preferred_element_type=F32)
        o = o2[:, :LANES] / (o2[:, LANES:] + jnp.exp2(sk - m))
        col0 = (2 * kvh + c) * LANES + hh * HEAD_DIM
        o_ref[0, rows, col0:col0 + HEAD_DIM] = o[:, hh * HEAD_DIM:(hh + 1) * HEAD_DIM].astype(BF16)

    units = [(kvh, j, c, hh) for j in range(SW_BLOCKS_PER_STEP) for kvh in range(SW_KV_HEADS)
             for c in range(2) for hh in range(2)]
    n_slots = s_ref.shape[0]
    for idx in range(len(units) + SW_SKEW):
        if idx < len(units):
            scores(*units[idx], idx % n_slots)
        if idx >= SW_SKEW:
            finish(*units[idx - SW_SKEW], (idx - SW_SKEW) % n_slots)


def _t5_bucket(rel):
    half = T5_BUCKETS // 2
    max_exact = half // 2
    ret = (rel > 0).astype(np.int32) * half
    n = np.abs(rel)
    large = max_exact + (np.log(np.maximum(n, 1) / max_exact)
                         / np.log(T5_MAX_DIST / max_exact) * (half - max_exact)).astype(np.int32)
    large = np.minimum(large, half - 1)
    return ret + np.where(n < max_exact, n, large)


def _swa_bucket_index():
    rel = np.arange(3 * SW_BLOCK)[None, :] - SW_BLOCK - np.arange(SW_BLOCK)[:, None]
    return np.where(np.abs(rel) <= SW_WINDOW, _t5_bucket(rel), -1).astype(np.int32)


def _swa_attn(sink, t5_table, qb, kbd, vbd):
    b, t, _ = qb.shape
    tq = SW_BLOCKS_PER_STEP * SW_BLOCK
    n_tiles = t // tq
    last_blk = t // SW_BLOCK - 1
    kv_w = kbd.shape[-1]
    qtile = pl.BlockSpec((1, tq, B_QW), lambda bi, ni: (bi, ni, 0))
    own = pl.BlockSpec((1, tq, kv_w), lambda bi, ni: (bi, ni, 0))
    prev = pl.BlockSpec((1, SW_BLOCK, kv_w),
                        lambda bi, ni: (bi, jnp.maximum(ni * SW_BLOCKS_PER_STEP - 1, 0), 0))
    nxt = pl.BlockSpec((1, SW_BLOCK, kv_w),
                       lambda bi, ni: (bi, jnp.minimum((ni + 1) * SW_BLOCKS_PER_STEP, last_blk), 0))
    smem = pl.BlockSpec(memory_space=pltpu.SMEM)
    table = jnp.pad(t5_table.astype(F32).T, ((0, 0), (0, LANES - t5_table.shape[0])))
    return pl.pallas_call(
        functools.partial(_swa_kernel, n_tiles=n_tiles),
        grid=(b, n_tiles),
        in_specs=[smem, _resident((SW_HEADS, LANES)), _resident((SW_BLOCK, 3 * SW_BLOCK)),
                  qtile, prev, own, nxt, prev, own, nxt],
        out_specs=qtile,
        out_shape=jax.ShapeDtypeStruct((b, t, B_QW), BF16),
        scratch_shapes=[pltpu.VMEM((SW_KV_HEADS, SW_GROUP * SW_BLOCK, 3 * SW_BLOCK), F32),
                        pltpu.VMEM((SW_SLOTS, SW_BLOCK, 3 * SW_BLOCK), F32)],
        compiler_params=_params(2),
        name="swa_attn",
    )(sink.astype(F32), table, jnp.asarray(_swa_bucket_index()),
      qb, kbd, kbd, kbd, vbd, vbd, vbd)


def _mix_kernel(x_ref, oa_ref, ob_ref, g_ref, wg_ref, wa_ref, wb_ref, wo_ref, wgate_f32, wup_f32, wdown_f32,
                x1_ref, wgate_bf, wup_bf, wdown_bf, h_scr, y_scr):
    _cast_rows(((wgate_f32, wgate_bf), (wup_f32, wup_bf), (wdown_f32, wdown_bf)))
    for rb in range(0, x_ref.shape[0], ROW_BLOCK):
        rows = slice(rb, rb + ROW_BLOCK)
        xf = x_ref[rows, :]
        h_scr[rows, :] = ((xf * _rms_scale(xf)) * g_ref[...]).astype(BF16)
        for c in range(0, D_MODEL, MXU_N):
            cs = slice(c, c + MXU_N)
            gs = slice(D_MODEL + c, D_MODEL + c + MXU_N)
            ga = jnp.dot(h_scr[rows, :], wg_ref[:, cs], preferred_element_type=F32)
            gb = jnp.dot(h_scr[rows, :], wg_ref[:, gs], preferred_element_type=F32)
            ya = jnp.dot(oa_ref[rows, :], wa_ref[:, cs], preferred_element_type=F32)
            yb = jnp.dot(ob_ref[rows, :], wb_ref[:, cs], preferred_element_type=F32)
            y_scr[rows, cs] = (jax.nn.sigmoid(ga) * ya + jax.nn.sigmoid(gb) * yb).astype(BF16)
        for c in range(0, D_MODEL, MXU_N):
            cs = slice(c, c + MXU_N)
            x1_ref[rows, cs] = x_ref[rows, cs] + jnp.dot(y_scr[rows, :], wo_ref[:, cs], preferred_element_type=F32)


def _mix_out(x2, oa, ob, g_mix, mix_weights, w_gate, w_up, w_down, layer):
    n = x2.shape[0]
    tm = TOK_TILE
    steps = n // tm
    tile = lambda w: pl.BlockSpec((tm, w), lambda i: (i, 0))
    streamed = (w_gate, w_up, w_down)
    outs = pl.pallas_call(
        _mix_kernel,
        grid=(steps,),
        in_specs=[tile(D_MODEL), tile(A_W), tile(B_QW), _resident((1, D_MODEL))]
                 + [_resident(w.shape) for w in mix_weights]
                 + [_row_stream(w.shape[1], w.shape[2], steps, layer) for w in streamed],
        out_specs=[tile(D_MODEL)]
                  + [pl.BlockSpec((w.shape[1] // steps, w.shape[2]), lambda i: (i, 0)) for w in streamed],
        out_shape=[jax.ShapeDtypeStruct((n, D_MODEL), F32)]
                  + [jax.ShapeDtypeStruct(w.shape[1:], BF16) for w in streamed],
        scratch_shapes=[pltpu.VMEM((tm, D_MODEL), BF16), pltpu.VMEM((tm, D_MODEL), BF16)],
        compiler_params=_params(1),
        name="mix_out",
    )(x2, oa, ob, g_mix, *mix_weights, w_gate, w_up, w_down)
    return outs[0], outs[1:]


def _ffn_kernel(x_ref, g_ref, wg_ref, wu_ref, wd_ref, o_ref, h_scr, a_scr):
    hidden = wg_ref.shape[1]
    for rb in range(0, x_ref.shape[0], ROW_BLOCK):
        rows = slice(rb, rb + ROW_BLOCK)
        xf = x_ref[rows, :]
        h_scr[rows, :] = ((xf * _rms_scale(xf)) * g_ref[...]).astype(BF16)
        for c in range(0, hidden, MXU_N):
            cs = slice(c, c + MXU_N)
            gate = jnp.dot(h_scr[rows, :], wg_ref[:, cs], preferred_element_type=F32)
            up = jnp.dot(h_scr[rows, :], wu_ref[:, cs], preferred_element_type=F32)
            a_scr[rows, cs] = (jax.nn.silu(gate) * up).astype(BF16)
        for c in range(0, D_MODEL, MXU_N):
            cs = slice(c, c + MXU_N)
            o_ref[rows, cs] = x_ref[rows, cs] + jnp.dot(a_scr[rows, :], wd_ref[:, cs], preferred_element_type=F32)


def _ffn(x1, g_ffn, w_gate, w_up, w_down):
    n = x1.shape[0]
    tm = TOK_TILE
    hidden = w_gate.shape[1]
    tile = pl.BlockSpec((tm, D_MODEL), lambda i: (i, 0))
    return pl.pallas_call(
        _ffn_kernel,
        grid=(n // tm,),
        in_specs=[tile, _resident((1, D_MODEL)), _resident(w_gate.shape), _resident(w_up.shape),
                  _resident(w_down.shape)],
        out_specs=tile,
        out_shape=jax.ShapeDtypeStruct((n, D_MODEL), F32),
        scratch_shapes=[pltpu.VMEM((tm, D_MODEL), BF16), pltpu.VMEM((tm, hidden), BF16)],
        compiler_params=_params(1),
        name="ffn",
    )(x1, g_ffn, w_gate, w_up, w_down)


def _layer(x2, b, t, layer, norm_mix, w_in, q_norm_a, k_norm_a, rpb_a, q_norm_b, k_norm_b, sink_b, t5_table,
           w_branch_a, w_branch_b, w_out, norm_ffn, w_gate, w_up, w_down):
    g_mix = norm_mix[layer].reshape(1, D_MODEL)
    reps = MXU_N // HEAD_DIM
    root_d = HEAD_DIM ** 0.5
    q_fold = QK_SCALE * root_d * LOG2E
    qk_gain = jnp.stack([jnp.tile(q_norm_a[layer] * q_fold, reps), jnp.tile(k_norm_a[layer] * root_d, reps),
                         jnp.tile(q_norm_b[layer] * q_fold, reps), jnp.tile(k_norm_b[layer] * root_d, reps)]
                        ).astype(F32)

    (qa, ka, va, qb, kbd, vbd), mix_weights = _in_proj(x2, g_mix, qk_gain, w_in, w_branch_a, w_branch_b,
                                                       w_out, layer)
    n = x2.shape[0]
    r3 = lambda a: a.reshape(b, t, a.shape[-1])
    oa = _na_attn(rpb_a[layer], r3(qa), r3(ka), r3(va))
    ob = _swa_attn(sink_b[layer], t5_table, r3(qb), r3(kbd), r3(vbd))
    x1, ffn_weights = _mix_out(x2, oa.reshape(n, A_W), ob.reshape(n, B_QW), g_mix, mix_weights,
                               w_gate, w_up, w_down, layer)
    return _ffn(x1, norm_ffn[layer].reshape(1, D_MODEL), *ffn_weights)


def kernel(x, norm_mix, w_in, q_norm_a, k_norm_a, rpb_a, q_norm_b, k_norm_b, sink_b, t5_table,
           w_branch_a, w_branch_b, w_out, norm_ffn, w_gate, w_up, w_down):
    b, t, d = x.shape
    x2 = x.reshape(b * t, d)
    for layer in range(norm_mix.shape[0]):
        x2 = _layer(x2, b, t, layer, norm_mix, w_in, q_norm_a, k_norm_a, rpb_a, q_norm_b, k_norm_b, sink_b,
                    t5_table, w_branch_a, w_branch_b, w_out, norm_ffn, w_gate, w_up, w_down)
    return x2.reshape(b, t, d)
```

```python
import functools

import jax
import jax.numpy as jnp
import numpy as np
from jax import lax
from jax.experimental import pallas as pl
from jax.experimental.pallas import tpu as pltpu

F32 = jnp.float32
BF16 = jnp.bfloat16

D_MODEL = 1024
HEAD_DIM = 64
GRID_W = 64
NA_HEADS = 8
NA_KH = 8
NA_KW = 16
SW_HEADS = 8
SW_KV_HEADS = 2
SW_GROUP = SW_HEADS // SW_KV_HEADS
SW_WINDOW = 128
SW_BLOCK = 128
T5_BUCKETS = 32
T5_MAX_DIST = 128
A_W = NA_HEADS * HEAD_DIM
B_QW = SW_HEADS * HEAD_DIM
B_KVW = SW_KV_HEADS * HEAD_DIM
QKV_W = 3 * A_W + B_QW + 2 * B_KVW
RMS_EPS = 1e-6
NEG_INF = -1e30
QK_SCALE = HEAD_DIM ** -0.5
LOG2E = 1.4426950408889634

LANES = 128
MXU_N = 256
VMEM_LIMIT = 56 * 1024 * 1024

TOK_TILE = 1024
ROW_BLOCK = 256
NA_ROWS_PER_STEP = 32
PROJ_SKEW = 4
NA_SKEW = 4
SW_SKEW = 4
SW_BLOCKS_PER_STEP = 16
NA_SLOTS = NA_SKEW + 1
NA_TABLE_HALF = 32
SW_SLOTS = SW_SKEW + 1


def _resident(shape):
    return pl.BlockSpec(shape, lambda *_: (0,) * len(shape), pipeline_mode=pl.Buffered(1))


def _params(n_axes):
    return pltpu.CompilerParams(dimension_semantics=("arbitrary",) * n_axes, vmem_limit_bytes=VMEM_LIMIT)


def _rms_scale(xf):
    return lax.rsqrt(jnp.mean(xf * xf, axis=-1, keepdims=True) + RMS_EPS)


def _group_rms_scale(p):
    y = p * p
    lo = lax.broadcasted_iota(jnp.int32, (p.shape[0], LANES), 1) < HEAD_DIM
    scales = []
    for c in range(0, p.shape[1], LANES):
        yc = y[:, c:c + LANES]
        s_lo = jnp.sum(jnp.where(lo, yc, 0.0), axis=-1, keepdims=True)
        s_hi = jnp.sum(jnp.where(lo, 0.0, yc), axis=-1, keepdims=True)
        r_lo = lax.rsqrt(s_lo + HEAD_DIM * RMS_EPS)
        r_hi = lax.rsqrt(s_hi + HEAD_DIM * RMS_EPS)
        scales.append(jnp.where(lo, r_lo, r_hi))
    return jnp.concatenate(scales, axis=1)


def _cast_rows(pairs):
    for src_ref, dst_ref in pairs:
        dst_ref[...] = src_ref[:, src_ref.shape[1] - dst_ref.shape[1]:].astype(BF16)


def _in_proj_kernel(x_ref, g_ref, w_ref, qk_gain_ref, wg_f32, wa_f32, wb_f32, wo_f32,
                    qa_ref, ka_ref, va_ref, qb_ref, kbd_ref, vbd_ref, wg_bf, wa_bf, wb_bf, wo_bf,
                    w_scr, h_scr, p_scr):
    @pl.when(pl.program_id(0) == 0)
    def _cast_own_weights():
        for c in range(0, QKV_W, MXU_N):
            w_scr[:, c:c + MXU_N] = w_ref[:, c:c + MXU_N].astype(BF16)

    _cast_rows(((wg_f32, wg_bf), (wa_f32, wa_bf), (wb_f32, wb_bf), (wo_f32, wo_bf)))

    def normed(out_ref, rows, c, gain_row):
        def epilogue(slot):
            p = p_scr[slot]
            scale = _group_rms_scale(p)
            out_ref[rows, c:c + MXU_N] = (p * scale * qk_gain_ref[gain_row:gain_row + 1, :]).astype(BF16)
        return epilogue

    def plain(out_ref, rows, c):
        def epilogue(slot):
            out_ref[rows, c:c + MXU_N] = p_scr[slot].astype(BF16)
        return epilogue

    def kv_b(rows):
        def epilogue(slot):
            kb = p_scr[slot, :, :B_KVW]
            kb = kb * _group_rms_scale(kb) * qk_gain_ref[3:4, :LANES]
            vb = p_scr[slot, :, B_KVW:]
            lo = lax.broadcasted_iota(jnp.int32, kb.shape, 1) < HEAD_DIM
            k_swapped = pltpu.roll(kb, HEAD_DIM, axis=1)
            kbd_ref[rows, :LANES] = jnp.where(lo, kb, k_swapped).astype(BF16)
            kbd_ref[rows, LANES:] = jnp.where(lo, k_swapped, kb).astype(BF16)
            v_swapped = pltpu.roll(vb, HEAD_DIM, axis=1)
            for blk, (left, right) in enumerate(((vb, 1.0), (1.0, v_swapped), (v_swapped, 1.0), (1.0, vb))):
                vbd_ref[rows, blk * LANES:(blk + 1) * LANES] = jnp.where(lo, left, right).astype(BF16)
        return epilogue

    units = []
    for rb in range(0, x_ref.shape[0], ROW_BLOCK):
        rows = slice(rb, rb + ROW_BLOCK)
        xf = x_ref[rows, :]
        h_scr[rows, :] = ((xf * _rms_scale(xf)) * g_ref[...]).astype(BF16)
        for c in range(0, A_W, MXU_N):
            units += [(rows, c, normed(qa_ref, rows, c, 0)), (rows, A_W + c, normed(ka_ref, rows, c, 1)),
                      (rows, 3 * A_W + c, normed(qb_ref, rows, c, 2))]
        units.append((rows, 3 * A_W + B_QW, kv_b(rows)))
        units += [(rows, 2 * A_W + c, plain(va_ref, rows, c)) for c in range(0, A_W, MXU_N)]
    n_slots = PROJ_SKEW + 1
    for idx in range(len(units) + PROJ_SKEW):
        if idx < len(units):
            rows, c0, _ = units[idx]
            p_scr[idx % n_slots] = jnp.dot(h_scr[rows, :], w_scr[:, c0:c0 + MXU_N], preferred_element_type=F32)
        if idx >= PROJ_SKEW:
            units[idx - PROJ_SKEW][2]((idx - PROJ_SKEW) % n_slots)


def _row_stream(n_rows, n_cols, steps, layer):
    return pl.BlockSpec((None, n_rows // steps, n_cols), lambda i: (layer, i, 0))


def _in_proj(x2, g_mix, qk_gain, w_in, w_a, w_b, w_o, layer):
    n = x2.shape[0]
    tm = TOK_TILE
    steps = n // tm
    tile = lambda w: pl.BlockSpec((tm, w), lambda i: (i, 0))
    gate_w = w_in.shape[2] - QKV_W
    out_w = (A_W, A_W, A_W, B_QW, 2 * LANES, 4 * LANES)
    streamed = ((w_in, gate_w), (w_a, D_MODEL), (w_b, D_MODEL), (w_o, D_MODEL))
    outs = pl.pallas_call(
        _in_proj_kernel,
        grid=(steps,),
        in_specs=[tile(D_MODEL), _resident((1, D_MODEL)),
                  pl.BlockSpec((None, D_MODEL, QKV_W), lambda i: (layer, 0, 0), pipeline_mode=pl.Buffered(1)),
                  _resident((4, MXU_N))]
                 + [_row_stream(w.shape[1], w.shape[2], steps, layer) for w, _ in streamed],
        out_specs=[tile(w) for w in out_w]
                  + [pl.BlockSpec((w.shape[1] // steps, cols), lambda i: (i, 0)) for w, cols in streamed],
        out_shape=[jax.ShapeDtypeStruct((n, w), BF16) for w in out_w]
                  + [jax.ShapeDtypeStruct((w.shape[1], cols), BF16) for w, cols in streamed],
        scratch_shapes=[pltpu.VMEM((D_MODEL, QKV_W), BF16), pltpu.VMEM((tm, D_MODEL), BF16),
                        pltpu.VMEM((PROJ_SKEW + 1, ROW_BLOCK, MXU_N), F32)],
        compiler_params=_params(1),
        name="in_proj",
    )(x2, g_mix, w_in, qk_gain, w_in, w_a, w_b, w_o)
    return outs[:6], outs[6:]


def _na_kernel(rpb_ref, dc_ref, q_ref, k_ref, v_ref, o_ref, bias_ref, s_ref, *, rows):
    t = pl.program_id(1)
    lo = lax.broadcasted_iota(jnp.int32, (GRID_W, LANES), 1) < HEAD_DIM
    nt = (((1,), (1,)), ((), ()))
    n_slab = 2 * NA_KH - 2

    @pl.when((pl.program_id(0) == 0) & (t == 0))
    def _build_bias():
        offs = dc_ref[...]
        in_win = offs >= 0
        take = jnp.maximum(offs, 0)

        def build(idx, carry):
            h = idx // n_slab
            d = idx - h * n_slab
            row = jnp.broadcast_to(rpb_ref[pl.ds(d * NA_HEADS + h, 1), :] * LOG2E, (GRID_W, LANES))
            slab = jnp.where(in_win, jnp.take_along_axis(row, take, axis=1), NEG_INF)
            row0 = pl.multiple_of((h % 2) * GRID_W, GRID_W)
            bias_ref[h // 2, d, pl.ds(row0, GRID_W), :] = slab
            return carry

        lax.fori_loop(0, NA_HEADS * n_slab, build, 0, unroll=NA_HEADS)

    win0 = _na_window_start(t, rows)
    row_info = []
    for i in range(NA_ROWS_PER_STEP):
        r = t * NA_ROWS_PER_STEP + i
        rs = jnp.clip(r - NA_KH // 2, 0, rows - NA_KH)
        row_info.append(((NA_KH - 1) - (r - rs), pl.multiple_of((rs - win0) * GRID_W, GRID_W)))

    def scores(i, p, slot):
        d0, kstart = row_info[i]
        cols = slice(p * LANES, (p + 1) * LANES)
        q = q_ref[0, i * GRID_W:(i + 1) * GRID_W, cols]
        zero = jnp.zeros_like(q)
        q2 = jnp.concatenate([jnp.where(lo, q, zero), jnp.where(lo, zero, q)], axis=0)
        kk = k_ref[0, pl.ds(kstart, NA_KH * GRID_W), cols]
        s = lax.dot_general(q2, kk, nt, preferred_element_type=F32)
        for v in range(NA_KH // 2):
            vcols = slice(v * LANES, (v + 1) * LANES)
            s_ref[slot, :, vcols] = s[:, vcols] + bias_ref[p, d0 + 2 * v]

    ones_cols = jnp.ones((NA_KH * GRID_W, LANES), BF16)

    def finish(i, p, slot):
        _, kstart = row_info[i]
        cols = slice(p * LANES, (p + 1) * LANES)
        m = jnp.max(s_ref[slot], axis=-1, keepdims=True)
        e = jnp.exp2(s_ref[slot] - m).astype(BF16)
        vv = jnp.concatenate([v_ref[0, pl.ds(kstart, NA_KH * GRID_W), cols], ones_cols], axis=1)
        o2 = jnp.dot(e, vv, preferred_element_type=F32)
        o2 = o2[:, :LANES] / o2[:, LANES:]
        o = jnp.where(lo, o2[:GRID_W], o2[GRID_W:])
        o_ref[0, i * GRID_W:(i + 1) * GRID_W, cols] = o.astype(BF16)

    units = [(i, p) for i in range(NA_ROWS_PER_STEP) for p in range(NA_HEADS // 2)]
    n_slots = s_ref.shape[0]
    for idx in range(len(units) + NA_SKEW):
        if idx < len(units):
            scores(*units[idx], idx % n_slots)
        if idx >= NA_SKEW:
            finish(*units[idx - NA_SKEW], (idx - NA_SKEW) % n_slots)


def _na_window_start(step, rows):
    return jnp.clip(step * NA_ROWS_PER_STEP - NA_KH // 2, 0, rows - (NA_ROWS_PER_STEP + NA_KH))


def _na_col_offsets():
    c = np.arange(GRID_W)
    kc = c[None, :]
    win_start = np.clip(c - NA_KW // 2, 0, GRID_W - NA_KW)[:, None]
    in_win = (kc >= win_start) & (kc < win_start + NA_KW)
    dc = kc - c[:, None] + (NA_KW - 1)
    return np.concatenate([np.where(in_win, dc, -1), np.where(in_win, dc + NA_TABLE_HALF, -1)],
                          axis=1).astype(np.int32)


def _na_attn(rpb, qa, ka, va):
    b, t, _ = qa.shape
    rows = t // GRID_W
    tq = NA_ROWS_PER_STEP * GRID_W
    tile = pl.BlockSpec((1, tq, A_W), lambda bi, ti: (bi, ti, 0))
    window = pl.BlockSpec((pl.Element(1), pl.Element((NA_ROWS_PER_STEP + NA_KH) * GRID_W), pl.Element(A_W)),
                          lambda bi, ti: (bi, _na_window_start(ti, rows) * GRID_W, 0))
    per_head = jnp.transpose(rpb.astype(F32), (0, 2, 1))
    pad = lambda a, w: jnp.pad(a, ((0, 0), (0, 0), (0, w - a.shape[-1])))
    table = jnp.concatenate([pad(per_head[:-1], NA_TABLE_HALF), pad(per_head[1:], LANES - NA_TABLE_HALF)], axis=-1)
    table = table.reshape(-1, LANES)
    return pl.pallas_call(
        functools.partial(_na_kernel, rows=rows),
        grid=(b, rows // NA_ROWS_PER_STEP),
        in_specs=[_resident(table.shape), _resident((GRID_W, LANES)), tile, window, window],
        out_specs=tile,
        out_shape=jax.ShapeDtypeStruct((b, t, A_W), BF16),
        scratch_shapes=[pltpu.VMEM((NA_HEADS // 2, 2 * NA_KH - 2, LANES, LANES), F32),
                        pltpu.VMEM((NA_SLOTS, 2 * GRID_W, NA_KH * GRID_W), F32)],
        compiler_params=_params(2),
        name="na_attn",
    )(table, jnp.asarray(_na_col_offsets()), qa, ka, va)


def _swa_kernel(sink_ref, t5_ref, bucket_ref, q_ref, kp_ref, kc_ref, kn_ref, vp_ref, vc_ref, vn_ref, o_ref,
                bias_ref, s_ref, *, n_tiles):
    n = pl.program_id(1)
    lo = lax.broadcasted_iota(jnp.int32, (SW_BLOCK, LANES), 1) < HEAD_DIM
    nt = (((1,), (1,)), ((), ()))

    @pl.when((pl.program_id(0) == 0) & (n == 0))
    def _build_bias():
        bucket = bucket_ref[...]

        in_band = bucket >= 0
        take = jnp.maximum(bucket, 0)

        def build(h, carry):
            row = jnp.broadcast_to(t5_ref[pl.ds(h, 1), :] * LOG2E, (SW_BLOCK, LANES))
            vals = [jnp.take_along_axis(row, take[:, c:c + LANES], axis=1) for c in range(0, 3 * SW_BLOCK, LANES)]
            row0 = pl.multiple_of((h % SW_GROUP) * SW_BLOCK, SW_BLOCK)
            bias_ref[h // SW_GROUP, pl.ds(row0, SW_BLOCK), :] = jnp.where(
                in_band, jnp.concatenate(vals, axis=1), NEG_INF)
            return carry

        lax.fori_loop(0, SW_HEADS, build, 0)

    def window(prev_ref, own_ref, next_ref, blk_col, j):
        kcols = slice(blk_col * LANES, (blk_col + 1) * LANES)
        blk = lambda b: own_ref[0, b * SW_BLOCK:(b + 1) * SW_BLOCK, kcols]
        first = prev_ref[0, :, kcols] if j == 0 else blk(j - 1)
        last = next_ref[0, :, kcols] if j == SW_BLOCKS_PER_STEP - 1 else blk(j + 1)
        return jnp.concatenate([first, blk(j), last], axis=0)

    def scores(kvh, j, c, hh, slot):
        rows = slice(j * SW_BLOCK, (j + 1) * SW_BLOCK)
        qb = q_ref[0, rows, (2 * kvh + c) * LANES:(2 * kvh + c + 1) * LANES]
        q1 = jnp.where(lo if hh == 0 else ~lo, qb, jnp.zeros_like(qb))
        s_ref[slot] = lax.dot_general(q1, window(kp_ref, kc_ref, kn_ref, kvh, j), nt,
                                      preferred_element_type=F32)

    def finish(kvh, j, c, hh, slot):
        rows = slice(j * SW_BLOCK, (j + 1) * SW_BLOCK)
        head = kvh * SW_GROUP + 2 * c + hh
        brows = slice((2 * c + hh) * SW_BLOCK, (2 * c + hh + 1) * SW_BLOCK)
        sg = s_ref[slot] + bias_ref[kvh, brows, :]
        if j == 0:
            sg = jnp.concatenate([jnp.where(n > 0, sg[:, :SW_BLOCK], NEG_INF), sg[:, SW_BLOCK:]], axis=1)
        if j == SW_BLOCKS_PER_STEP - 1:
            sg = jnp.concatenate([sg[:, :2 * SW_BLOCK],
                                  jnp.where(n < n_tiles - 1, sg[:, 2 * SW_BLOCK:], NEG_INF)], axis=1)
        sk = sink_ref[head] * LOG2E
        m = jnp.maximum(jnp.max(sg, axis=-1, keepdims=True), sk)
        e = jnp.exp2(sg - m).astype(BF16)
        o2 = jnp.dot(e, window(vp_ref, vc_ref, vn_ref, 2 * kvh + hh, j), preferred_element_type=F32)
        o = o2 / (pltpu.roll(o2, HEAD_DIM, axis=1) + jnp.exp2(sk - m))
        col0 = (2 * kvh + c) * LANES + hh * HEAD_DIM
        o_ref[0, rows, col0:col0 + HEAD_DIM] = o[:, hh * HEAD_DIM:(hh + 1) * HEAD_DIM].astype(BF16)

    units = [(kvh, j, c, hh) for j in range(SW_BLOCKS_PER_STEP) for kvh in range(SW_KV_HEADS)
             for c in range(2) for hh in range(2)]
    n_slots = s_ref.shape[0]
    for idx in range(len(units) + SW_SKEW):
        if idx < len(units):
            scores(*units[idx], idx % n_slots)
        if idx >= SW_SKEW:
            finish(*units[idx - SW_SKEW], (idx - SW_SKEW) % n_slots)


def _t5_bucket(rel):
    half = T5_BUCKETS // 2
    max_exact = half // 2
    ret = (rel > 0).astype(np.int32) * half
    n = np.abs(rel)
    large = max_exact + (np.log(np.maximum(n, 1) / max_exact)
                         / np.log(T5_MAX_DIST / max_exact) * (half - max_exact)).astype(np.int32)
    large = np.minimum(large, half - 1)
    return ret + np.where(n < max_exact, n, large)


def _swa_bucket_index():
    rel = np.arange(3 * SW_BLOCK)[None, :] - SW_BLOCK - np.arange(SW_BLOCK)[:, None]
    return np.where(np.abs(rel) <= SW_WINDOW, _t5_bucket(rel), -1).astype(np.int32)


def _swa_attn(sink, t5_table, qb, kbd, vbd):
    b, t, _ = qb.shape
    tq = SW_BLOCKS_PER_STEP * SW_BLOCK
    n_tiles = t // tq
    last_blk = t // SW_BLOCK - 1
    qtile = pl.BlockSpec((1, tq, B_QW), lambda bi, ni: (bi, ni, 0))
    own = lambda w: pl.BlockSpec((1, tq, w), lambda bi, ni: (bi, ni, 0))
    prev = lambda w: pl.BlockSpec((1, SW_BLOCK, w),
                                  lambda bi, ni: (bi, jnp.maximum(ni * SW_BLOCKS_PER_STEP - 1, 0), 0))
    nxt = lambda w: pl.BlockSpec((1, SW_BLOCK, w),
                                 lambda bi, ni: (bi, jnp.minimum((ni + 1) * SW_BLOCKS_PER_STEP, last_blk), 0))
    kw, vw = kbd.shape[-1], vbd.shape[-1]
    smem = pl.BlockSpec(memory_space=pltpu.SMEM)
    table = jnp.pad(t5_table.astype(F32).T, ((0, 0), (0, LANES - t5_table.shape[0])))
    return pl.pallas_call(
        functools.partial(_swa_kernel, n_tiles=n_tiles),
        grid=(b, n_tiles),
        in_specs=[smem, _resident((SW_HEADS, LANES)), _resident((SW_BLOCK, 3 * SW_BLOCK)),
                  qtile, prev(kw), own(kw), nxt(kw), prev(vw), own(vw), nxt(vw)],
        out_specs=qtile,
        out_shape=jax.ShapeDtypeStruct((b, t, B_QW), BF16),
        scratch_shapes=[pltpu.VMEM((SW_KV_HEADS, SW_GROUP * SW_BLOCK, 3 * SW_BLOCK), F32),
                        pltpu.VMEM((SW_SLOTS, SW_BLOCK, 3 * SW_BLOCK), F32)],
        compiler_params=_params(2),
        name="swa_attn",
    )(sink.astype(F32), table, jnp.asarray(_swa_bucket_index()),
      qb, kbd, kbd, kbd, vbd, vbd, vbd)


def _mix_kernel(x_ref, oa_ref, ob_ref, g_ref, wg_ref, wa_ref, wb_ref, wo_ref, wgate_f32, wup_f32, wdown_f32,
                x1_ref, wgate_bf, wup_bf, wdown_bf, h_scr, y_scr):
    _cast_rows(((wgate_f32, wgate_bf), (wup_f32, wup_bf), (wdown_f32, wdown_bf)))
    for rb in range(0, x_ref.shape[0], ROW_BLOCK):
        rows = slice(rb, rb + ROW_BLOCK)
        xf = x_ref[rows, :]
        h_scr[rows, :] = ((xf * _rms_scale(xf)) * g_ref[...]).astype(BF16)
        for c in range(0, D_MODEL, MXU_N):
            cs = slice(c, c + MXU_N)
            gs = slice(D_MODEL + c, D_MODEL + c + MXU_N)
            ga = jnp.dot(h_scr[rows, :], wg_ref[:, cs], preferred_element_type=F32)
            gb = jnp.dot(h_scr[rows, :], wg_ref[:, gs], preferred_element_type=F32)
            ya = jnp.dot(oa_ref[rows, :], wa_ref[:, cs], preferred_element_type=F32)
            yb = jnp.dot(ob_ref[rows, :], wb_ref[:, cs], preferred_element_type=F32)
            y_scr[rows, cs] = (jax.nn.sigmoid(ga) * ya + jax.nn.sigmoid(gb) * yb).astype(BF16)
        for c in range(0, D_MODEL, MXU_N):
            cs = slice(c, c + MXU_N)
            x1_ref[rows, cs] = x_ref[rows, cs] + jnp.dot(y_scr[rows, :], wo_ref[:, cs], preferred_element_type=F32)


def _mix_out(x2, oa, ob, g_mix, mix_weights, w_gate, w_up, w_down, layer):
    n = x2.shape[0]
    tm = TOK_TILE
    steps = n // tm
    tile = lambda w: pl.BlockSpec((tm, w), lambda i: (i, 0))
    streamed = (w_gate, w_up, w_down)
    outs = pl.pallas_call(
        _mix_kernel,
        grid=(steps,),
        in_specs=[tile(D_MODEL), tile(A_W), tile(B_QW), _resident((1, D_MODEL))]
                 + [_resident(w.shape) for w in mix_weights]
                 + [_row_stream(w.shape[1], w.shape[2], steps, layer) for w in streamed],
        out_specs=[tile(D_MODEL)]
                  + [pl.BlockSpec((w.shape[1] // steps, w.shape[2]), lambda i: (i, 0)) for w in streamed],
        out_shape=[jax.ShapeDtypeStruct((n, D_MODEL), F32)]
                  + [jax.ShapeDtypeStruct(w.shape[1:], BF16) for w in streamed],
        scratch_shapes=[pltpu.VMEM((tm, D_MODEL), BF16), pltpu.VMEM((tm, D_MODEL), BF16)],
        compiler_params=_params(1),
        name="mix_out",
    )(x2, oa, ob, g_mix, *mix_weights, w_gate, w_up, w_down)
    return outs[0], outs[1:]


def _ffn_kernel(x_ref, g_ref, wg_ref, wu_ref, wd_ref, o_ref, h_scr, a_scr):
    hidden = wg_ref.shape[1]
    for rb in range(0, x_ref.shape[0], ROW_BLOCK):
        rows = slice(rb, rb + ROW_BLOCK)
        xf = x_ref[rows, :]
        h_scr[rows, :] = ((xf * _rms_scale(xf)) * g_ref[...]).astype(BF16)
        for c in range(0, hidden, MXU_N):
            cs = slice(c, c + MXU_N)
            gate = jnp.dot(h_scr[rows, :], wg_ref[:, cs], preferred_element_type=F32)
            up = jnp.dot(h_scr[rows, :], wu_ref[:, cs], preferred_element_type=F32)
            a_scr[rows, cs] = (jax.nn.silu(gate) * up).astype(BF16)
        for c in range(0, D_MODEL, MXU_N):
            cs = slice(c, c + MXU_N)
            o_ref[rows, cs] = x_ref[rows, cs] + jnp.dot(a_scr[rows, :], wd_ref[:, cs], preferred_element_type=F32)


def _ffn(x1, g_ffn, w_gate, w_up, w_down):
    n = x1.shape[0]
    tm = TOK_TILE
    hidden = w_gate.shape[1]
    tile = pl.BlockSpec((tm, D_MODEL), lambda i: (i, 0))
    return pl.pallas_call(
        _ffn_kernel,
        grid=(n // tm,),
        in_specs=[tile, _resident((1, D_MODEL)), _resident(w_gate.shape), _resident(w_up.shape),
                  _resident(w_down.shape)],
        out_specs=tile,
        out_shape=jax.ShapeDtypeStruct((n, D_MODEL), F32),
        scratch_shapes=[pltpu.VMEM((tm, D_MODEL), BF16), pltpu.VMEM((tm, hidden), BF16)],
        compiler_params=_params(1),
        name="ffn",
    )(x1, g_ffn, w_gate, w_up, w_down)


def _layer(x2, b, t, layer, norm_mix, w_in, q_norm_a, k_norm_a, rpb_a, q_norm_b, k_norm_b, sink_b, t5_table,
           w_branch_a, w_branch_b, w_out, norm_ffn, w_gate, w_up, w_down):
    g_mix = norm_mix[layer].reshape(1, D_MODEL)
    reps = MXU_N // HEAD_DIM
    root_d = HEAD_DIM ** 0.5
    q_fold = QK_SCALE * root_d * LOG2E
    qk_gain = jnp.stack([jnp.tile(q_norm_a[layer] * q_fold, reps), jnp.tile(k_norm_a[layer] * root_d, reps),
                         jnp.tile(q_norm_b[layer] * q_fold, reps), jnp.tile(k_norm_b[layer] * root_d, reps)]
                        ).astype(F32)

    (qa, ka, va, qb, kbd, vbd), mix_weights = _in_proj(x2, g_mix, qk_gain, w_in, w_branch_a, w_branch_b,
                                                       w_out, layer)
    n = x2.shape[0]
    r3 = lambda a: a.reshape(b, t, a.shape[-1])
    oa = _na_attn(rpb_a[layer], r3(qa), r3(ka), r3(va))
    ob = _swa_attn(sink_b[layer], t5_table, r3(qb), r3(kbd), r3(vbd))
    x1, ffn_weights = _mix_out(x2, oa.reshape(n, A_W), ob.reshape(n, B_QW), g_mix, mix_weights,
                               w_gate, w_up, w_down, layer)
    return _ffn(x1, norm_ffn[layer].reshape(1, D_MODEL), *ffn_weights)


def kernel(x, norm_mix, w_in, q_norm_a, k_norm_a, rpb_a, q_norm_b, k_norm_b, sink_b, t5_table,
           w_branch_a, w_branch_b, w_out, norm_ffn, w_gate, w_up, w_down):
    b, t, d = x.shape
    x2 = x.reshape(b * t, d)
    for layer in range(norm_mix.shape[0]):
        x2 = _layer(x2, b, t, layer, norm_mix, w_in, q_norm_a, k_norm_a, rpb_a, q_norm_b, k_norm_b, sink_b,
                    t5_table, w_branch_a, w_branch_b, w_out, norm_ffn, w_gate, w_up, w_down)
    return x2.reshape(b, t, d)
```

```python
import functools

import jax
import jax.numpy as jnp
import numpy as np
from jax import lax
from jax.experimental import pallas as pl
from jax.experimental.pallas import tpu as pltpu

F32 = jnp.float32
BF16 = jnp.bfloat16

D_MODEL = 1024
HEAD_DIM = 64
GRID_W = 64
NA_HEADS = 8
NA_KH = 8
NA_KW = 16
SW_HEADS = 8
SW_KV_HEADS = 2
SW_GROUP = SW_HEADS // SW_KV_HEADS
SW_WINDOW = 128
SW_BLOCK = 128
T5_BUCKETS = 32
T5_MAX_DIST = 128
A_W = NA_HEADS * HEAD_DIM
B_QW = SW_HEADS * HEAD_DIM
B_KVW = SW_KV_HEADS * HEAD_DIM
QKV_W = 3 * A_W + B_QW + 2 * B_KVW
RMS_EPS = 1e-6
NEG_INF = -1e30
QK_SCALE = HEAD_DIM ** -0.5
LOG2E = 1.4426950408889634

LANES = 128
MXU_N = 256
VMEM_LIMIT = 56 * 1024 * 1024

TOK_TILE = 1024
ROW_BLOCK = 256
NA_ROWS_PER_STEP = 32
PROJ_SKEW = 4
NA_SKEW = 4
SW_SKEW = 4
SW_BLOCKS_PER_STEP = 16
NA_SLOTS = NA_SKEW + 1
NA_TABLE_HALF = 32
SW_SLOTS = SW_SKEW + 1


def _resident(shape):
    return pl.BlockSpec(shape, lambda *_: (0,) * len(shape), pipeline_mode=pl.Buffered(1))


def _params(n_axes):
    return pltpu.CompilerParams(dimension_semantics=("arbitrary",) * n_axes, vmem_limit_bytes=VMEM_LIMIT)


def _rms_scale(xf):
    return lax.rsqrt(jnp.mean(xf * xf, axis=-1, keepdims=True) + RMS_EPS)


def _group_rms_scale(p):
    y = p * p
    lo = lax.broadcasted_iota(jnp.int32, (p.shape[0], LANES), 1) < HEAD_DIM
    scales = []
    for c in range(0, p.shape[1], LANES):
        yc = y[:, c:c + LANES]
        s_lo = jnp.sum(jnp.where(lo, yc, 0.0), axis=-1, keepdims=True)
        s_hi = jnp.sum(jnp.where(lo, 0.0, yc), axis=-1, keepdims=True)
        r_lo = lax.rsqrt(s_lo + HEAD_DIM * RMS_EPS)
        r_hi = lax.rsqrt(s_hi + HEAD_DIM * RMS_EPS)
        scales.append(jnp.where(lo, r_lo, r_hi))
    return jnp.concatenate(scales, axis=1)


def _cast_rows(pairs):
    for src_ref, dst_ref in pairs:
        dst_ref[...] = src_ref[:, src_ref.shape[1] - dst_ref.shape[1]:].astype(BF16)


def _in_proj_kernel(x_ref, g_ref, w_ref, qk_gain_ref, wg_f32, wa_f32, wb_f32, wo_f32,
                    qa_ref, ka_ref, va_ref, qb_ref, kbd_ref, vbd_ref, wg_bf, wa_bf, wb_bf, wo_bf,
                    w_scr, h_scr, p_scr):
    @pl.when(pl.program_id(0) == 0)
    def _cast_own_weights():
        for c in range(0, QKV_W, MXU_N):
            w_scr[:, c:c + MXU_N] = w_ref[:, c:c + MXU_N].astype(BF16)

    _cast_rows(((wg_f32, wg_bf), (wa_f32, wa_bf), (wb_f32, wb_bf), (wo_f32, wo_bf)))

    def normed(out_ref, rows, c, gain_row):
        def epilogue(slot):
            p = p_scr[slot]
            scale = _group_rms_scale(p)
            out_ref[rows, c:c + MXU_N] = (p * scale * qk_gain_ref[gain_row:gain_row + 1, :]).astype(BF16)
        return epilogue

    def plain(out_ref, rows, c):
        def epilogue(slot):
            out_ref[rows, c:c + MXU_N] = p_scr[slot].astype(BF16)
        return epilogue

    def kv_b(rows):
        def epilogue(slot):
            kb = p_scr[slot, :, :B_KVW]
            kb = kb * _group_rms_scale(kb) * qk_gain_ref[3:4, :LANES]
            vb = p_scr[slot, :, B_KVW:]
            lo = lax.broadcasted_iota(jnp.int32, kb.shape, 1) < HEAD_DIM
            k_swapped = pltpu.roll(kb, HEAD_DIM, axis=1)
            kbd_ref[rows, :LANES] = jnp.where(lo, kb, k_swapped).astype(BF16)
            kbd_ref[rows, LANES:] = jnp.where(lo, k_swapped, kb).astype(BF16)
            v_swapped = pltpu.roll(vb, HEAD_DIM, axis=1)
            for blk, (left, right) in enumerate(((vb, 1.0), (1.0, v_swapped), (v_swapped, 1.0), (1.0, vb))):
                vbd_ref[rows, blk * LANES:(blk + 1) * LANES] = jnp.where(lo, left, right).astype(BF16)
        return epilogue

    units = []
    for rb in range(0, x_ref.shape[0], ROW_BLOCK):
        rows = slice(rb, rb + ROW_BLOCK)
        xf = x_ref[rows, :]
        h_scr[rows, :] = ((xf * _rms_scale(xf)) * g_ref[...]).astype(BF16)
        for c in range(0, A_W, MXU_N):
            units += [(rows, c, normed(qa_ref, rows, c, 0)), (rows, A_W + c, normed(ka_ref, rows, c, 1)),
                      (rows, 3 * A_W + c, normed(qb_ref, rows, c, 2))]
        units.append((rows, 3 * A_W + B_QW, kv_b(rows)))
        units += [(rows, 2 * A_W + c, plain(va_ref, rows, c)) for c in range(0, A_W, MXU_N)]
    n_slots = PROJ_SKEW + 1
    for idx in range(len(units) + PROJ_SKEW):
        if idx < len(units):
            rows, c0, _ = units[idx]
            p_scr[idx % n_slots] = jnp.dot(h_scr[rows, :], w_scr[:, c0:c0 + MXU_N], preferred_element_type=F32)
        if idx >= PROJ_SKEW:
            units[idx - PROJ_SKEW][2]((idx - PROJ_SKEW) % n_slots)


def _row_stream(n_rows, n_cols, steps, layer):
    return pl.BlockSpec((None, n_rows // steps, n_cols), lambda i: (layer, i, 0))


def _in_proj(x2, g_mix, qk_gain, w_in, w_a, w_b, w_o, layer):
    n = x2.shape[0]
    tm = TOK_TILE
    steps = n // tm
    tile = lambda w: pl.BlockSpec((tm, w), lambda i: (i, 0))
    gate_w = w_in.shape[2] - QKV_W
    out_w = (A_W, A_W, A_W, B_QW, 2 * LANES, 4 * LANES)
    streamed = ((w_in, gate_w), (w_a, D_MODEL), (w_b, D_MODEL), (w_o, D_MODEL))
    outs = pl.pallas_call(
        _in_proj_kernel,
        grid=(steps,),
        in_specs=[tile(D_MODEL), _resident((1, D_MODEL)),
                  pl.BlockSpec((None, D_MODEL, QKV_W), lambda i: (layer, 0, 0), pipeline_mode=pl.Buffered(1)),
                  _resident((4, MXU_N))]
                 + [_row_stream(w.shape[1], w.shape[2], steps, layer) for w, _ in streamed],
        out_specs=[tile(w) for w in out_w]
                  + [pl.BlockSpec((w.shape[1] // steps, cols), lambda i: (i, 0)) for w, cols in streamed],
        out_shape=[jax.ShapeDtypeStruct((n, w), BF16) for w in out_w]
                  + [jax.ShapeDtypeStruct((w.shape[1], cols), BF16) for w, cols in streamed],
        scratch_shapes=[pltpu.VMEM((D_MODEL, QKV_W), BF16), pltpu.VMEM((tm, D_MODEL), BF16),
                        pltpu.VMEM((PROJ_SKEW + 1, ROW_BLOCK, MXU_N), F32)],
        compiler_params=_params(1),
        name="in_proj",
    )(x2, g_mix, w_in, qk_gain, w_in, w_a, w_b, w_o)
    return outs[:6], outs[6:]


class _Stream:
    def __init__(self, units, scores, finish, skew, n_slots):
        self.units, self.scores, self.finish, self.skew, self.n_slots = units, scores, finish, skew, n_slots


def _run_streams(streams):
    steps = max(len(s.units) + s.skew for s in streams)
    for idx in range(steps):
        for s in streams:
            if idx < len(s.units):
                s.scores(*s.units[idx], idx % s.n_slots)
            done = idx - s.skew
            if 0 <= done < len(s.units):
                s.finish(*s.units[done], done % s.n_slots)


def _na_bias_build(rpb_ref, dc_ref, bias_ref):
    n_slab = 2 * NA_KH - 2
    offs = dc_ref[...]
    in_win = offs >= 0
    take = jnp.maximum(offs, 0)

    def build(idx, carry):
        h = idx // n_slab
        d = idx - h * n_slab
        row = jnp.broadcast_to(rpb_ref[pl.ds(d * NA_HEADS + h, 1), :] * LOG2E, (GRID_W, LANES))
        slab = jnp.where(in_win, jnp.take_along_axis(row, take, axis=1), NEG_INF)
        row0 = pl.multiple_of((h % 2) * GRID_W, GRID_W)
        bias_ref[h // 2, d, pl.ds(row0, GRID_W), :] = slab
        return carry

    lax.fori_loop(0, NA_HEADS * n_slab, build, 0, unroll=NA_HEADS)


def _na_stream(t, q_ref, k_ref, v_ref, o_ref, bias_ref, s_ref, *, rows):
    lo = lax.broadcasted_iota(jnp.int32, (GRID_W, LANES), 1) < HEAD_DIM
    nt = (((1,), (1,)), ((), ()))
    win0 = _na_window_start(t, rows)
    row_info = []
    for i in range(NA_ROWS_PER_STEP):
        r = t * NA_ROWS_PER_STEP + i
        rs = jnp.clip(r - NA_KH // 2, 0, rows - NA_KH)
        row_info.append(((NA_KH - 1) - (r - rs), pl.multiple_of((rs - win0) * GRID_W, GRID_W)))

    def scores(i, p, slot):
        d0, kstart = row_info[i]
        cols = slice(p * LANES, (p + 1) * LANES)
        q = q_ref[0, i * GRID_W:(i + 1) * GRID_W, cols]
        zero = jnp.zeros_like(q)
        q2 = jnp.concatenate([jnp.where(lo, q, zero), jnp.where(lo, zero, q)], axis=0)
        kk = k_ref[0, pl.ds(kstart, NA_KH * GRID_W), cols]
        s = lax.dot_general(q2, kk, nt, preferred_element_type=F32)
        for v in range(NA_KH // 2):
            vcols = slice(v * LANES, (v + 1) * LANES)
            s_ref[slot, :, vcols] = s[:, vcols] + bias_ref[p, d0 + 2 * v]

    ones_cols = jnp.ones((NA_KH * GRID_W, LANES), BF16)

    def finish(i, p, slot):
        _, kstart = row_info[i]
        cols = slice(p * LANES, (p + 1) * LANES)
        m = jnp.max(s_ref[slot], axis=-1, keepdims=True)
        e = jnp.exp2(s_ref[slot] - m).astype(BF16)
        vv = jnp.concatenate([v_ref[0, pl.ds(kstart, NA_KH * GRID_W), cols], ones_cols], axis=1)
        o2 = jnp.dot(e, vv, preferred_element_type=F32)
        o2 = o2[:, :LANES] / o2[:, LANES:]
        o = jnp.where(lo, o2[:GRID_W], o2[GRID_W:])
        o_ref[0, i * GRID_W:(i + 1) * GRID_W, cols] = o.astype(BF16)

    units = [(i, p) for i in range(NA_ROWS_PER_STEP) for p in range(NA_HEADS // 2)]
    return _Stream(units, scores, finish, NA_SKEW, s_ref.shape[0])


def _na_window_start(step, rows):
    return jnp.clip(step * NA_ROWS_PER_STEP - NA_KH // 2, 0, rows - (NA_ROWS_PER_STEP + NA_KH))


def _na_col_offsets():
    c = np.arange(GRID_W)
    kc = c[None, :]
    win_start = np.clip(c - NA_KW // 2, 0, GRID_W - NA_KW)[:, None]
    in_win = (kc >= win_start) & (kc < win_start + NA_KW)
    dc = kc - c[:, None] + (NA_KW - 1)
    return np.concatenate([np.where(in_win, dc, -1), np.where(in_win, dc + NA_TABLE_HALF, -1)],
                          axis=1).astype(np.int32)


def _swa_bias_build(t5_ref, bucket_ref, bias_ref):
    bucket = bucket_ref[...]
    in_band = bucket >= 0
    take = jnp.maximum(bucket, 0)

    def build(h, carry):
        row = jnp.broadcast_to(t5_ref[pl.ds(h, 1), :] * LOG2E, (SW_BLOCK, LANES))
        vals = [jnp.take_along_axis(row, take[:, c:c + LANES], axis=1) for c in range(0, 3 * SW_BLOCK, LANES)]
        row0 = pl.multiple_of((h % SW_GROUP) * SW_BLOCK, SW_BLOCK)
        bias_ref[h // SW_GROUP, pl.ds(row0, SW_BLOCK), :] = jnp.where(
            in_band, jnp.concatenate(vals, axis=1), NEG_INF)
        return carry

    lax.fori_loop(0, SW_HEADS, build, 0)


def _swa_stream(n, sink_ref, q_ref, kp_ref, kc_ref, kn_ref, vp_ref, vc_ref, vn_ref, o_ref, bias_ref, s_ref,
                *, n_tiles):
    lo = lax.broadcasted_iota(jnp.int32, (SW_BLOCK, LANES), 1) < HEAD_DIM
    nt = (((1,), (1,)), ((), ()))

    def window(prev_ref, own_ref, next_ref, blk_col, j):
        kcols = slice(blk_col * LANES, (blk_col + 1) * LANES)
        blk = lambda b: own_ref[0, b * SW_BLOCK:(b + 1) * SW_BLOCK, kcols]
        first = prev_ref[0, :, kcols] if j == 0 else blk(j - 1)
        last = next_ref[0, :, kcols] if j == SW_BLOCKS_PER_STEP - 1 else blk(j + 1)
        return jnp.concatenate([first, blk(j), last], axis=0)

    def scores(kvh, j, c, hh, slot):
        rows = slice(j * SW_BLOCK, (j + 1) * SW_BLOCK)
        qb = q_ref[0, rows, (2 * kvh + c) * LANES:(2 * kvh + c + 1) * LANES]
        q1 = jnp.where(lo if hh == 0 else ~lo, qb, jnp.zeros_like(qb))
        s_ref[slot] = lax.dot_general(q1, window(kp_ref, kc_ref, kn_ref, kvh, j), nt,
                                      preferred_element_type=F32)

    def finish(kvh, j, c, hh, slot):
        rows = slice(j * SW_BLOCK, (j + 1) * SW_BLOCK)
        head = kvh * SW_GROUP + 2 * c + hh
        brows = slice((2 * c + hh) * SW_BLOCK, (2 * c + hh + 1) * SW_BLOCK)
        sg = s_ref[slot] + bias_ref[kvh, brows, :]
        if j == 0:
            sg = jnp.concatenate([jnp.where(n > 0, sg[:, :SW_BLOCK], NEG_INF), sg[:, SW_BLOCK:]], axis=1)
        if j == SW_BLOCKS_PER_STEP - 1:
            sg = jnp.concatenate([sg[:, :2 * SW_BLOCK],
                                  jnp.where(n < n_tiles - 1, sg[:, 2 * SW_BLOCK:], NEG_INF)], axis=1)
        sk = sink_ref[head] * LOG2E
        m = jnp.maximum(jnp.max(sg, axis=-1, keepdims=True), sk)
        e = jnp.exp2(sg - m).astype(BF16)
        o2 = jnp.dot(e, window(vp_ref, vc_ref, vn_ref, 2 * kvh + hh, j), preferred_element_type=F32)
        o = o2 / (pltpu.roll(o2, HEAD_DIM, axis=1) + jnp.exp2(sk - m))
        col0 = (2 * kvh + c) * LANES + hh * HEAD_DIM
        o_ref[0, rows, col0:col0 + HEAD_DIM] = o[:, hh * HEAD_DIM:(hh + 1) * HEAD_DIM].astype(BF16)

    units = [(kvh, j, c, hh) for j in range(SW_BLOCKS_PER_STEP) for kvh in range(SW_KV_HEADS)
             for c in range(2) for hh in range(2)]
    return _Stream(units, scores, finish, SW_SKEW, s_ref.shape[0])


def _t5_bucket(rel):
    half = T5_BUCKETS // 2
    max_exact = half // 2
    ret = (rel > 0).astype(np.int32) * half
    n = np.abs(rel)
    large = max_exact + (np.log(np.maximum(n, 1) / max_exact)
                         / np.log(T5_MAX_DIST / max_exact) * (half - max_exact)).astype(np.int32)
    large = np.minimum(large, half - 1)
    return ret + np.where(n < max_exact, n, large)


def _swa_bucket_index():
    rel = np.arange(3 * SW_BLOCK)[None, :] - SW_BLOCK - np.arange(SW_BLOCK)[:, None]
    return np.where(np.abs(rel) <= SW_WINDOW, _t5_bucket(rel), -1).astype(np.int32)


def _attn_kernel(rpb_ref, dc_ref, sink_ref, t5_ref, bucket_ref,
                 qa_ref, ka_ref, va_ref, qb_ref, kp_ref, kc_ref, kn_ref, vp_ref, vc_ref, vn_ref,
                 oa_ref, ob_ref, na_bias, na_ring, sw_bias, sw_ring, *, rows, n_tiles):
    step = pl.program_id(1)

    @pl.when((pl.program_id(0) == 0) & (step == 0))
    def _build_bias():
        _na_bias_build(rpb_ref, dc_ref, na_bias)
        _swa_bias_build(t5_ref, bucket_ref, sw_bias)

    _run_streams([
        _na_stream(step, qa_ref, ka_ref, va_ref, oa_ref, na_bias, na_ring, rows=rows),
        _swa_stream(step, sink_ref, qb_ref, kp_ref, kc_ref, kn_ref, vp_ref, vc_ref, vn_ref, ob_ref,
                    sw_bias, sw_ring, n_tiles=n_tiles),
    ])


def _attn(rpb, sink, t5_table, qa, ka, va, qb, kbd, vbd):
    b, t, _ = qa.shape
    rows = t // GRID_W
    tq = NA_ROWS_PER_STEP * GRID_W
    assert tq == SW_BLOCKS_PER_STEP * SW_BLOCK
    n_tiles = t // tq
    last_blk = t // SW_BLOCK - 1
    tile = lambda w: pl.BlockSpec((1, tq, w), lambda bi, ti: (bi, ti, 0))
    na_window = pl.BlockSpec((pl.Element(1), pl.Element((NA_ROWS_PER_STEP + NA_KH) * GRID_W), pl.Element(A_W)),
                             lambda bi, ti: (bi, _na_window_start(ti, rows) * GRID_W, 0))
    prev = lambda w: pl.BlockSpec((1, SW_BLOCK, w),
                                  lambda bi, ti: (bi, jnp.maximum(ti * SW_BLOCKS_PER_STEP - 1, 0), 0))
    nxt = lambda w: pl.BlockSpec((1, SW_BLOCK, w),
                                 lambda bi, ti: (bi, jnp.minimum((ti + 1) * SW_BLOCKS_PER_STEP, last_blk), 0))
    kw, vw = kbd.shape[-1], vbd.shape[-1]
    per_head = jnp.transpose(rpb.astype(F32), (0, 2, 1))
    pad = lambda a, w: jnp.pad(a, ((0, 0), (0, 0), (0, w - a.shape[-1])))
    na_table = jnp.concatenate([pad(per_head[:-1], NA_TABLE_HALF), pad(per_head[1:], LANES - NA_TABLE_HALF)],
                               axis=-1).reshape(-1, LANES)
    sw_table = jnp.pad(t5_table.astype(F32).T, ((0, 0), (0, LANES - t5_table.shape[0])))
    return pl.pallas_call(
        functools.partial(_attn_kernel, rows=rows, n_tiles=n_tiles),
        grid=(b, n_tiles),
        in_specs=[_resident(na_table.shape), _resident((GRID_W, LANES)), pl.BlockSpec(memory_space=pltpu.SMEM),
                  _resident((SW_HEADS, LANES)), _resident((SW_BLOCK, 3 * SW_BLOCK)),
                  tile(A_W), na_window, na_window,
                  tile(B_QW), prev(kw), tile(kw), nxt(kw), prev(vw), tile(vw), nxt(vw)],
        out_specs=[tile(A_W), tile(B_QW)],
        out_shape=[jax.ShapeDtypeStruct((b, t, A_W), BF16), jax.ShapeDtypeStruct((b, t, B_QW), BF16)],
        scratch_shapes=[pltpu.VMEM((NA_HEADS // 2, 2 * NA_KH - 2, LANES, LANES), F32),
                        pltpu.VMEM((NA_SLOTS, 2 * GRID_W, NA_KH * GRID_W), F32),
                        pltpu.VMEM((SW_KV_HEADS, SW_GROUP * SW_BLOCK, 3 * SW_BLOCK), F32),
                        pltpu.VMEM((SW_SLOTS, SW_BLOCK, 3 * SW_BLOCK), F32)],
        compiler_params=_params(2),
        name="attn",
    )(na_table, jnp.asarray(_na_col_offsets()), sink.astype(F32), sw_table, jnp.asarray(_swa_bucket_index()),
      qa, ka, va, qb, kbd, kbd, kbd, vbd, vbd, vbd)


def _mix_kernel(x_ref, oa_ref, ob_ref, g_ref, wg_ref, wa_ref, wb_ref, wo_ref, wgate_f32, wup_f32, wdown_f32,
                x1_ref, wgate_bf, wup_bf, wdown_bf, h_scr, y_scr):
    _cast_rows(((wgate_f32, wgate_bf), (wup_f32, wup_bf), (wdown_f32, wdown_bf)))
    for rb in range(0, x_ref.shape[0], ROW_BLOCK):
        rows = slice(rb, rb + ROW_BLOCK)
        xf = x_ref[rows, :]
        h_scr[rows, :] = ((xf * _rms_scale(xf)) * g_ref[...]).astype(BF16)
        for c in range(0, D_MODEL, MXU_N):
            cs = slice(c, c + MXU_N)
            gs = slice(D_MODEL + c, D_MODEL + c + MXU_N)
            ga = jnp.dot(h_scr[rows, :], wg_ref[:, cs], preferred_element_type=F32)
            gb = jnp.dot(h_scr[rows, :], wg_ref[:, gs], preferred_element_type=F32)
            ya = jnp.dot(oa_ref[rows, :], wa_ref[:, cs], preferred_element_type=F32)
            yb = jnp.dot(ob_ref[rows, :], wb_ref[:, cs], preferred_element_type=F32)
            y_scr[rows, cs] = (jax.nn.sigmoid(ga) * ya + jax.nn.sigmoid(gb) * yb).astype(BF16)
        for c in range(0, D_MODEL, MXU_N):
            cs = slice(c, c + MXU_N)
            x1_ref[rows, cs] = x_ref[rows, cs] + jnp.dot(y_scr[rows, :], wo_ref[:, cs], preferred_element_type=F32)


def _mix_out(x2, oa, ob, g_mix, mix_weights, w_gate, w_up, w_down, layer):
    n = x2.shape[0]
    tm = TOK_TILE
    steps = n // tm
    tile = lambda w: pl.BlockSpec((tm, w), lambda i: (i, 0))
    streamed = (w_gate, w_up, w_down)
    outs = pl.pallas_call(
        _mix_kernel,
        grid=(steps,),
        in_specs=[tile(D_MODEL), tile(A_W), tile(B_QW), _resident((1, D_MODEL))]
                 + [_resident(w.shape) for w in mix_weights]
                 + [_row_stream(w.shape[1], w.shape[2], steps, layer) for w in streamed],
        out_specs=[tile(D_MODEL)]
                  + [pl.BlockSpec((w.shape[1] // steps, w.shape[2]), lambda i: (i, 0)) for w in streamed],
        out_shape=[jax.ShapeDtypeStruct((n, D_MODEL), F32)]
                  + [jax.ShapeDtypeStruct(w.shape[1:], BF16) for w in streamed],
        scratch_shapes=[pltpu.VMEM((tm, D_MODEL), BF16), pltpu.VMEM((tm, D_MODEL), BF16)],
        compiler_params=_params(1),
        name="mix_out",
    )(x2, oa, ob, g_mix, *mix_weights, w_gate, w_up, w_down)
    return outs[0], outs[1:]


def _ffn_kernel(x_ref, g_ref, wg_ref, wu_ref, wd_ref, o_ref, h_scr, a_scr):
    hidden = wg_ref.shape[1]
    for rb in range(0, x_ref.shape[0], ROW_BLOCK):
        rows = slice(rb, rb + ROW_BLOCK)
        xf = x_ref[rows, :]
        h_scr[rows, :] = ((xf * _rms_scale(xf)) * g_ref[...]).astype(BF16)
        for c in range(0, hidden, MXU_N):
            cs = slice(c, c + MXU_N)
            gate = jnp.dot(h_scr[rows, :], wg_ref[:, cs], preferred_element_type=F32)
            up = jnp.dot(h_scr[rows, :], wu_ref[:, cs], preferred_element_type=F32)
            a_scr[rows, cs] = (jax.nn.silu(gate) * up).astype(BF16)
        for c in range(0, D_MODEL, MXU_N):
            cs = slice(c, c + MXU_N)
            o_ref[rows, cs] = x_ref[rows, cs] + jnp.dot(a_scr[rows, :], wd_ref[:, cs], preferred_element_type=F32)


def _ffn(x1, g_ffn, w_gate, w_up, w_down):
    n = x1.shape[0]
    tm = TOK_TILE
    hidden = w_gate.shape[1]
    tile = pl.BlockSpec((tm, D_MODEL), lambda i: (i, 0))
    return pl.pallas_call(
        _ffn_kernel,
        grid=(n // tm,),
        in_specs=[tile, _resident((1, D_MODEL)), _resident(w_gate.shape), _resident(w_up.shape),
                  _resident(w_down.shape)],
        out_specs=tile,
        out_shape=jax.ShapeDtypeStruct((n, D_MODEL), F32),
        scratch_shapes=[pltpu.VMEM((tm, D_MODEL), BF16), pltpu.VMEM((tm, hidden), BF16)],
        compiler_params=_params(1),
        name="ffn",
    )(x1, g_ffn, w_gate, w_up, w_down)


def _layer(x2, b, t, layer, norm_mix, w_in, q_norm_a, k_norm_a, rpb_a, q_norm_b, k_norm_b, sink_b, t5_table,
           w_branch_a, w_branch_b, w_out, norm_ffn, w_gate, w_up, w_down):
    g_mix = norm_mix[layer].reshape(1, D_MODEL)
    reps = MXU_N // HEAD_DIM
    root_d = HEAD_DIM ** 0.5
    q_fold = QK_SCALE * root_d * LOG2E
    qk_gain = jnp.stack([jnp.tile(q_norm_a[layer] * q_fold, reps), jnp.tile(k_norm_a[layer] * root_d, reps),
                         jnp.tile(q_norm_b[layer] * q_fold, reps), jnp.tile(k_norm_b[layer] * root_d, reps)]
                        ).astype(F32)

    (qa, ka, va, qb, kbd, vbd), mix_weights = _in_proj(x2, g_mix, qk_gain, w_in, w_branch_a, w_branch_b,
                                                       w_out, layer)
    n = x2.shape[0]
    r3 = lambda a: a.reshape(b, t, a.shape[-1])
    oa, ob = _attn(rpb_a[layer], sink_b[layer], t5_table, r3(qa), r3(ka), r3(va), r3(qb), r3(kbd), r3(vbd))
    x1, ffn_weights = _mix_out(x2, oa.reshape(n, A_W), ob.reshape(n, B_QW), g_mix, mix_weights,
                               w_gate, w_up, w_down, layer)
    return _ffn(x1, norm_ffn[layer].reshape(1, D_MODEL), *ffn_weights)


def kernel(x, norm_mix, w_in, q_norm_a, k_norm_a, rpb_a, q_norm_b, k_norm_b, sink_b, t5_table,
           w_branch_a, w_branch_b, w_out, norm_ffn, w_gate, w_up, w_down):
    b, t, d = x.shape
    x2 = x.reshape(b * t, d)
    for layer in range(norm_mix.shape[0]):
        x2 = _layer(x2, b, t, layer, norm_mix, w_in, q_norm_a, k_norm_a, rpb_a, q_norm_b, k_norm_b, sink_b,
                    t5_table, w_branch_a, w_branch_b, w_out, norm_ffn, w_gate, w_up, w_down)
    return x2.reshape(b, t, d)
```

```python
import functools

import jax
import jax.numpy as jnp
import numpy as np
from jax import lax
from jax.experimental import pallas as pl
from jax.experimental.pallas import tpu as pltpu

F32 = jnp.float32
BF16 = jnp.bfloat16

D_MODEL = 1024
HEAD_DIM = 64
GRID_W = 64
NA_HEADS = 8
NA_KH = 8
NA_KW = 16
SW_HEADS = 8
SW_KV_HEADS = 2
SW_GROUP = SW_HEADS // SW_KV_HEADS
SW_WINDOW = 128
SW_BLOCK = 128
T5_BUCKETS = 32
T5_MAX_DIST = 128
A_W = NA_HEADS * HEAD_DIM
B_QW = SW_HEADS * HEAD_DIM
B_KVW = SW_KV_HEADS * HEAD_DIM
QKV_W = 3 * A_W + B_QW + 2 * B_KVW
RMS_EPS = 1e-6
NEG_INF = -1e30
QK_SCALE = HEAD_DIM ** -0.5
LOG2E = 1.4426950408889634

LANES = 128
MXU_N = 256
VMEM_LIMIT = 56 * 1024 * 1024

TOK_TILE = 1024
ROW_BLOCK = 256
NA_ROWS_PER_STEP = 32
PROJ_SKEW = 4
NA_SKEW = 4
SW_SKEW = 4
SW_BLOCKS_PER_STEP = 16
NA_SLOTS = NA_SKEW + 1
NA_TABLE_HALF = 32
SW_SLOTS = SW_SKEW + 1


def _resident(shape):
    return pl.BlockSpec(shape, lambda *_: (0,) * len(shape), pipeline_mode=pl.Buffered(1))


def _params(n_axes):
    return pltpu.CompilerParams(dimension_semantics=("arbitrary",) * n_axes, vmem_limit_bytes=VMEM_LIMIT)


def _rms_scale(xf):
    return lax.rsqrt(jnp.mean(xf * xf, axis=-1, keepdims=True) + RMS_EPS)


def _group_rms_scale(p):
    y = p * p
    lo = lax.broadcasted_iota(jnp.int32, (p.shape[0], LANES), 1) < HEAD_DIM
    scales = []
    for c in range(0, p.shape[1], LANES):
        yc = y[:, c:c + LANES]
        s_lo = jnp.sum(jnp.where(lo, yc, 0.0), axis=-1, keepdims=True)
        s_hi = jnp.sum(jnp.where(lo, 0.0, yc), axis=-1, keepdims=True)
        r_lo = lax.rsqrt(s_lo + HEAD_DIM * RMS_EPS)
        r_hi = lax.rsqrt(s_hi + HEAD_DIM * RMS_EPS)
        scales.append(jnp.where(lo, r_lo, r_hi))
    return jnp.concatenate(scales, axis=1)


def _cast_rows(pairs):
    for src_ref, dst_ref in pairs:
        dst_ref[...] = src_ref[:, src_ref.shape[1] - dst_ref.shape[1]:].astype(BF16)


def _in_proj_kernel(x_ref, g_ref, w_ref, qk_gain_ref, wg_f32, wa_f32, wb_f32, wo_f32,
                    qa_ref, ka_ref, va_ref, qb_ref, kbd_ref, vbd_ref, wg_bf, wa_bf, wb_bf, wo_bf,
                    w_scr, h_scr, p_scr):
    @pl.when(pl.program_id(0) == 0)
    def _cast_own_weights():
        for c in range(0, QKV_W, MXU_N):
            w_scr[:, c:c + MXU_N] = w_ref[:, c:c + MXU_N].astype(BF16)

    _cast_rows(((wg_f32, wg_bf), (wa_f32, wa_bf), (wb_f32, wb_bf), (wo_f32, wo_bf)))

    def normed(out_ref, rows, c, gain_row):
        def epilogue(slot):
            p = p_scr[slot]
            scale = _group_rms_scale(p)
            out_ref[rows, c:c + MXU_N] = (p * scale * qk_gain_ref[gain_row:gain_row + 1, :]).astype(BF16)
        return epilogue

    def plain(out_ref, rows, c):
        def epilogue(slot):
            out_ref[rows, c:c + MXU_N] = p_scr[slot].astype(BF16)
        return epilogue

    def kv_b(rows):
        def epilogue(slot):
            kb = p_scr[slot, :, :B_KVW]
            kb = kb * _group_rms_scale(kb) * qk_gain_ref[3:4, :LANES]
            vb = p_scr[slot, :, B_KVW:]
            lo = lax.broadcasted_iota(jnp.int32, kb.shape, 1) < HEAD_DIM
            k_swapped = pltpu.roll(kb, HEAD_DIM, axis=1)
            kbd_ref[rows, :LANES] = jnp.where(lo, kb, k_swapped).astype(BF16)
            kbd_ref[rows, LANES:] = jnp.where(lo, k_swapped, kb).astype(BF16)
            v_swapped = pltpu.roll(vb, HEAD_DIM, axis=1)
            for blk, (left, right) in enumerate(((vb, 1.0), (1.0, v_swapped), (v_swapped, 1.0), (1.0, vb))):
                vbd_ref[rows, blk * LANES:(blk + 1) * LANES] = jnp.where(lo, left, right).astype(BF16)
        return epilogue

    units = []
    for rb in range(0, x_ref.shape[0], ROW_BLOCK):
        rows = slice(rb, rb + ROW_BLOCK)
        xf = x_ref[rows, :]
        h_scr[rows, :] = ((xf * _rms_scale(xf)) * g_ref[...]).astype(BF16)
        for c in range(0, A_W, MXU_N):
            units += [(rows, c, normed(qa_ref, rows, c, 0)), (rows, A_W + c, normed(ka_ref, rows, c, 1)),
                      (rows, 3 * A_W + c, normed(qb_ref, rows, c, 2))]
        units.append((rows, 3 * A_W + B_QW, kv_b(rows)))
        units += [(rows, 2 * A_W + c, plain(va_ref, rows, c)) for c in range(0, A_W, MXU_N)]
    n_slots = PROJ_SKEW + 1
    for idx in range(len(units) + PROJ_SKEW):
        if idx < len(units):
            rows, c0, _ = units[idx]
            p_scr[idx % n_slots] = jnp.dot(h_scr[rows, :], w_scr[:, c0:c0 + MXU_N], preferred_element_type=F32)
        if idx >= PROJ_SKEW:
            units[idx - PROJ_SKEW][2]((idx - PROJ_SKEW) % n_slots)


def _row_stream(n_rows, n_cols, steps, layer):
    return pl.BlockSpec((None, n_rows // steps, n_cols), lambda i: (layer, i, 0))


def _in_proj(x2, g_mix, qk_gain, w_in, w_a, w_b, w_o, layer):
    n = x2.shape[0]
    tm = TOK_TILE
    steps = n // tm
    tile = lambda w: pl.BlockSpec((tm, w), lambda i: (i, 0))
    gate_w = w_in.shape[2] - QKV_W
    out_w = (A_W, A_W, A_W, B_QW, 2 * LANES, 4 * LANES)
    streamed = ((w_in, gate_w), (w_a, D_MODEL), (w_b, D_MODEL), (w_o, D_MODEL))
    outs = pl.pallas_call(
        _in_proj_kernel,
        grid=(steps,),
        in_specs=[tile(D_MODEL), _resident((1, D_MODEL)),
                  pl.BlockSpec((None, D_MODEL, QKV_W), lambda i: (layer, 0, 0), pipeline_mode=pl.Buffered(1)),
                  _resident((4, MXU_N))]
                 + [_row_stream(w.shape[1], w.shape[2], steps, layer) for w, _ in streamed],
        out_specs=[tile(w) for w in out_w]
                  + [pl.BlockSpec((w.shape[1] // steps, cols), lambda i: (i, 0)) for w, cols in streamed],
        out_shape=[jax.ShapeDtypeStruct((n, w), BF16) for w in out_w]
                  + [jax.ShapeDtypeStruct((w.shape[1], cols), BF16) for w, cols in streamed],
        scratch_shapes=[pltpu.VMEM((D_MODEL, QKV_W), BF16), pltpu.VMEM((tm, D_MODEL), BF16),
                        pltpu.VMEM((PROJ_SKEW + 1, ROW_BLOCK, MXU_N), F32)],
        compiler_params=_params(1),
        name="in_proj",
    )(x2, g_mix, w_in, qk_gain, w_in, w_a, w_b, w_o)
    return outs[:6], outs[6:]


def _na_kernel(rpb_ref, dc_ref, q_ref, k_ref, v_ref, o_ref, bias_ref, s_ref, *, rows):
    t = pl.program_id(1)
    lo = lax.broadcasted_iota(jnp.int32, (GRID_W, LANES), 1) < HEAD_DIM
    nt = (((1,), (1,)), ((), ()))
    n_slab = 2 * NA_KH - 2

    @pl.when((pl.program_id(0) == 0) & (t == 0))
    def _build_bias():
        offs = dc_ref[...]
        in_win = offs >= 0
        take = jnp.maximum(offs, 0)

        def build(idx, carry):
            h = idx // n_slab
            d = idx - h * n_slab
            row = jnp.broadcast_to(rpb_ref[pl.ds(d * NA_HEADS + h, 1), :] * LOG2E, (GRID_W, LANES))
            slab = jnp.where(in_win, jnp.take_along_axis(row, take, axis=1), NEG_INF)
            row0 = pl.multiple_of((h % 2) * GRID_W, GRID_W)
            bias_ref[h // 2, d, pl.ds(row0, GRID_W), :] = slab
            return carry

        lax.fori_loop(0, NA_HEADS * n_slab, build, 0, unroll=NA_HEADS)

    win0 = _na_window_start(t, rows)
    row_info = []
    for i in range(NA_ROWS_PER_STEP):
        r = t * NA_ROWS_PER_STEP + i
        rs = jnp.clip(r - NA_KH // 2, 0, rows - NA_KH)
        row_info.append(((NA_KH - 1) - (r - rs), pl.multiple_of((rs - win0) * GRID_W, GRID_W)))

    def scores(i, p, slot):
        d0, kstart = row_info[i]
        cols = slice(p * LANES, (p + 1) * LANES)
        q = q_ref[0, i * GRID_W:(i + 1) * GRID_W, cols]
        zero = jnp.zeros_like(q)
        q2 = jnp.concatenate([jnp.where(lo, q, zero), jnp.where(lo, zero, q)], axis=0)
        kk = k_ref[0, pl.ds(kstart, NA_KH * GRID_W), cols]
        s = lax.dot_general(q2, kk, nt, preferred_element_type=F32)
        for v in range(NA_KH // 2):
            vcols = slice(v * LANES, (v + 1) * LANES)
            s_ref[slot, :, vcols] = s[:, vcols] + bias_ref[p, d0 + 2 * v]

    ones_cols = jnp.ones((NA_KH * GRID_W, LANES), BF16)

    def finish(i, p, slot):
        _, kstart = row_info[i]
        cols = slice(p * LANES, (p + 1) * LANES)
        m = jnp.max(s_ref[slot], axis=-1, keepdims=True)
        e = jnp.exp2(s_ref[slot] - m).astype(BF16)
        vv = jnp.concatenate([v_ref[0, pl.ds(kstart, NA_KH * GRID_W), cols], ones_cols], axis=1)
        o2 = jnp.dot(e, vv, preferred_element_type=F32)
        o2 = o2[:, :LANES] / o2[:, LANES:]
        o = jnp.where(lo, o2[:GRID_W], o2[GRID_W:])
        o_ref[0, i * GRID_W:(i + 1) * GRID_W, cols] = o.astype(BF16)

    units = [(i, p) for i in range(NA_ROWS_PER_STEP) for p in range(NA_HEADS // 2)]
    n_slots = s_ref.shape[0]
    for idx in range(len(units) + NA_SKEW):
        if idx < len(units):
            scores(*units[idx], idx % n_slots)
        if idx >= NA_SKEW:
            finish(*units[idx - NA_SKEW], (idx - NA_SKEW) % n_slots)


def _na_window_start(step, rows):
    return jnp.clip(step * NA_ROWS_PER_STEP - NA_KH // 2, 0, rows - (NA_ROWS_PER_STEP + NA_KH))


def _na_col_offsets():
    c = np.arange(GRID_W)
    kc = c[None, :]
    win_start = np.clip(c - NA_KW // 2, 0, GRID_W - NA_KW)[:, None]
    in_win = (kc >= win_start) & (kc < win_start + NA_KW)
    dc = kc - c[:, None] + (NA_KW - 1)
    return np.concatenate([np.where(in_win, dc, -1), np.where(in_win, dc + NA_TABLE_HALF, -1)],
                          axis=1).astype(np.int32)


def _na_attn(rpb, qa, ka, va):
    b, t, _ = qa.shape
    rows = t // GRID_W
    tq = NA_ROWS_PER_STEP * GRID_W
    tile = pl.BlockSpec((1, tq, A_W), lambda bi, ti: (bi, ti, 0))
    window = pl.BlockSpec((pl.Element(1), pl.Element((NA_ROWS_PER_STEP + NA_KH) * GRID_W), pl.Element(A_W)),
                          lambda bi, ti: (bi, _na_window_start(ti, rows) * GRID_W, 0))
    per_head = jnp.transpose(rpb.astype(F32), (0, 2, 1))
    pad = lambda a, w: jnp.pad(a, ((0, 0), (0, 0), (0, w - a.shape[-1])))
    table = jnp.concatenate([pad(per_head[:-1], NA_TABLE_HALF), pad(per_head[1:], LANES - NA_TABLE_HALF)], axis=-1)
    table = table.reshape(-1, LANES)
    return pl.pallas_call(
        functools.partial(_na_kernel, rows=rows),
        grid=(b, rows // NA_ROWS_PER_STEP),
        in_specs=[_resident(table.shape), _resident((GRID_W, LANES)), tile, window, window],
        out_specs=tile,
        out_shape=jax.ShapeDtypeStruct((b, t, A_W), BF16),
        scratch_shapes=[pltpu.VMEM((NA_HEADS // 2, 2 * NA_KH - 2, LANES, LANES), F32),
                        pltpu.VMEM((NA_SLOTS, 2 * GRID_W, NA_KH * GRID_W), F32)],
        compiler_params=_params(2),
        name="na_attn",
    )(table, jnp.asarray(_na_col_offsets()), qa, ka, va)


def _swa_kernel(sink_ref, t5_ref, bucket_ref, q_ref, kp_ref, kc_ref, kn_ref, vp_ref, vc_ref, vn_ref, o_ref,
                bias_ref, s_ref, *, n_tiles):
    n = pl.program_id(1)
    lo = lax.broadcasted_iota(jnp.int32, (SW_BLOCK, LANES), 1) < HEAD_DIM
    nt = (((1,), (1,)), ((), ()))

    @pl.when((pl.program_id(0) == 0) & (n == 0))
    def _build_bias():
        bucket = bucket_ref[...]
        in_band = bucket >= 0
        take = jnp.maximum(bucket, 0)

        def build(h, carry):
            row = jnp.broadcast_to(t5_ref[pl.ds(h, 1), :] * LOG2E, (SW_BLOCK, LANES))
            vals = [jnp.take_along_axis(row, take[:, c:c + LANES], axis=1) for c in range(0, 3 * SW_BLOCK, LANES)]
            row0 = pl.multiple_of((h % SW_GROUP) * SW_BLOCK, SW_BLOCK)
            bias_ref[h // SW_GROUP, pl.ds(row0, SW_BLOCK), :] = jnp.where(
                in_band, jnp.concatenate(vals, axis=1), NEG_INF)
            return carry

        lax.fori_loop(0, SW_HEADS, build, 0)

    def window(prev_ref, own_ref, next_ref, blk_col, j):
        kcols = slice(blk_col * LANES, (blk_col + 1) * LANES)
        blk = lambda b: own_ref[0, b * SW_BLOCK:(b + 1) * SW_BLOCK, kcols]
        first = prev_ref[0, :, kcols] if j == 0 else blk(j - 1)
        last = next_ref[0, :, kcols] if j == SW_BLOCKS_PER_STEP - 1 else blk(j + 1)
        return jnp.concatenate([first, blk(j), last], axis=0)

    def scores(kvh, j, c, hh, slot):
        rows = slice(j * SW_BLOCK, (j + 1) * SW_BLOCK)
        qb = q_ref[0, rows, (2 * kvh + c) * LANES:(2 * kvh + c + 1) * LANES]
        q1 = jnp.where(lo if hh == 0 else ~lo, qb, jnp.zeros_like(qb))
        s_ref[slot] = lax.dot_general(q1, window(kp_ref, kc_ref, kn_ref, kvh, j), nt,
                                      preferred_element_type=F32)

    def finish(kvh, j, c, hh, slot):
        rows = slice(j * SW_BLOCK, (j + 1) * SW_BLOCK)
        head = kvh * SW_GROUP + 2 * c + hh
        brows = slice((2 * c + hh) * SW_BLOCK, (2 * c + hh + 1) * SW_BLOCK)
        sg = s_ref[slot] + bias_ref[kvh, brows, :]
        if j == 0:
            sg = jnp.concatenate([jnp.where(n > 0, sg[:, :SW_BLOCK], NEG_INF), sg[:, SW_BLOCK:]], axis=1)
        if j == SW_BLOCKS_PER_STEP - 1:
            sg = jnp.concatenate([sg[:, :2 * SW_BLOCK],
                                  jnp.where(n < n_tiles - 1, sg[:, 2 * SW_BLOCK:], NEG_INF)], axis=1)
        sk = sink_ref[head] * LOG2E
        m = jnp.maximum(jnp.max(sg, axis=-1, keepdims=True), sk)
        e = jnp.exp2(sg - m).astype(BF16)
        o2 = jnp.dot(e, window(vp_ref, vc_ref, vn_ref, 2 * kvh + hh, j), preferred_element_type=F32)
        o = o2 / (pltpu.roll(o2, HEAD_DIM, axis=1) + jnp.exp2(sk - m))
        col0 = (2 * kvh + c) * LANES + hh * HEAD_DIM
        o_ref[0, rows, col0:col0 + HEAD_DIM] = o[:, hh * HEAD_DIM:(hh + 1) * HEAD_DIM].astype(BF16)

    units = [(kvh, j, c, hh) for j in range(SW_BLOCKS_PER_STEP) for kvh in range(SW_KV_HEADS)
             for c in range(2) for hh in range(2)]
    n_slots = s_ref.shape[0]
    for idx in range(len(units) + SW_SKEW):
        if idx < len(units):
            scores(*units[idx], idx % n_slots)
        if idx >= SW_SKEW:
            finish(*units[idx - SW_SKEW], (idx - SW_SKEW) % n_slots)


def _t5_bucket(rel):
    half = T5_BUCKETS // 2
    max_exact = half // 2
    ret = (rel > 0).astype(np.int32) * half
    n = np.abs(rel)
    large = max_exact + (np.log(np.maximum(n, 1) / max_exact)
                         / np.log(T5_MAX_DIST / max_exact) * (half - max_exact)).astype(np.int32)
    large = np.minimum(large, half - 1)
    return ret + np.where(n < max_exact, n, large)


def _swa_bucket_index():
    rel = np.arange(3 * SW_BLOCK)[None, :] - SW_BLOCK - np.arange(SW_BLOCK)[:, None]
    return np.where(np.abs(rel) <= SW_WINDOW, _t5_bucket(rel), -1).astype(np.int32)


def _swa_attn(sink, t5_table, qb, kbd, vbd):
    b, t, _ = qb.shape
    tq = SW_BLOCKS_PER_STEP * SW_BLOCK
    n_tiles = t // tq
    last_blk = t // SW_BLOCK - 1
    qtile = pl.BlockSpec((1, tq, B_QW), lambda bi, ni: (bi, ni, 0))
    own = lambda w: pl.BlockSpec((1, tq, w), lambda bi, ni: (bi, ni, 0))
    prev = lambda w: pl.BlockSpec((1, SW_BLOCK, w),
                                  lambda bi, ni: (bi, jnp.maximum(ni * SW_BLOCKS_PER_STEP - 1, 0), 0))
    nxt = lambda w: pl.BlockSpec((1, SW_BLOCK, w),
                                 lambda bi, ni: (bi, jnp.minimum((ni + 1) * SW_BLOCKS_PER_STEP, last_blk), 0))
    kw, vw = kbd.shape[-1], vbd.shape[-1]
    smem = pl.BlockSpec(memory_space=pltpu.SMEM)
    table = jnp.pad(t5_table.astype(F32).T, ((0, 0), (0, LANES - t5_table.shape[0])))
    return pl.pallas_call(
        functools.partial(_swa_kernel, n_tiles=n_tiles),
        grid=(b, n_tiles),
        in_specs=[smem, _resident((SW_HEADS, LANES)), _resident((SW_BLOCK, 3 * SW_BLOCK)),
                  qtile, prev(kw), own(kw), nxt(kw), prev(vw), own(vw), nxt(vw)],
        out_specs=qtile,
        out_shape=jax.ShapeDtypeStruct((b, t, B_QW), BF16),
        scratch_shapes=[pltpu.VMEM((SW_KV_HEADS, SW_GROUP * SW_BLOCK, 3 * SW_BLOCK), F32),
                        pltpu.VMEM((SW_SLOTS, SW_BLOCK, 3 * SW_BLOCK), F32)],
        compiler_params=_params(2),
        name="swa_attn",
    )(sink.astype(F32), table, jnp.asarray(_swa_bucket_index()),
      qb, kbd, kbd, kbd, vbd, vbd, vbd)


def _mix_kernel(x_ref, oa_ref, ob_ref, g_ref, wg_ref, wa_ref, wb_ref, wo_ref, wgate_f32, wup_f32, wdown_f32,
                x1_ref, wgate_bf, wup_bf, wdown_bf, h_scr, y_scr):
    _cast_rows(((wgate_f32, wgate_bf), (wup_f32, wup_bf), (wdown_f32, wdown_bf)))
    for rb in range(0, x_ref.shape[0], ROW_BLOCK):
        rows = slice(rb, rb + ROW_BLOCK)
        xf = x_ref[rows, :]
        h_scr[rows, :] = ((xf * _rms_scale(xf)) * g_ref[...]).astype(BF16)
        for c in range(0, D_MODEL, MXU_N):
            cs = slice(c, c + MXU_N)
            gs = slice(D_MODEL + c, D_MODEL + c + MXU_N)
            ga = jnp.dot(h_scr[rows, :], wg_ref[:, cs], preferred_element_type=F32)
            gb = jnp.dot(h_scr[rows, :], wg_ref[:, gs], preferred_element_type=F32)
            ya = jnp.dot(oa_ref[rows, :], wa_ref[:, cs], preferred_element_type=F32)
            yb = jnp.dot(ob_ref[rows, :], wb_ref[:, cs], preferred_element_type=F32)
            y_scr[rows, cs] = (jax.nn.sigmoid(ga) * ya + jax.nn.sigmoid(gb) * yb).astype(BF16)
        for c in range(0, D_MODEL, MXU_N):
            cs = slice(c, c + MXU_N)
            x1_ref[rows, cs] = x_ref[rows, cs] + jnp.dot(y_scr[rows, :], wo_ref[:, cs], preferred_element_type=F32)


def _mix_out(x2, oa, ob, g_mix, mix_weights, w_gate, w_up, w_down, layer):
    n = x2.shape[0]
    tm = TOK_TILE
    steps = n // tm
    tile = lambda w: pl.BlockSpec((tm, w), lambda i: (i, 0))
    streamed = (w_gate, w_up, w_down)
    outs = pl.pallas_call(
        _mix_kernel,
        grid=(steps,),
        in_specs=[tile(D_MODEL), tile(A_W), tile(B_QW), _resident((1, D_MODEL))]
                 + [_resident(w.shape) for w in mix_weights]
                 + [_row_stream(w.shape[1], w.shape[2], steps, layer) for w in streamed],
        out_specs=[tile(D_MODEL)]
                  + [pl.BlockSpec((w.shape[1] // steps, w.shape[2]), lambda i: (i, 0)) for w in streamed],
        out_shape=[jax.ShapeDtypeStruct((n, D_MODEL), F32)]
                  + [jax.ShapeDtypeStruct(w.shape[1:], BF16) for w in streamed],
        scratch_shapes=[pltpu.VMEM((tm, D_MODEL), BF16), pltpu.VMEM((tm, D_MODEL), BF16)],
        compiler_params=_params(1),
        name="mix_out",
    )(x2, oa, ob, g_mix, *mix_weights, w_gate, w_up, w_down)
    return outs[0], outs[1:]


def _ffn_kernel(x_ref, g_ref, wg_ref, wu_ref, wd_ref, o_ref, h_scr, a_scr):
    hidden = wg_ref.shape[1]
    for rb in range(0, x_ref.shape[0], ROW_BLOCK):
        rows = slice(rb, rb + ROW_BLOCK)
        xf = x_ref[rows, :]
        h_scr[rows, :] = ((xf * _rms_scale(xf)) * g_ref[...]).astype(BF16)
        for c in range(0, hidden, MXU_N):
            cs = slice(c, c + MXU_N)
            gate = jnp.dot(h_scr[rows, :], wg_ref[:, cs], preferred_element_type=F32)
            up = jnp.dot(h_scr[rows, :], wu_ref[:, cs], preferred_element_type=F32)
            a_scr[rows, cs] = (jax.nn.silu(gate) * up).astype(BF16)
        for c in range(0, D_MODEL, MXU_N):
            cs = slice(c, c + MXU_N)
            o_ref[rows, cs] = x_ref[rows, cs] + jnp.dot(a_scr[rows, :], wd_ref[:, cs], preferred_element_type=F32)


def _ffn(x1, g_ffn, w_gate, w_up, w_down):
    n = x1.shape[0]
    tm = TOK_TILE
    hidden = w_gate.shape[1]
    tile = pl.BlockSpec((tm, D_MODEL), lambda i: (i, 0))
    return pl.pallas_call(
        _ffn_kernel,
        grid=(n // tm,),
        in_specs=[tile, _resident((1, D_MODEL)), _resident(w_gate.shape), _resident(w_up.shape),
                  _resident(w_down.shape)],
        out_specs=tile,
        out_shape=jax.ShapeDtypeStruct((n, D_MODEL), F32),
        scratch_shapes=[pltpu.VMEM((tm, D_MODEL), BF16), pltpu.VMEM((tm, hidden), BF16)],
        compiler_params=_params(1),
        name="ffn",
    )(x1, g_ffn, w_gate, w_up, w_down)


def _layer(x2, b, t, layer, norm_mix, w_in, q_norm_a, k_norm_a, rpb_a, q_norm_b, k_norm_b, sink_b, t5_table,
           w_branch_a, w_branch_b, w_out, norm_ffn, w_gate, w_up, w_down):
    g_mix = norm_mix[layer].reshape(1, D_MODEL)
    reps = MXU_N // HEAD_DIM
    root_d = HEAD_DIM ** 0.5
    q_fold = QK_SCALE * root_d * LOG2E
    qk_gain = jnp.stack([jnp.tile(q_norm_a[layer] * q_fold, reps), jnp.tile(k_norm_a[layer] * root_d, reps),
                         jnp.tile(q_norm_b[layer] * q_fold, reps), jnp.tile(k_norm_b[layer] * root_d, reps)]
                        ).astype(F32)

    (qa, ka, va, qb, kbd, vbd), mix_weights = _in_proj(x2, g_mix, qk_gain, w_in, w_branch_a, w_branch_b,
                                                       w_out, layer)
    n = x2.shape[0]
    r3 = lambda a: a.reshape(b, t, a.shape[-1])
    oa = _na_attn(rpb_a[layer], r3(qa), r3(ka), r3(va))
    ob = _swa_attn(sink_b[layer], t5_table, r3(qb), r3(kbd), r3(vbd))
    x1, ffn_weights = _mix_out(x2, oa.reshape(n, A_W), ob.reshape(n, B_QW), g_mix, mix_weights,
                               w_gate, w_up, w_down, layer)
    return _ffn(x1, norm_ffn[layer].reshape(1, D_MODEL), *ffn_weights)


def kernel(x, norm_mix, w_in, q_norm_a, k_norm_a, rpb_a, q_norm_b, k_norm_b, sink_b, t5_table,
           w_branch_a, w_branch_b, w_out, norm_ffn, w_gate, w_up, w_down):
    b, t, d = x.shape
    x2 = x.reshape(b * t, d)
    for layer in range(norm_mix.shape[0]):
        x2 = _layer(x2, b, t, layer, norm_mix, w_in, q_norm_a, k_norm_a, rpb_a, q_norm_b, k_norm_b, sink_b,
                    t5_table, w_branch_a, w_branch_b, w_out, norm_ffn, w_gate, w_up, w_down)
    return x2.reshape(b, t, d)
```

```python
import functools

import jax
import jax.numpy as jnp
import numpy as np
from jax import lax
from jax.experimental import pallas as pl
from jax.experimental.pallas import tpu as pltpu

F32 = jnp.float32
BF16 = jnp.bfloat16

D_MODEL = 1024
HEAD_DIM = 64
GRID_W = 64
NA_HEADS = 8
NA_KH = 8
NA_KW = 16
SW_HEADS = 8
SW_KV_HEADS = 2
SW_GROUP = SW_HEADS // SW_KV_HEADS
SW_WINDOW = 128
SW_BLOCK = 128
T5_BUCKETS = 32
T5_MAX_DIST = 128
A_W = NA_HEADS * HEAD_DIM
B_QW = SW_HEADS * HEAD_DIM
B_KVW = SW_KV_HEADS * HEAD_DIM
QKV_W = 3 * A_W + B_QW + 2 * B_KVW
RMS_EPS = 1e-6
NEG_INF = -1e30
QK_SCALE = HEAD_DIM ** -0.5
LOG2E = 1.4426950408889634

LANES = 128
MXU_N = 256
VMEM_LIMIT = 56 * 1024 * 1024

TOK_TILE = 1024
ROW_BLOCK = 256
NA_ROWS_PER_STEP = 32
PROJ_SKEW = 4
NA_SKEW = 2
SW_SKEW = 4
SW_BLOCKS_PER_STEP = 16
NA_SLOTS = NA_SKEW + 1
NA_TABLE_HALF = 32
SW_SLOTS = SW_SKEW + 1


def _resident(shape):
    return pl.BlockSpec(shape, lambda *_: (0,) * len(shape), pipeline_mode=pl.Buffered(1))


def _params(n_axes):
    return pltpu.CompilerParams(dimension_semantics=("arbitrary",) * n_axes, vmem_limit_bytes=VMEM_LIMIT)


def _rms_scale(xf):
    return lax.rsqrt(jnp.mean(xf * xf, axis=-1, keepdims=True) + RMS_EPS)


def _group_rms_scale(p):
    y = p * p
    lo = lax.broadcasted_iota(jnp.int32, (p.shape[0], LANES), 1) < HEAD_DIM
    scales = []
    for c in range(0, p.shape[1], LANES):
        yc = y[:, c:c + LANES]
        s_lo = jnp.sum(jnp.where(lo, yc, 0.0), axis=-1, keepdims=True)
        s_hi = jnp.sum(jnp.where(lo, 0.0, yc), axis=-1, keepdims=True)
        r_lo = lax.rsqrt(s_lo + HEAD_DIM * RMS_EPS)
        r_hi = lax.rsqrt(s_hi + HEAD_DIM * RMS_EPS)
        scales.append(jnp.where(lo, r_lo, r_hi))
    return jnp.concatenate(scales, axis=1)


def _cast_rows(pairs):
    for src_ref, dst_ref in pairs:
        dst_ref[...] = src_ref[:, src_ref.shape[1] - dst_ref.shape[1]:].astype(BF16)


def _in_proj_kernel(x_ref, g_ref, w_ref, qk_gain_ref, wg_f32, wa_f32, wb_f32, wo_f32,
                    qa_ref, ka_ref, va_ref, qb_ref, kbd_ref, vbd_ref, wg_bf, wa_bf, wb_bf, wo_bf,
                    w_scr, h_scr, p_scr):
    @pl.when(pl.program_id(0) == 0)
    def _cast_own_weights():
        for c in range(0, QKV_W, MXU_N):
            w_scr[:, c:c + MXU_N] = w_ref[:, c:c + MXU_N].astype(BF16)

    _cast_rows(((wg_f32, wg_bf), (wa_f32, wa_bf), (wb_f32, wb_bf), (wo_f32, wo_bf)))

    def normed(out_ref, rows, c, gain_row):
        def epilogue(slot):
            p = p_scr[slot]
            scale = _group_rms_scale(p)
            out_ref[rows, c:c + MXU_N] = (p * scale * qk_gain_ref[gain_row:gain_row + 1, :]).astype(BF16)
        return epilogue

    def plain(out_ref, rows, c):
        def epilogue(slot):
            out_ref[rows, c:c + MXU_N] = p_scr[slot].astype(BF16)
        return epilogue

    def kv_b(rows):
        def epilogue(slot):
            kb = p_scr[slot, :, :B_KVW]
            kb = kb * _group_rms_scale(kb) * qk_gain_ref[3:4, :LANES]
            vb = p_scr[slot, :, B_KVW:]
            lo = lax.broadcasted_iota(jnp.int32, kb.shape, 1) < HEAD_DIM
            k_swapped = pltpu.roll(kb, HEAD_DIM, axis=1)
            kbd_ref[rows, :LANES] = jnp.where(lo, kb, k_swapped).astype(BF16)
            kbd_ref[rows, LANES:] = jnp.where(lo, k_swapped, kb).astype(BF16)
            v_swapped = pltpu.roll(vb, HEAD_DIM, axis=1)
            for blk, (left, right) in enumerate(((vb, 1.0), (1.0, v_swapped), (v_swapped, 1.0), (1.0, vb))):
                vbd_ref[rows, blk * LANES:(blk + 1) * LANES] = jnp.where(lo, left, right).astype(BF16)
        return epilogue

    units = []
    for rb in range(0, x_ref.shape[0], ROW_BLOCK):
        rows = slice(rb, rb + ROW_BLOCK)
        xf = x_ref[rows, :]
        h_scr[rows, :] = ((xf * _rms_scale(xf)) * g_ref[...]).astype(BF16)
        for c in range(0, A_W, MXU_N):
            units += [(rows, c, normed(qa_ref, rows, c, 0)), (rows, A_W + c, normed(ka_ref, rows, c, 1)),
                      (rows, 3 * A_W + c, normed(qb_ref, rows, c, 2))]
        units.append((rows, 3 * A_W + B_QW, kv_b(rows)))
        units += [(rows, 2 * A_W + c, plain(va_ref, rows, c)) for c in range(0, A_W, MXU_N)]
    n_slots = PROJ_SKEW + 1
    for idx in range(len(units) + PROJ_SKEW):
        if idx < len(units):
            rows, c0, _ = units[idx]
            p_scr[idx % n_slots] = jnp.dot(h_scr[rows, :], w_scr[:, c0:c0 + MXU_N], preferred_element_type=F32)
        if idx >= PROJ_SKEW:
            units[idx - PROJ_SKEW][2]((idx - PROJ_SKEW) % n_slots)


def _row_stream(n_rows, n_cols, steps, layer):
    return pl.BlockSpec((None, n_rows // steps, n_cols), lambda i: (layer, i, 0))


def _in_proj(x2, g_mix, qk_gain, w_in, w_a, w_b, w_o, layer):
    n = x2.shape[0]
    tm = TOK_TILE
    steps = n // tm
    tile = lambda w: pl.BlockSpec((tm, w), lambda i: (i, 0))
    gate_w = w_in.shape[2] - QKV_W
    out_w = (A_W, A_W, A_W, B_QW, 2 * LANES, 4 * LANES)
    streamed = ((w_in, gate_w), (w_a, D_MODEL), (w_b, D_MODEL), (w_o, D_MODEL))
    outs = pl.pallas_call(
        _in_proj_kernel,
        grid=(steps,),
        in_specs=[tile(D_MODEL), _resident((1, D_MODEL)),
                  pl.BlockSpec((None, D_MODEL, QKV_W), lambda i: (layer, 0, 0), pipeline_mode=pl.Buffered(1)),
                  _resident((4, MXU_N))]
                 + [_row_stream(w.shape[1], w.shape[2], steps, layer) for w, _ in streamed],
        out_specs=[tile(w) for w in out_w]
                  + [pl.BlockSpec((w.shape[1] // steps, cols), lambda i: (i, 0)) for w, cols in streamed],
        out_shape=[jax.ShapeDtypeStruct((n, w), BF16) for w in out_w]
                  + [jax.ShapeDtypeStruct((w.shape[1], cols), BF16) for w, cols in streamed],
        scratch_shapes=[pltpu.VMEM((D_MODEL, QKV_W), BF16), pltpu.VMEM((tm, D_MODEL), BF16),
                        pltpu.VMEM((PROJ_SKEW + 1, ROW_BLOCK, MXU_N), F32)],
        compiler_params=_params(1),
        name="in_proj",
    )(x2, g_mix, w_in, qk_gain, w_in, w_a, w_b, w_o)
    return outs[:6], outs[6:]


def _na_kernel(rpb_ref, dc_ref, q_ref, k_ref, v_ref, o_ref, bias_ref, s_ref, *, rows):
    t = pl.program_id(1)
    lo = lax.broadcasted_iota(jnp.int32, (GRID_W, LANES), 1) < HEAD_DIM
    nt = (((1,), (1,)), ((), ()))
    n_slab = 2 * NA_KH - 2

    @pl.when((pl.program_id(0) == 0) & (t == 0))
    def _build_bias():
        offs = dc_ref[...]
        in_win = offs >= 0
        take = jnp.maximum(offs, 0)

        def build(idx, carry):
            h = idx // n_slab
            d = idx - h * n_slab
            row = jnp.broadcast_to(rpb_ref[pl.ds(d * NA_HEADS + h, 1), :] * LOG2E, (GRID_W, LANES))
            slab = jnp.where(in_win, jnp.take_along_axis(row, take, axis=1), NEG_INF)
            row0 = pl.multiple_of((h % 2) * GRID_W, GRID_W)
            bias_ref[h // 2, d, pl.ds(row0, GRID_W), :] = slab
            return carry

        lax.fori_loop(0, NA_HEADS * n_slab, build, 0, unroll=NA_HEADS)

    win0 = _na_window_start(t, rows)
    row_info = []
    for i in range(NA_ROWS_PER_STEP):
        r = t * NA_ROWS_PER_STEP + i
        rs = jnp.clip(r - NA_KH // 2, 0, rows - NA_KH)
        row_info.append(((NA_KH - 1) - (r - rs), pl.multiple_of((rs - win0) * GRID_W, GRID_W)))

    def scores(i, p, slot):
        d0, kstart = row_info[i]
        cols = slice(p * LANES, (p + 1) * LANES)
        q = q_ref[0, i * GRID_W:(i + 1) * GRID_W, cols]
        zero = jnp.zeros_like(q)
        q2 = jnp.concatenate([jnp.where(lo, q, zero), jnp.where(lo, zero, q)], axis=0)
        kk = k_ref[0, pl.ds(kstart, NA_KH * GRID_W), cols]
        s = lax.dot_general(q2, kk, nt, preferred_element_type=F32)
        for v in range(NA_KH // 2):
            vcols = slice(v * LANES, (v + 1) * LANES)
            s_ref[slot, :, vcols] = s[:, vcols] + bias_ref[p, d0 + 2 * v]

    ones_cols = jnp.ones((NA_KH * GRID_W, LANES), BF16)

    def finish(i, p, slot):
        _, kstart = row_info[i]
        cols = slice(p * LANES, (p + 1) * LANES)
        m = jnp.max(s_ref[slot], axis=-1, keepdims=True)
        e = jnp.exp2(s_ref[slot] - m).astype(BF16)
        vv = jnp.concatenate([v_ref[0, pl.ds(kstart, NA_KH * GRID_W), cols], ones_cols], axis=1)
        o2 = jnp.dot(e, vv, preferred_element_type=F32)
        o2 = o2[:, :LANES] / o2[:, LANES:]
        o = jnp.where(lo, o2[:GRID_W], o2[GRID_W:])
        o_ref[0, i * GRID_W:(i + 1) * GRID_W, cols] = o.astype(BF16)

    units = [(i, p) for i in range(NA_ROWS_PER_STEP) for p in range(NA_HEADS // 2)]
    n_slots = s_ref.shape[0]
    for idx in range(len(units) + NA_SKEW):
        if idx < len(units):
            scores(*units[idx], idx % n_slots)
        if idx >= NA_SKEW:
            finish(*units[idx - NA_SKEW], (idx - NA_SKEW) % n_slots)


def _na_window_start(step, rows):
    return jnp.clip(step * NA_ROWS_PER_STEP - NA_KH // 2, 0, rows - (NA_ROWS_PER_STEP + NA_KH))


def _na_col_offsets():
    c = np.arange(GRID_W)
    kc = c[None, :]
    win_start = np.clip(c - NA_KW // 2, 0, GRID_W - NA_KW)[:, None]
    in_win = (kc >= win_start) & (kc < win_start + NA_KW)
    dc = kc - c[:, None] + (NA_KW - 1)
    return np.concatenate([np.where(in_win, dc, -1), np.where(in_win, dc + NA_TABLE_HALF, -1)],
                          axis=1).astype(np.int32)


def _na_attn(rpb, qa, ka, va):
    b, t, _ = qa.shape
    rows = t // GRID_W
    tq = NA_ROWS_PER_STEP * GRID_W
    tile = pl.BlockSpec((1, tq, A_W), lambda bi, ti: (bi, ti, 0))
    window = pl.BlockSpec((pl.Element(1), pl.Element((NA_ROWS_PER_STEP + NA_KH) * GRID_W), pl.Element(A_W)),
                          lambda bi, ti: (bi, _na_window_start(ti, rows) * GRID_W, 0))
    per_head = jnp.transpose(rpb.astype(F32), (0, 2, 1))
    pad = lambda a, w: jnp.pad(a, ((0, 0), (0, 0), (0, w - a.shape[-1])))
    table = jnp.concatenate([pad(per_head[:-1], NA_TABLE_HALF), pad(per_head[1:], LANES - NA_TABLE_HALF)], axis=-1)
    table = table.reshape(-1, LANES)
    return pl.pallas_call(
        functools.partial(_na_kernel, rows=rows),
        grid=(b, rows // NA_ROWS_PER_STEP),
        in_specs=[_resident(table.shape), _resident((GRID_W, LANES)), tile, window, window],
        out_specs=tile,
        out_shape=jax.ShapeDtypeStruct((b, t, A_W), BF16),
        scratch_shapes=[pltpu.VMEM((NA_HEADS // 2, 2 * NA_KH - 2, LANES, LANES), F32),
                        pltpu.VMEM((NA_SLOTS, 2 * GRID_W, NA_KH * GRID_W), F32)],
        compiler_params=_params(2),
        name="na_attn",
    )(table, jnp.asarray(_na_col_offsets()), qa, ka, va)


def _swa_kernel(sink_ref, t5_ref, bucket_ref, q_ref, kp_ref, kc_ref, kn_ref, vp_ref, vc_ref, vn_ref, o_ref,
                bias_ref, s_ref, *, n_tiles):
    n = pl.program_id(1)
    lo = lax.broadcasted_iota(jnp.int32, (SW_BLOCK, LANES), 1) < HEAD_DIM
    nt = (((1,), (1,)), ((), ()))

    @pl.when((pl.program_id(0) == 0) & (n == 0))
    def _build_bias():
        bucket = bucket_ref[...]
        in_band = bucket >= 0
        take = jnp.maximum(bucket, 0)

        def build(h, carry):
            row = jnp.broadcast_to(t5_ref[pl.ds(h, 1), :] * LOG2E, (SW_BLOCK, LANES))
            vals = [jnp.take_along_axis(row, take[:, c:c + LANES], axis=1) for c in range(0, 3 * SW_BLOCK, LANES)]
            row0 = pl.multiple_of((h % SW_GROUP) * SW_BLOCK, SW_BLOCK)
            bias_ref[h // SW_GROUP, pl.ds(row0, SW_BLOCK), :] = jnp.where(
                in_band, jnp.concatenate(vals, axis=1), NEG_INF)
            return carry

        lax.fori_loop(0, SW_HEADS, build, 0)

    def window(prev_ref, own_ref, next_ref, blk_col, j):
        kcols = slice(blk_col * LANES, (blk_col + 1) * LANES)
        blk = lambda b: own_ref[0, b * SW_BLOCK:(b + 1) * SW_BLOCK, kcols]
        first = prev_ref[0, :, kcols] if j == 0 else blk(j - 1)
        last = next_ref[0, :, kcols] if j == SW_BLOCKS_PER_STEP - 1 else blk(j + 1)
        return jnp.concatenate([first, blk(j), last], axis=0)

    def scores(kvh, j, c, hh, slot):
        rows = slice(j * SW_BLOCK, (j + 1) * SW_BLOCK)
        qb = q_ref[0, rows, (2 * kvh + c) * LANES:(2 * kvh + c + 1) * LANES]
        q1 = jnp.where(lo if hh == 0 else ~lo, qb, jnp.zeros_like(qb))
        s_ref[slot] = lax.dot_general(q1, window(kp_ref, kc_ref, kn_ref, kvh, j), nt,
                                      preferred_element_type=F32)

    def finish(kvh, j, c, hh, slot):
        rows = slice(j * SW_BLOCK, (j + 1) * SW_BLOCK)
        head = kvh * SW_GROUP + 2 * c + hh
        brows = slice((2 * c + hh) * SW_BLOCK, (2 * c + hh + 1) * SW_BLOCK)
        sg = s_ref[slot] + bias_ref[kvh, brows, :]
        if j == 0:
            sg = jnp.concatenate([jnp.where(n > 0, sg[:, :SW_BLOCK], NEG_INF), sg[:, SW_BLOCK:]], axis=1)
        if j == SW_BLOCKS_PER_STEP - 1:
            sg = jnp.concatenate([sg[:, :2 * SW_BLOCK],
                                  jnp.where(n < n_tiles - 1, sg[:, 2 * SW_BLOCK:], NEG_INF)], axis=1)
        sk = sink_ref[head] * LOG2E
        m = jnp.maximum(jnp.max(sg, axis=-1, keepdims=True), sk)
        e = jnp.exp2(sg - m).astype(BF16)
        o2 = jnp.dot(e, window(vp_ref, vc_ref, vn_ref, 2 * kvh + hh, j), preferred_element_type=F32)
        o = o2 / (pltpu.roll(o2, HEAD_DIM, axis=1) + jnp.exp2(sk - m))
        col0 = (2 * kvh + c) * LANES + hh * HEAD_DIM
        o_ref[0, rows, col0:col0 + HEAD_DIM] = o[:, hh * HEAD_DIM:(hh + 1) * HEAD_DIM].astype(BF16)

    units = [(kvh, j, c, hh) for j in range(SW_BLOCKS_PER_STEP) for kvh in range(SW_KV_HEADS)
             for c in range(2) for hh in range(2)]
    n_slots = s_ref.shape[0]
    for idx in range(len(units) + SW_SKEW):
        if idx < len(units):
            scores(*units[idx], idx % n_slots)
        if idx >= SW_SKEW:
            finish(*units[idx - SW_SKEW], (idx - SW_SKEW) % n_slots)


def _t5_bucket(rel):
    half = T5_BUCKETS // 2
    max_exact = half // 2
    ret = (rel > 0).astype(np.int32) * half
    n = np.abs(rel)
    large = max_exact + (np.log(np.maximum(n, 1) / max_exact)
                         / np.log(T5_MAX_DIST / max_exact) * (half - max_exact)).astype(np.int32)
    large = np.minimum(large, half - 1)
    return ret + np.where(n < max_exact, n, large)


def _swa_bucket_index():
    rel = np.arange(3 * SW_BLOCK)[None, :] - SW_BLOCK - np.arange(SW_BLOCK)[:, None]
    return np.where(np.abs(rel) <= SW_WINDOW, _t5_bucket(rel), -1).astype(np.int32)


def _swa_attn(sink, t5_table, qb, kbd, vbd):
    b, t, _ = qb.shape
    tq = SW_BLOCKS_PER_STEP * SW_BLOCK
    n_tiles = t // tq
    last_blk = t // SW_BLOCK - 1
    qtile = pl.BlockSpec((1, tq, B_QW), lambda bi, ni: (bi, ni, 0))
    own = lambda w: pl.BlockSpec((1, tq, w), lambda bi, ni: (bi, ni, 0))
    prev = lambda w: pl.BlockSpec((1, SW_BLOCK, w),
                                  lambda bi, ni: (bi, jnp.maximum(ni * SW_BLOCKS_PER_STEP - 1, 0), 0))
    nxt = lambda w: pl.BlockSpec((1, SW_BLOCK, w),
                                 lambda bi, ni: (bi, jnp.minimum((ni + 1) * SW_BLOCKS_PER_STEP, last_blk), 0))
    kw, vw = kbd.shape[-1], vbd.shape[-1]
    smem = pl.BlockSpec(memory_space=pltpu.SMEM)
    table = jnp.pad(t5_table.astype(F32).T, ((0, 0), (0, LANES - t5_table.shape[0])))
    return pl.pallas_call(
        functools.partial(_swa_kernel, n_tiles=n_tiles),
        grid=(b, n_tiles),
        in_specs=[smem, _resident((SW_HEADS, LANES)), _resident((SW_BLOCK, 3 * SW_BLOCK)),
                  qtile, prev(kw), own(kw), nxt(kw), prev(vw), own(vw), nxt(vw)],
        out_specs=qtile,
        out_shape=jax.ShapeDtypeStruct((b, t, B_QW), BF16),
        scratch_shapes=[pltpu.VMEM((SW_KV_HEADS, SW_GROUP * SW_BLOCK, 3 * SW_BLOCK), F32),
                        pltpu.VMEM((SW_SLOTS, SW_BLOCK, 3 * SW_BLOCK), F32)],
        compiler_params=_params(2),
        name="swa_attn",
    )(sink.astype(F32), table, jnp.asarray(_swa_bucket_index()),
      qb, kbd, kbd, kbd, vbd, vbd, vbd)


def _mix_kernel(x_ref, oa_ref, ob_ref, g_ref, wg_ref, wa_ref, wb_ref, wo_ref, wgate_f32, wup_f32, wdown_f32,
                x1_ref, wgate_bf, wup_bf, wdown_bf, h_scr, y_scr):
    _cast_rows(((wgate_f32, wgate_bf), (wup_f32, wup_bf), (wdown_f32, wdown_bf)))
    for rb in range(0, x_ref.shape[0], ROW_BLOCK):
        rows = slice(rb, rb + ROW_BLOCK)
        xf = x_ref[rows, :]
        h_scr[rows, :] = ((xf * _rms_scale(xf)) * g_ref[...]).astype(BF16)
        for c in range(0, D_MODEL, MXU_N):
            cs = slice(c, c + MXU_N)
            gs = slice(D_MODEL + c, D_MODEL + c + MXU_N)
            ga = jnp.dot(h_scr[rows, :], wg_ref[:, cs], preferred_element_type=F32)
            gb = jnp.dot(h_scr[rows, :], wg_ref[:, gs], preferred_element_type=F32)
            ya = jnp.dot(oa_ref[rows, :], wa_ref[:, cs], preferred_element_type=F32)
            yb = jnp.dot(ob_ref[rows, :], wb_ref[:, cs], preferred_element_type=F32)
            y_scr[rows, cs] = (jax.nn.sigmoid(ga) * ya + jax.nn.sigmoid(gb) * yb).astype(BF16)
        for c in range(0, D_MODEL, MXU_N):
            cs = slice(c, c + MXU_N)
            x1_ref[rows, cs] = x_ref[rows, cs] + jnp.dot(y_scr[rows, :], wo_ref[:, cs], preferred_element_type=F32)


def _mix_out(x2, oa, ob, g_mix, mix_weights, w_gate, w_up, w_down, layer):
    n = x2.shape[0]
    tm = TOK_TILE
    steps = n // tm
    tile = lambda w: pl.BlockSpec((tm, w), lambda i: (i, 0))
    streamed = (w_gate, w_up, w_down)
    outs = pl.pallas_call(
        _mix_kernel,
        grid=(steps,),
        in_specs=[tile(D_MODEL), tile(A_W), tile(B_QW), _resident((1, D_MODEL))]
                 + [_resident(w.shape) for w in mix_weights]
                 + [_row_stream(w.shape[1], w.shape[2], steps, layer) for w in streamed],
        out_specs=[tile(D_MODEL)]
                  + [pl.BlockSpec((w.shape[1] // steps, w.shape[2]), lambda i: (i, 0)) for w in streamed],
        out_shape=[jax.ShapeDtypeStruct((n, D_MODEL), F32)]
                  + [jax.ShapeDtypeStruct(w.shape[1:], BF16) for w in streamed],
        scratch_shapes=[pltpu.VMEM((tm, D_MODEL), BF16), pltpu.VMEM((tm, D_MODEL), BF16)],
        compiler_params=_params(1),
        name="mix_out",
    )(x2, oa, ob, g_mix, *mix_weights, w_gate, w_up, w_down)
    return outs[0], outs[1:]


def _ffn_kernel(x_ref, g_ref, wg_ref, wu_ref, wd_ref, o_ref, h_scr, a_scr):
    hidden = wg_ref.shape[1]
    for rb in range(0, x_ref.shape[0], ROW_BLOCK):
        rows = slice(rb, rb + ROW_BLOCK)
        xf = x_ref[rows, :]
        h_scr[rows, :] = ((xf * _rms_scale(xf)) * g_ref[...]).astype(BF16)
        for c in range(0, hidden, MXU_N):
            cs = slice(c, c + MXU_N)
            gate = jnp.dot(h_scr[rows, :], wg_ref[:, cs], preferred_element_type=F32)
            up = jnp.dot(h_scr[rows, :], wu_ref[:, cs], preferred_element_type=F32)
            a_scr[rows, cs] = (jax.nn.silu(gate) * up).astype(BF16)
        for c in range(0, D_MODEL, MXU_N):
            cs = slice(c, c + MXU_N)
            o_ref[rows, cs] = x_ref[rows, cs] + jnp.dot(a_scr[rows, :], wd_ref[:, cs], preferred_element_type=F32)


def _ffn(x1, g_ffn, w_gate, w_up, w_down):
    n = x1.shape[0]
    tm = TOK_TILE
    hidden = w_gate.shape[1]
    tile = pl.BlockSpec((tm, D_MODEL), lambda i: (i, 0))
    return pl.pallas_call(
        _ffn_kernel,
        grid=(n // tm,),
        in_specs=[tile, _resident((1, D_MODEL)), _resident(w_gate.shape), _resident(w_up.shape),
                  _resident(w_down.shape)],
        out_specs=tile,
        out_shape=jax.ShapeDtypeStruct((n, D_MODEL), F32),
        scratch_shapes=[pltpu.VMEM((tm, D_MODEL), BF16), pltpu.VMEM((tm, hidden), BF16)],
        compiler_params=_params(1),
        name="ffn",
    )(x1, g_ffn, w_gate, w_up, w_down)


def _layer(x2, b, t, layer, norm_mix, w_in, q_norm_a, k_norm_a, rpb_a, q_norm_b, k_norm_b, sink_b, t5_table,
           w_branch_a, w_branch_b, w_out, norm_ffn, w_gate, w_up, w_down):
    g_mix = norm_mix[layer].reshape(1, D_MODEL)
    reps = MXU_N // HEAD_DIM
    root_d = HEAD_DIM ** 0.5
    q_fold = QK_SCALE * root_d * LOG2E
    qk_gain = jnp.stack([jnp.tile(q_norm_a[layer] * q_fold, reps), jnp.tile(k_norm_a[layer] * root_d, reps),
                         jnp.tile(q_norm_b[layer] * q_fold, reps), jnp.tile(k_norm_b[layer] * root_d, reps)]
                        ).astype(F32)

    (qa, ka, va, qb, kbd, vbd), mix_weights = _in_proj(x2, g_mix, qk_gain, w_in, w_branch_a, w_branch_b,
                                                       w_out, layer)
    n = x2.shape[0]
    r3 = lambda a: a.reshape(b, t, a.shape[-1])
    oa = _na_attn(rpb_a[layer], r3(qa), r3(ka), r3(va))
    ob = _swa_attn(sink_b[layer], t5_table, r3(qb), r3(kbd), r3(vbd))
    x1, ffn_weights = _mix_out(x2, oa.reshape(n, A_W), ob.reshape(n, B_QW), g_mix, mix_weights,
                               w_gate, w_up, w_down, layer)
    return _ffn(x1, norm_ffn[layer].reshape(1, D_MODEL), *ffn_weights)


def kernel(x, norm_mix, w_in, q_norm_a, k_norm_a, rpb_a, q_norm_b, k_norm_b, sink_b, t5_table,
           w_branch_a, w_branch_b, w_out, norm_ffn, w_gate, w_up, w_down):
    b, t, d = x.shape
    x2 = x.reshape(b * t, d)
    for layer in range(norm_mix.shape[0]):
        x2 = _layer(x2, b, t, layer, norm_mix, w_in, q_norm_a, k_norm_a, rpb_a, q_norm_b, k_norm_b, sink_b,
                    t5_table, w_branch_a, w_branch_b, w_out, norm_ffn, w_gate, w_up, w_down)
    return x2.reshape(b, t, d)
```

```python
import functools

import jax
import jax.numpy as jnp
import numpy as np
from jax import lax
from jax.experimental import pallas as pl
from jax.experimental.pallas import tpu as pltpu

F32 = jnp.float32
BF16 = jnp.bfloat16

D_MODEL = 1024
HEAD_DIM = 64
GRID_W = 64
NA_HEADS = 8
NA_KH = 8
NA_KW = 16
SW_HEADS = 8
SW_KV_HEADS = 2
SW_GROUP = SW_HEADS // SW_KV_HEADS
SW_WINDOW = 128
SW_BLOCK = 128
T5_BUCKETS = 32
T5_MAX_DIST = 128
A_W = NA_HEADS * HEAD_DIM
B_QW = SW_HEADS * HEAD_DIM
B_KVW = SW_KV_HEADS * HEAD_DIM
QKV_W = 3 * A_W + B_QW + 2 * B_KVW
RMS_EPS = 1e-6
NEG_INF = -1e30
QK_SCALE = HEAD_DIM ** -0.5
LOG2E = 1.4426950408889634

LANES = 128
MXU_N = 256
VMEM_LIMIT = 56 * 1024 * 1024

TOK_TILE = 1024
ROW_BLOCK = 256
NA_ROWS_PER_STEP = 32
PROJ_SKEW = 4
NA_SKEW = 2
SW_SKEW = 4
SW_BLOCKS_PER_STEP = 16
NA_SLOTS = NA_SKEW + 1
NA_TABLE_HALF = 32
SW_SLOTS = SW_SKEW + 1


def _resident(shape):
    return pl.BlockSpec(shape, lambda *_: (0,) * len(shape), pipeline_mode=pl.Buffered(1))


def _params(n_axes):
    return pltpu.CompilerParams(dimension_semantics=("arbitrary",) * n_axes, vmem_limit_bytes=VMEM_LIMIT)


def _rms_scale(xf):
    return lax.rsqrt(jnp.mean(xf * xf, axis=-1, keepdims=True) + RMS_EPS)


def _group_rms_scale(p):
    y = p * p
    lo = lax.broadcasted_iota(jnp.int32, (p.shape[0], LANES), 1) < HEAD_DIM
    scales = []
    for c in range(0, p.shape[1], LANES):
        yc = y[:, c:c + LANES]
        s_lo = jnp.sum(jnp.where(lo, yc, 0.0), axis=-1, keepdims=True)
        s_hi = jnp.sum(jnp.where(lo, 0.0, yc), axis=-1, keepdims=True)
        r_lo = lax.rsqrt(s_lo + HEAD_DIM * RMS_EPS)
        r_hi = lax.rsqrt(s_hi + HEAD_DIM * RMS_EPS)
        scales.append(jnp.where(lo, r_lo, r_hi))
    return jnp.concatenate(scales, axis=1)


def _cast_rows(pairs):
    for src_ref, dst_ref in pairs:
        dst_ref[...] = src_ref[:, src_ref.shape[1] - dst_ref.shape[1]:].astype(BF16)


def _in_proj_kernel(x_ref, g_ref, w_ref, qk_gain_ref, wg_f32, wa_f32, wb_f32, wo_f32,
                    qa_ref, ka_ref, va_ref, qb_ref, kbd_ref, vbd_ref, wg_bf, wa_bf, wb_bf, wo_bf,
                    w_scr, h_scr, p_scr):
    @pl.when(pl.program_id(0) == 0)
    def _cast_own_weights():
        for c in range(0, QKV_W, MXU_N):
            w_scr[:, c:c + MXU_N] = w_ref[:, c:c + MXU_N].astype(BF16)

    _cast_rows(((wg_f32, wg_bf), (wa_f32, wa_bf), (wb_f32, wb_bf), (wo_f32, wo_bf)))

    def normed(out_ref, rows, c, gain_row):
        def epilogue(slot):
            p = p_scr[slot]
            scale = _group_rms_scale(p)
            out_ref[rows, c:c + MXU_N] = (p * scale * qk_gain_ref[gain_row:gain_row + 1, :]).astype(BF16)
        return epilogue

    def plain(out_ref, rows, c):
        def epilogue(slot):
            out_ref[rows, c:c + MXU_N] = p_scr[slot].astype(BF16)
        return epilogue

    def kv_b(rows):
        def epilogue(slot):
            kb = p_scr[slot, :, :B_KVW]
            kb = kb * _group_rms_scale(kb) * qk_gain_ref[3:4, :LANES]
            vb = p_scr[slot, :, B_KVW:]
            lo = lax.broadcasted_iota(jnp.int32, kb.shape, 1) < HEAD_DIM
            k_swapped = pltpu.roll(kb, HEAD_DIM, axis=1)
            kbd_ref[rows, :LANES] = jnp.where(lo, kb, k_swapped).astype(BF16)
            kbd_ref[rows, LANES:] = jnp.where(lo, k_swapped, kb).astype(BF16)
            v_swapped = pltpu.roll(vb, HEAD_DIM, axis=1)
            for blk, (left, right) in enumerate(((vb, 1.0), (1.0, v_swapped), (v_swapped, 1.0), (1.0, vb))):
                vbd_ref[rows, blk * LANES:(blk + 1) * LANES] = jnp.where(lo, left, right).astype(BF16)
        return epilogue

    units = []
    for rb in range(0, x_ref.shape[0], ROW_BLOCK):
        rows = slice(rb, rb + ROW_BLOCK)
        xf = x_ref[rows, :]
        h_scr[rows, :] = ((xf * _rms_scale(xf)) * g_ref[...]).astype(BF16)
        for c in range(0, A_W, MXU_N):
            units += [(rows, c, normed(qa_ref, rows, c, 0)), (rows, A_W + c, normed(ka_ref, rows, c, 1)),
                      (rows, 3 * A_W + c, normed(qb_ref, rows, c, 2))]
        units.append((rows, 3 * A_W + B_QW, kv_b(rows)))
        units += [(rows, 2 * A_W + c, plain(va_ref, rows, c)) for c in range(0, A_W, MXU_N)]
    n_slots = PROJ_SKEW + 1
    for idx in range(len(units) + PROJ_SKEW):
        if idx < len(units):
            rows, c0, _ = units[idx]
            p_scr[idx % n_slots] = jnp.dot(h_scr[rows, :], w_scr[:, c0:c0 + MXU_N], preferred_element_type=F32)
        if idx >= PROJ_SKEW:
            units[idx - PROJ_SKEW][2]((idx - PROJ_SKEW) % n_slots)


def _row_stream(n_rows, n_cols, steps, layer):
    return pl.BlockSpec((None, n_rows // steps, n_cols), lambda i: (layer, i, 0))


def _in_proj(x2, g_mix, qk_gain, w_in, w_a, w_b, w_o, layer):
    n = x2.shape[0]
    tm = TOK_TILE
    steps = n // tm
    tile = lambda w: pl.BlockSpec((tm, w), lambda i: (i, 0))
    gate_w = w_in.shape[2] - QKV_W
    out_w = (A_W, A_W, A_W, B_QW, 2 * LANES, 4 * LANES)
    streamed = ((w_in, gate_w), (w_a, D_MODEL), (w_b, D_MODEL), (w_o, D_MODEL))
    outs = pl.pallas_call(
        _in_proj_kernel,
        grid=(steps,),
        in_specs=[tile(D_MODEL), _resident((1, D_MODEL)),
                  pl.BlockSpec((None, D_MODEL, QKV_W), lambda i: (layer, 0, 0), pipeline_mode=pl.Buffered(1)),
                  _resident((4, MXU_N))]
                 + [_row_stream(w.shape[1], w.shape[2], steps, layer) for w, _ in streamed],
        out_specs=[tile(w) for w in out_w]
                  + [pl.BlockSpec((w.shape[1] // steps, cols), lambda i: (i, 0)) for w, cols in streamed],
        out_shape=[jax.ShapeDtypeStruct((n, w), BF16) for w in out_w]
                  + [jax.ShapeDtypeStruct((w.shape[1], cols), BF16) for w, cols in streamed],
        scratch_shapes=[pltpu.VMEM((D_MODEL, QKV_W), BF16), pltpu.VMEM((tm, D_MODEL), BF16),
                        pltpu.VMEM((PROJ_SKEW + 1, ROW_BLOCK, MXU_N), F32)],
        compiler_params=_params(1),
        name="in_proj",
    )(x2, g_mix, w_in, qk_gain, w_in, w_a, w_b, w_o)
    return outs[:6], outs[6:]


def _na_kernel(rpb_ref, dc_ref, q_ref, k_ref, v_ref, o_ref, bias_ref, s_ref, *, rows):
    t = pl.program_id(1)
    lo = lax.broadcasted_iota(jnp.int32, (GRID_W, LANES), 1) < HEAD_DIM
    nt = (((1,), (1,)), ((), ()))
    n_slab = 2 * NA_KH - 2

    @pl.when((pl.program_id(0) == 0) & (t == 0))
    def _build_bias():
        offs = dc_ref[...]
        in_win = offs >= 0
        take = jnp.maximum(offs, 0)

        def build(idx, carry):
            h = idx // n_slab
            d = idx - h * n_slab
            row = jnp.broadcast_to(rpb_ref[pl.ds(d * NA_HEADS + h, 1), :] * LOG2E, (GRID_W, LANES))
            slab = jnp.where(in_win, jnp.take_along_axis(row, take, axis=1), NEG_INF)
            row0 = pl.multiple_of((h % 2) * GRID_W, GRID_W)
            bias_ref[h // 2, d, pl.ds(row0, GRID_W), :] = slab
            return carry

        lax.fori_loop(0, NA_HEADS * n_slab, build, 0, unroll=NA_HEADS)

    win0 = _na_window_start(t, rows)
    row_info = []
    for i in range(NA_ROWS_PER_STEP):
        r = t * NA_ROWS_PER_STEP + i
        rs = jnp.clip(r - NA_KH // 2, 0, rows - NA_KH)
        row_info.append(((NA_KH - 1) - (r - rs), pl.multiple_of((rs - win0) * GRID_W, GRID_W)))

    def scores(i, p, slot):
        d0, kstart = row_info[i]
        cols = slice(p * LANES, (p + 1) * LANES)
        q = q_ref[0, i * GRID_W:(i + 1) * GRID_W, cols]
        zero = jnp.zeros_like(q)
        q2 = jnp.concatenate([jnp.where(lo, q, zero), jnp.where(lo, zero, q)], axis=0)
        kk = k_ref[0, pl.ds(kstart, NA_KH * GRID_W), cols]
        s = lax.dot_general(q2, kk, nt, preferred_element_type=F32)
        for v in range(NA_KH // 2):
            vcols = slice(v * LANES, (v + 1) * LANES)
            s_ref[slot, :, vcols] = s[:, vcols] + bias_ref[p, d0 + 2 * v]

    ones_cols = jnp.ones((NA_KH * GRID_W, LANES), BF16)

    def finish(i, p, slot):
        _, kstart = row_info[i]
        cols = slice(p * LANES, (p + 1) * LANES)
        m = jnp.max(s_ref[slot], axis=-1, keepdims=True)
        e = jnp.exp2(s_ref[slot] - m).astype(BF16)
        vv = jnp.concatenate([v_ref[0, pl.ds(kstart, NA_KH * GRID_W), cols], ones_cols], axis=1)
        o2 = jnp.dot(e, vv, preferred_element_type=F32)
        o2 = o2[:, :LANES] / o2[:, LANES:]
        o = jnp.where(lo, o2[:GRID_W], o2[GRID_W:])
        o_ref[0, i * GRID_W:(i + 1) * GRID_W, cols] = o.astype(BF16)

    units = [(i, p) for i in range(NA_ROWS_PER_STEP) for p in range(NA_HEADS // 2)]
    n_slots = s_ref.shape[0]
    for idx in range(len(units) + NA_SKEW):
        if idx < len(units):
            scores(*units[idx], idx % n_slots)
        if idx >= NA_SKEW:
            finish(*units[idx - NA_SKEW], (idx - NA_SKEW) % n_slots)


def _na_window_start(step, rows):
    return jnp.clip(step * NA_ROWS_PER_STEP - NA_KH // 2, 0, rows - (NA_ROWS_PER_STEP + NA_KH))


def _na_col_offsets():
    c = np.arange(GRID_W)
    kc = c[None, :]
    win_start = np.clip(c - NA_KW // 2, 0, GRID_W - NA_KW)[:, None]
    in_win = (kc >= win_start) & (kc < win_start + NA_KW)
    dc = kc - c[:, None] + (NA_KW - 1)
    return np.concatenate([np.where(in_win, dc, -1), np.where(in_win, dc + NA_TABLE_HALF, -1)],
                          axis=1).astype(np.int32)


def _na_attn(rpb, qa, ka, va):
    b, t, _ = qa.shape
    rows = t // GRID_W
    tq = NA_ROWS_PER_STEP * GRID_W
    tile = pl.BlockSpec((1, tq, A_W), lambda bi, ti: (bi, ti, 0))
    window = pl.BlockSpec((pl.Element(1), pl.Element((NA_ROWS_PER_STEP + NA_KH) * GRID_W), pl.Element(A_W)),
                          lambda bi, ti: (bi, _na_window_start(ti, rows) * GRID_W, 0))
    per_head = jnp.transpose(rpb.astype(F32), (0, 2, 1))
    pad = lambda a, w: jnp.pad(a, ((0, 0), (0, 0), (0, w - a.shape[-1])))
    table = jnp.concatenate([pad(per_head[:-1], NA_TABLE_HALF), pad(per_head[1:], LANES - NA_TABLE_HALF)], axis=-1)
    table = table.reshape(-1, LANES)
    return pl.pallas_call(
        functools.partial(_na_kernel, rows=rows),
        grid=(b, rows // NA_ROWS_PER_STEP),
        in_specs=[_resident(table.shape), _resident((GRID_W, LANES)), tile, window, window],
        out_specs=tile,
        out_shape=jax.ShapeDtypeStruct((b, t, A_W), BF16),
        scratch_shapes=[pltpu.VMEM((NA_HEADS // 2, 2 * NA_KH - 2, LANES, LANES), F32),
                        pltpu.VMEM((NA_SLOTS, 2 * GRID_W, NA_KH * GRID_W), F32)],
        compiler_params=_params(2),
        name="na_attn",
    )(table, jnp.asarray(_na_col_offsets()), qa, ka, va)


def _swa_kernel(sink_ref, t5_ref, bucket_ref, q_ref, kp_ref, kc_ref, kn_ref, vp_ref, vc_ref, vn_ref, o_ref,
                bias_ref, s_ref, *, n_tiles):
    n = pl.program_id(1)
    lo = lax.broadcasted_iota(jnp.int32, (SW_BLOCK, LANES), 1) < HEAD_DIM

    @pl.when((pl.program_id(0) == 0) & (n == 0))
    def _build_bias():
        bucket = bucket_ref[...]
        in_band = bucket >= 0
        take = jnp.maximum(bucket, 0)

        def build(h, carry):
            row = jnp.broadcast_to(t5_ref[pl.ds(h, 1), :] * LOG2E, (SW_BLOCK, LANES))
            vals = [jnp.take_along_axis(row, take[:, c:c + LANES], axis=1) for c in range(0, 3 * SW_BLOCK, LANES)]
            row0 = pl.multiple_of((h % SW_GROUP) * SW_BLOCK, SW_BLOCK)
            bias_ref[h // SW_GROUP, pl.ds(row0, SW_BLOCK), :] = jnp.where(
                in_band, jnp.concatenate(vals, axis=1), NEG_INF)
            return carry

        lax.fori_loop(0, SW_HEADS, build, 0)

    def window(prev_ref, own_ref, next_ref, blk_col, j):
        kcols = slice(blk_col * LANES, (blk_col + 1) * LANES)
        blk = lambda b: own_ref[0, b * SW_BLOCK:(b + 1) * SW_BLOCK, kcols]
        first = prev_ref[0, :, kcols] if j == 0 else blk(j - 1)
        last = next_ref[0, :, kcols] if j == SW_BLOCKS_PER_STEP - 1 else blk(j + 1)
        return jnp.concatenate([first, blk(j), last], axis=0)

    def window_t(kvh, j):
        krows = slice(kvh * LANES, (kvh + 1) * LANES)
        blk = lambda b: kc_ref[0, krows, b * SW_BLOCK:(b + 1) * SW_BLOCK]
        first = kp_ref[0, krows, :] if j == 0 else blk(j - 1)
        last = kn_ref[0, krows, :] if j == SW_BLOCKS_PER_STEP - 1 else blk(j + 1)
        return jnp.concatenate([first, blk(j), last], axis=1)

    def scores(kvh, j, c, hh, slot):
        rows = slice(j * SW_BLOCK, (j + 1) * SW_BLOCK)
        qb = q_ref[0, rows, (2 * kvh + c) * LANES:(2 * kvh + c + 1) * LANES]
        q1 = jnp.where(lo if hh == 0 else ~lo, qb, jnp.zeros_like(qb))
        s_ref[slot] = jnp.dot(q1, window_t(kvh, j), preferred_element_type=F32)

    def finish(kvh, j, c, hh, slot):
        rows = slice(j * SW_BLOCK, (j + 1) * SW_BLOCK)
        head = kvh * SW_GROUP + 2 * c + hh
        brows = slice((2 * c + hh) * SW_BLOCK, (2 * c + hh + 1) * SW_BLOCK)
        sg = s_ref[slot] + bias_ref[kvh, brows, :]
        if j == 0:
            sg = jnp.concatenate([jnp.where(n > 0, sg[:, :SW_BLOCK], NEG_INF), sg[:, SW_BLOCK:]], axis=1)
        if j == SW_BLOCKS_PER_STEP - 1:
            sg = jnp.concatenate([sg[:, :2 * SW_BLOCK],
                                  jnp.where(n < n_tiles - 1, sg[:, 2 * SW_BLOCK:], NEG_INF)], axis=1)
        sk = sink_ref[head] * LOG2E
        m = jnp.maximum(jnp.max(sg, axis=-1, keepdims=True), sk)
        e = jnp.exp2(sg - m).astype(BF16)
        o2 = jnp.dot(e, window(vp_ref, vc_ref, vn_ref, 2 * kvh + hh, j), preferred_element_type=F32)
        o = o2 / (pltpu.roll(o2, HEAD_DIM, axis=1) + jnp.exp2(sk - m))
        col0 = (2 * kvh + c) * LANES + hh * HEAD_DIM
        o_ref[0, rows, col0:col0 + HEAD_DIM] = o[:, hh * HEAD_DIM:(hh + 1) * HEAD_DIM].astype(BF16)

    units = [(kvh, j, c, hh) for j in range(SW_BLOCKS_PER_STEP) for kvh in range(SW_KV_HEADS)
             for c in range(2) for hh in range(2)]
    n_slots = s_ref.shape[0]
    for idx in range(len(units) + SW_SKEW):
        if idx < len(units):
            scores(*units[idx], idx % n_slots)
        if idx >= SW_SKEW:
            finish(*units[idx - SW_SKEW], (idx - SW_SKEW) % n_slots)


def _t5_bucket(rel):
    half = T5_BUCKETS // 2
    max_exact = half // 2
    ret = (rel > 0).astype(np.int32) * half
    n = np.abs(rel)
    large = max_exact + (np.log(np.maximum(n, 1) / max_exact)
                         / np.log(T5_MAX_DIST / max_exact) * (half - max_exact)).astype(np.int32)
    large = np.minimum(large, half - 1)
    return ret + np.where(n < max_exact, n, large)


def _swa_bucket_index():
    rel = np.arange(3 * SW_BLOCK)[None, :] - SW_BLOCK - np.arange(SW_BLOCK)[:, None]
    return np.where(np.abs(rel) <= SW_WINDOW, _t5_bucket(rel), -1).astype(np.int32)


def _swa_attn(sink, t5_table, qb, kbd, vbd):
    b, t, _ = qb.shape
    tq = SW_BLOCKS_PER_STEP * SW_BLOCK
    n_tiles = t // tq
    last_blk = t // SW_BLOCK - 1
    qtile = pl.BlockSpec((1, tq, B_QW), lambda bi, ni: (bi, ni, 0))
    own = lambda w: pl.BlockSpec((1, tq, w), lambda bi, ni: (bi, ni, 0))
    prev = lambda w: pl.BlockSpec((1, SW_BLOCK, w),
                                  lambda bi, ni: (bi, jnp.maximum(ni * SW_BLOCKS_PER_STEP - 1, 0), 0))
    nxt = lambda w: pl.BlockSpec((1, SW_BLOCK, w),
                                 lambda bi, ni: (bi, jnp.minimum((ni + 1) * SW_BLOCKS_PER_STEP, last_blk), 0))
    kw, vw = kbd.shape[-1], vbd.shape[-1]
    kbd_t = jnp.swapaxes(kbd, 1, 2)
    own_t = pl.BlockSpec((1, kw, tq), lambda bi, ni: (bi, 0, ni))
    prev_t = pl.BlockSpec((1, kw, SW_BLOCK), lambda bi, ni: (bi, 0, jnp.maximum(ni * SW_BLOCKS_PER_STEP - 1, 0)))
    nxt_t = pl.BlockSpec((1, kw, SW_BLOCK),
                         lambda bi, ni: (bi, 0, jnp.minimum((ni + 1) * SW_BLOCKS_PER_STEP, last_blk)))
    smem = pl.BlockSpec(memory_space=pltpu.SMEM)
    table = jnp.pad(t5_table.astype(F32).T, ((0, 0), (0, LANES - t5_table.shape[0])))
    return pl.pallas_call(
        functools.partial(_swa_kernel, n_tiles=n_tiles),
        grid=(b, n_tiles),
        in_specs=[smem, _resident((SW_HEADS, LANES)), _resident((SW_BLOCK, 3 * SW_BLOCK)),
                  qtile, prev_t, own_t, nxt_t, prev(vw), own(vw), nxt(vw)],
        out_specs=qtile,
        out_shape=jax.ShapeDtypeStruct((b, t, B_QW), BF16),
        scratch_shapes=[pltpu.VMEM((SW_KV_HEADS, SW_GROUP * SW_BLOCK, 3 * SW_BLOCK), F32),
                        pltpu.VMEM((SW_SLOTS, SW_BLOCK, 3 * SW_BLOCK), F32)],
        compiler_params=_params(2),
        name="swa_attn",
    )(sink.astype(F32), table, jnp.asarray(_swa_bucket_index()),
      qb, kbd_t, kbd_t, kbd_t, vbd, vbd, vbd)


def _mix_kernel(x_ref, oa_ref, ob_ref, g_ref, wg_ref, wa_ref, wb_ref, wo_ref, wgate_f32, wup_f32, wdown_f32,
                x1_ref, wgate_bf, wup_bf, wdown_bf, h_scr, y_scr):
    _cast_rows(((wgate_f32, wgate_bf), (wup_f32, wup_bf), (wdown_f32, wdown_bf)))
    for rb in range(0, x_ref.shape[0], ROW_BLOCK):
        rows = slice(rb, rb + ROW_BLOCK)
        xf = x_ref[rows, :]
        h_scr[rows, :] = ((xf * _rms_scale(xf)) * g_ref[...]).astype(BF16)
        for c in range(0, D_MODEL, MXU_N):
            cs = slice(c, c + MXU_N)
            gs = slice(D_MODEL + c, D_MODEL + c + MXU_N)
            ga = jnp.dot(h_scr[rows, :], wg_ref[:, cs], preferred_element_type=F32)
            gb = jnp.dot(h_scr[rows, :], wg_ref[:, gs], preferred_element_type=F32)
            ya = jnp.dot(oa_ref[rows, :], wa_ref[:, cs], preferred_element_type=F32)
            yb = jnp.dot(ob_ref[rows, :], wb_ref[:, cs], preferred_element_type=F32)
            y_scr[rows, cs] = (jax.nn.sigmoid(ga) * ya + jax.nn.sigmoid(gb) * yb).astype(BF16)
        for c in range(0, D_MODEL, MXU_N):
            cs = slice(c, c + MXU_N)
            x1_ref[rows, cs] = x_ref[rows, cs] + jnp.dot(y_scr[rows, :], wo_ref[:, cs], preferred_element_type=F32)


def _mix_out(x2, oa, ob, g_mix, mix_weights, w_gate, w_up, w_down, layer):
    n = x2.shape[0]
    tm = TOK_TILE
    steps = n // tm
    tile = lambda w: pl.BlockSpec((tm, w), lambda i: (i, 0))
    streamed = (w_gate, w_up, w_down)
    outs = pl.pallas_call(
        _mix_kernel,
        grid=(steps,),
        in_specs=[tile(D_MODEL), tile(A_W), tile(B_QW), _resident((1, D_MODEL))]
                 + [_resident(w.shape) for w in mix_weights]
                 + [_row_stream(w.shape[1], w.shape[2], steps, layer) for w in streamed],
        out_specs=[tile(D_MODEL)]
                  + [pl.BlockSpec((w.shape[1] // steps, w.shape[2]), lambda i: (i, 0)) for w in streamed],
        out_shape=[jax.ShapeDtypeStruct((n, D_MODEL), F32)]
                  + [jax.ShapeDtypeStruct(w.shape[1:], BF16) for w in streamed],
        scratch_shapes=[pltpu.VMEM((tm, D_MODEL), BF16), pltpu.VMEM((tm, D_MODEL), BF16)],
        compiler_params=_params(1),
        name="mix_out",
    )(x2, oa, ob, g_mix, *mix_weights, w_gate, w_up, w_down)
    return outs[0], outs[1:]


def _ffn_kernel(x_ref, g_ref, wg_ref, wu_ref, wd_ref, o_ref, h_scr, a_scr):
    hidden = wg_ref.shape[1]
    for rb in range(0, x_ref.shape[0], ROW_BLOCK):
        rows = slice(rb, rb + ROW_BLOCK)
        xf = x_ref[rows, :]
        h_scr[rows, :] = ((xf * _rms_scale(xf)) * g_ref[...]).astype(BF16)
        for c in range(0, hidden, MXU_N):
            cs = slice(c, c + MXU_N)
            gate = jnp.dot(h_scr[rows, :], wg_ref[:, cs], preferred_element_type=F32)
            up = jnp.dot(h_scr[rows, :], wu_ref[:, cs], preferred_element_type=F32)
            a_scr[rows, cs] = (jax.nn.silu(gate) * up).astype(BF16)
        for c in range(0, D_MODEL, MXU_N):
            cs = slice(c, c + MXU_N)
            o_ref[rows, cs] = x_ref[rows, cs] + jnp.dot(a_scr[rows, :], wd_ref[:, cs], preferred_element_type=F32)


def _ffn(x1, g_ffn, w_gate, w_up, w_down):
    n = x1.shape[0]
    tm = TOK_TILE
    hidden = w_gate.shape[1]
    tile = pl.BlockSpec((tm, D_MODEL), lambda i: (i, 0))
    return pl.pallas_call(
        _ffn_kernel,
        grid=(n // tm,),
        in_specs=[tile, _resident((1, D_MODEL)), _resident(w_gate.shape), _resident(w_up.shape),
                  _resident(w_down.shape)],
        out_specs=tile,
        out_shape=jax.ShapeDtypeStruct((n, D_MODEL), F32),
        scratch_shapes=[pltpu.VMEM((tm, D_MODEL), BF16), pltpu.VMEM((tm, hidden), BF16)],
        compiler_params=_params(1),
        name="ffn",
    )(x1, g_ffn, w_gate, w_up, w_down)


def _layer(x2, b, t, layer, norm_mix, w_in, q_norm_a, k_norm_a, rpb_a, q_norm_b, k_norm_b, sink_b, t5_table,
           w_branch_a, w_branch_b, w_out, norm_ffn, w_gate, w_up, w_down):
    g_mix = norm_mix[layer].reshape(1, D_MODEL)
    reps = MXU_N // HEAD_DIM
    root_d = HEAD_DIM ** 0.5
    q_fold = QK_SCALE * root_d * LOG2E
    qk_gain = jnp.stack([jnp.tile(q_norm_a[layer] * q_fold, reps), jnp.tile(k_norm_a[layer] * root_d, reps),
                         jnp.tile(q_norm_b[layer] * q_fold, reps), jnp.tile(k_norm_b[layer] * root_d, reps)]
                        ).astype(F32)

    (qa, ka, va, qb, kbd, vbd), mix_weights = _in_proj(x2, g_mix, qk_gain, w_in, w_branch_a, w_branch_b,
                                                       w_out, layer)
    n = x2.shape[0]
    r3 = lambda a: a.reshape(b, t, a.shape[-1])
    oa = _na_attn(rpb_a[layer], r3(qa), r3(ka), r3(va))
    ob = _swa_attn(sink_b[layer], t5_table, r3(qb), r3(kbd), r3(vbd))
    x1, ffn_weights = _mix_out(x2, oa.reshape(n, A_W), ob.reshape(n, B_QW), g_mix, mix_weights,
                               w_gate, w_up, w_down, layer)
    return _ffn(x1, norm_ffn[layer].reshape(1, D_MODEL), *ffn_weights)


def kernel(x, norm_mix, w_in, q_norm_a, k_norm_a, rpb_a, q_norm_b, k_norm_b, sink_b, t5_table,
           w_branch_a, w_branch_b, w_out, norm_ffn, w_gate, w_up, w_down):
    b, t, d = x.shape
    x2 = x.reshape(b * t, d)
    for layer in range(norm_mix.shape[0]):
        x2 = _layer(x2, b, t, layer, norm_mix, w_in, q_norm_a, k_norm_a, rpb_a, q_norm_b, k_norm_b, sink_b,
                    t5_table, w_branch_a, w_branch_b, w_out, norm_ffn, w_gate, w_up, w_down)
    return x2.reshape(b, t, d)
```

```python
import functools

import jax
import jax.numpy as jnp
import numpy as np
from jax import lax
from jax.experimental import pallas as pl
from jax.experimental.pallas import tpu as pltpu

F32 = jnp.float32
BF16 = jnp.bfloat16

D_MODEL = 1024
HEAD_DIM = 64
GRID_W = 64
NA_HEADS = 8
NA_KH = 8
NA_KW = 16
SW_HEADS = 8
SW_KV_HEADS = 2
SW_GROUP = SW_HEADS // SW_KV_HEADS
SW_WINDOW = 128
SW_BLOCK = 128
T5_BUCKETS = 32
T5_MAX_DIST = 128
A_W = NA_HEADS * HEAD_DIM
B_QW = SW_HEADS * HEAD_DIM
B_KVW = SW_KV_HEADS * HEAD_DIM
QKV_W = 3 * A_W + B_QW + 2 * B_KVW
RMS_EPS = 1e-6
NEG_INF = -1e30
QK_SCALE = HEAD_DIM ** -0.5
LOG2E = 1.4426950408889634

LANES = 128
MXU_N = 256
VMEM_LIMIT = 56 * 1024 * 1024

TOK_TILE = 1024
ROW_BLOCK = 256
NA_ROWS_PER_STEP = 32
PROJ_SKEW = 4
NA_SKEW = 2
SW_SKEW = 4
SW_BLOCKS_PER_STEP = 16
NA_SLOTS = NA_SKEW + 1
NA_TABLE_HALF = 32
SW_SLOTS = SW_SKEW + 1


def _resident(shape):
    return pl.BlockSpec(shape, lambda *_: (0,) * len(shape), pipeline_mode=pl.Buffered(1))


def _params(n_axes):
    return pltpu.CompilerParams(dimension_semantics=("arbitrary",) * n_axes, vmem_limit_bytes=VMEM_LIMIT)


def _rms_scale(xf):
    return lax.rsqrt(jnp.mean(xf * xf, axis=-1, keepdims=True) + RMS_EPS)


def _group_rms_scale(p):
    y = p * p
    lo = lax.broadcasted_iota(jnp.int32, (p.shape[0], LANES), 1) < HEAD_DIM
    scales = []
    for c in range(0, p.shape[1], LANES):
        yc = y[:, c:c + LANES]
        s_lo = jnp.sum(jnp.where(lo, yc, 0.0), axis=-1, keepdims=True)
        s_hi = jnp.sum(jnp.where(lo, 0.0, yc), axis=-1, keepdims=True)
        r_lo = lax.rsqrt(s_lo + HEAD_DIM * RMS_EPS)
        r_hi = lax.rsqrt(s_hi + HEAD_DIM * RMS_EPS)
        scales.append(jnp.where(lo, r_lo, r_hi))
    return jnp.concatenate(scales, axis=1)


def _cast_rows(pairs):
    for src_ref, dst_ref in pairs:
        dst_ref[...] = src_ref[:, src_ref.shape[1] - dst_ref.shape[1]:].astype(BF16)


def _in_proj_kernel(x_ref, g_ref, w_ref, qk_gain_ref, wg_f32, wa_f32, wb_f32, wo_f32,
                    qa_ref, ka_ref, va_ref, qb_ref, kbd_ref, vbd_ref, wg_bf, wa_bf, wb_bf, wo_bf,
                    w_scr, h_scr, p_scr):
    @pl.when(pl.program_id(0) == 0)
    def _cast_own_weights():
        for c in range(0, QKV_W, MXU_N):
            w_scr[:, c:c + MXU_N] = w_ref[:, c:c + MXU_N].astype(BF16)

    _cast_rows(((wg_f32, wg_bf), (wa_f32, wa_bf), (wb_f32, wb_bf), (wo_f32, wo_bf)))

    def normed(out_ref, rows, c, gain_row):
        def epilogue(slot):
            p = p_scr[slot]
            y = p * _group_rms_scale(p)
            if gain_row is not None:
                y = y * qk_gain_ref[gain_row:gain_row + 1, :]
            out_ref[rows, c:c + MXU_N] = y.astype(BF16)
        return epilogue

    def plain(out_ref, rows, c):
        def epilogue(slot):
            out_ref[rows, c:c + MXU_N] = p_scr[slot].astype(BF16)
        return epilogue

    def kv_b(rows):
        def epilogue(slot):
            kb = p_scr[slot, :, :B_KVW]
            kb = kb * _group_rms_scale(kb) * qk_gain_ref[1:2, :LANES]
            vb = p_scr[slot, :, B_KVW:]
            lo = lax.broadcasted_iota(jnp.int32, kb.shape, 1) < HEAD_DIM
            k_swapped = pltpu.roll(kb, HEAD_DIM, axis=1)
            kbd_ref[rows, :LANES] = jnp.where(lo, kb, k_swapped).astype(BF16)
            kbd_ref[rows, LANES:] = jnp.where(lo, k_swapped, kb).astype(BF16)
            v_swapped = pltpu.roll(vb, HEAD_DIM, axis=1)
            for blk, (left, right) in enumerate(((vb, 1.0), (1.0, v_swapped), (v_swapped, 1.0), (1.0, vb))):
                vbd_ref[rows, blk * LANES:(blk + 1) * LANES] = jnp.where(lo, left, right).astype(BF16)
        return epilogue

    units = []
    for rb in range(0, x_ref.shape[0], ROW_BLOCK):
        rows = slice(rb, rb + ROW_BLOCK)
        xf = x_ref[rows, :]
        h_scr[rows, :] = ((xf * _rms_scale(xf)) * g_ref[...]).astype(BF16)
        for c in range(0, A_W, MXU_N):
            units += [(rows, c, normed(qa_ref, rows, c, None)), (rows, A_W + c, normed(ka_ref, rows, c, 0)),
                      (rows, 3 * A_W + c, normed(qb_ref, rows, c, None))]
        units.append((rows, 3 * A_W + B_QW, kv_b(rows)))
        units += [(rows, 2 * A_W + c, plain(va_ref, rows, c)) for c in range(0, A_W, MXU_N)]
    n_slots = PROJ_SKEW + 1
    for idx in range(len(units) + PROJ_SKEW):
        if idx < len(units):
            rows, c0, _ = units[idx]
            p_scr[idx % n_slots] = jnp.dot(h_scr[rows, :], w_scr[:, c0:c0 + MXU_N], preferred_element_type=F32)
        if idx >= PROJ_SKEW:
            units[idx - PROJ_SKEW][2]((idx - PROJ_SKEW) % n_slots)


def _row_stream(n_rows, n_cols, steps, layer):
    return pl.BlockSpec((None, n_rows // steps, n_cols), lambda i: (layer, i, 0))


def _in_proj(x2, g_mix, qk_gain, w_in, w_a, w_b, w_o, layer):
    n = x2.shape[0]
    tm = TOK_TILE
    steps = n // tm
    tile = lambda w: pl.BlockSpec((tm, w), lambda i: (i, 0))
    gate_w = w_in.shape[2] - QKV_W
    out_w = (A_W, A_W, A_W, B_QW, 2 * LANES, 4 * LANES)
    streamed = ((w_in, gate_w), (w_a, D_MODEL), (w_b, D_MODEL), (w_o, D_MODEL))
    outs = pl.pallas_call(
        _in_proj_kernel,
        grid=(steps,),
        in_specs=[tile(D_MODEL), _resident((1, D_MODEL)),
                  pl.BlockSpec((None, D_MODEL, QKV_W), lambda i: (layer, 0, 0), pipeline_mode=pl.Buffered(1)),
                  _resident((2, MXU_N))]
                 + [_row_stream(w.shape[1], w.shape[2], steps, layer) for w, _ in streamed],
        out_specs=[tile(w) for w in out_w]
                  + [pl.BlockSpec((w.shape[1] // steps, cols), lambda i: (i, 0)) for w, cols in streamed],
        out_shape=[jax.ShapeDtypeStruct((n, w), BF16) for w in out_w]
                  + [jax.ShapeDtypeStruct((w.shape[1], cols), BF16) for w, cols in streamed],
        scratch_shapes=[pltpu.VMEM((D_MODEL, QKV_W), BF16), pltpu.VMEM((tm, D_MODEL), BF16),
                        pltpu.VMEM((PROJ_SKEW + 1, ROW_BLOCK, MXU_N), F32)],
        compiler_params=_params(1),
        name="in_proj",
    )(x2, g_mix, w_in, qk_gain, w_in, w_a, w_b, w_o)
    return outs[:6], outs[6:]


def _na_kernel(rpb_ref, dc_ref, q_ref, k_ref, v_ref, o_ref, bias_ref, s_ref, *, rows):
    t = pl.program_id(1)
    lo = lax.broadcasted_iota(jnp.int32, (GRID_W, LANES), 1) < HEAD_DIM
    nt = (((1,), (1,)), ((), ()))
    n_slab = 2 * NA_KH - 2

    @pl.when((pl.program_id(0) == 0) & (t == 0))
    def _build_bias():
        offs = dc_ref[...]
        in_win = offs >= 0
        take = jnp.maximum(offs, 0)

        def build(idx, carry):
            h = idx // n_slab
            d = idx - h * n_slab
            row = jnp.broadcast_to(rpb_ref[pl.ds(d * NA_HEADS + h, 1), :] * LOG2E, (GRID_W, LANES))
            slab = jnp.where(in_win, jnp.take_along_axis(row, take, axis=1), NEG_INF)
            row0 = pl.multiple_of((h % 2) * GRID_W, GRID_W)
            bias_ref[h // 2, d, pl.ds(row0, GRID_W), :] = slab
            return carry

        lax.fori_loop(0, NA_HEADS * n_slab, build, 0, unroll=NA_HEADS)

    win0 = _na_window_start(t, rows)
    row_info = []
    for i in range(NA_ROWS_PER_STEP):
        r = t * NA_ROWS_PER_STEP + i
        rs = jnp.clip(r - NA_KH // 2, 0, rows - NA_KH)
        row_info.append(((NA_KH - 1) - (r - rs), pl.multiple_of((rs - win0) * GRID_W, GRID_W)))

    def scores(i, p, slot):
        d0, kstart = row_info[i]
        cols = slice(p * LANES, (p + 1) * LANES)
        q = q_ref[0, i * GRID_W:(i + 1) * GRID_W, cols]
        zero = jnp.zeros_like(q)
        q2 = jnp.concatenate([jnp.where(lo, q, zero), jnp.where(lo, zero, q)], axis=0)
        kk = k_ref[0, pl.ds(kstart, NA_KH * GRID_W), cols]
        s = lax.dot_general(q2, kk, nt, preferred_element_type=F32)
        for v in range(NA_KH // 2):
            vcols = slice(v * LANES, (v + 1) * LANES)
            s_ref[slot, :, vcols] = s[:, vcols] + bias_ref[p, d0 + 2 * v]

    ones_cols = jnp.ones((NA_KH * GRID_W, LANES), BF16)

    def finish(i, p, slot):
        _, kstart = row_info[i]
        cols = slice(p * LANES, (p + 1) * LANES)
        m = jnp.max(s_ref[slot], axis=-1, keepdims=True)
        e = jnp.exp2(s_ref[slot] - m).astype(BF16)
        vv = jnp.concatenate([v_ref[0, pl.ds(kstart, NA_KH * GRID_W), cols], ones_cols], axis=1)
        o2 = jnp.dot(e, vv, preferred_element_type=F32)
        num = jnp.where(lo, o2[:GRID_W, :LANES], o2[GRID_W:, :LANES])
        den = jnp.where(lo, o2[:GRID_W, LANES:], o2[GRID_W:, LANES:])
        o_ref[0, i * GRID_W:(i + 1) * GRID_W, cols] = (num / den).astype(BF16)

    units = [(i, p) for i in range(NA_ROWS_PER_STEP) for p in range(NA_HEADS // 2)]
    n_slots = s_ref.shape[0]
    for idx in range(len(units) + NA_SKEW):
        if idx < len(units):
            scores(*units[idx], idx % n_slots)
        if idx >= NA_SKEW:
            finish(*units[idx - NA_SKEW], (idx - NA_SKEW) % n_slots)


def _na_window_start(step, rows):
    return jnp.clip(step * NA_ROWS_PER_STEP - NA_KH // 2, 0, rows - (NA_ROWS_PER_STEP + NA_KH))


def _na_col_offsets():
    c = np.arange(GRID_W)
    kc = c[None, :]
    win_start = np.clip(c - NA_KW // 2, 0, GRID_W - NA_KW)[:, None]
    in_win = (kc >= win_start) & (kc < win_start + NA_KW)
    dc = kc - c[:, None] + (NA_KW - 1)
    return np.concatenate([np.where(in_win, dc, -1), np.where(in_win, dc + NA_TABLE_HALF, -1)],
                          axis=1).astype(np.int32)


def _na_attn(rpb, qa, ka, va):
    b, t, _ = qa.shape
    rows = t // GRID_W
    tq = NA_ROWS_PER_STEP * GRID_W
    tile = pl.BlockSpec((1, tq, A_W), lambda bi, ti: (bi, ti, 0))
    window = pl.BlockSpec((pl.Element(1), pl.Element((NA_ROWS_PER_STEP + NA_KH) * GRID_W), pl.Element(A_W)),
                          lambda bi, ti: (bi, _na_window_start(ti, rows) * GRID_W, 0))
    per_head = jnp.transpose(rpb.astype(F32), (0, 2, 1))
    pad = lambda a, w: jnp.pad(a, ((0, 0), (0, 0), (0, w - a.shape[-1])))
    table = jnp.concatenate([pad(per_head[:-1], NA_TABLE_HALF), pad(per_head[1:], LANES - NA_TABLE_HALF)], axis=-1)
    table = table.reshape(-1, LANES)
    return pl.pallas_call(
        functools.partial(_na_kernel, rows=rows),
        grid=(b, rows // NA_ROWS_PER_STEP),
        in_specs=[_resident(table.shape), _resident((GRID_W, LANES)), tile, window, window],
        out_specs=tile,
        out_shape=jax.ShapeDtypeStruct((b, t, A_W), BF16),
        scratch_shapes=[pltpu.VMEM((NA_HEADS // 2, 2 * NA_KH - 2, LANES, LANES), F32),
                        pltpu.VMEM((NA_SLOTS, 2 * GRID_W, NA_KH * GRID_W), F32)],
        compiler_params=_params(2),
        name="na_attn",
    )(table, jnp.asarray(_na_col_offsets()), qa, ka, va)


def _swa_kernel(sink_ref, t5_ref, bucket_ref, q_ref, kp_ref, kc_ref, kn_ref, vp_ref, vc_ref, vn_ref, o_ref,
                bias_ref, s_ref, *, n_tiles):
    n = pl.program_id(1)
    lo = lax.broadcasted_iota(jnp.int32, (SW_BLOCK, LANES), 1) < HEAD_DIM
    nt = (((1,), (1,)), ((), ()))

    @pl.when((pl.program_id(0) == 0) & (n == 0))
    def _build_bias():
        bucket = bucket_ref[...]
        in_band = bucket >= 0
        take = jnp.maximum(bucket, 0)

        def build(h, carry):
            row = jnp.broadcast_to(t5_ref[pl.ds(h, 1), :] * LOG2E, (SW_BLOCK, LANES))
            vals = [jnp.take_along_axis(row, take[:, c:c + LANES], axis=1) for c in range(0, 3 * SW_BLOCK, LANES)]
            row0 = pl.multiple_of((h % SW_GROUP) * SW_BLOCK, SW_BLOCK)
            bias_ref[h // SW_GROUP, pl.ds(row0, SW_BLOCK), :] = jnp.where(
                in_band, jnp.concatenate(vals, axis=1), NEG_INF)
            return carry

        lax.fori_loop(0, SW_HEADS, build, 0)

    def window(prev_ref, own_ref, next_ref, blk_col, j):
        kcols = slice(blk_col * LANES, (blk_col + 1) * LANES)
        blk = lambda b: own_ref[0, b * SW_BLOCK:(b + 1) * SW_BLOCK, kcols]
        first = prev_ref[0, :, kcols] if j == 0 else blk(j - 1)
        last = next_ref[0, :, kcols] if j == SW_BLOCKS_PER_STEP - 1 else blk(j + 1)
        return jnp.concatenate([first, blk(j), last], axis=0)

    def scores(kvh, j, c, hh, slot):
        rows = slice(j * SW_BLOCK, (j + 1) * SW_BLOCK)
        qb = q_ref[0, rows, (2 * kvh + c) * LANES:(2 * kvh + c + 1) * LANES]
        q1 = jnp.where(lo if hh == 0 else ~lo, qb, jnp.zeros_like(qb))
        s_ref[slot] = lax.dot_general(q1, window(kp_ref, kc_ref, kn_ref, kvh, j), nt,
                                      preferred_element_type=F32)

    def finish(kvh, j, c, hh, slot):
        rows = slice(j * SW_BLOCK, (j + 1) * SW_BLOCK)
        head = kvh * SW_GROUP + 2 * c + hh
        brows = slice((2 * c + hh) * SW_BLOCK, (2 * c + hh + 1) * SW_BLOCK)
        sg = s_ref[slot] + bias_ref[kvh, brows, :]
        if j == 0:
            sg = jnp.concatenate([jnp.where(n > 0, sg[:, :SW_BLOCK], NEG_INF), sg[:, SW_BLOCK:]], axis=1)
        if j == SW_BLOCKS_PER_STEP - 1:
            sg = jnp.concatenate([sg[:, :2 * SW_BLOCK],
                                  jnp.where(n < n_tiles - 1, sg[:, 2 * SW_BLOCK:], NEG_INF)], axis=1)
        sk = sink_ref[head] * LOG2E
        m = jnp.maximum(jnp.max(sg, axis=-1, keepdims=True), sk)
        e = jnp.exp2(sg - m).astype(BF16)
        o2 = jnp.dot(e, window(vp_ref, vc_ref, vn_ref, 2 * kvh + hh, j), preferred_element_type=F32)
        o = o2 / (pltpu.roll(o2, HEAD_DIM, axis=1) + jnp.exp2(sk - m))
        col0 = (2 * kvh + c) * LANES + hh * HEAD_DIM
        o_ref[0, rows, col0:col0 + HEAD_DIM] = o[:, hh * HEAD_DIM:(hh + 1) * HEAD_DIM].astype(BF16)

    units = [(kvh, j, c, hh) for j in range(SW_BLOCKS_PER_STEP) for kvh in range(SW_KV_HEADS)
             for c in range(2) for hh in range(2)]
    n_slots = s_ref.shape[0]
    for idx in range(len(units) + SW_SKEW):
        if idx < len(units):
            scores(*units[idx], idx % n_slots)
        if idx >= SW_SKEW:
            finish(*units[idx - SW_SKEW], (idx - SW_SKEW) % n_slots)


def _t5_bucket(rel):
    half = T5_BUCKETS // 2
    max_exact = half // 2
    ret = (rel > 0).astype(np.int32) * half
    n = np.abs(rel)
    large = max_exact + (np.log(np.maximum(n, 1) / max_exact)
                         / np.log(T5_MAX_DIST / max_exact) * (half - max_exact)).astype(np.int32)
    large = np.minimum(large, half - 1)
    return ret + np.where(n < max_exact, n, large)


def _swa_bucket_index():
    rel = np.arange(3 * SW_BLOCK)[None, :] - SW_BLOCK - np.arange(SW_BLOCK)[:, None]
    return np.where(np.abs(rel) <= SW_WINDOW, _t5_bucket(rel), -1).astype(np.int32)


def _swa_attn(sink, t5_table, qb, kbd, vbd):
    b, t, _ = qb.shape
    tq = SW_BLOCKS_PER_STEP * SW_BLOCK
    n_tiles = t // tq
    last_blk = t // SW_BLOCK - 1
    qtile = pl.BlockSpec((1, tq, B_QW), lambda bi, ni: (bi, ni, 0))
    own = lambda w: pl.BlockSpec((1, tq, w), lambda bi, ni: (bi, ni, 0))
    prev = lambda w: pl.BlockSpec((1, SW_BLOCK, w),
                                  lambda bi, ni: (bi, jnp.maximum(ni * SW_BLOCKS_PER_STEP - 1, 0), 0))
    nxt = lambda w: pl.BlockSpec((1, SW_BLOCK, w),
                                 lambda bi, ni: (bi, jnp.minimum((ni + 1) * SW_BLOCKS_PER_STEP, last_blk), 0))
    kw, vw = kbd.shape[-1], vbd.shape[-1]
    smem = pl.BlockSpec(memory_space=pltpu.SMEM)
    table = jnp.pad(t5_table.astype(F32).T, ((0, 0), (0, LANES - t5_table.shape[0])))
    return pl.pallas_call(
        functools.partial(_swa_kernel, n_tiles=n_tiles),
        grid=(b, n_tiles),
        in_specs=[smem, _resident((SW_HEADS, LANES)), _resident((SW_BLOCK, 3 * SW_BLOCK)),
                  qtile, prev(kw), own(kw), nxt(kw), prev(vw), own(vw), nxt(vw)],
        out_specs=qtile,
        out_shape=jax.ShapeDtypeStruct((b, t, B_QW), BF16),
        scratch_shapes=[pltpu.VMEM((SW_KV_HEADS, SW_GROUP * SW_BLOCK, 3 * SW_BLOCK), F32),
                        pltpu.VMEM((SW_SLOTS, SW_BLOCK, 3 * SW_BLOCK), F32)],
        compiler_params=_params(2),
        name="swa_attn",
    )(sink.astype(F32), table, jnp.asarray(_swa_bucket_index()),
      qb, kbd, kbd, kbd, vbd, vbd, vbd)


def _mix_kernel(x_ref, oa_ref, ob_ref, g_ref, wg_ref, wa_ref, wb_ref, wo_ref, wgate_f32, wup_f32, wdown_f32,
                x1_ref, wgate_bf, wup_bf, wdown_bf, h_scr, y_scr):
    _cast_rows(((wgate_f32, wgate_bf), (wup_f32, wup_bf), (wdown_f32, wdown_bf)))
    for rb in range(0, x_ref.shape[0], ROW_BLOCK):
        rows = slice(rb, rb + ROW_BLOCK)
        xf = x_ref[rows, :]
        h_scr[rows, :] = ((xf * _rms_scale(xf)) * g_ref[...]).astype(BF16)
        for c in range(0, D_MODEL, MXU_N):
            cs = slice(c, c + MXU_N)
            gs = slice(D_MODEL + c, D_MODEL + c + MXU_N)
            ga = jnp.dot(h_scr[rows, :], wg_ref[:, cs], preferred_element_type=F32)
            gb = jnp.dot(h_scr[rows, :], wg_ref[:, gs], preferred_element_type=F32)
            ya = jnp.dot(oa_ref[rows, :], wa_ref[:, cs], preferred_element_type=F32)
            yb = jnp.dot(ob_ref[rows, :], wb_ref[:, cs], preferred_element_type=F32)
            y_scr[rows, cs] = (jax.nn.sigmoid(ga) * ya + jax.nn.sigmoid(gb) * yb).astype(BF16)
        for c in range(0, D_MODEL, MXU_N):
            cs = slice(c, c + MXU_N)
            x1_ref[rows, cs] = x_ref[rows, cs] + jnp.dot(y_scr[rows, :], wo_ref[:, cs], preferred_element_type=F32)


def _mix_out(x2, oa, ob, g_mix, mix_weights, w_gate, w_up, w_down, layer):
    n = x2.shape[0]
    tm = TOK_TILE
    steps = n // tm
    tile = lambda w: pl.BlockSpec((tm, w), lambda i: (i, 0))
    streamed = (w_gate, w_up, w_down)
    outs = pl.pallas_call(
        _mix_kernel,
        grid=(steps,),
        in_specs=[tile(D_MODEL), tile(A_W), tile(B_QW), _resident((1, D_MODEL))]
                 + [_resident(w.shape) for w in mix_weights]
                 + [_row_stream(w.shape[1], w.shape[2], steps, layer) for w in streamed],
        out_specs=[tile(D_MODEL)]
                  + [pl.BlockSpec((w.shape[1] // steps, w.shape[2]), lambda i: (i, 0)) for w in streamed],
        out_shape=[jax.ShapeDtypeStruct((n, D_MODEL), F32)]
                  + [jax.ShapeDtypeStruct(w.shape[1:], BF16) for w in streamed],
        scratch_shapes=[pltpu.VMEM((tm, D_MODEL), BF16), pltpu.VMEM((tm, D_MODEL), BF16)],
        compiler_params=_params(1),
        name="mix_out",
    )(x2, oa, ob, g_mix, *mix_weights, w_gate, w_up, w_down)
    return outs[0], outs[1:]


def _ffn_kernel(x_ref, g_ref, wg_ref, wu_ref, wd_ref, o_ref, h_scr, a_scr):
    hidden = wg_ref.shape[1]
    for rb in range(0, x_ref.shape[0], ROW_BLOCK):
        rows = slice(rb, rb + ROW_BLOCK)
        xf = x_ref[rows, :]
        h_scr[rows, :] = ((xf * _rms_scale(xf)) * g_ref[...]).astype(BF16)
        for c in range(0, hidden, MXU_N):
            cs = slice(c, c + MXU_N)
            gate = jnp.dot(h_scr[rows, :], wg_ref[:, cs], preferred_element_type=F32)
            up = jnp.dot(h_scr[rows, :], wu_ref[:, cs], preferred_element_type=F32)
            a_scr[rows, cs] = (jax.nn.silu(gate) * up).astype(BF16)
        for c in range(0, D_MODEL, MXU_N):
            cs = slice(c, c + MXU_N)
            o_ref[rows, cs] = x_ref[rows, cs] + jnp.dot(a_scr[rows, :], wd_ref[:, cs], preferred_element_type=F32)


def _ffn(x1, g_ffn, w_gate, w_up, w_down):
    n = x1.shape[0]
    tm = TOK_TILE
    hidden = w_gate.shape[1]
    tile = pl.BlockSpec((tm, D_MODEL), lambda i: (i, 0))
    return pl.pallas_call(
        _ffn_kernel,
        grid=(n // tm,),
        in_specs=[tile, _resident((1, D_MODEL)), _resident(w_gate.shape), _resident(w_up.shape),
                  _resident(w_down.shape)],
        out_specs=tile,
        out_shape=jax.ShapeDtypeStruct((n, D_MODEL), F32),
        scratch_shapes=[pltpu.VMEM((tm, D_MODEL), BF16), pltpu.VMEM((tm, hidden), BF16)],
        compiler_params=_params(1),
        name="ffn",
    )(x1, g_ffn, w_gate, w_up, w_down)


def _layer(x2, b, t, layer, norm_mix, w_in, q_norm_a, k_norm_a, rpb_a, q_norm_b, k_norm_b, sink_b, t5_table,
           w_branch_a, w_branch_b, w_out, norm_ffn, w_gate, w_up, w_down):
    g_mix = norm_mix[layer].reshape(1, D_MODEL)
    reps = MXU_N // HEAD_DIM
    fold = HEAD_DIM * QK_SCALE * LOG2E
    qk_gain = jnp.stack([jnp.tile(q_norm_a[layer] * k_norm_a[layer] * fold, reps),
                         jnp.tile(q_norm_b[layer] * k_norm_b[layer] * fold, reps)]).astype(F32)

    (qa, ka, va, qb, kbd, vbd), mix_weights = _in_proj(x2, g_mix, qk_gain, w_in, w_branch_a, w_branch_b,
                                                       w_out, layer)
    n = x2.shape[0]
    r3 = lambda a: a.reshape(b, t, a.shape[-1])
    oa = _na_attn(rpb_a[layer], r3(qa), r3(ka), r3(va))
    ob = _swa_attn(sink_b[layer], t5_table, r3(qb), r3(kbd), r3(vbd))
    x1, ffn_weights = _mix_out(x2, oa.reshape(n, A_W), ob.reshape(n, B_QW), g_mix, mix_weights,
                               w_gate, w_up, w_down, layer)
    return _ffn(x1, norm_ffn[layer].reshape(1, D_MODEL), *ffn_weights)


def kernel(x, norm_mix, w_in, q_norm_a, k_norm_a, rpb_a, q_norm_b, k_norm_b, sink_b, t5_table,
           w_branch_a, w_branch_b, w_out, norm_ffn, w_gate, w_up, w_down):
    b, t, d = x.shape
    x2 = x.reshape(b * t, d)
    for layer in range(norm_mix.shape[0]):
        x2 = _layer(x2, b, t, layer, norm_mix, w_in, q_norm_a, k_norm_a, rpb_a, q_norm_b, k_norm_b, sink_b,
                    t5_table, w_branch_a, w_branch_b, w_out, norm_ffn, w_gate, w_up, w_down)
    return x2.reshape(b, t, d)
```

```python
import functools

import jax
import jax.numpy as jnp
import numpy as np
from jax import lax
from jax.experimental import pallas as pl
from jax.experimental.pallas import tpu as pltpu

F32 = jnp.float32
BF16 = jnp.bfloat16

D_MODEL = 1024
HEAD_DIM = 64
GRID_W = 64
NA_HEADS = 8
NA_KH = 8
NA_KW = 16
SW_HEADS = 8
SW_KV_HEADS = 2
SW_GROUP = SW_HEADS // SW_KV_HEADS
SW_WINDOW = 128
SW_BLOCK = 128
T5_BUCKETS = 32
T5_MAX_DIST = 128
A_W = NA_HEADS * HEAD_DIM
B_QW = SW_HEADS * HEAD_DIM
B_KVW = SW_KV_HEADS * HEAD_DIM
QKV_W = 3 * A_W + B_QW + 2 * B_KVW
RMS_EPS = 1e-6
NEG_INF = -1e30
QK_SCALE = HEAD_DIM ** -0.5
LOG2E = 1.4426950408889634

LANES = 128
MXU_N = 256
VMEM_LIMIT = 56 * 1024 * 1024

TOK_TILE = 1024
ROW_BLOCK = 256
NA_ROWS_PER_STEP = 32
PROJ_SKEW = 4
NA_SKEW = 2
SW_SKEW = 4
SW_BLOCKS_PER_STEP = 32
NA_SLOTS = NA_SKEW + 1
NA_TABLE_HALF = 32
SW_SLOTS = SW_SKEW + 1


def _resident(shape):
    return pl.BlockSpec(shape, lambda *_: (0,) * len(shape), pipeline_mode=pl.Buffered(1))


def _params(n_axes):
    return pltpu.CompilerParams(dimension_semantics=("arbitrary",) * n_axes, vmem_limit_bytes=VMEM_LIMIT)


def _rms_scale(xf):
    return lax.rsqrt(jnp.mean(xf * xf, axis=-1, keepdims=True) + RMS_EPS)


def _group_rms_scale(p):
    y = p * p
    lo = lax.broadcasted_iota(jnp.int32, (p.shape[0], LANES), 1) < HEAD_DIM
    scales = []
    for c in range(0, p.shape[1], LANES):
        yc = y[:, c:c + LANES]
        s_lo = jnp.sum(jnp.where(lo, yc, 0.0), axis=-1, keepdims=True)
        s_hi = jnp.sum(jnp.where(lo, 0.0, yc), axis=-1, keepdims=True)
        r_lo = lax.rsqrt(s_lo + HEAD_DIM * RMS_EPS)
        r_hi = lax.rsqrt(s_hi + HEAD_DIM * RMS_EPS)
        scales.append(jnp.where(lo, r_lo, r_hi))
    return jnp.concatenate(scales, axis=1)


def _cast_rows(pairs):
    for src_ref, dst_ref in pairs:
        dst_ref[...] = src_ref[:, src_ref.shape[1] - dst_ref.shape[1]:].astype(BF16)


def _in_proj_kernel(x_ref, g_ref, w_ref, qk_gain_ref, wg_f32, wa_f32, wb_f32, wo_f32,
                    qa_ref, ka_ref, va_ref, qb_ref, kbd_ref, vbd_ref, wg_bf, wa_bf, wb_bf, wo_bf,
                    w_scr, h_scr, p_scr):
    @pl.when(pl.program_id(0) == 0)
    def _cast_own_weights():
        for c in range(0, QKV_W, MXU_N):
            w_scr[:, c:c + MXU_N] = w_ref[:, c:c + MXU_N].astype(BF16)

    _cast_rows(((wg_f32, wg_bf), (wa_f32, wa_bf), (wb_f32, wb_bf), (wo_f32, wo_bf)))

    def normed(out_ref, rows, c, gain_row):
        def epilogue(slot):
            p = p_scr[slot]
            y = p * _group_rms_scale(p)
            if gain_row is not None:
                y = y * qk_gain_ref[gain_row:gain_row + 1, :]
            out_ref[rows, c:c + MXU_N] = y.astype(BF16)
        return epilogue

    def plain(out_ref, rows, c):
        def epilogue(slot):
            out_ref[rows, c:c + MXU_N] = p_scr[slot].astype(BF16)
        return epilogue

    def kv_b(rows):
        def epilogue(slot):
            kb = p_scr[slot, :, :B_KVW]
            kb = kb * _group_rms_scale(kb) * qk_gain_ref[1:2, :LANES]
            vb = p_scr[slot, :, B_KVW:]
            lo = lax.broadcasted_iota(jnp.int32, kb.shape, 1) < HEAD_DIM
            k_swapped = pltpu.roll(kb, HEAD_DIM, axis=1)
            kbd_ref[rows, :LANES] = jnp.where(lo, kb, k_swapped).astype(BF16)
            kbd_ref[rows, LANES:] = jnp.where(lo, k_swapped, kb).astype(BF16)
            v_swapped = pltpu.roll(vb, HEAD_DIM, axis=1)
            for blk, (left, right) in enumerate(((vb, 1.0), (1.0, v_swapped), (v_swapped, 1.0), (1.0, vb))):
                vbd_ref[rows, blk * LANES:(blk + 1) * LANES] = jnp.where(lo, left, right).astype(BF16)
        return epilogue

    units = []
    for rb in range(0, x_ref.shape[0], ROW_BLOCK):
        rows = slice(rb, rb + ROW_BLOCK)
        xf = x_ref[rows, :]
        h_scr[rows, :] = ((xf * _rms_scale(xf)) * g_ref[...]).astype(BF16)
        for c in range(0, A_W, MXU_N):
            units += [(rows, c, normed(qa_ref, rows, c, None)), (rows, A_W + c, normed(ka_ref, rows, c, 0)),
                      (rows, 3 * A_W + c, normed(qb_ref, rows, c, None))]
        units.append((rows, 3 * A_W + B_QW, kv_b(rows)))
        units += [(rows, 2 * A_W + c, plain(va_ref, rows, c)) for c in range(0, A_W, MXU_N)]
    n_slots = PROJ_SKEW + 1
    for idx in range(len(units) + PROJ_SKEW):
        if idx < len(units):
            rows, c0, _ = units[idx]
            p_scr[idx % n_slots] = jnp.dot(h_scr[rows, :], w_scr[:, c0:c0 + MXU_N], preferred_element_type=F32)
        if idx >= PROJ_SKEW:
            units[idx - PROJ_SKEW][2]((idx - PROJ_SKEW) % n_slots)


def _row_stream(n_rows, n_cols, steps, layer):
    return pl.BlockSpec((None, n_rows // steps, n_cols), lambda i: (layer, i, 0))


def _in_proj(x2, g_mix, qk_gain, w_in, w_a, w_b, w_o, layer):
    n = x2.shape[0]
    tm = TOK_TILE
    steps = n // tm
    tile = lambda w: pl.BlockSpec((tm, w), lambda i: (i, 0))
    gate_w = w_in.shape[2] - QKV_W
    out_w = (A_W, A_W, A_W, B_QW, 2 * LANES, 4 * LANES)
    streamed = ((w_in, gate_w), (w_a, D_MODEL), (w_b, D_MODEL), (w_o, D_MODEL))
    outs = pl.pallas_call(
        _in_proj_kernel,
        grid=(steps,),
        in_specs=[tile(D_MODEL), _resident((1, D_MODEL)),
                  pl.BlockSpec((None, D_MODEL, QKV_W), lambda i: (layer, 0, 0), pipeline_mode=pl.Buffered(1)),
                  _resident((2, MXU_N))]
                 + [_row_stream(w.shape[1], w.shape[2], steps, layer) for w, _ in streamed],
        out_specs=[tile(w) for w in out_w]
                  + [pl.BlockSpec((w.shape[1] // steps, cols), lambda i: (i, 0)) for w, cols in streamed],
        out_shape=[jax.ShapeDtypeStruct((n, w), BF16) for w in out_w]
                  + [jax.ShapeDtypeStruct((w.shape[1], cols), BF16) for w, cols in streamed],
        scratch_shapes=[pltpu.VMEM((D_MODEL, QKV_W), BF16), pltpu.VMEM((tm, D_MODEL), BF16),
                        pltpu.VMEM((PROJ_SKEW + 1, ROW_BLOCK, MXU_N), F32)],
        compiler_params=_params(1),
        name="in_proj",
    )(x2, g_mix, w_in, qk_gain, w_in, w_a, w_b, w_o)
    return outs[:6], outs[6:]


def _na_kernel(rpb_ref, dc_ref, q_ref, k_ref, v_ref, o_ref, bias_ref, s_ref, *, rows):
    t = pl.program_id(1)
    lo = lax.broadcasted_iota(jnp.int32, (GRID_W, LANES), 1) < HEAD_DIM
    nt = (((1,), (1,)), ((), ()))
    n_slab = 2 * NA_KH - 2

    @pl.when((pl.program_id(0) == 0) & (t == 0))
    def _build_bias():
        offs = dc_ref[...]
        in_win = offs >= 0
        take = jnp.maximum(offs, 0)

        def build(idx, carry):
            h = idx // n_slab
            d = idx - h * n_slab
            row = jnp.broadcast_to(rpb_ref[pl.ds(d * NA_HEADS + h, 1), :] * LOG2E, (GRID_W, LANES))
            slab = jnp.where(in_win, jnp.take_along_axis(row, take, axis=1), NEG_INF)
            row0 = pl.multiple_of((h % 2) * GRID_W, GRID_W)
            bias_ref[h // 2, d, pl.ds(row0, GRID_W), :] = slab
            return carry

        lax.fori_loop(0, NA_HEADS * n_slab, build, 0, unroll=NA_HEADS)

    win0 = _na_window_start(t, rows)
    row_info = []
    for i in range(NA_ROWS_PER_STEP):
        r = t * NA_ROWS_PER_STEP + i
        rs = jnp.clip(r - NA_KH // 2, 0, rows - NA_KH)
        row_info.append(((NA_KH - 1) - (r - rs), pl.multiple_of((rs - win0) * GRID_W, GRID_W)))

    def scores(i, p, slot):
        d0, kstart = row_info[i]
        cols = slice(p * LANES, (p + 1) * LANES)
        q = q_ref[0, i * GRID_W:(i + 1) * GRID_W, cols]
        zero = jnp.zeros_like(q)
        q2 = jnp.concatenate([jnp.where(lo, q, zero), jnp.where(lo, zero, q)], axis=0)
        kk = k_ref[0, pl.ds(kstart, NA_KH * GRID_W), cols]
        s = lax.dot_general(q2, kk, nt, preferred_element_type=F32)
        for v in range(NA_KH // 2):
            vcols = slice(v * LANES, (v + 1) * LANES)
            s_ref[slot, :, vcols] = s[:, vcols] + bias_ref[p, d0 + 2 * v]

    ones_cols = jnp.ones((NA_KH * GRID_W, LANES), BF16)

    def finish(i, p, slot):
        _, kstart = row_info[i]
        cols = slice(p * LANES, (p + 1) * LANES)
        m = jnp.max(s_ref[slot], axis=-1, keepdims=True)
        e = jnp.exp2(s_ref[slot] - m).astype(BF16)
        vv = jnp.concatenate([v_ref[0, pl.ds(kstart, NA_KH * GRID_W), cols], ones_cols], axis=1)
        o2 = jnp.dot(e, vv, preferred_element_type=F32)
        num = jnp.where(lo, o2[:GRID_W, :LANES], o2[GRID_W:, :LANES])
        den = jnp.where(lo, o2[:GRID_W, LANES:], o2[GRID_W:, LANES:])
        o_ref[0, i * GRID_W:(i + 1) * GRID_W, cols] = (num / den).astype(BF16)

    units = [(i, p) for i in range(NA_ROWS_PER_STEP) for p in range(NA_HEADS // 2)]
    n_slots = s_ref.shape[0]
    for idx in range(len(units) + NA_SKEW):
        if idx < len(units):
            scores(*units[idx], idx % n_slots)
        if idx >= NA_SKEW:
            finish(*units[idx - NA_SKEW], (idx - NA_SKEW) % n_slots)


def _na_window_start(step, rows):
    return jnp.clip(step * NA_ROWS_PER_STEP - NA_KH // 2, 0, rows - (NA_ROWS_PER_STEP + NA_KH))


def _na_col_offsets():
    c = np.arange(GRID_W)
    kc = c[None, :]
    win_start = np.clip(c - NA_KW // 2, 0, GRID_W - NA_KW)[:, None]
    in_win = (kc >= win_start) & (kc < win_start + NA_KW)
    dc = kc - c[:, None] + (NA_KW - 1)
    return np.concatenate([np.where(in_win, dc, -1), np.where(in_win, dc + NA_TABLE_HALF, -1)],
                          axis=1).astype(np.int32)


def _na_attn(rpb, qa, ka, va):
    b, t, _ = qa.shape
    rows = t // GRID_W
    tq = NA_ROWS_PER_STEP * GRID_W
    tile = pl.BlockSpec((1, tq, A_W), lambda bi, ti: (bi, ti, 0))
    window = pl.BlockSpec((pl.Element(1), pl.Element((NA_ROWS_PER_STEP + NA_KH) * GRID_W), pl.Element(A_W)),
                          lambda bi, ti: (bi, _na_window_start(ti, rows) * GRID_W, 0))
    per_head = jnp.transpose(rpb.astype(F32), (0, 2, 1))
    pad = lambda a, w: jnp.pad(a, ((0, 0), (0, 0), (0, w - a.shape[-1])))
    table = jnp.concatenate([pad(per_head[:-1], NA_TABLE_HALF), pad(per_head[1:], LANES - NA_TABLE_HALF)], axis=-1)
    table = table.reshape(-1, LANES)
    return pl.pallas_call(
        functools.partial(_na_kernel, rows=rows),
        grid=(b, rows // NA_ROWS_PER_STEP),
        in_specs=[_resident(table.shape), _resident((GRID_W, LANES)), tile, window, window],
        out_specs=tile,
        out_shape=jax.ShapeDtypeStruct((b, t, A_W), BF16),
        scratch_shapes=[pltpu.VMEM((NA_HEADS // 2, 2 * NA_KH - 2, LANES, LANES), F32),
                        pltpu.VMEM((NA_SLOTS, 2 * GRID_W, NA_KH * GRID_W), F32)],
        compiler_params=_params(2),
        name="na_attn",
    )(table, jnp.asarray(_na_col_offsets()), qa, ka, va)


def _swa_kernel(sink_ref, t5_ref, bucket_ref, q_ref, kp_ref, kc_ref, kn_ref, vp_ref, vc_ref, vn_ref, o_ref,
                bias_ref, s_ref, *, n_tiles):
    n = pl.program_id(1)
    lo = lax.broadcasted_iota(jnp.int32, (SW_BLOCK, LANES), 1) < HEAD_DIM
    nt = (((1,), (1,)), ((), ()))

    @pl.when((pl.program_id(0) == 0) & (n == 0))
    def _build_bias():
        bucket = bucket_ref[...]
        in_band = bucket >= 0
        take = jnp.maximum(bucket, 0)

        def build(h, carry):
            row = jnp.broadcast_to(t5_ref[pl.ds(h, 1), :] * LOG2E, (SW_BLOCK, LANES))
            vals = [jnp.take_along_axis(row, take[:, c:c + LANES], axis=1) for c in range(0, 3 * SW_BLOCK, LANES)]
            row0 = pl.multiple_of((h % SW_GROUP) * SW_BLOCK, SW_BLOCK)
            bias_ref[h // SW_GROUP, pl.ds(row0, SW_BLOCK), :] = jnp.where(
                in_band, jnp.concatenate(vals, axis=1), NEG_INF)
            return carry

        lax.fori_loop(0, SW_HEADS, build, 0)

    def window(prev_ref, own_ref, next_ref, blk_col, j):
        kcols = slice(blk_col * LANES, (blk_col + 1) * LANES)
        blk = lambda b: own_ref[0, b * SW_BLOCK:(b + 1) * SW_BLOCK, kcols]
        first = prev_ref[0, :, kcols] if j == 0 else blk(j - 1)
        last = next_ref[0, :, kcols] if j == SW_BLOCKS_PER_STEP - 1 else blk(j + 1)
        return jnp.concatenate([first, blk(j), last], axis=0)

    def scores(kvh, j, c, hh, slot):
        rows = slice(j * SW_BLOCK, (j + 1) * SW_BLOCK)
        qb = q_ref[0, rows, (2 * kvh + c) * LANES:(2 * kvh + c + 1) * LANES]
        q1 = jnp.where(lo if hh == 0 else ~lo, qb, jnp.zeros_like(qb))
        s_ref[slot] = lax.dot_general(q1, window(kp_ref, kc_ref, kn_ref, kvh, j), nt,
                                      preferred_element_type=F32)

    def finish(kvh, j, c, hh, slot):
        rows = slice(j * SW_BLOCK, (j + 1) * SW_BLOCK)
        head = kvh * SW_GROUP + 2 * c + hh
        brows = slice((2 * c + hh) * SW_BLOCK, (2 * c + hh + 1) * SW_BLOCK)
        sg = s_ref[slot] + bias_ref[kvh, brows, :]
        if j == 0:
            sg = jnp.concatenate([jnp.where(n > 0, sg[:, :SW_BLOCK], NEG_INF), sg[:, SW_BLOCK:]], axis=1)
        if j == SW_BLOCKS_PER_STEP - 1:
            sg = jnp.concatenate([sg[:, :2 * SW_BLOCK],
                                  jnp.where(n < n_tiles - 1, sg[:, 2 * SW_BLOCK:], NEG_INF)], axis=1)
        sk = sink_ref[head] * LOG2E
        m = jnp.maximum(jnp.max(sg, axis=-1, keepdims=True), sk)
        e = jnp.exp2(sg - m).astype(BF16)
        o2 = jnp.dot(e, window(vp_ref, vc_ref, vn_ref, 2 * kvh + hh, j), preferred_element_type=F32)
        o = o2 / (pltpu.roll(o2, HEAD_DIM, axis=1) + jnp.exp2(sk - m))
        col0 = (2 * kvh + c) * LANES + hh * HEAD_DIM
        o_ref[0, rows, col0:col0 + HEAD_DIM] = o[:, hh * HEAD_DIM:(hh + 1) * HEAD_DIM].astype(BF16)

    units = [(kvh, j, c, hh) for j in range(SW_BLOCKS_PER_STEP) for kvh in range(SW_KV_HEADS)
             for c in range(2) for hh in range(2)]
    n_slots = s_ref.shape[0]
    for idx in range(len(units) + SW_SKEW):
        if idx < len(units):
            scores(*units[idx], idx % n_slots)
        if idx >= SW_SKEW:
            finish(*units[idx - SW_SKEW], (idx - SW_SKEW) % n_slots)


def _t5_bucket(rel):
    half = T5_BUCKETS // 2
    max_exact = half // 2
    ret = (rel > 0).astype(np.int32) * half
    n = np.abs(rel)
    large = max_exact + (np.log(np.maximum(n, 1) / max_exact)
                         / np.log(T5_MAX_DIST / max_exact) * (half - max_exact)).astype(np.int32)
    large = np.minimum(large, half - 1)
    return ret + np.where(n < max_exact, n, large)


def _swa_bucket_index():
    rel = np.arange(3 * SW_BLOCK)[None, :] - SW_BLOCK - np.arange(SW_BLOCK)[:, None]
    return np.where(np.abs(rel) <= SW_WINDOW, _t5_bucket(rel), -1).astype(np.int32)


def _swa_attn(sink, t5_table, qb, kbd, vbd):
    b, t, _ = qb.shape
    tq = SW_BLOCKS_PER_STEP * SW_BLOCK
    n_tiles = t // tq
    last_blk = t // SW_BLOCK - 1
    qtile = pl.BlockSpec((1, tq, B_QW), lambda bi, ni: (bi, ni, 0))
    own = lambda w: pl.BlockSpec((1, tq, w), lambda bi, ni: (bi, ni, 0))
    prev = lambda w: pl.BlockSpec((1, SW_BLOCK, w),
                                  lambda bi, ni: (bi, jnp.maximum(ni * SW_BLOCKS_PER_STEP - 1, 0), 0))
    nxt = lambda w: pl.BlockSpec((1, SW_BLOCK, w),
                                 lambda bi, ni: (bi, jnp.minimum((ni + 1) * SW_BLOCKS_PER_STEP, last_blk), 0))
    kw, vw = kbd.shape[-1], vbd.shape[-1]
    smem = pl.BlockSpec(memory_space=pltpu.SMEM)
    table = jnp.pad(t5_table.astype(F32).T, ((0, 0), (0, LANES - t5_table.shape[0])))
    return pl.pallas_call(
        functools.partial(_swa_kernel, n_tiles=n_tiles),
        grid=(b, n_tiles),
        in_specs=[smem, _resident((SW_HEADS, LANES)), _resident((SW_BLOCK, 3 * SW_BLOCK)),
                  qtile, prev(kw), own(kw), nxt(kw), prev(vw), own(vw), nxt(vw)],
        out_specs=qtile,
        out_shape=jax.ShapeDtypeStruct((b, t, B_QW), BF16),
        scratch_shapes=[pltpu.VMEM((SW_KV_HEADS, SW_GROUP * SW_BLOCK, 3 * SW_BLOCK), F32),
                        pltpu.VMEM((SW_SLOTS, SW_BLOCK, 3 * SW_BLOCK), F32)],
        compiler_params=_params(2),
        name="swa_attn",
    )(sink.astype(F32), table, jnp.asarray(_swa_bucket_index()),
      qb, kbd, kbd, kbd, vbd, vbd, vbd)


def _mix_kernel(x_ref, oa_ref, ob_ref, g_ref, wg_ref, wa_ref, wb_ref, wo_ref, wgate_f32, wup_f32, wdown_f32,
                x1_ref, wgate_bf, wup_bf, wdown_bf, h_scr, y_scr):
    _cast_rows(((wgate_f32, wgate_bf), (wup_f32, wup_bf), (wdown_f32, wdown_bf)))
    for rb in range(0, x_ref.shape[0], ROW_BLOCK):
        rows = slice(rb, rb + ROW_BLOCK)
        xf = x_ref[rows, :]
        h_scr[rows, :] = ((xf * _rms_scale(xf)) * g_ref[...]).astype(BF16)
        for c in range(0, D_MODEL, MXU_N):
            cs = slice(c, c + MXU_N)
            gs = slice(D_MODEL + c, D_MODEL + c + MXU_N)
            ga = jnp.dot(h_scr[rows, :], wg_ref[:, cs], preferred_element_type=F32)
            gb = jnp.dot(h_scr[rows, :], wg_ref[:, gs], preferred_element_type=F32)
            ya = jnp.dot(oa_ref[rows, :], wa_ref[:, cs], preferred_element_type=F32)
            yb = jnp.dot(ob_ref[rows, :], wb_ref[:, cs], preferred_element_type=F32)
            y_scr[rows, cs] = (jax.nn.sigmoid(ga) * ya + jax.nn.sigmoid(gb) * yb).astype(BF16)
        for c in range(0, D_MODEL, MXU_N):
            cs = slice(c, c + MXU_N)
            x1_ref[rows, cs] = x_ref[rows, cs] + jnp.dot(y_scr[rows, :], wo_ref[:, cs], preferred_element_type=F32)


def _mix_out(x2, oa, ob, g_mix, mix_weights, w_gate, w_up, w_down, layer):
    n = x2.shape[0]
    tm = TOK_TILE
    steps = n // tm
    tile = lambda w: pl.BlockSpec((tm, w), lambda i: (i, 0))
    streamed = (w_gate, w_up, w_down)
    outs = pl.pallas_call(
        _mix_kernel,
        grid=(steps,),
        in_specs=[tile(D_MODEL), tile(A_W), tile(B_QW), _resident((1, D_MODEL))]
                 + [_resident(w.shape) for w in mix_weights]
                 + [_row_stream(w.shape[1], w.shape[2], steps, layer) for w in streamed],
        out_specs=[tile(D_MODEL)]
                  + [pl.BlockSpec((w.shape[1] // steps, w.shape[2]), lambda i: (i, 0)) for w in streamed],
        out_shape=[jax.ShapeDtypeStruct((n, D_MODEL), F32)]
                  + [jax.ShapeDtypeStruct(w.shape[1:], BF16) for w in streamed],
        scratch_shapes=[pltpu.VMEM((tm, D_MODEL), BF16), pltpu.VMEM((tm, D_MODEL), BF16)],
        compiler_params=_params(1),
        name="mix_out",
    )(x2, oa, ob, g_mix, *mix_weights, w_gate, w_up, w_down)
    return outs[0], outs[1:]


def _ffn_kernel(x_ref, g_ref, wg_ref, wu_ref, wd_ref, o_ref, h_scr, a_scr):
    hidden = wg_ref.shape[1]
    for rb in range(0, x_ref.shape[0], ROW_BLOCK):
        rows = slice(rb, rb + ROW_BLOCK)
        xf = x_ref[rows, :]
        h_scr[rows, :] = ((xf * _rms_scale(xf)) * g_ref[...]).astype(BF16)
        for c in range(0, hidden, MXU_N):
            cs = slice(c, c + MXU_N)
            gate = jnp.dot(h_scr[rows, :], wg_ref[:, cs], preferred_element_type=F32)
            up = jnp.dot(h_scr[rows, :], wu_ref[:, cs], preferred_element_type=F32)
            a_scr[rows, cs] = (jax.nn.silu(gate) * up).astype(BF16)
        for c in range(0, D_MODEL, MXU_N):
            cs = slice(c, c + MXU_N)
            o_ref[rows, cs] = x_ref[rows, cs] + jnp.dot(a_scr[rows, :], wd_ref[:, cs], preferred_element_type=F32)


def _ffn(x1, g_ffn, w_gate, w_up, w_down):
    n = x1.shape[0]
    tm = TOK_TILE
    hidden = w_gate.shape[1]
    tile = pl.BlockSpec((tm, D_MODEL), lambda i: (i, 0))
    return pl.pallas_call(
        _ffn_kernel,
        grid=(n // tm,),
        in_specs=[tile, _resident((1, D_MODEL)), _resident(w_gate.shape), _resident(w_up.shape),
                  _resident(w_down.shape)],
        out_specs=tile,
        out_shape=jax.ShapeDtypeStruct((n, D_MODEL), F32),
        scratch_shapes=[pltpu.VMEM((tm, D_MODEL), BF16), pltpu.VMEM((tm, hidden), BF16)],
        compiler_params=_params(1),
        name="ffn",
    )(x1, g_ffn, w_gate, w_up, w_down)


def _layer(x2, b, t, layer, norm_mix, w_in, q_norm_a, k_norm_a, rpb_a, q_norm_b, k_norm_b, sink_b, t5_table,
           w_branch_a, w_branch_b, w_out, norm_ffn, w_gate, w_up, w_down):
    g_mix = norm_mix[layer].reshape(1, D_MODEL)
    reps = MXU_N // HEAD_DIM
    fold = HEAD_DIM * QK_SCALE * LOG2E
    qk_gain = jnp.stack([jnp.tile(q_norm_a[layer] * k_norm_a[layer] * fold, reps),
                         jnp.tile(q_norm_b[layer] * k_norm_b[layer] * fold, reps)]).astype(F32)

    (qa, ka, va, qb, kbd, vbd), mix_weights = _in_proj(x2, g_mix, qk_gain, w_in, w_branch_a, w_branch_b,
                                                       w_out, layer)
    n = x2.shape[0]
    r3 = lambda a: a.reshape(b, t, a.shape[-1])
    oa = _na_attn(rpb_a[layer], r3(qa), r3(ka), r3(va))
    ob = _swa_attn(sink_b[layer], t5_table, r3(qb), r3(kbd), r3(vbd))
    x1, ffn_weights = _mix_out(x2, oa.reshape(n, A_W), ob.reshape(n, B_QW), g_mix, mix_weights,
                               w_gate, w_up, w_down, layer)
    return _ffn(x1, norm_ffn[layer].reshape(1, D_MODEL), *ffn_weights)


def kernel(x, norm_mix, w_in, q_norm_a, k_norm_a, rpb_a, q_norm_b, k_norm_b, sink_b, t5_table,
           w_branch_a, w_branch_b, w_out, norm_ffn, w_gate, w_up, w_down):
    b, t, d = x.shape
    x2 = x.reshape(b * t, d)
    for layer in range(norm_mix.shape[0]):
        x2 = _layer(x2, b, t, layer, norm_mix, w_in, q_norm_a, k_norm_a, rpb_a, q_norm_b, k_norm_b, sink_b,
                    t5_table, w_branch_a, w_branch_b, w_out, norm_ffn, w_gate, w_up, w_down)
    return x2.reshape(b, t, d)
```

```python
import functools

import jax
import jax.numpy as jnp
import numpy as np
from jax import lax
from jax.experimental import pallas as pl
from jax.experimental.pallas import tpu as pltpu

F32 = jnp.float32
BF16 = jnp.bfloat16

D_MODEL = 1024
HEAD_DIM = 64
GRID_W = 64
NA_HEADS = 8
NA_KH = 8
NA_KW = 16
SW_HEADS = 8
SW_KV_HEADS = 2
SW_GROUP = SW_HEADS // SW_KV_HEADS
SW_WINDOW = 128
SW_BLOCK = 128
T5_BUCKETS = 32
T5_MAX_DIST = 128
A_W = NA_HEADS * HEAD_DIM
B_QW = SW_HEADS * HEAD_DIM
B_KVW = SW_KV_HEADS * HEAD_DIM
QKV_W = 3 * A_W + B_QW + 2 * B_KVW
RMS_EPS = 1e-6
NEG_INF = -1e30
QK_SCALE = HEAD_DIM ** -0.5
LOG2E = 1.4426950408889634

LANES = 128
MXU_N = 256
VMEM_LIMIT = 56 * 1024 * 1024

TOK_TILE = 1024
ROW_BLOCK = 256
NA_ROWS_PER_STEP = 32
PROJ_SKEW = 4
NA_SKEW = 2
SW_SKEW = 4
SW_BLOCKS_PER_STEP = 16
NA_SLOTS = NA_SKEW + 1
NA_TABLE_HALF = 32
PT_NA = 0
PT_T5 = (2 * NA_KH - 2) * NA_HEADS
PT_GAIN = PT_T5 + SW_HEADS
PT_ROWS = 128
SW_SLOTS = SW_SKEW + 1


def _resident(shape):
    return pl.BlockSpec(shape, lambda *_: (0,) * len(shape), pipeline_mode=pl.Buffered(1))


def _params(n_axes):
    return pltpu.CompilerParams(dimension_semantics=("arbitrary",) * n_axes, vmem_limit_bytes=VMEM_LIMIT)


def _rms_scale(xf):
    return lax.rsqrt(jnp.mean(xf * xf, axis=-1, keepdims=True) + RMS_EPS)


def _group_rms_scale(p):
    y = p * p
    lo = lax.broadcasted_iota(jnp.int32, (p.shape[0], LANES), 1) < HEAD_DIM
    scales = []
    for c in range(0, p.shape[1], LANES):
        yc = y[:, c:c + LANES]
        s_lo = jnp.sum(jnp.where(lo, yc, 0.0), axis=-1, keepdims=True)
        s_hi = jnp.sum(jnp.where(lo, 0.0, yc), axis=-1, keepdims=True)
        r_lo = lax.rsqrt(s_lo + HEAD_DIM * RMS_EPS)
        r_hi = lax.rsqrt(s_hi + HEAD_DIM * RMS_EPS)
        scales.append(jnp.where(lo, r_lo, r_hi))
    return jnp.concatenate(scales, axis=1)


def _cast_rows(pairs):
    for src_ref, dst_ref in pairs:
        dst_ref[...] = src_ref[:, src_ref.shape[1] - dst_ref.shape[1]:].astype(BF16)


def _in_proj_kernel(x_ref, g_ref, w_ref, ptab_ref, wg_f32, wa_f32, wb_f32, wo_f32,
                    qa_ref, ka_ref, va_ref, qb_ref, kbd_ref, vbd_ref, wg_bf, wa_bf, wb_bf, wo_bf,
                    w_scr, h_scr, p_scr):
    @pl.when(pl.program_id(0) == 0)
    def _cast_own_weights():
        for c in range(0, QKV_W, MXU_N):
            w_scr[:, c:c + MXU_N] = w_ref[:, c:c + MXU_N].astype(BF16)

    _cast_rows(((wg_f32, wg_bf), (wa_f32, wa_bf), (wb_f32, wb_bf), (wo_f32, wo_bf)))

    def normed(out_ref, rows, c, gain_row):
        def epilogue(slot):
            p = p_scr[slot]
            y = p * _group_rms_scale(p)
            if gain_row is not None:
                gain = ptab_ref[PT_GAIN + gain_row:PT_GAIN + gain_row + 1, :]
                y = y * jnp.concatenate([gain] * (MXU_N // LANES), axis=1)
            out_ref[rows, c:c + MXU_N] = y.astype(BF16)
        return epilogue

    def plain(out_ref, rows, c):
        def epilogue(slot):
            out_ref[rows, c:c + MXU_N] = p_scr[slot].astype(BF16)
        return epilogue

    def kv_b(rows):
        def epilogue(slot):
            kb = p_scr[slot, :, :B_KVW]
            kb = kb * _group_rms_scale(kb) * ptab_ref[PT_GAIN + 1:PT_GAIN + 2, :]
            vb = p_scr[slot, :, B_KVW:]
            lo = lax.broadcasted_iota(jnp.int32, kb.shape, 1) < HEAD_DIM
            k_swapped = pltpu.roll(kb, HEAD_DIM, axis=1)
            kbd_ref[rows, :LANES] = jnp.where(lo, kb, k_swapped).astype(BF16)
            kbd_ref[rows, LANES:] = jnp.where(lo, k_swapped, kb).astype(BF16)
            v_swapped = pltpu.roll(vb, HEAD_DIM, axis=1)
            for blk, (left, right) in enumerate(((vb, 1.0), (1.0, v_swapped), (v_swapped, 1.0), (1.0, vb))):
                vbd_ref[rows, blk * LANES:(blk + 1) * LANES] = jnp.where(lo, left, right).astype(BF16)
        return epilogue

    units = []
    for rb in range(0, x_ref.shape[0], ROW_BLOCK):
        rows = slice(rb, rb + ROW_BLOCK)
        xf = x_ref[rows, :]
        h_scr[rows, :] = ((xf * _rms_scale(xf)) * g_ref[...]).astype(BF16)
        for c in range(0, A_W, MXU_N):
            units += [(rows, c, normed(qa_ref, rows, c, None)), (rows, A_W + c, normed(ka_ref, rows, c, 0)),
                      (rows, 3 * A_W + c, normed(qb_ref, rows, c, None))]
        units.append((rows, 3 * A_W + B_QW, kv_b(rows)))
        units += [(rows, 2 * A_W + c, plain(va_ref, rows, c)) for c in range(0, A_W, MXU_N)]
    n_slots = PROJ_SKEW + 1
    for idx in range(len(units) + PROJ_SKEW):
        if idx < len(units):
            rows, c0, _ = units[idx]
            p_scr[idx % n_slots] = jnp.dot(h_scr[rows, :], w_scr[:, c0:c0 + MXU_N], preferred_element_type=F32)
        if idx >= PROJ_SKEW:
            units[idx - PROJ_SKEW][2]((idx - PROJ_SKEW) % n_slots)


def _row_stream(n_rows, n_cols, steps, layer):
    return pl.BlockSpec((None, n_rows // steps, n_cols), lambda i: (layer, i, 0))


def _in_proj(x2, g_mix, ptab, w_in, w_a, w_b, w_o, layer):
    n = x2.shape[0]
    tm = TOK_TILE
    steps = n // tm
    tile = lambda w: pl.BlockSpec((tm, w), lambda i: (i, 0))
    gate_w = w_in.shape[2] - QKV_W
    out_w = (A_W, A_W, A_W, B_QW, 2 * LANES, 4 * LANES)
    streamed = ((w_in, gate_w), (w_a, D_MODEL), (w_b, D_MODEL), (w_o, D_MODEL))
    outs = pl.pallas_call(
        _in_proj_kernel,
        grid=(steps,),
        in_specs=[tile(D_MODEL), _resident((1, D_MODEL)),
                  pl.BlockSpec((None, D_MODEL, QKV_W), lambda i: (layer, 0, 0), pipeline_mode=pl.Buffered(1)),
                  _resident(ptab.shape)]
                 + [_row_stream(w.shape[1], w.shape[2], steps, layer) for w, _ in streamed],
        out_specs=[tile(w) for w in out_w]
                  + [pl.BlockSpec((w.shape[1] // steps, cols), lambda i: (i, 0)) for w, cols in streamed],
        out_shape=[jax.ShapeDtypeStruct((n, w), BF16) for w in out_w]
                  + [jax.ShapeDtypeStruct((w.shape[1], cols), BF16) for w, cols in streamed],
        scratch_shapes=[pltpu.VMEM((D_MODEL, QKV_W), BF16), pltpu.VMEM((tm, D_MODEL), BF16),
                        pltpu.VMEM((PROJ_SKEW + 1, ROW_BLOCK, MXU_N), F32)],
        compiler_params=_params(1),
        name="in_proj",
    )(x2, g_mix, w_in, ptab, w_in, w_a, w_b, w_o)
    return outs[:6], outs[6:]


def _na_kernel(rpb_ref, dc_ref, q_ref, k_ref, v_ref, o_ref, bias_ref, s_ref, *, rows):
    t = pl.program_id(1)
    lo = lax.broadcasted_iota(jnp.int32, (GRID_W, LANES), 1) < HEAD_DIM
    nt = (((1,), (1,)), ((), ()))
    n_slab = 2 * NA_KH - 2

    @pl.when((pl.program_id(0) == 0) & (t == 0))
    def _build_bias():
        offs = dc_ref[...]
        in_win = offs >= 0
        take = jnp.maximum(offs, 0)

        def build(idx, carry):
            h = idx // n_slab
            d = idx - h * n_slab
            row = jnp.broadcast_to(rpb_ref[pl.ds(PT_NA + d * NA_HEADS + h, 1), :] * LOG2E, (GRID_W, LANES))
            slab = jnp.where(in_win, jnp.take_along_axis(row, take, axis=1), NEG_INF)
            row0 = pl.multiple_of((h % 2) * GRID_W, GRID_W)
            bias_ref[h // 2, d, pl.ds(row0, GRID_W), :] = slab
            return carry

        lax.fori_loop(0, NA_HEADS * n_slab, build, 0, unroll=NA_HEADS)

    win0 = _na_window_start(t, rows)
    row_info = []
    for i in range(NA_ROWS_PER_STEP):
        r = t * NA_ROWS_PER_STEP + i
        rs = jnp.clip(r - NA_KH // 2, 0, rows - NA_KH)
        row_info.append(((NA_KH - 1) - (r - rs), pl.multiple_of((rs - win0) * GRID_W, GRID_W)))

    def scores(i, p, slot):
        d0, kstart = row_info[i]
        cols = slice(p * LANES, (p + 1) * LANES)
        q = q_ref[0, i * GRID_W:(i + 1) * GRID_W, cols]
        zero = jnp.zeros_like(q)
        q2 = jnp.concatenate([jnp.where(lo, q, zero), jnp.where(lo, zero, q)], axis=0)
        kk = k_ref[0, pl.ds(kstart, NA_KH * GRID_W), cols]
        s = lax.dot_general(q2, kk, nt, preferred_element_type=F32)
        for v in range(NA_KH // 2):
            vcols = slice(v * LANES, (v + 1) * LANES)
            s_ref[slot, :, vcols] = s[:, vcols] + bias_ref[p, d0 + 2 * v]

    ones_cols = jnp.ones((NA_KH * GRID_W, LANES), BF16)

    def finish(i, p, slot):
        _, kstart = row_info[i]
        cols = slice(p * LANES, (p + 1) * LANES)
        m = jnp.max(s_ref[slot], axis=-1, keepdims=True)
        e = jnp.exp2(s_ref[slot] - m).astype(BF16)
        vv = jnp.concatenate([v_ref[0, pl.ds(kstart, NA_KH * GRID_W), cols], ones_cols], axis=1)
        o2 = jnp.dot(e, vv, preferred_element_type=F32)
        num = jnp.where(lo, o2[:GRID_W, :LANES], o2[GRID_W:, :LANES])
        den = jnp.where(lo, o2[:GRID_W, LANES:], o2[GRID_W:, LANES:])
        o_ref[0, i * GRID_W:(i + 1) * GRID_W, cols] = (num / den).astype(BF16)

    units = [(i, p) for i in range(NA_ROWS_PER_STEP) for p in range(NA_HEADS // 2)]
    n_slots = s_ref.shape[0]
    for idx in range(len(units) + NA_SKEW):
        if idx < len(units):
            scores(*units[idx], idx % n_slots)
        if idx >= NA_SKEW:
            finish(*units[idx - NA_SKEW], (idx - NA_SKEW) % n_slots)


def _na_window_start(step, rows):
    return jnp.clip(step * NA_ROWS_PER_STEP - NA_KH // 2, 0, rows - (NA_ROWS_PER_STEP + NA_KH))


def _na_col_offsets():
    c = np.arange(GRID_W)
    kc = c[None, :]
    win_start = np.clip(c - NA_KW // 2, 0, GRID_W - NA_KW)[:, None]
    in_win = (kc >= win_start) & (kc < win_start + NA_KW)
    dc = kc - c[:, None] + (NA_KW - 1)
    return np.concatenate([np.where(in_win, dc, -1), np.where(in_win, dc + NA_TABLE_HALF, -1)],
                          axis=1).astype(np.int32)


def _na_attn(ptab, qa, ka, va):
    b, t, _ = qa.shape
    rows = t // GRID_W
    tq = NA_ROWS_PER_STEP * GRID_W
    tile = pl.BlockSpec((1, tq, A_W), lambda bi, ti: (bi, ti, 0))
    window = pl.BlockSpec((pl.Element(1), pl.Element((NA_ROWS_PER_STEP + NA_KH) * GRID_W), pl.Element(A_W)),
                          lambda bi, ti: (bi, _na_window_start(ti, rows) * GRID_W, 0))
    return pl.pallas_call(
        functools.partial(_na_kernel, rows=rows),
        grid=(b, rows // NA_ROWS_PER_STEP),
        in_specs=[_resident(ptab.shape), _resident((GRID_W, LANES)), tile, window, window],
        out_specs=tile,
        out_shape=jax.ShapeDtypeStruct((b, t, A_W), BF16),
        scratch_shapes=[pltpu.VMEM((NA_HEADS // 2, 2 * NA_KH - 2, LANES, LANES), F32),
                        pltpu.VMEM((NA_SLOTS, 2 * GRID_W, NA_KH * GRID_W), F32)],
        compiler_params=_params(2),
        name="na_attn",
    )(ptab, jnp.asarray(_na_col_offsets()), qa, ka, va)


def _swa_kernel(sink_ref, t5_ref, bucket_ref, q_ref, kp_ref, kc_ref, kn_ref, vp_ref, vc_ref, vn_ref, o_ref,
                bias_ref, s_ref, *, n_tiles):
    n = pl.program_id(1)
    lo = lax.broadcasted_iota(jnp.int32, (SW_BLOCK, LANES), 1) < HEAD_DIM
    nt = (((1,), (1,)), ((), ()))

    @pl.when((pl.program_id(0) == 0) & (n == 0))
    def _build_bias():
        bucket = bucket_ref[...]
        in_band = bucket >= 0
        take = jnp.maximum(bucket, 0)

        def build(h, carry):
            row = jnp.broadcast_to(t5_ref[pl.ds(PT_T5 + h, 1), :] * LOG2E, (SW_BLOCK, LANES))
            vals = [jnp.take_along_axis(row, take[:, c:c + LANES], axis=1) for c in range(0, 3 * SW_BLOCK, LANES)]
            row0 = pl.multiple_of((h % SW_GROUP) * SW_BLOCK, SW_BLOCK)
            bias_ref[h // SW_GROUP, pl.ds(row0, SW_BLOCK), :] = jnp.where(
                in_band, jnp.concatenate(vals, axis=1), NEG_INF)
            return carry

        lax.fori_loop(0, SW_HEADS, build, 0)

    def window(prev_ref, own_ref, next_ref, blk_col, j):
        kcols = slice(blk_col * LANES, (blk_col + 1) * LANES)
        blk = lambda b: own_ref[0, b * SW_BLOCK:(b + 1) * SW_BLOCK, kcols]
        first = prev_ref[0, :, kcols] if j == 0 else blk(j - 1)
        last = next_ref[0, :, kcols] if j == SW_BLOCKS_PER_STEP - 1 else blk(j + 1)
        return jnp.concatenate([first, blk(j), last], axis=0)

    def scores(kvh, j, c, hh, slot):
        rows = slice(j * SW_BLOCK, (j + 1) * SW_BLOCK)
        qb = q_ref[0, rows, (2 * kvh + c) * LANES:(2 * kvh + c + 1) * LANES]
        q1 = jnp.where(lo if hh == 0 else ~lo, qb, jnp.zeros_like(qb))
        s_ref[slot] = lax.dot_general(q1, window(kp_ref, kc_ref, kn_ref, kvh, j), nt,
                                      preferred_element_type=F32)

    def finish(kvh, j, c, hh, slot):
        rows = slice(j * SW_BLOCK, (j + 1) * SW_BLOCK)
        head = kvh * SW_GROUP + 2 * c + hh
        brows = slice((2 * c + hh) * SW_BLOCK, (2 * c + hh + 1) * SW_BLOCK)
        sg = s_ref[slot] + bias_ref[kvh, brows, :]
        if j == 0:
            sg = jnp.concatenate([jnp.where(n > 0, sg[:, :SW_BLOCK], NEG_INF), sg[:, SW_BLOCK:]], axis=1)
        if j == SW_BLOCKS_PER_STEP - 1:
            sg = jnp.concatenate([sg[:, :2 * SW_BLOCK],
                                  jnp.where(n < n_tiles - 1, sg[:, 2 * SW_BLOCK:], NEG_INF)], axis=1)
        sk = sink_ref[head] * LOG2E
        m = jnp.maximum(jnp.max(sg, axis=-1, keepdims=True), sk)
        e = jnp.exp2(sg - m).astype(BF16)
        o2 = jnp.dot(e, window(vp_ref, vc_ref, vn_ref, 2 * kvh + hh, j), preferred_element_type=F32)
        o = o2 / (pltpu.roll(o2, HEAD_DIM, axis=1) + jnp.exp2(sk - m))
        col0 = (2 * kvh + c) * LANES + hh * HEAD_DIM
        o_ref[0, rows, col0:col0 + HEAD_DIM] = o[:, hh * HEAD_DIM:(hh + 1) * HEAD_DIM].astype(BF16)

    units = [(kvh, j, c, hh) for j in range(SW_BLOCKS_PER_STEP) for kvh in range(SW_KV_HEADS)
             for c in range(2) for hh in range(2)]
    n_slots = s_ref.shape[0]
    for idx in range(len(units) + SW_SKEW):
        if idx < len(units):
            scores(*units[idx], idx % n_slots)
        if idx >= SW_SKEW:
            finish(*units[idx - SW_SKEW], (idx - SW_SKEW) % n_slots)


def _t5_bucket(rel):
    half = T5_BUCKETS // 2
    max_exact = half // 2
    ret = (rel > 0).astype(np.int32) * half
    n = np.abs(rel)
    large = max_exact + (np.log(np.maximum(n, 1) / max_exact)
                         / np.log(T5_MAX_DIST / max_exact) * (half - max_exact)).astype(np.int32)
    large = np.minimum(large, half - 1)
    return ret + np.where(n < max_exact, n, large)


def _swa_bucket_index():
    rel = np.arange(3 * SW_BLOCK)[None, :] - SW_BLOCK - np.arange(SW_BLOCK)[:, None]
    return np.where(np.abs(rel) <= SW_WINDOW, _t5_bucket(rel), -1).astype(np.int32)


def _swa_attn(sink, ptab, qb, kbd, vbd):
    b, t, _ = qb.shape
    tq = SW_BLOCKS_PER_STEP * SW_BLOCK
    n_tiles = t // tq
    last_blk = t // SW_BLOCK - 1
    qtile = pl.BlockSpec((1, tq, B_QW), lambda bi, ni: (bi, ni, 0))
    own = lambda w: pl.BlockSpec((1, tq, w), lambda bi, ni: (bi, ni, 0))
    prev = lambda w: pl.BlockSpec((1, SW_BLOCK, w),
                                  lambda bi, ni: (bi, jnp.maximum(ni * SW_BLOCKS_PER_STEP - 1, 0), 0))
    nxt = lambda w: pl.BlockSpec((1, SW_BLOCK, w),
                                 lambda bi, ni: (bi, jnp.minimum((ni + 1) * SW_BLOCKS_PER_STEP, last_blk), 0))
    kw, vw = kbd.shape[-1], vbd.shape[-1]
    smem = pl.BlockSpec(memory_space=pltpu.SMEM)
    return pl.pallas_call(
        functools.partial(_swa_kernel, n_tiles=n_tiles),
        grid=(b, n_tiles),
        in_specs=[smem, _resident(ptab.shape), _resident((SW_BLOCK, 3 * SW_BLOCK)),
                  qtile, prev(kw), own(kw), nxt(kw), prev(vw), own(vw), nxt(vw)],
        out_specs=qtile,
        out_shape=jax.ShapeDtypeStruct((b, t, B_QW), BF16),
        scratch_shapes=[pltpu.VMEM((SW_KV_HEADS, SW_GROUP * SW_BLOCK, 3 * SW_BLOCK), F32),
                        pltpu.VMEM((SW_SLOTS, SW_BLOCK, 3 * SW_BLOCK), F32)],
        compiler_params=_params(2),
        name="swa_attn",
    )(sink.astype(F32), ptab, jnp.asarray(_swa_bucket_index()),
      qb, kbd, kbd, kbd, vbd, vbd, vbd)


def _mix_kernel(x_ref, oa_ref, ob_ref, g_ref, wg_ref, wa_ref, wb_ref, wo_ref, wgate_f32, wup_f32, wdown_f32,
                x1_ref, wgate_bf, wup_bf, wdown_bf, h_scr, y_scr):
    _cast_rows(((wgate_f32, wgate_bf), (wup_f32, wup_bf), (wdown_f32, wdown_bf)))
    for rb in range(0, x_ref.shape[0], ROW_BLOCK):
        rows = slice(rb, rb + ROW_BLOCK)
        xf = x_ref[rows, :]
        h_scr[rows, :] = ((xf * _rms_scale(xf)) * g_ref[...]).astype(BF16)
        for c in range(0, D_MODEL, MXU_N):
            cs = slice(c, c + MXU_N)
            gs = slice(D_MODEL + c, D_MODEL + c + MXU_N)
            ga = jnp.dot(h_scr[rows, :], wg_ref[:, cs], preferred_element_type=F32)
            gb = jnp.dot(h_scr[rows, :], wg_ref[:, gs], preferred_element_type=F32)
            ya = jnp.dot(oa_ref[rows, :], wa_ref[:, cs], preferred_element_type=F32)
            yb = jnp.dot(ob_ref[rows, :], wb_ref[:, cs], preferred_element_type=F32)
            y_scr[rows, cs] = (jax.nn.sigmoid(ga) * ya + jax.nn.sigmoid(gb) * yb).astype(BF16)
        for c in range(0, D_MODEL, MXU_N):
            cs = slice(c, c + MXU_N)
            x1_ref[rows, cs] = x_ref[rows, cs] + jnp.dot(y_scr[rows, :], wo_ref[:, cs], preferred_element_type=F32)


def _mix_out(x2, oa, ob, g_mix, mix_weights, w_gate, w_up, w_down, layer):
    n = x2.shape[0]
    tm = TOK_TILE
    steps = n // tm
    tile = lambda w: pl.BlockSpec((tm, w), lambda i: (i, 0))
    streamed = (w_gate, w_up, w_down)
    outs = pl.pallas_call(
        _mix_kernel,
        grid=(steps,),
        in_specs=[tile(D_MODEL), tile(A_W), tile(B_QW), _resident((1, D_MODEL))]
                 + [_resident(w.shape) for w in mix_weights]
                 + [_row_stream(w.shape[1], w.shape[2], steps, layer) for w in streamed],
        out_specs=[tile(D_MODEL)]
                  + [pl.BlockSpec((w.shape[1] // steps, w.shape[2]), lambda i: (i, 0)) for w in streamed],
        out_shape=[jax.ShapeDtypeStruct((n, D_MODEL), F32)]
                  + [jax.ShapeDtypeStruct(w.shape[1:], BF16) for w in streamed],
        scratch_shapes=[pltpu.VMEM((tm, D_MODEL), BF16), pltpu.VMEM((tm, D_MODEL), BF16)],
        compiler_params=_params(1),
        name="mix_out",
    )(x2, oa, ob, g_mix, *mix_weights, w_gate, w_up, w_down)
    return outs[0], outs[1:]


def _ffn_kernel(x_ref, g_ref, wg_ref, wu_ref, wd_ref, o_ref, h_scr, a_scr):
    hidden = wg_ref.shape[1]
    for rb in range(0, x_ref.shape[0], ROW_BLOCK):
        rows = slice(rb, rb + ROW_BLOCK)
        xf = x_ref[rows, :]
        h_scr[rows, :] = ((xf * _rms_scale(xf)) * g_ref[...]).astype(BF16)
        for c in range(0, hidden, MXU_N):
            cs = slice(c, c + MXU_N)
            gate = jnp.dot(h_scr[rows, :], wg_ref[:, cs], preferred_element_type=F32)
            up = jnp.dot(h_scr[rows, :], wu_ref[:, cs], preferred_element_type=F32)
            a_scr[rows, cs] = (jax.nn.silu(gate) * up).astype(BF16)
        for c in range(0, D_MODEL, MXU_N):
            cs = slice(c, c + MXU_N)
            o_ref[rows, cs] = x_ref[rows, cs] + jnp.dot(a_scr[rows, :], wd_ref[:, cs], preferred_element_type=F32)


def _ffn(x1, g_ffn, w_gate, w_up, w_down):
    n = x1.shape[0]
    tm = TOK_TILE
    hidden = w_gate.shape[1]
    tile = pl.BlockSpec((tm, D_MODEL), lambda i: (i, 0))
    return pl.pallas_call(
        _ffn_kernel,
        grid=(n // tm,),
        in_specs=[tile, _resident((1, D_MODEL)), _resident(w_gate.shape), _resident(w_up.shape),
                  _resident(w_down.shape)],
        out_specs=tile,
        out_shape=jax.ShapeDtypeStruct((n, D_MODEL), F32),
        scratch_shapes=[pltpu.VMEM((tm, D_MODEL), BF16), pltpu.VMEM((tm, hidden), BF16)],
        compiler_params=_params(1),
        name="ffn",
    )(x1, g_ffn, w_gate, w_up, w_down)


def _param_table(rpb, t5_table, gain_a, gain_b):
    per_head = jnp.transpose(rpb.astype(F32), (0, 2, 1))
    pad = lambda a, w: jnp.pad(a, ((0, 0),) * (a.ndim - 1) + ((0, w - a.shape[-1]),))
    na = jnp.concatenate([pad(per_head[:-1], NA_TABLE_HALF), pad(per_head[1:], LANES - NA_TABLE_HALF)], axis=-1)
    fold = HEAD_DIM * QK_SCALE * LOG2E
    gains = jnp.tile(jnp.stack([gain_a, gain_b]).astype(F32) * fold, (1, LANES // HEAD_DIM))
    rows = jnp.concatenate([na.reshape(-1, LANES), pad(t5_table.astype(F32).T, LANES), gains], axis=0)
    return jnp.pad(rows, ((0, PT_ROWS - rows.shape[0]), (0, 0)))


def _layer(x2, b, t, layer, norm_mix, w_in, q_norm_a, k_norm_a, rpb_a, q_norm_b, k_norm_b, sink_b, t5_table,
           w_branch_a, w_branch_b, w_out, norm_ffn, w_gate, w_up, w_down):
    g_mix = norm_mix[layer].reshape(1, D_MODEL)
    ptab = _param_table(rpb_a[layer], t5_table, q_norm_a[layer] * k_norm_a[layer], q_norm_b[layer] * k_norm_b[layer])

    (qa, ka, va, qb, kbd, vbd), mix_weights = _in_proj(x2, g_mix, ptab, w_in, w_branch_a, w_branch_b,
                                                       w_out, layer)
    n = x2.shape[0]
    r3 = lambda a: a.reshape(b, t, a.shape[-1])
    oa = _na_attn(ptab, r3(qa), r3(ka), r3(va))
    ob = _swa_attn(sink_b[layer], ptab, r3(qb), r3(kbd), r3(vbd))
    x1, ffn_weights = _mix_out(x2, oa.reshape(n, A_W), ob.reshape(n, B_QW), g_mix, mix_weights,
                               w_gate, w_up, w_down, layer)
    return _ffn(x1, norm_ffn[layer].reshape(1, D_MODEL), *ffn_weights)


def kernel(x, norm_mix, w_in, q_norm_a, k_norm_a, rpb_a, q_norm_b, k_norm_b, sink_b, t5_table,
           w_branch_a, w_branch_b, w_out, norm_ffn, w_gate, w_up, w_down):
    b, t, d = x.shape
    x2 = x.reshape(b * t, d)
    for layer in range(norm_mix.shape[0]):
        x2 = _layer(x2, b, t, layer, norm_mix, w_in, q_norm_a, k_norm_a, rpb_a, q_norm_b, k_norm_b, sink_b,
                    t5_table, w_branch_a, w_branch_b, w_out, norm_ffn, w_gate, w_up, w_down)
    return x2.reshape(b, t, d)
```

```python
import functools

import jax
import jax.numpy as jnp
import numpy as np
from jax import lax
from jax.experimental import pallas as pl
from jax.experimental.pallas import tpu as pltpu

F32 = jnp.float32
BF16 = jnp.bfloat16

D_MODEL = 1024
HEAD_DIM = 64
GRID_W = 64
NA_HEADS = 8
NA_KH = 8
NA_KW = 16
SW_HEADS = 8
SW_KV_HEADS = 2
SW_GROUP = SW_HEADS // SW_KV_HEADS
SW_WINDOW = 128
SW_BLOCK = 128
T5_BUCKETS = 32
T5_MAX_DIST = 128
A_W = NA_HEADS * HEAD_DIM
B_QW = SW_HEADS * HEAD_DIM
B_KVW = SW_KV_HEADS * HEAD_DIM
QKV_W = 3 * A_W + B_QW + 2 * B_KVW
RMS_EPS = 1e-6
NEG_INF = -1e30
QK_SCALE = HEAD_DIM ** -0.5
LOG2E = 1.4426950408889634

LANES = 128
MXU_N = 256
VMEM_LIMIT = 56 * 1024 * 1024

TOK_TILE = 1024
ROW_BLOCK = 256
NA_ROWS_PER_STEP = 32
PROJ_SKEW = 4
NA_SKEW = 2
SW_SKEW = 4
SW_BLOCKS_PER_STEP = 16
NA_SLOTS = NA_SKEW + 1
NA_TABLE_HALF = 32
FFN_COPY_COLS = MXU_N
PT_NA = 0
PT_T5 = (2 * NA_KH - 2) * NA_HEADS
PT_GAIN = PT_T5 + SW_HEADS
PT_ROWS = 128
SW_SLOTS = SW_SKEW + 1


def _resident(shape):
    return pl.BlockSpec(shape, lambda *_: (0,) * len(shape), pipeline_mode=pl.Buffered(1))


def _params(n_axes):
    return pltpu.CompilerParams(dimension_semantics=("arbitrary",) * n_axes, vmem_limit_bytes=VMEM_LIMIT)


def _rms_scale(xf):
    return lax.rsqrt(jnp.mean(xf * xf, axis=-1, keepdims=True) + RMS_EPS)


def _group_rms_scale(p):
    y = p * p
    lo = lax.broadcasted_iota(jnp.int32, (p.shape[0], LANES), 1) < HEAD_DIM
    scales = []
    for c in range(0, p.shape[1], LANES):
        yc = y[:, c:c + LANES]
        s_lo = jnp.sum(jnp.where(lo, yc, 0.0), axis=-1, keepdims=True)
        s_hi = jnp.sum(jnp.where(lo, 0.0, yc), axis=-1, keepdims=True)
        r_lo = lax.rsqrt(s_lo + HEAD_DIM * RMS_EPS)
        r_hi = lax.rsqrt(s_hi + HEAD_DIM * RMS_EPS)
        scales.append(jnp.where(lo, r_lo, r_hi))
    return jnp.concatenate(scales, axis=1)


def _cast_rows(pairs):
    for src_ref, dst_ref in pairs:
        dst_ref[...] = src_ref[:, src_ref.shape[1] - dst_ref.shape[1]:].astype(BF16)


def _in_proj_kernel(x_ref, g_ref, w_ref, ptab_ref, wg_f32, wa_f32, wb_f32, wo_f32,
                    qa_ref, ka_ref, va_ref, qb_ref, kbd_ref, vbd_ref, wg_bf, wa_bf, wb_bf, wo_bf,
                    w_scr, h_scr, p_scr):
    @pl.when(pl.program_id(0) == 0)
    def _cast_own_weights():
        for c in range(0, QKV_W, MXU_N):
            w_scr[:, c:c + MXU_N] = w_ref[:, c:c + MXU_N].astype(BF16)

    _cast_rows(((wg_f32, wg_bf), (wa_f32, wa_bf), (wb_f32, wb_bf), (wo_f32, wo_bf)))

    def normed(out_ref, rows, c, gain_row):
        def epilogue(slot):
            p = p_scr[slot]
            y = p * _group_rms_scale(p)
            if gain_row is not None:
                gain = ptab_ref[PT_GAIN + gain_row:PT_GAIN + gain_row + 1, :]
                y = y * jnp.concatenate([gain] * (MXU_N // LANES), axis=1)
            out_ref[rows, c:c + MXU_N] = y.astype(BF16)
        return epilogue

    def plain(out_ref, rows, c):
        def epilogue(slot):
            out_ref[rows, c:c + MXU_N] = p_scr[slot].astype(BF16)
        return epilogue

    def kv_b(rows):
        def epilogue(slot):
            kb = p_scr[slot, :, :B_KVW]
            kb = kb * _group_rms_scale(kb) * ptab_ref[PT_GAIN + 1:PT_GAIN + 2, :]
            vb = p_scr[slot, :, B_KVW:]
            lo = lax.broadcasted_iota(jnp.int32, kb.shape, 1) < HEAD_DIM
            k_swapped = pltpu.roll(kb, HEAD_DIM, axis=1)
            kbd_ref[rows, :LANES] = jnp.where(lo, kb, k_swapped).astype(BF16)
            kbd_ref[rows, LANES:] = jnp.where(lo, k_swapped, kb).astype(BF16)
            v_swapped = pltpu.roll(vb, HEAD_DIM, axis=1)
            for blk, (left, right) in enumerate(((vb, 1.0), (1.0, v_swapped), (v_swapped, 1.0), (1.0, vb))):
                vbd_ref[rows, blk * LANES:(blk + 1) * LANES] = jnp.where(lo, left, right).astype(BF16)
        return epilogue

    units = []
    for rb in range(0, x_ref.shape[0], ROW_BLOCK):
        rows = slice(rb, rb + ROW_BLOCK)
        xf = x_ref[rows, :]
        h_scr[rows, :] = ((xf * _rms_scale(xf)) * g_ref[...]).astype(BF16)
        for c in range(0, A_W, MXU_N):
            units += [(rows, c, normed(qa_ref, rows, c, None)), (rows, A_W + c, normed(ka_ref, rows, c, 0)),
                      (rows, 3 * A_W + c, normed(qb_ref, rows, c, None))]
        units.append((rows, 3 * A_W + B_QW, kv_b(rows)))
        units += [(rows, 2 * A_W + c, plain(va_ref, rows, c)) for c in range(0, A_W, MXU_N)]
    n_slots = PROJ_SKEW + 1
    for idx in range(len(units) + PROJ_SKEW):
        if idx < len(units):
            rows, c0, _ = units[idx]
            p_scr[idx % n_slots] = jnp.dot(h_scr[rows, :], w_scr[:, c0:c0 + MXU_N], preferred_element_type=F32)
        if idx >= PROJ_SKEW:
            units[idx - PROJ_SKEW][2]((idx - PROJ_SKEW) % n_slots)


def _row_stream(n_rows, n_cols, steps, layer):
    return pl.BlockSpec((None, n_rows // steps, n_cols), lambda i: (layer, i, 0))


def _in_proj(x2, g_mix, ptab, w_in, w_a, w_b, w_o, layer):
    n = x2.shape[0]
    tm = TOK_TILE
    steps = n // tm
    tile = lambda w: pl.BlockSpec((tm, w), lambda i: (i, 0))
    gate_w = w_in.shape[2] - QKV_W
    out_w = (A_W, A_W, A_W, B_QW, 2 * LANES, 4 * LANES)
    streamed = ((w_in, gate_w), (w_a, D_MODEL), (w_b, D_MODEL), (w_o, D_MODEL))
    outs = pl.pallas_call(
        _in_proj_kernel,
        grid=(steps,),
        in_specs=[tile(D_MODEL), _resident((1, D_MODEL)),
                  pl.BlockSpec((None, D_MODEL, QKV_W), lambda i: (layer, 0, 0), pipeline_mode=pl.Buffered(1)),
                  _resident(ptab.shape)]
                 + [_row_stream(w.shape[1], w.shape[2], steps, layer) for w, _ in streamed],
        out_specs=[tile(w) for w in out_w]
                  + [pl.BlockSpec((w.shape[1] // steps, cols), lambda i: (i, 0)) for w, cols in streamed],
        out_shape=[jax.ShapeDtypeStruct((n, w), BF16) for w in out_w]
                  + [jax.ShapeDtypeStruct((w.shape[1], cols), BF16) for w, cols in streamed],
        scratch_shapes=[pltpu.VMEM((D_MODEL, QKV_W), BF16), pltpu.VMEM((tm, D_MODEL), BF16),
                        pltpu.VMEM((PROJ_SKEW + 1, ROW_BLOCK, MXU_N), F32)],
        compiler_params=_params(1),
        name="in_proj",
    )(x2, g_mix, w_in, ptab, w_in, w_a, w_b, w_o)
    return outs[:6], outs[6:]


def _na_kernel(rpb_ref, dc_ref, q_ref, k_ref, v_ref, o_ref, bias_ref, s_ref, *, rows):
    t = pl.program_id(1)
    lo = lax.broadcasted_iota(jnp.int32, (GRID_W, LANES), 1) < HEAD_DIM
    nt = (((1,), (1,)), ((), ()))
    n_slab = 2 * NA_KH - 2

    @pl.when((pl.program_id(0) == 0) & (t == 0))
    def _build_bias():
        offs = dc_ref[...]
        in_win = offs >= 0
        take = jnp.maximum(offs, 0)

        def build(idx, carry):
            h = idx // n_slab
            d = idx - h * n_slab
            row = jnp.broadcast_to(rpb_ref[pl.ds(PT_NA + d * NA_HEADS + h, 1), :] * LOG2E, (GRID_W, LANES))
            slab = jnp.where(in_win, jnp.take_along_axis(row, take, axis=1), NEG_INF)
            row0 = pl.multiple_of((h % 2) * GRID_W, GRID_W)
            bias_ref[h // 2, d, pl.ds(row0, GRID_W), :] = slab
            return carry

        lax.fori_loop(0, NA_HEADS * n_slab, build, 0, unroll=NA_HEADS)

    win0 = _na_window_start(t, rows)
    row_info = []
    for i in range(NA_ROWS_PER_STEP):
        r = t * NA_ROWS_PER_STEP + i
        rs = jnp.clip(r - NA_KH // 2, 0, rows - NA_KH)
        row_info.append(((NA_KH - 1) - (r - rs), pl.multiple_of((rs - win0) * GRID_W, GRID_W)))

    def scores(i, p, slot):
        d0, kstart = row_info[i]
        cols = slice(p * LANES, (p + 1) * LANES)
        q = q_ref[0, i * GRID_W:(i + 1) * GRID_W, cols]
        zero = jnp.zeros_like(q)
        q2 = jnp.concatenate([jnp.where(lo, q, zero), jnp.where(lo, zero, q)], axis=0)
        kk = k_ref[0, pl.ds(kstart, NA_KH * GRID_W), cols]
        s = lax.dot_general(q2, kk, nt, preferred_element_type=F32)
        for v in range(NA_KH // 2):
            vcols = slice(v * LANES, (v + 1) * LANES)
            s_ref[slot, :, vcols] = s[:, vcols] + bias_ref[p, d0 + 2 * v]

    ones_cols = jnp.ones((NA_KH * GRID_W, LANES), BF16)

    def finish(i, p, slot):
        _, kstart = row_info[i]
        cols = slice(p * LANES, (p + 1) * LANES)
        m = jnp.max(s_ref[slot], axis=-1, keepdims=True)
        e = jnp.exp2(s_ref[slot] - m).astype(BF16)
        vv = jnp.concatenate([v_ref[0, pl.ds(kstart, NA_KH * GRID_W), cols], ones_cols], axis=1)
        o2 = jnp.dot(e, vv, preferred_element_type=F32)
        num = jnp.where(lo, o2[:GRID_W, :LANES], o2[GRID_W:, :LANES])
        den = jnp.where(lo, o2[:GRID_W, LANES:], o2[GRID_W:, LANES:])
        o_ref[0, i * GRID_W:(i + 1) * GRID_W, cols] = (num / den).astype(BF16)

    units = [(i, p) for i in range(NA_ROWS_PER_STEP) for p in range(NA_HEADS // 2)]
    n_slots = s_ref.shape[0]
    for idx in range(len(units) + NA_SKEW):
        if idx < len(units):
            scores(*units[idx], idx % n_slots)
        if idx >= NA_SKEW:
            finish(*units[idx - NA_SKEW], (idx - NA_SKEW) % n_slots)


def _na_window_start(step, rows):
    return jnp.clip(step * NA_ROWS_PER_STEP - NA_KH // 2, 0, rows - (NA_ROWS_PER_STEP + NA_KH))


def _na_col_offsets():
    c = np.arange(GRID_W)
    kc = c[None, :]
    win_start = np.clip(c - NA_KW // 2, 0, GRID_W - NA_KW)[:, None]
    in_win = (kc >= win_start) & (kc < win_start + NA_KW)
    dc = kc - c[:, None] + (NA_KW - 1)
    return np.concatenate([np.where(in_win, dc, -1), np.where(in_win, dc + NA_TABLE_HALF, -1)],
                          axis=1).astype(np.int32)


def _na_attn(ptab, qa, ka, va):
    b, t, _ = qa.shape
    rows = t // GRID_W
    tq = NA_ROWS_PER_STEP * GRID_W
    tile = pl.BlockSpec((1, tq, A_W), lambda bi, ti: (bi, ti, 0))
    window = pl.BlockSpec((pl.Element(1), pl.Element((NA_ROWS_PER_STEP + NA_KH) * GRID_W), pl.Element(A_W)),
                          lambda bi, ti: (bi, _na_window_start(ti, rows) * GRID_W, 0))
    return pl.pallas_call(
        functools.partial(_na_kernel, rows=rows),
        grid=(b, rows // NA_ROWS_PER_STEP),
        in_specs=[_resident(ptab.shape), _resident((GRID_W, LANES)), tile, window, window],
        out_specs=tile,
        out_shape=jax.ShapeDtypeStruct((b, t, A_W), BF16),
        scratch_shapes=[pltpu.VMEM((NA_HEADS // 2, 2 * NA_KH - 2, LANES, LANES), F32),
                        pltpu.VMEM((NA_SLOTS, 2 * GRID_W, NA_KH * GRID_W), F32)],
        compiler_params=_params(2),
        name="na_attn",
    )(ptab, jnp.asarray(_na_col_offsets()), qa, ka, va)


def _swa_kernel(sink_ref, t5_ref, bucket_ref, q_ref, kp_ref, kc_ref, kn_ref, vp_ref, vc_ref, vn_ref, o_ref,
                bias_ref, s_ref, *, n_tiles):
    n = pl.program_id(1)
    lo = lax.broadcasted_iota(jnp.int32, (SW_BLOCK, LANES), 1) < HEAD_DIM
    nt = (((1,), (1,)), ((), ()))

    @pl.when((pl.program_id(0) == 0) & (n == 0))
    def _build_bias():
        bucket = bucket_ref[...]
        in_band = bucket >= 0
        take = jnp.maximum(bucket, 0)

        def build(h, carry):
            row = jnp.broadcast_to(t5_ref[pl.ds(PT_T5 + h, 1), :] * LOG2E, (SW_BLOCK, LANES))
            vals = [jnp.take_along_axis(row, take[:, c:c + LANES], axis=1) for c in range(0, 3 * SW_BLOCK, LANES)]
            row0 = pl.multiple_of((h % SW_GROUP) * SW_BLOCK, SW_BLOCK)
            bias_ref[h // SW_GROUP, pl.ds(row0, SW_BLOCK), :] = jnp.where(
                in_band, jnp.concatenate(vals, axis=1), NEG_INF)
            return carry

        lax.fori_loop(0, SW_HEADS, build, 0)

    def window(prev_ref, own_ref, next_ref, blk_col, j):
        kcols = slice(blk_col * LANES, (blk_col + 1) * LANES)
        blk = lambda b: own_ref[0, b * SW_BLOCK:(b + 1) * SW_BLOCK, kcols]
        first = prev_ref[0, :, kcols] if j == 0 else blk(j - 1)
        last = next_ref[0, :, kcols] if j == SW_BLOCKS_PER_STEP - 1 else blk(j + 1)
        return jnp.concatenate([first, blk(j), last], axis=0)

    def scores(kvh, j, c, hh, slot):
        rows = slice(j * SW_BLOCK, (j + 1) * SW_BLOCK)
        qb = q_ref[0, rows, (2 * kvh + c) * LANES:(2 * kvh + c + 1) * LANES]
        q1 = jnp.where(lo if hh == 0 else ~lo, qb, jnp.zeros_like(qb))
        s_ref[slot] = lax.dot_general(q1, window(kp_ref, kc_ref, kn_ref, kvh, j), nt,
                                      preferred_element_type=F32)

    def finish(kvh, j, c, hh, slot):
        rows = slice(j * SW_BLOCK, (j + 1) * SW_BLOCK)
        head = kvh * SW_GROUP + 2 * c + hh
        brows = slice((2 * c + hh) * SW_BLOCK, (2 * c + hh + 1) * SW_BLOCK)
        sg = s_ref[slot] + bias_ref[kvh, brows, :]
        if j == 0:
            sg = jnp.concatenate([jnp.where(n > 0, sg[:, :SW_BLOCK], NEG_INF), sg[:, SW_BLOCK:]], axis=1)
        if j == SW_BLOCKS_PER_STEP - 1:
            sg = jnp.concatenate([sg[:, :2 * SW_BLOCK],
                                  jnp.where(n < n_tiles - 1, sg[:, 2 * SW_BLOCK:], NEG_INF)], axis=1)
        sk = sink_ref[head] * LOG2E
        m = jnp.maximum(jnp.max(sg, axis=-1, keepdims=True), sk)
        e = jnp.exp2(sg - m).astype(BF16)
        o2 = jnp.dot(e, window(vp_ref, vc_ref, vn_ref, 2 * kvh + hh, j), preferred_element_type=F32)
        o = o2 / (pltpu.roll(o2, HEAD_DIM, axis=1) + jnp.exp2(sk - m))
        col0 = (2 * kvh + c) * LANES + hh * HEAD_DIM
        o_ref[0, rows, col0:col0 + HEAD_DIM] = o[:, hh * HEAD_DIM:(hh + 1) * HEAD_DIM].astype(BF16)

    units = [(kvh, j, c, hh) for j in range(SW_BLOCKS_PER_STEP) for kvh in range(SW_KV_HEADS)
             for c in range(2) for hh in range(2)]
    n_slots = s_ref.shape[0]
    for idx in range(len(units) + SW_SKEW):
        if idx < len(units):
            scores(*units[idx], idx % n_slots)
        if idx >= SW_SKEW:
            finish(*units[idx - SW_SKEW], (idx - SW_SKEW) % n_slots)


def _t5_bucket(rel):
    half = T5_BUCKETS // 2
    max_exact = half // 2
    ret = (rel > 0).astype(np.int32) * half
    n = np.abs(rel)
    large = max_exact + (np.log(np.maximum(n, 1) / max_exact)
                         / np.log(T5_MAX_DIST / max_exact) * (half - max_exact)).astype(np.int32)
    large = np.minimum(large, half - 1)
    return ret + np.where(n < max_exact, n, large)


def _swa_bucket_index():
    rel = np.arange(3 * SW_BLOCK)[None, :] - SW_BLOCK - np.arange(SW_BLOCK)[:, None]
    return np.where(np.abs(rel) <= SW_WINDOW, _t5_bucket(rel), -1).astype(np.int32)


def _swa_attn(sink, ptab, qb, kbd, vbd):
    b, t, _ = qb.shape
    tq = SW_BLOCKS_PER_STEP * SW_BLOCK
    n_tiles = t // tq
    last_blk = t // SW_BLOCK - 1
    qtile = pl.BlockSpec((1, tq, B_QW), lambda bi, ni: (bi, ni, 0))
    own = lambda w: pl.BlockSpec((1, tq, w), lambda bi, ni: (bi, ni, 0))
    prev = lambda w: pl.BlockSpec((1, SW_BLOCK, w),
                                  lambda bi, ni: (bi, jnp.maximum(ni * SW_BLOCKS_PER_STEP - 1, 0), 0))
    nxt = lambda w: pl.BlockSpec((1, SW_BLOCK, w),
                                 lambda bi, ni: (bi, jnp.minimum((ni + 1) * SW_BLOCKS_PER_STEP, last_blk), 0))
    kw, vw = kbd.shape[-1], vbd.shape[-1]
    smem = pl.BlockSpec(memory_space=pltpu.SMEM)
    return pl.pallas_call(
        functools.partial(_swa_kernel, n_tiles=n_tiles),
        grid=(b, n_tiles),
        in_specs=[smem, _resident(ptab.shape), _resident((SW_BLOCK, 3 * SW_BLOCK)),
                  qtile, prev(kw), own(kw), nxt(kw), prev(vw), own(vw), nxt(vw)],
        out_specs=qtile,
        out_shape=jax.ShapeDtypeStruct((b, t, B_QW), BF16),
        scratch_shapes=[pltpu.VMEM((SW_KV_HEADS, SW_GROUP * SW_BLOCK, 3 * SW_BLOCK), F32),
                        pltpu.VMEM((SW_SLOTS, SW_BLOCK, 3 * SW_BLOCK), F32)],
        compiler_params=_params(2),
        name="swa_attn",
    )(sink.astype(F32), ptab, jnp.asarray(_swa_bucket_index()),
      qb, kbd, kbd, kbd, vbd, vbd, vbd)


def _mix_kernel(x_ref, oa_ref, ob_ref, g_ref, wg_ref, wa_ref, wb_ref, wo_ref, wgate_f32, wup_f32, wdown_f32,
                x1_ref, wgate_bf, wup_bf, wdown_bf, h_scr, y_scr):
    _cast_rows(((wgate_f32, wgate_bf), (wup_f32, wup_bf), (wdown_f32, wdown_bf)))
    for rb in range(0, x_ref.shape[0], ROW_BLOCK):
        rows = slice(rb, rb + ROW_BLOCK)
        xf = x_ref[rows, :]
        h_scr[rows, :] = ((xf * _rms_scale(xf)) * g_ref[...]).astype(BF16)
        for c in range(0, D_MODEL, MXU_N):
            cs = slice(c, c + MXU_N)
            gs = slice(D_MODEL + c, D_MODEL + c + MXU_N)
            ga = jnp.dot(h_scr[rows, :], wg_ref[:, cs], preferred_element_type=F32)
            gb = jnp.dot(h_scr[rows, :], wg_ref[:, gs], preferred_element_type=F32)
            ya = jnp.dot(oa_ref[rows, :], wa_ref[:, cs], preferred_element_type=F32)
            yb = jnp.dot(ob_ref[rows, :], wb_ref[:, cs], preferred_element_type=F32)
            y_scr[rows, cs] = (jax.nn.sigmoid(ga) * ya + jax.nn.sigmoid(gb) * yb).astype(BF16)
        for c in range(0, D_MODEL, MXU_N):
            cs = slice(c, c + MXU_N)
            x1_ref[rows, cs] = x_ref[rows, cs] + jnp.dot(y_scr[rows, :], wo_ref[:, cs], preferred_element_type=F32)


def _mix_out(x2, oa, ob, g_mix, mix_weights, w_gate, w_up, w_down, layer):
    n = x2.shape[0]
    tm = TOK_TILE
    steps = n // tm
    tile = lambda w: pl.BlockSpec((tm, w), lambda i: (i, 0))
    streamed = (w_gate, w_up, w_down)
    outs = pl.pallas_call(
        _mix_kernel,
        grid=(steps,),
        in_specs=[tile(D_MODEL), tile(A_W), tile(B_QW), _resident((1, D_MODEL))]
                 + [_resident(w.shape) for w in mix_weights]
                 + [_row_stream(w.shape[1], w.shape[2], steps, layer) for w in streamed],
        out_specs=[tile(D_MODEL)]
                  + [pl.BlockSpec((w.shape[1] // steps, w.shape[2]), lambda i: (i, 0)) for w in streamed],
        out_shape=[jax.ShapeDtypeStruct((n, D_MODEL), F32)]
                  + [jax.ShapeDtypeStruct(w.shape[1:], BF16) for w in streamed],
        scratch_shapes=[pltpu.VMEM((tm, D_MODEL), BF16), pltpu.VMEM((tm, D_MODEL), BF16)],
        compiler_params=_params(1),
        name="mix_out",
    )(x2, oa, ob, g_mix, *mix_weights, w_gate, w_up, w_down)
    return outs[0], outs[1:]


def _ffn_weight_copies(hbm_ref, vmem_ref, sem_ref, which):
    return [pltpu.make_async_copy(hbm_ref.at[:, pl.ds(c, FFN_COPY_COLS)], vmem_ref.at[:, pl.ds(c, FFN_COPY_COLS)],
                                  sem_ref.at[which, c // FFN_COPY_COLS])
            for c in range(0, hbm_ref.shape[1], FFN_COPY_COLS)]


def _ffn_kernel(x_ref, g_ref, wg_hbm, wu_hbm, wd_hbm, o_ref, wg_ref, wu_ref, wd_ref, sem_ref, h_scr, a_scr):
    hidden = wg_ref.shape[1]
    gate_copies = _ffn_weight_copies(wg_hbm, wg_ref, sem_ref, 0)
    up_copies = _ffn_weight_copies(wu_hbm, wu_ref, sem_ref, 1)
    down_copies = _ffn_weight_copies(wd_hbm, wd_ref, sem_ref, 2)

    def step(first):
        for rb in range(0, x_ref.shape[0], ROW_BLOCK):
            rows = slice(rb, rb + ROW_BLOCK)
            arriving = first and rb == 0
            xf = x_ref[rows, :]
            h_scr[rows, :] = ((xf * _rms_scale(xf)) * g_ref[...]).astype(BF16)
            for c in range(0, hidden, MXU_N):
                cs = slice(c, c + MXU_N)
                if arriving and c % FFN_COPY_COLS == 0:
                    gate_copies[c // FFN_COPY_COLS].wait()
                    up_copies[c // FFN_COPY_COLS].wait()
                gate = jnp.dot(h_scr[rows, :], wg_ref[:, cs], preferred_element_type=F32)
                up = jnp.dot(h_scr[rows, :], wu_ref[:, cs], preferred_element_type=F32)
                a_scr[rows, cs] = (jax.nn.silu(gate) * up).astype(BF16)
            for c in range(0, D_MODEL, MXU_N):
                cs = slice(c, c + MXU_N)
                if arriving and c % FFN_COPY_COLS == 0:
                    down_copies[c // FFN_COPY_COLS].wait()
                o_ref[rows, cs] = x_ref[rows, cs] + jnp.dot(a_scr[rows, :], wd_ref[:, cs],
                                                            preferred_element_type=F32)

    @pl.when(pl.program_id(0) == 0)
    def _():
        for gate_copy, up_copy in zip(gate_copies, up_copies):
            gate_copy.start()
            up_copy.start()
        for down_copy in down_copies:
            down_copy.start()
        step(first=True)

    @pl.when(pl.program_id(0) != 0)
    def _():
        step(first=False)


def _ffn(x1, g_ffn, w_gate, w_up, w_down):
    n = x1.shape[0]
    tm = TOK_TILE
    hidden = w_gate.shape[1]
    tile = pl.BlockSpec((tm, D_MODEL), lambda i: (i, 0))
    return pl.pallas_call(
        _ffn_kernel,
        grid=(n // tm,),
        in_specs=[tile, _resident((1, D_MODEL))] + [pl.BlockSpec(memory_space=pltpu.HBM)] * 3,
        out_specs=tile,
        out_shape=jax.ShapeDtypeStruct((n, D_MODEL), F32),
        scratch_shapes=[pltpu.VMEM(w_gate.shape, BF16), pltpu.VMEM(w_up.shape, BF16), pltpu.VMEM(w_down.shape, BF16),
                        pltpu.SemaphoreType.DMA((3, hidden // FFN_COPY_COLS)),
                        pltpu.VMEM((tm, D_MODEL), BF16), pltpu.VMEM((tm, hidden), BF16)],
        compiler_params=_params(1),
        name="ffn",
    )(x1, g_ffn, w_gate, w_up, w_down)


def _param_table(rpb, t5_table, gain_a, gain_b):
    per_head = jnp.transpose(rpb.astype(F32), (0, 2, 1))
    pad = lambda a, w: jnp.pad(a, ((0, 0),) * (a.ndim - 1) + ((0, w - a.shape[-1]),))
    na = jnp.concatenate([pad(per_head[:-1], NA_TABLE_HALF), pad(per_head[1:], LANES - NA_TABLE_HALF)], axis=-1)
    fold = HEAD_DIM * QK_SCALE * LOG2E
    gains = jnp.tile(jnp.stack([gain_a, gain_b]).astype(F32) * fold, (1, LANES // HEAD_DIM))
    rows = jnp.concatenate([na.reshape(-1, LANES), pad(t5_table.astype(F32).T, LANES), gains], axis=0)
    return jnp.pad(rows, ((0, PT_ROWS - rows.shape[0]), (0, 0)))


def _layer(x2, b, t, layer, norm_mix, w_in, q_norm_a, k_norm_a, rpb_a, q_norm_b, k_norm_b, sink_b, t5_table,
           w_branch_a, w_branch_b, w_out, norm_ffn, w_gate, w_up, w_down):
    g_mix = norm_mix[layer].reshape(1, D_MODEL)
    ptab = _param_table(rpb_a[layer], t5_table, q_norm_a[layer] * k_norm_a[layer], q_norm_b[layer] * k_norm_b[layer])

    (qa, ka, va, qb, kbd, vbd), mix_weights = _in_proj(x2, g_mix, ptab, w_in, w_branch_a, w_branch_b,
                                                       w_out, layer)
    n = x2.shape[0]
    r3 = lambda a: a.reshape(b, t, a.shape[-1])
    oa = _na_attn(ptab, r3(qa), r3(ka), r3(va))
    ob = _swa_attn(sink_b[layer], ptab, r3(qb), r3(kbd), r3(vbd))
    x1, ffn_weights = _mix_out(x2, oa.reshape(n, A_W), ob.reshape(n, B_QW), g_mix, mix_weights,
                               w_gate, w_up, w_down, layer)
    return _ffn(x1, norm_ffn[layer].reshape(1, D_MODEL), *ffn_weights)


def kernel(x, norm_mix, w_in, q_norm_a, k_norm_a, rpb_a, q_norm_b, k_norm_b, sink_b, t5_table,
           w_branch_a, w_branch_b, w_out, norm_ffn, w_gate, w_up, w_down):
    b, t, d = x.shape
    x2 = x.reshape(b * t, d)
    for layer in range(norm_mix.shape[0]):
        x2 = _layer(x2, b, t, layer, norm_mix, w_in, q_norm_a, k_norm_a, rpb_a, q_norm_b, k_norm_b, sink_b,
                    t5_table, w_branch_a, w_branch_b, w_out, norm_ffn, w_gate, w_up, w_down)
    return x2.reshape(b, t, d)
```

```python
import functools

import jax
import jax.numpy as jnp
import numpy as np
from jax import lax
from jax.experimental import pallas as pl
from jax.experimental.pallas import tpu as pltpu

F32 = jnp.float32
BF16 = jnp.bfloat16

D_MODEL = 1024
HEAD_DIM = 64
GRID_W = 64
NA_HEADS = 8
NA_KH = 8
NA_KW = 16
SW_HEADS = 8
SW_KV_HEADS = 2
SW_GROUP = SW_HEADS // SW_KV_HEADS
SW_WINDOW = 128
SW_BLOCK = 128
T5_BUCKETS = 32
T5_MAX_DIST = 128
A_W = NA_HEADS * HEAD_DIM
B_QW = SW_HEADS * HEAD_DIM
B_KVW = SW_KV_HEADS * HEAD_DIM
QKV_W = 3 * A_W + B_QW + 2 * B_KVW
RMS_EPS = 1e-6
NEG_INF = -1e30
QK_SCALE = HEAD_DIM ** -0.5
LOG2E = 1.4426950408889634

LANES = 128
MXU_N = 256
VMEM_LIMIT = 56 * 1024 * 1024

TOK_TILE = 1024
ROW_BLOCK = 256
NA_ROWS_PER_STEP = 32
PROJ_SKEW = 4
NA_SKEW = 2
SW_SKEW = 4
SW_BLOCKS_PER_STEP = 16
NA_SLOTS = NA_SKEW + 1
NA_TABLE_HALF = 32
FFN_COPY_COLS = MXU_N
PT_NA = 0
PT_T5 = (2 * NA_KH - 2) * NA_HEADS
PT_GAIN = PT_T5 + SW_HEADS
PT_ROWS = 128
SW_SLOTS = SW_SKEW + 1


def _resident(shape):
    return pl.BlockSpec(shape, lambda *_: (0,) * len(shape), pipeline_mode=pl.Buffered(1))


def _params(n_axes):
    return pltpu.CompilerParams(dimension_semantics=("arbitrary",) * n_axes, vmem_limit_bytes=VMEM_LIMIT)


def _rms_scale(xf):
    return lax.rsqrt(jnp.mean(xf * xf, axis=-1, keepdims=True) + RMS_EPS)


def _group_rms_scale(p):
    y = p * p
    lo = lax.broadcasted_iota(jnp.int32, (p.shape[0], LANES), 1) < HEAD_DIM
    scales = []
    for c in range(0, p.shape[1], LANES):
        yc = y[:, c:c + LANES]
        s_lo = jnp.sum(jnp.where(lo, yc, 0.0), axis=-1, keepdims=True)
        s_hi = jnp.sum(jnp.where(lo, 0.0, yc), axis=-1, keepdims=True)
        r_lo = lax.rsqrt(s_lo + HEAD_DIM * RMS_EPS)
        r_hi = lax.rsqrt(s_hi + HEAD_DIM * RMS_EPS)
        scales.append(jnp.where(lo, r_lo, r_hi))
    return jnp.concatenate(scales, axis=1)


def _cast_rows(pairs):
    for src_ref, dst_ref in pairs:
        dst_ref[...] = src_ref[:, src_ref.shape[1] - dst_ref.shape[1]:].astype(BF16)


def _cast_row_pieces(pairs):
    for src_ref, dst_ref in pairs:
        pieces, _, width = dst_ref.shape
        for j in range(pieces):
            dst_ref[j] = src_ref[:, j * width:(j + 1) * width].astype(BF16)


def _in_proj_kernel(x_ref, g_ref, w_ref, ptab_ref, wg_f32, wa_f32, wb_f32, wo_f32,
                    qa_ref, ka_ref, va_ref, qb_ref, kbd_ref, vbd_ref, wg_bf, wa_bf, wb_bf, wo_bf,
                    w_scr, h_scr, p_scr):
    @pl.when(pl.program_id(0) == 0)
    def _cast_own_weights():
        for c in range(0, QKV_W, MXU_N):
            w_scr[:, c:c + MXU_N] = w_ref[:, c:c + MXU_N].astype(BF16)

    _cast_rows(((wg_f32, wg_bf), (wa_f32, wa_bf), (wb_f32, wb_bf), (wo_f32, wo_bf)))

    def normed(out_ref, rows, c, gain_row):
        def epilogue(slot):
            p = p_scr[slot]
            y = p * _group_rms_scale(p)
            if gain_row is not None:
                gain = ptab_ref[PT_GAIN + gain_row:PT_GAIN + gain_row + 1, :]
                y = y * jnp.concatenate([gain] * (MXU_N // LANES), axis=1)
            out_ref[rows, c:c + MXU_N] = y.astype(BF16)
        return epilogue

    def plain(out_ref, rows, c):
        def epilogue(slot):
            out_ref[rows, c:c + MXU_N] = p_scr[slot].astype(BF16)
        return epilogue

    def kv_b(rows):
        def epilogue(slot):
            kb = p_scr[slot, :, :B_KVW]
            kb = kb * _group_rms_scale(kb) * ptab_ref[PT_GAIN + 1:PT_GAIN + 2, :]
            vb = p_scr[slot, :, B_KVW:]
            lo = lax.broadcasted_iota(jnp.int32, kb.shape, 1) < HEAD_DIM
            k_swapped = pltpu.roll(kb, HEAD_DIM, axis=1)
            kbd_ref[rows, :LANES] = jnp.where(lo, kb, k_swapped).astype(BF16)
            kbd_ref[rows, LANES:] = jnp.where(lo, k_swapped, kb).astype(BF16)
            v_swapped = pltpu.roll(vb, HEAD_DIM, axis=1)
            for blk, (left, right) in enumerate(((vb, 1.0), (1.0, v_swapped), (v_swapped, 1.0), (1.0, vb))):
                vbd_ref[rows, blk * LANES:(blk + 1) * LANES] = jnp.where(lo, left, right).astype(BF16)
        return epilogue

    units = []
    for rb in range(0, x_ref.shape[0], ROW_BLOCK):
        rows = slice(rb, rb + ROW_BLOCK)
        xf = x_ref[rows, :]
        h_scr[rows, :] = ((xf * _rms_scale(xf)) * g_ref[...]).astype(BF16)
        for c in range(0, A_W, MXU_N):
            units += [(rows, c, normed(qa_ref, rows, c, None)), (rows, A_W + c, normed(ka_ref, rows, c, 0)),
                      (rows, 3 * A_W + c, normed(qb_ref, rows, c, None))]
        units.append((rows, 3 * A_W + B_QW, kv_b(rows)))
        units += [(rows, 2 * A_W + c, plain(va_ref, rows, c)) for c in range(0, A_W, MXU_N)]
    n_slots = PROJ_SKEW + 1
    for idx in range(len(units) + PROJ_SKEW):
        if idx < len(units):
            rows, c0, _ = units[idx]
            p_scr[idx % n_slots] = jnp.dot(h_scr[rows, :], w_scr[:, c0:c0 + MXU_N], preferred_element_type=F32)
        if idx >= PROJ_SKEW:
            units[idx - PROJ_SKEW][2]((idx - PROJ_SKEW) % n_slots)


def _row_stream(n_rows, n_cols, steps, layer):
    return pl.BlockSpec((None, n_rows // steps, n_cols), lambda i: (layer, i, 0))


def _in_proj(x2, g_mix, ptab, w_in, w_a, w_b, w_o, layer):
    n = x2.shape[0]
    tm = TOK_TILE
    steps = n // tm
    tile = lambda w: pl.BlockSpec((tm, w), lambda i: (i, 0))
    gate_w = w_in.shape[2] - QKV_W
    out_w = (A_W, A_W, A_W, B_QW, 2 * LANES, 4 * LANES)
    streamed = ((w_in, gate_w), (w_a, D_MODEL), (w_b, D_MODEL), (w_o, D_MODEL))
    outs = pl.pallas_call(
        _in_proj_kernel,
        grid=(steps,),
        in_specs=[tile(D_MODEL), _resident((1, D_MODEL)),
                  pl.BlockSpec((None, D_MODEL, QKV_W), lambda i: (layer, 0, 0), pipeline_mode=pl.Buffered(1)),
                  _resident(ptab.shape)]
                 + [_row_stream(w.shape[1], w.shape[2], steps, layer) for w, _ in streamed],
        out_specs=[tile(w) for w in out_w]
                  + [pl.BlockSpec((w.shape[1] // steps, cols), lambda i: (i, 0)) for w, cols in streamed],
        out_shape=[jax.ShapeDtypeStruct((n, w), BF16) for w in out_w]
                  + [jax.ShapeDtypeStruct((w.shape[1], cols), BF16) for w, cols in streamed],
        scratch_shapes=[pltpu.VMEM((D_MODEL, QKV_W), BF16), pltpu.VMEM((tm, D_MODEL), BF16),
                        pltpu.VMEM((PROJ_SKEW + 1, ROW_BLOCK, MXU_N), F32)],
        compiler_params=_params(1),
        name="in_proj",
    )(x2, g_mix, w_in, ptab, w_in, w_a, w_b, w_o)
    return outs[:6], outs[6:]


def _na_kernel(rpb_ref, dc_ref, q_ref, k_ref, v_ref, o_ref, bias_ref, s_ref, *, rows):
    t = pl.program_id(1)
    lo = lax.broadcasted_iota(jnp.int32, (GRID_W, LANES), 1) < HEAD_DIM
    nt = (((1,), (1,)), ((), ()))
    n_slab = 2 * NA_KH - 2

    @pl.when((pl.program_id(0) == 0) & (t == 0))
    def _build_bias():
        offs = dc_ref[...]
        in_win = offs >= 0
        take = jnp.maximum(offs, 0)

        def build(idx, carry):
            h = idx // n_slab
            d = idx - h * n_slab
            row = jnp.broadcast_to(rpb_ref[pl.ds(PT_NA + d * NA_HEADS + h, 1), :] * LOG2E, (GRID_W, LANES))
            slab = jnp.where(in_win, jnp.take_along_axis(row, take, axis=1), NEG_INF)
            row0 = pl.multiple_of((h % 2) * GRID_W, GRID_W)
            bias_ref[h // 2, d, pl.ds(row0, GRID_W), :] = slab
            return carry

        lax.fori_loop(0, NA_HEADS * n_slab, build, 0, unroll=NA_HEADS)

    win0 = _na_window_start(t, rows)
    row_info = []
    for i in range(NA_ROWS_PER_STEP):
        r = t * NA_ROWS_PER_STEP + i
        rs = jnp.clip(r - NA_KH // 2, 0, rows - NA_KH)
        row_info.append(((NA_KH - 1) - (r - rs), pl.multiple_of((rs - win0) * GRID_W, GRID_W)))

    def scores(i, p, slot):
        d0, kstart = row_info[i]
        cols = slice(p * LANES, (p + 1) * LANES)
        q = q_ref[0, i * GRID_W:(i + 1) * GRID_W, cols]
        zero = jnp.zeros_like(q)
        q2 = jnp.concatenate([jnp.where(lo, q, zero), jnp.where(lo, zero, q)], axis=0)
        kk = k_ref[0, pl.ds(kstart, NA_KH * GRID_W), cols]
        s = lax.dot_general(q2, kk, nt, preferred_element_type=F32)
        for v in range(NA_KH // 2):
            vcols = slice(v * LANES, (v + 1) * LANES)
            s_ref[slot, :, vcols] = s[:, vcols] + bias_ref[p, d0 + 2 * v]

    ones_cols = jnp.ones((NA_KH * GRID_W, LANES), BF16)

    def finish(i, p, slot):
        _, kstart = row_info[i]
        cols = slice(p * LANES, (p + 1) * LANES)
        m = jnp.max(s_ref[slot], axis=-1, keepdims=True)
        e = jnp.exp2(s_ref[slot] - m).astype(BF16)
        vv = jnp.concatenate([v_ref[0, pl.ds(kstart, NA_KH * GRID_W), cols], ones_cols], axis=1)
        o2 = jnp.dot(e, vv, preferred_element_type=F32)
        num = jnp.where(lo, o2[:GRID_W, :LANES], o2[GRID_W:, :LANES])
        den = jnp.where(lo, o2[:GRID_W, LANES:], o2[GRID_W:, LANES:])
        o_ref[0, i * GRID_W:(i + 1) * GRID_W, cols] = (num / den).astype(BF16)

    units = [(i, p) for i in range(NA_ROWS_PER_STEP) for p in range(NA_HEADS // 2)]
    n_slots = s_ref.shape[0]
    for idx in range(len(units) + NA_SKEW):
        if idx < len(units):
            scores(*units[idx], idx % n_slots)
        if idx >= NA_SKEW:
            finish(*units[idx - NA_SKEW], (idx - NA_SKEW) % n_slots)


def _na_window_start(step, rows):
    return jnp.clip(step * NA_ROWS_PER_STEP - NA_KH // 2, 0, rows - (NA_ROWS_PER_STEP + NA_KH))


def _na_col_offsets():
    c = np.arange(GRID_W)
    kc = c[None, :]
    win_start = np.clip(c - NA_KW // 2, 0, GRID_W - NA_KW)[:, None]
    in_win = (kc >= win_start) & (kc < win_start + NA_KW)
    dc = kc - c[:, None] + (NA_KW - 1)
    return np.concatenate([np.where(in_win, dc, -1), np.where(in_win, dc + NA_TABLE_HALF, -1)],
                          axis=1).astype(np.int32)


def _na_attn(ptab, qa, ka, va):
    b, t, _ = qa.shape
    rows = t // GRID_W
    tq = NA_ROWS_PER_STEP * GRID_W
    tile = pl.BlockSpec((1, tq, A_W), lambda bi, ti: (bi, ti, 0))
    window = pl.BlockSpec((pl.Element(1), pl.Element((NA_ROWS_PER_STEP + NA_KH) * GRID_W), pl.Element(A_W)),
                          lambda bi, ti: (bi, _na_window_start(ti, rows) * GRID_W, 0))
    return pl.pallas_call(
        functools.partial(_na_kernel, rows=rows),
        grid=(b, rows // NA_ROWS_PER_STEP),
        in_specs=[_resident(ptab.shape), _resident((GRID_W, LANES)), tile, window, window],
        out_specs=tile,
        out_shape=jax.ShapeDtypeStruct((b, t, A_W), BF16),
        scratch_shapes=[pltpu.VMEM((NA_HEADS // 2, 2 * NA_KH - 2, LANES, LANES), F32),
                        pltpu.VMEM((NA_SLOTS, 2 * GRID_W, NA_KH * GRID_W), F32)],
        compiler_params=_params(2),
        name="na_attn",
    )(ptab, jnp.asarray(_na_col_offsets()), qa, ka, va)


def _swa_kernel(sink_ref, t5_ref, bucket_ref, q_ref, kp_ref, kc_ref, kn_ref, vp_ref, vc_ref, vn_ref, o_ref,
                bias_ref, s_ref, *, n_tiles):
    n = pl.program_id(1)
    lo = lax.broadcasted_iota(jnp.int32, (SW_BLOCK, LANES), 1) < HEAD_DIM
    nt = (((1,), (1,)), ((), ()))

    @pl.when((pl.program_id(0) == 0) & (n == 0))
    def _build_bias():
        bucket = bucket_ref[...]
        in_band = bucket >= 0
        take = jnp.maximum(bucket, 0)

        def build(h, carry):
            row = jnp.broadcast_to(t5_ref[pl.ds(PT_T5 + h, 1), :] * LOG2E, (SW_BLOCK, LANES))
            vals = [jnp.take_along_axis(row, take[:, c:c + LANES], axis=1) for c in range(0, 3 * SW_BLOCK, LANES)]
            row0 = pl.multiple_of((h % SW_GROUP) * SW_BLOCK, SW_BLOCK)
            bias_ref[h // SW_GROUP, pl.ds(row0, SW_BLOCK), :] = jnp.where(
                in_band, jnp.concatenate(vals, axis=1), NEG_INF)
            return carry

        lax.fori_loop(0, SW_HEADS, build, 0)

    def window(prev_ref, own_ref, next_ref, blk_col, j):
        kcols = slice(blk_col * LANES, (blk_col + 1) * LANES)
        blk = lambda b: own_ref[0, b * SW_BLOCK:(b + 1) * SW_BLOCK, kcols]
        first = prev_ref[0, :, kcols] if j == 0 else blk(j - 1)
        last = next_ref[0, :, kcols] if j == SW_BLOCKS_PER_STEP - 1 else blk(j + 1)
        return jnp.concatenate([first, blk(j), last], axis=0)

    def scores(kvh, j, c, hh, slot):
        rows = slice(j * SW_BLOCK, (j + 1) * SW_BLOCK)
        qb = q_ref[0, rows, (2 * kvh + c) * LANES:(2 * kvh + c + 1) * LANES]
        q1 = jnp.where(lo if hh == 0 else ~lo, qb, jnp.zeros_like(qb))
        s_ref[slot] = lax.dot_general(q1, window(kp_ref, kc_ref, kn_ref, kvh, j), nt,
                                      preferred_element_type=F32)

    def finish(kvh, j, c, hh, slot):
        rows = slice(j * SW_BLOCK, (j + 1) * SW_BLOCK)
        head = kvh * SW_GROUP + 2 * c + hh
        brows = slice((2 * c + hh) * SW_BLOCK, (2 * c + hh + 1) * SW_BLOCK)
        sg = s_ref[slot] + bias_ref[kvh, brows, :]
        if j == 0:
            sg = jnp.concatenate([jnp.where(n > 0, sg[:, :SW_BLOCK], NEG_INF), sg[:, SW_BLOCK:]], axis=1)
        if j == SW_BLOCKS_PER_STEP - 1:
            sg = jnp.concatenate([sg[:, :2 * SW_BLOCK],
                                  jnp.where(n < n_tiles - 1, sg[:, 2 * SW_BLOCK:], NEG_INF)], axis=1)
        sk = sink_ref[head] * LOG2E
        m = jnp.maximum(jnp.max(sg, axis=-1, keepdims=True), sk)
        e = jnp.exp2(sg - m).astype(BF16)
        o2 = jnp.dot(e, window(vp_ref, vc_ref, vn_ref, 2 * kvh + hh, j), preferred_element_type=F32)
        o = o2 / (pltpu.roll(o2, HEAD_DIM, axis=1) + jnp.exp2(sk - m))
        col0 = (2 * kvh + c) * LANES + hh * HEAD_DIM
        o_ref[0, rows, col0:col0 + HEAD_DIM] = o[:, hh * HEAD_DIM:(hh + 1) * HEAD_DIM].astype(BF16)

    units = [(kvh, j, c, hh) for j in range(SW_BLOCKS_PER_STEP) for kvh in range(SW_KV_HEADS)
             for c in range(2) for hh in range(2)]
    n_slots = s_ref.shape[0]
    for idx in range(len(units) + SW_SKEW):
        if idx < len(units):
            scores(*units[idx], idx % n_slots)
        if idx >= SW_SKEW:
            finish(*units[idx - SW_SKEW], (idx - SW_SKEW) % n_slots)


def _t5_bucket(rel):
    half = T5_BUCKETS // 2
    max_exact = half // 2
    ret = (rel > 0).astype(np.int32) * half
    n = np.abs(rel)
    large = max_exact + (np.log(np.maximum(n, 1) / max_exact)
                         / np.log(T5_MAX_DIST / max_exact) * (half - max_exact)).astype(np.int32)
    large = np.minimum(large, half - 1)
    return ret + np.where(n < max_exact, n, large)


def _swa_bucket_index():
    rel = np.arange(3 * SW_BLOCK)[None, :] - SW_BLOCK - np.arange(SW_BLOCK)[:, None]
    return np.where(np.abs(rel) <= SW_WINDOW, _t5_bucket(rel), -1).astype(np.int32)


def _swa_attn(sink, ptab, qb, kbd, vbd):
    b, t, _ = qb.shape
    tq = SW_BLOCKS_PER_STEP * SW_BLOCK
    n_tiles = t // tq
    last_blk = t // SW_BLOCK - 1
    qtile = pl.BlockSpec((1, tq, B_QW), lambda bi, ni: (bi, ni, 0))
    own = lambda w: pl.BlockSpec((1, tq, w), lambda bi, ni: (bi, ni, 0))
    prev = lambda w: pl.BlockSpec((1, SW_BLOCK, w),
                                  lambda bi, ni: (bi, jnp.maximum(ni * SW_BLOCKS_PER_STEP - 1, 0), 0))
    nxt = lambda w: pl.BlockSpec((1, SW_BLOCK, w),
                                 lambda bi, ni: (bi, jnp.minimum((ni + 1) * SW_BLOCKS_PER_STEP, last_blk), 0))
    kw, vw = kbd.shape[-1], vbd.shape[-1]
    smem = pl.BlockSpec(memory_space=pltpu.SMEM)
    return pl.pallas_call(
        functools.partial(_swa_kernel, n_tiles=n_tiles),
        grid=(b, n_tiles),
        in_specs=[smem, _resident(ptab.shape), _resident((SW_BLOCK, 3 * SW_BLOCK)),
                  qtile, prev(kw), own(kw), nxt(kw), prev(vw), own(vw), nxt(vw)],
        out_specs=qtile,
        out_shape=jax.ShapeDtypeStruct((b, t, B_QW), BF16),
        scratch_shapes=[pltpu.VMEM((SW_KV_HEADS, SW_GROUP * SW_BLOCK, 3 * SW_BLOCK), F32),
                        pltpu.VMEM((SW_SLOTS, SW_BLOCK, 3 * SW_BLOCK), F32)],
        compiler_params=_params(2),
        name="swa_attn",
    )(sink.astype(F32), ptab, jnp.asarray(_swa_bucket_index()),
      qb, kbd, kbd, kbd, vbd, vbd, vbd)


def _mix_kernel(x_ref, oa_ref, ob_ref, g_ref, wg_ref, wa_ref, wb_ref, wo_ref, wgate_f32, wup_f32, wdown_f32,
                x1_ref, wgate_bf, wup_bf, wdown_bf, h_scr, y_scr):
    _cast_row_pieces(((wgate_f32, wgate_bf), (wup_f32, wup_bf), (wdown_f32, wdown_bf)))
    for rb in range(0, x_ref.shape[0], ROW_BLOCK):
        rows = slice(rb, rb + ROW_BLOCK)
        xf = x_ref[rows, :]
        h_scr[rows, :] = ((xf * _rms_scale(xf)) * g_ref[...]).astype(BF16)
        for c in range(0, D_MODEL, MXU_N):
            cs = slice(c, c + MXU_N)
            gs = slice(D_MODEL + c, D_MODEL + c + MXU_N)
            ga = jnp.dot(h_scr[rows, :], wg_ref[:, cs], preferred_element_type=F32)
            gb = jnp.dot(h_scr[rows, :], wg_ref[:, gs], preferred_element_type=F32)
            ya = jnp.dot(oa_ref[rows, :], wa_ref[:, cs], preferred_element_type=F32)
            yb = jnp.dot(ob_ref[rows, :], wb_ref[:, cs], preferred_element_type=F32)
            y_scr[rows, cs] = (jax.nn.sigmoid(ga) * ya + jax.nn.sigmoid(gb) * yb).astype(BF16)
        for c in range(0, D_MODEL, MXU_N):
            cs = slice(c, c + MXU_N)
            x1_ref[rows, cs] = x_ref[rows, cs] + jnp.dot(y_scr[rows, :], wo_ref[:, cs], preferred_element_type=F32)


def _mix_out(x2, oa, ob, g_mix, mix_weights, w_gate, w_up, w_down, layer):
    n = x2.shape[0]
    tm = TOK_TILE
    steps = n // tm
    tile = lambda w: pl.BlockSpec((tm, w), lambda i: (i, 0))
    streamed = (w_gate, w_up, w_down)
    pieces = lambda w, rows: (w.shape[2] // FFN_COPY_COLS, rows, FFN_COPY_COLS)
    outs = pl.pallas_call(
        _mix_kernel,
        grid=(steps,),
        in_specs=[tile(D_MODEL), tile(A_W), tile(B_QW), _resident((1, D_MODEL))]
                 + [_resident(w.shape) for w in mix_weights]
                 + [_row_stream(w.shape[1], w.shape[2], steps, layer) for w in streamed],
        out_specs=[tile(D_MODEL)]
                  + [pl.BlockSpec(pieces(w, w.shape[1] // steps), lambda i: (0, i, 0)) for w in streamed],
        out_shape=[jax.ShapeDtypeStruct((n, D_MODEL), F32)]
                  + [jax.ShapeDtypeStruct(pieces(w, w.shape[1]), BF16) for w in streamed],
        scratch_shapes=[pltpu.VMEM((tm, D_MODEL), BF16), pltpu.VMEM((tm, D_MODEL), BF16)],
        compiler_params=_params(1),
        name="mix_out",
    )(x2, oa, ob, g_mix, *mix_weights, w_gate, w_up, w_down)
    return outs[0], outs[1:]


def _ffn_weight_copies(hbm_ref, vmem_ref, sem_ref, which):
    return [pltpu.make_async_copy(hbm_ref.at[j], vmem_ref.at[j], sem_ref.at[which, j])
            for j in range(hbm_ref.shape[0])]


def _ffn_kernel(x_ref, g_ref, wg_hbm, wu_hbm, wd_hbm, o_ref, wg_ref, wu_ref, wd_ref, sem_ref, h_scr, a_scr):
    hidden = wg_ref.shape[0] * FFN_COPY_COLS
    gate_copies = _ffn_weight_copies(wg_hbm, wg_ref, sem_ref, 0)
    up_copies = _ffn_weight_copies(wu_hbm, wu_ref, sem_ref, 1)
    down_copies = _ffn_weight_copies(wd_hbm, wd_ref, sem_ref, 2)

    def step(first):
        for rb in range(0, x_ref.shape[0], ROW_BLOCK):
            rows = slice(rb, rb + ROW_BLOCK)
            arriving = first and rb == 0
            xf = x_ref[rows, :]
            h_scr[rows, :] = ((xf * _rms_scale(xf)) * g_ref[...]).astype(BF16)
            for c in range(0, hidden, MXU_N):
                cs = slice(c, c + MXU_N)
                if arriving and c % FFN_COPY_COLS == 0:
                    gate_copies[c // FFN_COPY_COLS].wait()
                    up_copies[c // FFN_COPY_COLS].wait()
                gate = jnp.dot(h_scr[rows, :], wg_ref[c // MXU_N], preferred_element_type=F32)
                up = jnp.dot(h_scr[rows, :], wu_ref[c // MXU_N], preferred_element_type=F32)
                a_scr[rows, cs] = (jax.nn.silu(gate) * up).astype(BF16)
            for c in range(0, D_MODEL, MXU_N):
                cs = slice(c, c + MXU_N)
                if arriving and c % FFN_COPY_COLS == 0:
                    down_copies[c // FFN_COPY_COLS].wait()
                o_ref[rows, cs] = x_ref[rows, cs] + jnp.dot(a_scr[rows, :], wd_ref[c // MXU_N],
                                                            preferred_element_type=F32)

    @pl.when(pl.program_id(0) == 0)
    def _():
        for gate_copy, up_copy in zip(gate_copies, up_copies):
            gate_copy.start()
            up_copy.start()
        for down_copy in down_copies:
            down_copy.start()
        step(first=True)

    @pl.when(pl.program_id(0) != 0)
    def _():
        step(first=False)


def _ffn(x1, g_ffn, w_gate, w_up, w_down):
    n = x1.shape[0]
    tm = TOK_TILE
    hidden = w_gate.shape[0] * FFN_COPY_COLS
    tile = pl.BlockSpec((tm, D_MODEL), lambda i: (i, 0))
    return pl.pallas_call(
        _ffn_kernel,
        grid=(n // tm,),
        in_specs=[tile, _resident((1, D_MODEL))] + [pl.BlockSpec(memory_space=pltpu.HBM)] * 3,
        out_specs=tile,
        out_shape=jax.ShapeDtypeStruct((n, D_MODEL), F32),
        scratch_shapes=[pltpu.VMEM(w_gate.shape, BF16), pltpu.VMEM(w_up.shape, BF16), pltpu.VMEM(w_down.shape, BF16),
                        pltpu.SemaphoreType.DMA((3, w_gate.shape[0])),
                        pltpu.VMEM((tm, D_MODEL), BF16), pltpu.VMEM((tm, hidden), BF16)],
        compiler_params=_params(1),
        name="ffn",
    )(x1, g_ffn, w_gate, w_up, w_down)


def _param_table(rpb, t5_table, gain_a, gain_b):
    per_head = jnp.transpose(rpb.astype(F32), (0, 2, 1))
    pad = lambda a, w: jnp.pad(a, ((0, 0),) * (a.ndim - 1) + ((0, w - a.shape[-1]),))
    na = jnp.concatenate([pad(per_head[:-1], NA_TABLE_HALF), pad(per_head[1:], LANES - NA_TABLE_HALF)], axis=-1)
    fold = HEAD_DIM * QK_SCALE * LOG2E
    gains = jnp.tile(jnp.stack([gain_a, gain_b]).astype(F32) * fold, (1, LANES // HEAD_DIM))
    rows = jnp.concatenate([na.reshape(-1, LANES), pad(t5_table.astype(F32).T, LANES), gains], axis=0)
    return jnp.pad(rows, ((0, PT_ROWS - rows.shape[0]), (0, 0)))


def _layer(x2, b, t, layer, norm_mix, w_in, q_norm_a, k_norm_a, rpb_a, q_norm_b, k_norm_b, sink_b, t5_table,
           w_branch_a, w_branch_b, w_out, norm_ffn, w_gate, w_up, w_down):
    g_mix = norm_mix[layer].reshape(1, D_MODEL)
    ptab = _param_table(rpb_a[layer], t5_table, q_norm_a[layer] * k_norm_a[layer], q_norm_b[layer] * k_norm_b[layer])

    (qa, ka, va, qb, kbd, vbd), mix_weights = _in_proj(x2, g_mix, ptab, w_in, w_branch_a, w_branch_b,
                                                       w_out, layer)
    n = x2.shape[0]
    r3 = lambda a: a.reshape(b, t, a.shape[-1])
    oa = _na_attn(ptab, r3(qa), r3(ka), r3(va))
    ob = _swa_attn(sink_b[layer], ptab, r3(qb), r3(kbd), r3(vbd))
    x1, ffn_weights = _mix_out(x2, oa.reshape(n, A_W), ob.reshape(n, B_QW), g_mix, mix_weights,
                               w_gate, w_up, w_down, layer)
    return _ffn(x1, norm_ffn[layer].reshape(1, D_MODEL), *ffn_weights)


def kernel(x, norm_mix, w_in, q_norm_a, k_norm_a, rpb_a, q_norm_b, k_norm_b, sink_b, t5_table,
           w_branch_a, w_branch_b, w_out, norm_ffn, w_gate, w_up, w_down):
    b, t, d = x.shape
    x2 = x.reshape(b * t, d)
    for layer in range(norm_mix.shape[0]):
        x2 = _layer(x2, b, t, layer, norm_mix, w_in, q_norm_a, k_norm_a, rpb_a, q_norm_b, k_norm_b, sink_b,
                    t5_table, w_branch_a, w_branch_b, w_out, norm_ffn, w_gate, w_up, w_down)
    return x2.reshape(b, t, d)
```

```python
import functools

import jax
import jax.numpy as jnp
import numpy as np
from jax import lax
from jax.experimental import pallas as pl
from jax.experimental.pallas import tpu as pltpu

F32 = jnp.float32
BF16 = jnp.bfloat16

D_MODEL = 1024
HEAD_DIM = 64
GRID_W = 64
NA_HEADS = 8
NA_KH = 8
NA_KW = 16
SW_HEADS = 8
SW_KV_HEADS = 2
SW_GROUP = SW_HEADS // SW_KV_HEADS
SW_WINDOW = 128
SW_BLOCK = 128
T5_BUCKETS = 32
T5_MAX_DIST = 128
A_W = NA_HEADS * HEAD_DIM
B_QW = SW_HEADS * HEAD_DIM
B_KVW = SW_KV_HEADS * HEAD_DIM
QKV_W = 3 * A_W + B_QW + 2 * B_KVW
RMS_EPS = 1e-6
NEG_INF = -1e30
QK_SCALE = HEAD_DIM ** -0.5
LOG2E = 1.4426950408889634

LANES = 128
MXU_N = 256
VMEM_LIMIT = 56 * 1024 * 1024

TOK_TILE = 1024
ROW_BLOCK = 256
NA_ROWS_PER_STEP = 32
PROJ_SKEW = 4
NA_SKEW = 2
SW_SKEW = 4
SW_BLOCKS_PER_STEP = 16
NA_SLOTS = NA_SKEW + 1
NA_TABLE_HALF = 32
PT_NA = 0
PT_T5 = (2 * NA_KH - 2) * NA_HEADS
PT_GAIN = PT_T5 + SW_HEADS
PT_ROWS = 128
SW_SLOTS = SW_SKEW + 1


def _resident(shape):
    return pl.BlockSpec(shape, lambda *_: (0,) * len(shape), pipeline_mode=pl.Buffered(1))


def _params(n_axes):
    return pltpu.CompilerParams(dimension_semantics=("arbitrary",) * n_axes, vmem_limit_bytes=VMEM_LIMIT)


def _rms_scale(xf):
    return lax.rsqrt(jnp.mean(xf * xf, axis=-1, keepdims=True) + RMS_EPS)


def _group_rms_scale(p):
    y = p * p
    lo = lax.broadcasted_iota(jnp.int32, (p.shape[0], LANES), 1) < HEAD_DIM
    scales = []
    for c in range(0, p.shape[1], LANES):
        yc = y[:, c:c + LANES]
        s_lo = jnp.sum(jnp.where(lo, yc, 0.0), axis=-1, keepdims=True)
        s_hi = jnp.sum(jnp.where(lo, 0.0, yc), axis=-1, keepdims=True)
        r_lo = lax.rsqrt(s_lo + HEAD_DIM * RMS_EPS)
        r_hi = lax.rsqrt(s_hi + HEAD_DIM * RMS_EPS)
        scales.append(jnp.where(lo, r_lo, r_hi))
    return jnp.concatenate(scales, axis=1)


def _cast_rows(pairs):
    for src_ref, dst_ref in pairs:
        dst_ref[...] = src_ref[:, src_ref.shape[1] - dst_ref.shape[1]:].astype(BF16)


def _in_proj_kernel(x_ref, g_ref, w_ref, ptab_ref, wg_f32, wa_f32, wb_f32, wo_f32,
                    qa_ref, ka_ref, va_ref, qb_ref, kbd_ref, vbd_ref, wg_bf, wa_bf, wb_bf, wo_bf,
                    w_scr, h_scr, p_scr):
    @pl.when(pl.program_id(0) == 0)
    def _cast_own_weights():
        for c in range(0, QKV_W, MXU_N):
            w_scr[:, c:c + MXU_N] = w_ref[:, c:c + MXU_N].astype(BF16)

    _cast_rows(((wg_f32, wg_bf), (wa_f32, wa_bf), (wb_f32, wb_bf), (wo_f32, wo_bf)))

    def normed(out_ref, rows, c, gain_row):
        def epilogue(slot):
            p = p_scr[slot]
            y = p * _group_rms_scale(p)
            if gain_row is not None:
                gain = ptab_ref[PT_GAIN + gain_row:PT_GAIN + gain_row + 1, :]
                y = y * jnp.concatenate([gain] * (MXU_N // LANES), axis=1)
            out_ref[rows, c:c + MXU_N] = y.astype(BF16)
        return epilogue

    def plain(out_ref, rows, c):
        def epilogue(slot):
            out_ref[rows, c:c + MXU_N] = p_scr[slot].astype(BF16)
        return epilogue

    def kv_b(rows):
        def epilogue(slot):
            kb = p_scr[slot, :, :B_KVW]
            kb = kb * _group_rms_scale(kb) * ptab_ref[PT_GAIN + 1:PT_GAIN + 2, :]
            vb = p_scr[slot, :, B_KVW:]
            lo = lax.broadcasted_iota(jnp.int32, kb.shape, 1) < HEAD_DIM
            k_swapped = pltpu.roll(kb, HEAD_DIM, axis=1)
            kbd_ref[rows, :LANES] = jnp.where(lo, kb, k_swapped).astype(BF16)
            kbd_ref[rows, LANES:] = jnp.where(lo, k_swapped, kb).astype(BF16)
            v_swapped = pltpu.roll(vb, HEAD_DIM, axis=1)
            for blk, (left, right) in enumerate(((vb, 1.0), (1.0, v_swapped), (v_swapped, 1.0), (1.0, vb))):
                vbd_ref[rows, blk * LANES:(blk + 1) * LANES] = jnp.where(lo, left, right).astype(BF16)
        return epilogue

    units = []
    for rb in range(0, x_ref.shape[0], ROW_BLOCK):
        rows = slice(rb, rb + ROW_BLOCK)
        xf = x_ref[rows, :]
        h_scr[rows, :] = ((xf * _rms_scale(xf)) * g_ref[...]).astype(BF16)
        for c in range(0, A_W, MXU_N):
            units += [(rows, c, normed(qa_ref, rows, c, None)), (rows, A_W + c, normed(ka_ref, rows, c, 0)),
                      (rows, 3 * A_W + c, normed(qb_ref, rows, c, None))]
        units.append((rows, 3 * A_W + B_QW, kv_b(rows)))
        units += [(rows, 2 * A_W + c, plain(va_ref, rows, c)) for c in range(0, A_W, MXU_N)]
    n_slots = PROJ_SKEW + 1
    for idx in range(len(units) + PROJ_SKEW):
        if idx < len(units):
            rows, c0, _ = units[idx]
            p_scr[idx % n_slots] = jnp.dot(h_scr[rows, :], w_scr[:, c0:c0 + MXU_N], preferred_element_type=F32)
        if idx >= PROJ_SKEW:
            units[idx - PROJ_SKEW][2]((idx - PROJ_SKEW) % n_slots)


def _row_stream(n_rows, n_cols, steps, layer):
    return pl.BlockSpec((None, n_rows // steps, n_cols), lambda i: (layer, i, 0))


def _in_proj(x2, g_mix, ptab, w_in, w_a, w_b, w_o, layer):
    n = x2.shape[0]
    tm = TOK_TILE
    steps = n // tm
    tile = lambda w: pl.BlockSpec((tm, w), lambda i: (i, 0))
    gate_w = w_in.shape[2] - QKV_W
    out_w = (A_W, A_W, A_W, B_QW, 2 * LANES, 4 * LANES)
    streamed = ((w_in, gate_w), (w_a, D_MODEL), (w_b, D_MODEL), (w_o, D_MODEL))
    outs = pl.pallas_call(
        _in_proj_kernel,
        grid=(steps,),
        in_specs=[tile(D_MODEL), _resident((1, D_MODEL)),
                  pl.BlockSpec((None, D_MODEL, QKV_W), lambda i: (layer, 0, 0), pipeline_mode=pl.Buffered(1)),
                  _resident(ptab.shape)]
                 + [_row_stream(w.shape[1], w.shape[2], steps, layer) for w, _ in streamed],
        out_specs=[tile(w) for w in out_w]
                  + [pl.BlockSpec((w.shape[1] // steps, cols), lambda i: (i, 0)) for w, cols in streamed],
        out_shape=[jax.ShapeDtypeStruct((n, w), BF16) for w in out_w]
                  + [jax.ShapeDtypeStruct((w.shape[1], cols), BF16) for w, cols in streamed],
        scratch_shapes=[pltpu.VMEM((D_MODEL, QKV_W), BF16), pltpu.VMEM((tm, D_MODEL), BF16),
                        pltpu.VMEM((PROJ_SKEW + 1, ROW_BLOCK, MXU_N), F32)],
        compiler_params=_params(1),
        name="in_proj",
    )(x2, g_mix, w_in, ptab, w_in, w_a, w_b, w_o)
    return outs[:6], outs[6:]


def _na_kernel(rpb_ref, dc_ref, q_ref, k_ref, v_ref, o_ref, bias_ref, s_ref, *, rows):
    t = pl.program_id(1)
    lo = lax.broadcasted_iota(jnp.int32, (GRID_W, LANES), 1) < HEAD_DIM
    nt = (((1,), (1,)), ((), ()))
    n_slab = 2 * NA_KH - 2

    @pl.when((pl.program_id(0) == 0) & (t == 0))
    def _build_bias():
        offs = dc_ref[...]
        in_win = offs >= 0
        take = jnp.maximum(offs, 0)

        def build(idx, carry):
            h = idx // n_slab
            d = idx - h * n_slab
            row = jnp.broadcast_to(rpb_ref[pl.ds(PT_NA + d * NA_HEADS + h, 1), :] * LOG2E, (GRID_W, LANES))
            slab = jnp.where(in_win, jnp.take_along_axis(row, take, axis=1), NEG_INF)
            row0 = pl.multiple_of((h % 2) * GRID_W, GRID_W)
            bias_ref[h // 2, d, pl.ds(row0, GRID_W), :] = slab
            return carry

        lax.fori_loop(0, NA_HEADS * n_slab, build, 0, unroll=NA_HEADS)

    win0 = _na_window_start(t, rows)
    row_info = []
    for i in range(NA_ROWS_PER_STEP):
        r = t * NA_ROWS_PER_STEP + i
        rs = jnp.clip(r - NA_KH // 2, 0, rows - NA_KH)
        row_info.append(((NA_KH - 1) - (r - rs), pl.multiple_of((rs - win0) * GRID_W, GRID_W)))

    def scores(i, p, slot):
        d0, kstart = row_info[i]
        cols = slice(p * LANES, (p + 1) * LANES)
        q = q_ref[0, i * GRID_W:(i + 1) * GRID_W, cols]
        zero = jnp.zeros_like(q)
        q2 = jnp.concatenate([jnp.where(lo, q, zero), jnp.where(lo, zero, q)], axis=0)
        kk = k_ref[0, pl.ds(kstart, NA_KH * GRID_W), cols]
        s = lax.dot_general(q2, kk, nt, preferred_element_type=F32)
        for v in range(NA_KH // 2):
            vcols = slice(v * LANES, (v + 1) * LANES)
            s_ref[slot, :, vcols] = s[:, vcols] + bias_ref[p, d0 + 2 * v]

    ones_cols = jnp.ones((NA_KH * GRID_W, LANES), BF16)

    def finish(i, p, slot):
        _, kstart = row_info[i]
        cols = slice(p * LANES, (p + 1) * LANES)
        m = jnp.max(s_ref[slot], axis=-1, keepdims=True)
        e = jnp.exp2(s_ref[slot] - m).astype(BF16)
        vv = jnp.concatenate([v_ref[0, pl.ds(kstart, NA_KH * GRID_W), cols], ones_cols], axis=1)
        o2 = jnp.dot(e, vv, preferred_element_type=F32)
        num = jnp.where(lo, o2[:GRID_W, :LANES], o2[GRID_W:, :LANES])
        den = jnp.where(lo, o2[:GRID_W, LANES:], o2[GRID_W:, LANES:])
        o_ref[0, i * GRID_W:(i + 1) * GRID_W, cols] = (num / den).astype(BF16)

    units = [(i, p) for i in range(NA_ROWS_PER_STEP) for p in range(NA_HEADS // 2)]
    n_slots = s_ref.shape[0]
    for idx in range(len(units) + NA_SKEW):
        if idx < len(units):
            scores(*units[idx], idx % n_slots)
        if idx >= NA_SKEW:
            finish(*units[idx - NA_SKEW], (idx - NA_SKEW) % n_slots)


def _na_window_start(step, rows):
    return jnp.clip(step * NA_ROWS_PER_STEP - NA_KH // 2, 0, rows - (NA_ROWS_PER_STEP + NA_KH))


def _na_col_offsets():
    c = np.arange(GRID_W)
    kc = c[None, :]
    win_start = np.clip(c - NA_KW // 2, 0, GRID_W - NA_KW)[:, None]
    in_win = (kc >= win_start) & (kc < win_start + NA_KW)
    dc = kc - c[:, None] + (NA_KW - 1)
    return np.concatenate([np.where(in_win, dc, -1), np.where(in_win, dc + NA_TABLE_HALF, -1)],
                          axis=1).astype(np.int32)


def _na_attn(ptab, qa, ka, va):
    b, t, _ = qa.shape
    rows = t // GRID_W
    tq = NA_ROWS_PER_STEP * GRID_W
    tile = pl.BlockSpec((1, tq, A_W), lambda bi, ti: (bi, ti, 0))
    window = pl.BlockSpec((pl.Element(1), pl.Element((NA_ROWS_PER_STEP + NA_KH) * GRID_W), pl.Element(A_W)),
                          lambda bi, ti: (bi, _na_window_start(ti, rows) * GRID_W, 0))
    return pl.pallas_call(
        functools.partial(_na_kernel, rows=rows),
        grid=(b, rows // NA_ROWS_PER_STEP),
        in_specs=[_resident(ptab.shape), _resident((GRID_W, LANES)), tile, window, window],
        out_specs=tile,
        out_shape=jax.ShapeDtypeStruct((b, t, A_W), BF16),
        scratch_shapes=[pltpu.VMEM((NA_HEADS // 2, 2 * NA_KH - 2, LANES, LANES), F32),
                        pltpu.VMEM((NA_SLOTS, 2 * GRID_W, NA_KH * GRID_W), F32)],
        compiler_params=_params(2),
        name="na_attn",
    )(ptab, jnp.asarray(_na_col_offsets()), qa, ka, va)


def _swa_kernel(sink_ref, t5_ref, bucket_ref, q_ref, kp_ref, kc_ref, kn_ref, vp_ref, vc_ref, vn_ref, o_ref,
                bias_ref, s_ref, *, n_tiles):
    n = pl.program_id(1)
    lo = lax.broadcasted_iota(jnp.int32, (SW_BLOCK, LANES), 1) < HEAD_DIM
    nt = (((1,), (1,)), ((), ()))

    @pl.when((pl.program_id(0) == 0) & (n == 0))
    def _build_bias():
        bucket = bucket_ref[...]
        in_band = bucket >= 0
        take = jnp.maximum(bucket, 0)

        def build(h, carry):
            row = jnp.broadcast_to(t5_ref[pl.ds(PT_T5 + h, 1), :] * LOG2E, (SW_BLOCK, LANES))
            vals = [jnp.take_along_axis(row, take[:, c:c + LANES], axis=1) for c in range(0, 3 * SW_BLOCK, LANES)]
            row0 = pl.multiple_of((h % SW_GROUP) * SW_BLOCK, SW_BLOCK)
            bias_ref[h // SW_GROUP, pl.ds(row0, SW_BLOCK), :] = jnp.where(
                in_band, jnp.concatenate(vals, axis=1), NEG_INF)
            return carry

        lax.fori_loop(0, SW_HEADS, build, 0)

    def window(prev_ref, own_ref, next_ref, blk_col, j):
        kcols = slice(blk_col * LANES, (blk_col + 1) * LANES)
        blk = lambda b: own_ref[0, b * SW_BLOCK:(b + 1) * SW_BLOCK, kcols]
        first = prev_ref[0, :, kcols] if j == 0 else blk(j - 1)
        last = next_ref[0, :, kcols] if j == SW_BLOCKS_PER_STEP - 1 else blk(j + 1)
        return jnp.concatenate([first, blk(j), last], axis=0)

    def scores(kvh, j, c, hh, slot):
        rows = slice(j * SW_BLOCK, (j + 1) * SW_BLOCK)
        qb = q_ref[0, rows, (2 * kvh + c) * LANES:(2 * kvh + c + 1) * LANES]
        q1 = jnp.where(lo if hh == 0 else ~lo, qb, jnp.zeros_like(qb))
        s_ref[slot] = lax.dot_general(q1, window(kp_ref, kc_ref, kn_ref, kvh, j), nt,
                                      preferred_element_type=F32)

    def finish(kvh, j, c, hh, slot):
        rows = slice(j * SW_BLOCK, (j + 1) * SW_BLOCK)
        head = kvh * SW_GROUP + 2 * c + hh
        brows = slice((2 * c + hh) * SW_BLOCK, (2 * c + hh + 1) * SW_BLOCK)
        sg = s_ref[slot] + bias_ref[kvh, brows, :]
        if j == 0:
            sg = jnp.concatenate([jnp.where(n > 0, sg[:, :SW_BLOCK], NEG_INF), sg[:, SW_BLOCK:]], axis=1)
        if j == SW_BLOCKS_PER_STEP - 1:
            sg = jnp.concatenate([sg[:, :2 * SW_BLOCK],
                                  jnp.where(n < n_tiles - 1, sg[:, 2 * SW_BLOCK:], NEG_INF)], axis=1)
        sk = sink_ref[head] * LOG2E
        m = jnp.maximum(jnp.max(sg, axis=-1, keepdims=True), sk)
        e = jnp.exp2(sg - m).astype(BF16)
        o2 = jnp.dot(e, window(vp_ref, vc_ref, vn_ref, 2 * kvh + hh, j), preferred_element_type=F32)
        o = o2 / (pltpu.roll(o2, HEAD_DIM, axis=1) + jnp.exp2(sk - m))
        col0 = (2 * kvh + c) * LANES + hh * HEAD_DIM
        o_ref[0, rows, col0:col0 + HEAD_DIM] = o[:, hh * HEAD_DIM:(hh + 1) * HEAD_DIM].astype(BF16)

    units = [(kvh, j, c, hh) for j in range(SW_BLOCKS_PER_STEP) for kvh in range(SW_KV_HEADS)
             for c in range(2) for hh in range(2)]
    n_slots = s_ref.shape[0]
    for idx in range(len(units) + SW_SKEW):
        if idx < len(units):
            scores(*units[idx], idx % n_slots)
        if idx >= SW_SKEW:
            finish(*units[idx - SW_SKEW], (idx - SW_SKEW) % n_slots)


def _t5_bucket(rel):
    half = T5_BUCKETS // 2
    max_exact = half // 2
    ret = (rel > 0).astype(np.int32) * half
    n = np.abs(rel)
    large = max_exact + (np.log(np.maximum(n, 1) / max_exact)
                         / np.log(T5_MAX_DIST / max_exact) * (half - max_exact)).astype(np.int32)
    large = np.minimum(large, half - 1)
    return ret + np.where(n < max_exact, n, large)


def _swa_bucket_index():
    rel = np.arange(3 * SW_BLOCK)[None, :] - SW_BLOCK - np.arange(SW_BLOCK)[:, None]
    return np.where(np.abs(rel) <= SW_WINDOW, _t5_bucket(rel), -1).astype(np.int32)


def _swa_attn(sink, ptab, qb, kbd, vbd):
    b, t, _ = qb.shape
    tq = SW_BLOCKS_PER_STEP * SW_BLOCK
    n_tiles = t // tq
    last_blk = t // SW_BLOCK - 1
    qtile = pl.BlockSpec((1, tq, B_QW), lambda bi, ni: (bi, ni, 0))
    own = lambda w: pl.BlockSpec((1, tq, w), lambda bi, ni: (bi, ni, 0))
    prev = lambda w: pl.BlockSpec((1, SW_BLOCK, w),
                                  lambda bi, ni: (bi, jnp.maximum(ni * SW_BLOCKS_PER_STEP - 1, 0), 0))
    nxt = lambda w: pl.BlockSpec((1, SW_BLOCK, w),
                                 lambda bi, ni: (bi, jnp.minimum((ni + 1) * SW_BLOCKS_PER_STEP, last_blk), 0))
    kw, vw = kbd.shape[-1], vbd.shape[-1]
    smem = pl.BlockSpec(memory_space=pltpu.SMEM)
    return pl.pallas_call(
        functools.partial(_swa_kernel, n_tiles=n_tiles),
        grid=(b, n_tiles),
        in_specs=[smem, _resident(ptab.shape), _resident((SW_BLOCK, 3 * SW_BLOCK)),
                  qtile, prev(kw), own(kw), nxt(kw), prev(vw), own(vw), nxt(vw)],
        out_specs=qtile,
        out_shape=jax.ShapeDtypeStruct((b, t, B_QW), BF16),
        scratch_shapes=[pltpu.VMEM((SW_KV_HEADS, SW_GROUP * SW_BLOCK, 3 * SW_BLOCK), F32),
                        pltpu.VMEM((SW_SLOTS, SW_BLOCK, 3 * SW_BLOCK), F32)],
        compiler_params=_params(2),
        name="swa_attn",
    )(sink.astype(F32), ptab, jnp.asarray(_swa_bucket_index()),
      qb, kbd, kbd, kbd, vbd, vbd, vbd)


def _mix_kernel(x_ref, oa_ref, ob_ref, g_ref, wg_ref, wa_ref, wb_ref, wo_ref, wgate_f32, wup_f32, wdown_f32,
                x1_ref, wgate_bf, wup_bf, wdown_bf, h_scr, y_scr):
    _cast_rows(((wgate_f32, wgate_bf), (wup_f32, wup_bf), (wdown_f32, wdown_bf)))
    for rb in range(0, x_ref.shape[0], ROW_BLOCK):
        rows = slice(rb, rb + ROW_BLOCK)
        xf = x_ref[rows, :]
        h_scr[rows, :] = ((xf * _rms_scale(xf)) * g_ref[...]).astype(BF16)
        for c in range(0, D_MODEL, MXU_N):
            cs = slice(c, c + MXU_N)
            gs = slice(D_MODEL + c, D_MODEL + c + MXU_N)
            ga = jnp.dot(h_scr[rows, :], wg_ref[:, cs], preferred_element_type=F32)
            gb = jnp.dot(h_scr[rows, :], wg_ref[:, gs], preferred_element_type=F32)
            ya = jnp.dot(oa_ref[rows, :], wa_ref[:, cs], preferred_element_type=F32)
            yb = jnp.dot(ob_ref[rows, :], wb_ref[:, cs], preferred_element_type=F32)
            y_scr[rows, cs] = (jax.nn.sigmoid(ga) * ya + jax.nn.sigmoid(gb) * yb).astype(BF16)
        for c in range(0, D_MODEL, MXU_N):
            cs = slice(c, c + MXU_N)
            x1_ref[rows, cs] = x_ref[rows, cs] + jnp.dot(y_scr[rows, :], wo_ref[:, cs], preferred_element_type=F32)


def _mix_out(x2, oa, ob, g_mix, mix_weights, w_gate, w_up, w_down, layer):
    n = x2.shape[0]
    tm = TOK_TILE
    steps = n // tm
    tile = lambda w: pl.BlockSpec((tm, w), lambda i: (i, 0))
    streamed = (w_gate, w_up, w_down)
    outs = pl.pallas_call(
        _mix_kernel,
        grid=(steps,),
        in_specs=[tile(D_MODEL), tile(A_W), tile(B_QW), _resident((1, D_MODEL))]
                 + [_resident(w.shape) for w in mix_weights]
                 + [_row_stream(w.shape[1], w.shape[2], steps, layer) for w in streamed],
        out_specs=[tile(D_MODEL)]
                  + [pl.BlockSpec((w.shape[1] // steps, w.shape[2]), lambda i: (i, 0)) for w in streamed],
        out_shape=[jax.ShapeDtypeStruct((n, D_MODEL), F32)]
                  + [jax.ShapeDtypeStruct(w.shape[1:], BF16) for w in streamed],
        scratch_shapes=[pltpu.VMEM((tm, D_MODEL), BF16), pltpu.VMEM((tm, D_MODEL), BF16)],
        compiler_params=_params(1),
        name="mix_out",
    )(x2, oa, ob, g_mix, *mix_weights, w_gate, w_up, w_down)
    return outs[0], outs[1:]


def _ffn_kernel(x_ref, g_ref, wg_ref, wu_ref, wd_ref, o_ref, h_scr, a_scr):
    hidden = wg_ref.shape[1]

    def row_block(i, carry):
        rows = pl.ds(pl.multiple_of(i * ROW_BLOCK, ROW_BLOCK), ROW_BLOCK)
        xf = x_ref[rows, :]
        h_scr[rows, :] = ((xf * _rms_scale(xf)) * g_ref[...]).astype(BF16)
        for c in range(0, hidden, MXU_N):
            cs = slice(c, c + MXU_N)
            gate = jnp.dot(h_scr[rows, :], wg_ref[:, cs], preferred_element_type=F32)
            up = jnp.dot(h_scr[rows, :], wu_ref[:, cs], preferred_element_type=F32)
            a_scr[rows, cs] = (jax.nn.silu(gate) * up).astype(BF16)
        for c in range(0, D_MODEL, MXU_N):
            cs = slice(c, c + MXU_N)
            o_ref[rows, cs] = x_ref[rows, cs] + jnp.dot(a_scr[rows, :], wd_ref[:, cs], preferred_element_type=F32)
        return carry

    lax.fori_loop(0, x_ref.shape[0] // ROW_BLOCK, row_block, 0)


def _ffn(x1, g_ffn, w_gate, w_up, w_down):
    n = x1.shape[0]
    tm = TOK_TILE
    hidden = w_gate.shape[1]
    tile = pl.BlockSpec((tm, D_MODEL), lambda i: (i, 0))
    return pl.pallas_call(
        _ffn_kernel,
        grid=(n // tm,),
        in_specs=[tile, _resident((1, D_MODEL)), _resident(w_gate.shape), _resident(w_up.shape),
                  _resident(w_down.shape)],
        out_specs=tile,
        out_shape=jax.ShapeDtypeStruct((n, D_MODEL), F32),
        scratch_shapes=[pltpu.VMEM((tm, D_MODEL), BF16), pltpu.VMEM((tm, hidden), BF16)],
        compiler_params=_params(1),
        name="ffn",
    )(x1, g_ffn, w_gate, w_up, w_down)


def _param_table(rpb, t5_table, gain_a, gain_b):
    per_head = jnp.transpose(rpb.astype(F32), (0, 2, 1))
    pad = lambda a, w: jnp.pad(a, ((0, 0),) * (a.ndim - 1) + ((0, w - a.shape[-1]),))
    na = jnp.concatenate([pad(per_head[:-1], NA_TABLE_HALF), pad(per_head[1:], LANES - NA_TABLE_HALF)], axis=-1)
    fold = HEAD_DIM * QK_SCALE * LOG2E
    gains = jnp.tile(jnp.stack([gain_a, gain_b]).astype(F32) * fold, (1, LANES // HEAD_DIM))
    rows = jnp.concatenate([na.reshape(-1, LANES), pad(t5_table.astype(F32).T, LANES), gains], axis=0)
    return jnp.pad(rows, ((0, PT_ROWS - rows.shape[0]), (0, 0)))


def _layer(x2, b, t, layer, norm_mix, w_in, q_norm_a, k_norm_a, rpb_a, q_norm_b, k_norm_b, sink_b, t5_table,
           w_branch_a, w_branch_b, w_out, norm_ffn, w_gate, w_up, w_down):
    g_mix = norm_mix[layer].reshape(1, D_MODEL)
    ptab = _param_table(rpb_a[layer], t5_table, q_norm_a[layer] * k_norm_a[layer], q_norm_b[layer] * k_norm_b[layer])

    (qa, ka, va, qb, kbd, vbd), mix_weights = _in_proj(x2, g_mix, ptab, w_in, w_branch_a, w_branch_b,
                                                       w_out, layer)
    n = x2.shape[0]
    r3 = lambda a: a.reshape(b, t, a.shape[-1])
    oa = _na_attn(ptab, r3(qa), r3(ka), r3(va))
    ob = _swa_attn(sink_b[layer], ptab, r3(qb), r3(kbd), r3(vbd))
    x1, ffn_weights = _mix_out(x2, oa.reshape(n, A_W), ob.reshape(n, B_QW), g_mix, mix_weights,
                               w_gate, w_up, w_down, layer)
    return _ffn(x1, norm_ffn[layer].reshape(1, D_MODEL), *ffn_weights)


def kernel(x, norm_mix, w_in, q_norm_a, k_norm_a, rpb_a, q_norm_b, k_norm_b, sink_b, t5_table,
           w_branch_a, w_branch_b, w_out, norm_ffn, w_gate, w_up, w_down):
    b, t, d = x.shape
    x2 = x.reshape(b * t, d)
    for layer in range(norm_mix.shape[0]):
        x2 = _layer(x2, b, t, layer, norm_mix, w_in, q_norm_a, k_norm_a, rpb_a, q_norm_b, k_norm_b, sink_b,
                    t5_table, w_branch_a, w_branch_b, w_out, norm_ffn, w_gate, w_up, w_down)
    return x2.reshape(b, t, d)
```
